```python
import math
import jax
import jax.numpy as jnp
from jax import lax
import numpy as np

D_MODEL = 2048
BATCH = 8
SEQ = 4096
DEPTH = 2

N_A_LAYERS = DEPTH // 2
N_B_LAYERS = DEPTH - N_A_LAYERS
EPS = 1e-6

SSM_EXPAND = 2
D_INNER = SSM_EXPAND * D_MODEL
SSM_HEAD_DIM = 64
SSM_HEADS = D_INNER // SSM_HEAD_DIM
SSM_GROUPS = 8
SSM_HEADS_PER_GROUP = SSM_HEADS // SSM_GROUPS
D_STATE = 128
CONV_WIDTH = 4
SSD_CHUNK = 128
D_BC = SSM_GROUPS * D_STATE
D_XBC = D_INNER + 2 * D_BC
D_IN_PROJ = D_INNER + D_XBC + SSM_HEADS

ATT_HEAD_DIM = 128
ATT_HEADS = D_MODEL // ATT_HEAD_DIM
D_ATT = ATT_HEADS * ATT_HEAD_DIM
Q_BLOCK = 128

FFN_MULT_OF = 256
D_FF = -(-8 * D_MODEL // (3 * FFN_MULT_OF)) * FFN_MULT_OF

kernel_name = 'yoco_mamba2_fox_hybrid'


def rmsnorm(x, w):
    xf = x.astype(jnp.float32)
    y = xf * lax.rsqrt(jnp.mean(xf * xf, axis=-1, keepdims=True) + EPS)
    return (y * w.astype(jnp.float32)).astype(x.dtype)


def causal_depthwise_conv(u, w, b):
    c = u.shape[-1]
    out = lax.conv_general_dilated(
        u, w[:, None, :].astype(u.dtype), window_strides=(1,),
        padding=[(CONV_WIDTH - 1, 0)], dimension_numbers=('NWC', 'WIO', 'NWC'),
        feature_group_count=c)
    return out + b.astype(u.dtype)


def ssd_chunked_scan(xdt, a, bm, cm):
    bsz, seq = xdt.shape[:2]
    nc = seq // SSD_CHUNK
    g, kh, p = SSM_GROUPS, SSM_HEADS_PER_GROUP, SSM_HEAD_DIM
    f32 = jnp.float32

    def chunks(t):
        t = t.reshape((bsz, nc, SSD_CHUNK) + t.shape[2:])
        return jnp.moveaxis(t, 1, 0)

    xc = chunks(xdt.astype(f32).reshape(bsz, seq, g, kh, p))
    ac = chunks(a.astype(f32).reshape(bsz, seq, g, kh))
    bc = chunks(bm.astype(f32))
    cc = chunks(cm.astype(f32))
    causal = jnp.tril(jnp.ones((SSD_CHUNK, SSD_CHUNK), dtype=bool))[None, :, :, None, None]

    def step(state, inp):
        x_, a_, b_, c_ = inp
        a_cum = jnp.cumsum(a_, axis=1)
        seg = a_cum[:, :, None] - a_cum[:, None, :]
        decay = jnp.exp(jnp.where(causal, seg, -jnp.inf))
        cb = jnp.einsum('btgn,bsgn->btsg', c_, b_)
        y_diag = jnp.einsum('btsg,btsgk,bsgkp->btgkp', cb, decay, x_)
        y_off = jnp.einsum('btgn,bgkpn->btgkp', c_, state) * jnp.exp(a_cum)[..., None]
        a_last = a_cum[:, -1]
        w_in = jnp.exp(a_last[:, None] - a_cum)
        new_state = state * jnp.exp(a_last)[..., None, None] + jnp.einsum('bsgn,bsgk,bsgkp->bgkpn', b_, w_in, x_)
        return new_state, y_diag + y_off

    state0 = jnp.zeros((bsz, g, kh, p, D_STATE), f32)
    _, y = lax.scan(step, state0, (xc, ac, bc, cc))
    return jnp.moveaxis(y, 0, 1).reshape(bsz, seq, SSM_HEADS, p)


def mamba2_mixer(h, in_proj, conv_w, conv_b, dt_bias, a_log, d_skip, gnorm_w, out_proj):
    bsz, seq, _ = h.shape
    f32 = jnp.float32
    zxbcdt = h @ in_proj
    z, xbc, dt_raw = jnp.split(zxbcdt, [D_INNER, D_INNER + D_XBC], axis=-1)
    xbc = jax.nn.silu(causal_depthwise_conv(xbc, conv_w, conv_b))
    xs, bm, cm = jnp.split(xbc, [D_INNER, D_INNER + D_BC], axis=-1)
    xs = xs.astype(f32).reshape(bsz, seq, SSM_HEADS, SSM_HEAD_DIM)
    bm = bm.reshape(bsz, seq, SSM_GROUPS, D_STATE)
    cm = cm.reshape(bsz, seq, SSM_GROUPS, D_STATE)
    dt = jax.nn.softplus(dt_raw.astype(f32) + dt_bias.astype(f32))
    a_neg = -jnp.exp(a_log.astype(f32))
    y = ssd_chunked_scan(xs * dt[..., None], dt * a_neg, bm, cm)
    y = y + xs * d_skip.astype(f32)[:, None]
    y = y.reshape(bsz, seq, D_INNER) * jax.nn.silu(z.astype(f32))
    yg = y.reshape(bsz, seq, SSM_GROUPS, D_INNER // SSM_GROUPS)
    yg = yg * lax.rsqrt(jnp.mean(yg * yg, axis=-1, keepdims=True) + EPS)
    y = (yg.reshape(bsz, seq, D_INNER) * gnorm_w.astype(f32)).astype(h.dtype)
    return y @ out_proj


def shared_kv(s, kv_norm_w, w_kvf, b_f, k_norm_w):
    bsz, seq, _ = s.shape
    kvf = rmsnorm(s, kv_norm_w) @ w_kvf
    k, v, f_logit = jnp.split(kvf, [D_ATT, 2 * D_ATT], axis=-1)
    k = rmsnorm(k.reshape(bsz, seq, ATT_HEADS, ATT_HEAD_DIM), k_norm_w)
    v = v.reshape(bsz, seq, ATT_HEADS, ATT_HEAD_DIM)
    log_f = jax.nn.log_sigmoid(f_logit.astype(jnp.float32) + b_f.astype(jnp.float32))
    cum = jnp.cumsum(log_f, axis=1)
    return k, v, cum


def forgetting_attention(h, k, v, cum, w_q, q_norm_w, w_o):
    bsz, seq, _ = h.shape
    q = rmsnorm((h @ w_q).reshape(bsz, seq, ATT_HEADS, ATT_HEAD_DIM), q_norm_w)
    scale = ATT_HEAD_DIM ** -0.5
    cum_h = jnp.swapaxes(cum, 1, 2)
    outs = []
    for blk in range(seq // Q_BLOCK):
        q0 = blk * Q_BLOCK
        kend = q0 + Q_BLOCK
        logits = jnp.einsum('bthd,bshd->bhts', q[:, q0:kend], k[:, :kend]).astype(jnp.float32) * scale
        logits = logits + (cum_h[:, :, q0:kend, None] - cum_h[:, :, None, :kend])
        mask = jnp.arange(kend)[None, :] <= (q0 + jnp.arange(Q_BLOCK))[:, None]
        logits = jnp.where(mask, logits, -jnp.inf)
        probs = jax.nn.softmax(logits, axis=-1).astype(v.dtype)
        outs.append(jnp.einsum('bhts,bshd->bthd', probs, v[:, :kend]))
    o = jnp.concatenate(outs, axis=1).reshape(bsz, seq, D_ATT)
    return o @ w_o


def swiglu(h, w_gate_up, w_down):
    g, u = jnp.split(h @ w_gate_up, 2, axis=-1)
    return (jax.nn.silu(g) * u) @ w_down


def _fwd_setup_inputs(seed: int = 0) -> dict:
    key = jax.random.key(seed)
    ks = jax.random.split(key, 24)
    f32 = jnp.float32

    def nrm(k, shape, scale):
        return jax.random.normal(k, shape, f32) * scale

    def gain(k, shape):
        return 1.0 + 0.02 * jax.random.normal(k, shape, f32)

    x = nrm(ks[0], (BATCH, SEQ, D_MODEL), 1.0)
    a_norm_w = gain(ks[1], (N_A_LAYERS, D_MODEL))
    a_in_proj = nrm(ks[2], (N_A_LAYERS, D_MODEL, D_IN_PROJ), D_MODEL ** -0.5)
    a_conv_w = nrm(ks[3], (N_A_LAYERS, CONV_WIDTH, D_XBC), CONV_WIDTH ** -0.5)
    a_conv_b = nrm(ks[4], (N_A_LAYERS, D_XBC), 0.02)
    dt0 = jnp.exp(jax.random.uniform(ks[5], (N_A_LAYERS, SSM_HEADS), f32, math.log(1e-3), math.log(1e-1)))
    a_dt_bias = dt0 + jnp.log(-jnp.expm1(-dt0))
    a_A_log = jnp.log(jax.random.uniform(ks[6], (N_A_LAYERS, SSM_HEADS), f32, 1.0, 16.0))
    a_D = 1.0 + 0.1 * jax.random.normal(ks[7], (N_A_LAYERS, SSM_HEADS), f32)
    a_gnorm_w = gain(ks[8], (N_A_LAYERS, D_INNER))
    a_out_proj = nrm(ks[9], (N_A_LAYERS, D_INNER, D_MODEL), D_INNER ** -0.5)
    kv_norm_w = gain(ks[10], (D_MODEL,))
    w_kvf = nrm(ks[11], (D_MODEL, 2 * D_ATT + ATT_HEADS), D_MODEL ** -0.5)
    b_f = jax.random.uniform(ks[12], (ATT_HEADS,), f32, 1.0, 4.0)
    k_norm_w = gain(ks[13], (ATT_HEAD_DIM,))
    b_norm_w = gain(ks[14], (N_B_LAYERS, D_MODEL))
    w_q = nrm(ks[15], (N_B_LAYERS, D_MODEL, D_ATT), D_MODEL ** -0.5)
    q_norm_w = gain(ks[16], (N_B_LAYERS, ATT_HEAD_DIM))
    w_o = nrm(ks[17], (N_B_LAYERS, D_ATT, D_MODEL), D_ATT ** -0.5)
    ffn_norm_w = gain(ks[18], (DEPTH, D_MODEL))
    w_gate_up = nrm(ks[19], (DEPTH, D_MODEL, 2 * D_FF), D_MODEL ** -0.5)
    w_down = nrm(ks[20], (DEPTH, D_FF, D_MODEL), D_FF ** -0.5)
    return {'x': x, 'a_norm_w': a_norm_w, 'a_in_proj': a_in_proj, 'a_conv_w': a_conv_w,
            'a_conv_b': a_conv_b, 'a_dt_bias': a_dt_bias, 'a_A_log': a_A_log, 'a_D': a_D,
            'a_gnorm_w': a_gnorm_w, 'a_out_proj': a_out_proj, 'kv_norm_w': kv_norm_w,
            'w_kvf': w_kvf, 'b_f': b_f, 'k_norm_w': k_norm_w, 'b_norm_w': b_norm_w,
            'w_q': w_q, 'q_norm_w': q_norm_w, 'w_o': w_o, 'ffn_norm_w': ffn_norm_w,
            'w_gate_up': w_gate_up, 'w_down': w_down}


def _fwd_reference(x, a_norm_w, a_in_proj, a_conv_w, a_conv_b, a_dt_bias, a_A_log, a_D,
              a_gnorm_w, a_out_proj, kv_norm_w, w_kvf, b_f, k_norm_w, b_norm_w,
              w_q, q_norm_w, w_o, ffn_norm_w, w_gate_up, w_down):
    h = x
    k_sh = v_sh = cum_sh = None
    for layer in range(DEPTH):
        if layer < N_A_LAYERS:
            i = layer
            h = h + mamba2_mixer(rmsnorm(h, a_norm_w[i]), a_in_proj[i], a_conv_w[i], a_conv_b[i],
                                 a_dt_bias[i], a_A_log[i], a_D[i], a_gnorm_w[i], a_out_proj[i])
        else:
            j = layer - N_A_LAYERS
            if j == 0:
                k_sh, v_sh, cum_sh = shared_kv(h, kv_norm_w, w_kvf, b_f, k_norm_w)
            h = h + forgetting_attention(rmsnorm(h, b_norm_w[j]), k_sh, v_sh, cum_sh,
                                         w_q[j], q_norm_w[j], w_o[j])
        h = h + swiglu(rmsnorm(h, ffn_norm_w[layer]), w_gate_up[layer], w_down[layer])
    return h


import jax as _jax
import jax.numpy as _jnp

TWIN_FORMAT = 'train_step'
FWD_PARAMS = ['x', 'a_norm_w', 'a_in_proj', 'a_conv_w', 'a_conv_b', 'a_dt_bias', 'a_A_log', 'a_D', 'a_gnorm_w', 'a_out_proj', 'kv_norm_w', 'w_kvf', 'b_f', 'k_norm_w', 'b_norm_w', 'w_q', 'q_norm_w', 'w_o', 'ffn_norm_w', 'w_gate_up', 'w_down']
TWIN_WEIGHTS = ['a_norm_w', 'a_in_proj', 'a_conv_w', 'a_conv_b', 'a_dt_bias', 'a_A_log', 'a_D', 'a_gnorm_w', 'a_out_proj', 'kv_norm_w', 'w_kvf', 'b_f', 'k_norm_w', 'b_norm_w', 'w_q', 'q_norm_w', 'w_o', 'ffn_norm_w', 'w_gate_up', 'w_down']
TWIN_DIFF_INPUT = 'x'
TWIN_INPUTS = ['x', 'a_norm_w', 'a_in_proj', 'a_conv_w', 'a_conv_b', 'a_dt_bias', 'a_A_log', 'a_D', 'a_gnorm_w', 'a_out_proj', 'kv_norm_w', 'w_kvf', 'b_f', 'k_norm_w', 'b_norm_w', 'w_q', 'q_norm_w', 'w_o', 'ffn_norm_w', 'w_gate_up', 'w_down', 'loss_target', 'm_a_norm_w', 'm_a_in_proj', 'm_a_conv_w', 'm_a_conv_b', 'm_a_dt_bias', 'm_a_A_log', 'm_a_D', 'm_a_gnorm_w', 'm_a_out_proj', 'm_kv_norm_w', 'm_w_kvf', 'm_b_f', 'm_k_norm_w', 'm_b_norm_w', 'm_w_q', 'm_q_norm_w', 'm_w_o', 'm_ffn_norm_w', 'm_w_gate_up', 'm_w_down', 'v_a_norm_w', 'v_a_in_proj', 'v_a_conv_w', 'v_a_conv_b', 'v_a_dt_bias', 'v_a_A_log', 'v_a_D', 'v_a_gnorm_w', 'v_a_out_proj', 'v_kv_norm_w', 'v_w_kvf', 'v_b_f', 'v_k_norm_w', 'v_b_norm_w', 'v_w_q', 'v_q_norm_w', 'v_w_o', 'v_ffn_norm_w', 'v_w_gate_up', 'v_w_down']
TWIN_OUTPUTS = ['loss', 'grad_x', 'grad_a_norm_w', 'grad_a_in_proj', 'grad_a_conv_w', 'grad_a_conv_b', 'grad_a_dt_bias', 'grad_a_A_log', 'grad_a_D', 'grad_a_gnorm_w', 'grad_a_out_proj', 'grad_kv_norm_w', 'grad_w_kvf', 'grad_b_f', 'grad_k_norm_w', 'grad_b_norm_w', 'grad_w_q', 'grad_q_norm_w', 'grad_w_o', 'grad_ffn_norm_w', 'grad_w_gate_up', 'grad_w_down', 'delta_a_norm_w', 'delta_a_in_proj', 'delta_a_conv_w', 'delta_a_conv_b', 'delta_a_dt_bias', 'delta_a_A_log', 'delta_a_D', 'delta_a_gnorm_w', 'delta_a_out_proj', 'delta_kv_norm_w', 'delta_w_kvf', 'delta_b_f', 'delta_k_norm_w', 'delta_b_norm_w', 'delta_w_q', 'delta_q_norm_w', 'delta_w_o', 'delta_ffn_norm_w', 'delta_w_gate_up', 'delta_w_down', 'new_m_a_norm_w', 'new_m_a_in_proj', 'new_m_a_conv_w', 'new_m_a_conv_b', 'new_m_a_dt_bias', 'new_m_a_A_log', 'new_m_a_D', 'new_m_a_gnorm_w', 'new_m_a_out_proj', 'new_m_kv_norm_w', 'new_m_w_kvf', 'new_m_b_f', 'new_m_k_norm_w', 'new_m_b_norm_w', 'new_m_w_q', 'new_m_q_norm_w', 'new_m_w_o', 'new_m_ffn_norm_w', 'new_m_w_gate_up', 'new_m_w_down', 'new_v_a_norm_w', 'new_v_a_in_proj', 'new_v_a_conv_w', 'new_v_a_conv_b', 'new_v_a_dt_bias', 'new_v_a_A_log', 'new_v_a_D', 'new_v_a_gnorm_w', 'new_v_a_out_proj', 'new_v_kv_norm_w', 'new_v_w_kvf', 'new_v_b_f', 'new_v_k_norm_w', 'new_v_b_norm_w', 'new_v_w_q', 'new_v_q_norm_w', 'new_v_w_o', 'new_v_ffn_norm_w', 'new_v_w_gate_up', 'new_v_w_down']
TWIN_LEAF_KINDS = {'loss': 'loss', 'grad_x': 'grad_x', 'grad_a_norm_w': 'grad_w', 'grad_a_in_proj': 'grad_w', 'grad_a_conv_w': 'grad_w', 'grad_a_conv_b': 'grad_w', 'grad_a_dt_bias': 'grad_w', 'grad_a_A_log': 'grad_w', 'grad_a_D': 'grad_w', 'grad_a_gnorm_w': 'grad_w', 'grad_a_out_proj': 'grad_w', 'grad_kv_norm_w': 'grad_w', 'grad_w_kvf': 'grad_w', 'grad_b_f': 'grad_w', 'grad_k_norm_w': 'grad_w', 'grad_b_norm_w': 'grad_w', 'grad_w_q': 'grad_w', 'grad_q_norm_w': 'grad_w', 'grad_w_o': 'grad_w', 'grad_ffn_norm_w': 'grad_w', 'grad_w_gate_up': 'grad_w', 'grad_w_down': 'grad_w', 'delta_a_norm_w': 'delta_w', 'delta_a_in_proj': 'delta_w', 'delta_a_conv_w': 'delta_w', 'delta_a_conv_b': 'delta_w', 'delta_a_dt_bias': 'delta_w', 'delta_a_A_log': 'delta_w', 'delta_a_D': 'delta_w', 'delta_a_gnorm_w': 'delta_w', 'delta_a_out_proj': 'delta_w', 'delta_kv_norm_w': 'delta_w', 'delta_w_kvf': 'delta_w', 'delta_b_f': 'delta_w', 'delta_k_norm_w': 'delta_w', 'delta_b_norm_w': 'delta_w', 'delta_w_q': 'delta_w', 'delta_q_norm_w': 'delta_w', 'delta_w_o': 'delta_w', 'delta_ffn_norm_w': 'delta_w', 'delta_w_gate_up': 'delta_w', 'delta_w_down': 'delta_w', 'new_m_a_norm_w': 'new_m', 'new_m_a_in_proj': 'new_m', 'new_m_a_conv_w': 'new_m', 'new_m_a_conv_b': 'new_m', 'new_m_a_dt_bias': 'new_m', 'new_m_a_A_log': 'new_m', 'new_m_a_D': 'new_m', 'new_m_a_gnorm_w': 'new_m', 'new_m_a_out_proj': 'new_m', 'new_m_kv_norm_w': 'new_m', 'new_m_w_kvf': 'new_m', 'new_m_b_f': 'new_m', 'new_m_k_norm_w': 'new_m', 'new_m_b_norm_w': 'new_m', 'new_m_w_q': 'new_m', 'new_m_q_norm_w': 'new_m', 'new_m_w_o': 'new_m', 'new_m_ffn_norm_w': 'new_m', 'new_m_w_gate_up': 'new_m', 'new_m_w_down': 'new_m', 'new_v_a_norm_w': 'new_v', 'new_v_a_in_proj': 'new_v', 'new_v_a_conv_w': 'new_v', 'new_v_a_conv_b': 'new_v', 'new_v_a_dt_bias': 'new_v', 'new_v_a_A_log': 'new_v', 'new_v_a_D': 'new_v', 'new_v_a_gnorm_w': 'new_v', 'new_v_a_out_proj': 'new_v', 'new_v_kv_norm_w': 'new_v', 'new_v_w_kvf': 'new_v', 'new_v_b_f': 'new_v', 'new_v_k_norm_w': 'new_v', 'new_v_b_norm_w': 'new_v', 'new_v_w_q': 'new_v', 'new_v_q_norm_w': 'new_v', 'new_v_w_o': 'new_v', 'new_v_ffn_norm_w': 'new_v', 'new_v_w_gate_up': 'new_v', 'new_v_w_down': 'new_v'}


def _forward(args):
    return _fwd_reference(*[args[k] for k in FWD_PARAMS])


def _output_shape():
    def fwd():
        inp = _fwd_setup_inputs(0)
        return _fwd_reference(*[inp[k] for k in FWD_PARAMS])
    out = _jax.eval_shape(fwd)
    return out.shape, out.dtype

N_MICROBATCH = 1
ADAM_LR = 0.001
ADAM_B1 = 0.9
ADAM_B2 = 0.999
ADAM_EPS = 1e-08
ADAM_WD = 0.01
ADAM_STEP = 10
PER_EXAMPLE_BATCH_AXIS = {'x': 0, 'loss_target': 0}
SHARED_INPUTS = []
_WEIGHT_DTYPES = {'a_norm_w': _jnp.float32, 'a_in_proj': _jnp.float32, 'a_conv_w': _jnp.float32, 'a_conv_b': _jnp.float32, 'a_dt_bias': _jnp.float32, 'a_A_log': _jnp.float32, 'a_D': _jnp.float32, 'a_gnorm_w': _jnp.float32, 'a_out_proj': _jnp.float32, 'kv_norm_w': _jnp.float32, 'w_kvf': _jnp.float32, 'b_f': _jnp.float32, 'k_norm_w': _jnp.float32, 'b_norm_w': _jnp.float32, 'w_q': _jnp.float32, 'q_norm_w': _jnp.float32, 'w_o': _jnp.float32, 'ffn_norm_w': _jnp.float32, 'w_gate_up': _jnp.float32, 'w_down': _jnp.float32}
MOMENT_SCALE = {'a_norm_w': 5.014761e-01, 'a_in_proj': 1.956325e-01, 'a_conv_w': 5.990326e-01, 'a_conv_b': 2.041323e+00, 'a_dt_bias': 1.020044e+00, 'a_A_log': 4.710966e+00, 'a_D': 4.164541e+00, 'a_gnorm_w': 1.126223e+01, 'a_out_proj': 1.508391e+00, 'kv_norm_w': 3.150191e+00, 'w_kvf': 8.355292e-01, 'b_f': 8.883344e+01, 'k_norm_w': 1.356883e+01, 'b_norm_w': 8.045569e-02, 'w_q': 8.469152e-02, 'q_norm_w': 1.358573e+01, 'w_o': 4.814923e-01, 'ffn_norm_w': 1.236325e+01, 'w_gate_up': 1.815482e-01, 'w_down': 2.262354e-01}


def _to_microbatches(a, axis):
    t = _jnp.moveaxis(a, axis, 0)
    t = t.reshape((N_MICROBATCH, t.shape[0] // N_MICROBATCH) + t.shape[1:])
    return _jnp.moveaxis(t, 1, axis + 1)


def setup_inputs(seed: int = 0) -> dict:
    inp = _fwd_setup_inputs(seed)
    key = _jax.random.fold_in(_jax.random.key(seed), 7919)
    shape, _ = _output_shape()
    out = dict(inp)
    out["loss_target"] = _jax.random.normal(_jax.random.fold_in(key, 0), shape, _jnp.float32)
    for i, name in enumerate(TWIN_WEIGHTS):
        w = inp[name].astype(_jnp.float32)
        if MOMENT_SCALE is None:
            s = _jnp.sqrt(_jnp.mean(_jnp.square(w)) + 1e-30)
        else:
            s = MOMENT_SCALE[name]
        km, kv = _jax.random.split(_jax.random.fold_in(key, i + 1))
        out[name] = w
        out["m_" + name] = s * _jax.random.normal(km, w.shape, _jnp.float32)
        out["v_" + name] = (s * s) * _jax.random.uniform(kv, w.shape, _jnp.float32, 0.5, 1.5)
    if N_MICROBATCH > 1:
        for name, axis in PER_EXAMPLE_BATCH_AXIS.items():
            out[name] = _to_microbatches(out[name], axis)
    return {'x': out['x'], 'a_norm_w': out['a_norm_w'], 'a_in_proj': out['a_in_proj'], 'a_conv_w': out['a_conv_w'], 'a_conv_b': out['a_conv_b'], 'a_dt_bias': out['a_dt_bias'], 'a_A_log': out['a_A_log'], 'a_D': out['a_D'], 'a_gnorm_w': out['a_gnorm_w'], 'a_out_proj': out['a_out_proj'], 'kv_norm_w': out['kv_norm_w'], 'w_kvf': out['w_kvf'], 'b_f': out['b_f'], 'k_norm_w': out['k_norm_w'], 'b_norm_w': out['b_norm_w'], 'w_q': out['w_q'], 'q_norm_w': out['q_norm_w'], 'w_o': out['w_o'], 'ffn_norm_w': out['ffn_norm_w'], 'w_gate_up': out['w_gate_up'], 'w_down': out['w_down'], 'loss_target': out['loss_target'], 'm_a_norm_w': out['m_a_norm_w'], 'm_a_in_proj': out['m_a_in_proj'], 'm_a_conv_w': out['m_a_conv_w'], 'm_a_conv_b': out['m_a_conv_b'], 'm_a_dt_bias': out['m_a_dt_bias'], 'm_a_A_log': out['m_a_A_log'], 'm_a_D': out['m_a_D'], 'm_a_gnorm_w': out['m_a_gnorm_w'], 'm_a_out_proj': out['m_a_out_proj'], 'm_kv_norm_w': out['m_kv_norm_w'], 'm_w_kvf': out['m_w_kvf'], 'm_b_f': out['m_b_f'], 'm_k_norm_w': out['m_k_norm_w'], 'm_b_norm_w': out['m_b_norm_w'], 'm_w_q': out['m_w_q'], 'm_q_norm_w': out['m_q_norm_w'], 'm_w_o': out['m_w_o'], 'm_ffn_norm_w': out['m_ffn_norm_w'], 'm_w_gate_up': out['m_w_gate_up'], 'm_w_down': out['m_w_down'], 'v_a_norm_w': out['v_a_norm_w'], 'v_a_in_proj': out['v_a_in_proj'], 'v_a_conv_w': out['v_a_conv_w'], 'v_a_conv_b': out['v_a_conv_b'], 'v_a_dt_bias': out['v_a_dt_bias'], 'v_a_A_log': out['v_a_A_log'], 'v_a_D': out['v_a_D'], 'v_a_gnorm_w': out['v_a_gnorm_w'], 'v_a_out_proj': out['v_a_out_proj'], 'v_kv_norm_w': out['v_kv_norm_w'], 'v_w_kvf': out['v_w_kvf'], 'v_b_f': out['v_b_f'], 'v_k_norm_w': out['v_k_norm_w'], 'v_b_norm_w': out['v_b_norm_w'], 'v_w_q': out['v_w_q'], 'v_q_norm_w': out['v_q_norm_w'], 'v_w_o': out['v_w_o'], 'v_ffn_norm_w': out['v_ffn_norm_w'], 'v_w_gate_up': out['v_w_gate_up'], 'v_w_down': out['v_w_down']}


def _loss(weights, diff, rest, loss_target):
    with _jax.named_scope("forward"):
        args = {**rest, TWIN_DIFF_INPUT: diff, **{k: w.astype(_WEIGHT_DTYPES[k]) for k, w in weights.items()}}
        y = _forward(args)
    with _jax.named_scope("loss_head"):
        err = _jnp.square(y.astype(_jnp.float32) - loss_target)
        return 0.5 * _jnp.sum(_jnp.mean(err, axis=-1)) if err.ndim else 0.5 * err


def _adamw(w, g, m, v):
    m = ADAM_B1 * m + (1.0 - ADAM_B1) * g
    v = ADAM_B2 * v + (1.0 - ADAM_B2) * _jnp.square(g)
    m_hat = m / (1.0 - ADAM_B1 ** ADAM_STEP)
    v_hat = v / (1.0 - ADAM_B2 ** ADAM_STEP)
    delta = -ADAM_LR * (m_hat / (_jnp.sqrt(v_hat) + ADAM_EPS) + ADAM_WD * w)
    return delta, m, v


def reference(x, a_norm_w, a_in_proj, a_conv_w, a_conv_b, a_dt_bias, a_A_log, a_D, a_gnorm_w, a_out_proj, kv_norm_w, w_kvf, b_f, k_norm_w, b_norm_w, w_q, q_norm_w, w_o, ffn_norm_w, w_gate_up, w_down, loss_target, m_a_norm_w, m_a_in_proj, m_a_conv_w, m_a_conv_b, m_a_dt_bias, m_a_A_log, m_a_D, m_a_gnorm_w, m_a_out_proj, m_kv_norm_w, m_w_kvf, m_b_f, m_k_norm_w, m_b_norm_w, m_w_q, m_q_norm_w, m_w_o, m_ffn_norm_w, m_w_gate_up, m_w_down, v_a_norm_w, v_a_in_proj, v_a_conv_w, v_a_conv_b, v_a_dt_bias, v_a_A_log, v_a_D, v_a_gnorm_w, v_a_out_proj, v_kv_norm_w, v_w_kvf, v_b_f, v_k_norm_w, v_b_norm_w, v_w_q, v_q_norm_w, v_w_o, v_ffn_norm_w, v_w_gate_up, v_w_down):
    given = dict(x=x, a_norm_w=a_norm_w, a_in_proj=a_in_proj, a_conv_w=a_conv_w, a_conv_b=a_conv_b, a_dt_bias=a_dt_bias, a_A_log=a_A_log, a_D=a_D, a_gnorm_w=a_gnorm_w, a_out_proj=a_out_proj, kv_norm_w=kv_norm_w, w_kvf=w_kvf, b_f=b_f, k_norm_w=k_norm_w, b_norm_w=b_norm_w, w_q=w_q, q_norm_w=q_norm_w, w_o=w_o, ffn_norm_w=ffn_norm_w, w_gate_up=w_gate_up, w_down=w_down, loss_target=loss_target, m_a_norm_w=m_a_norm_w, m_a_in_proj=m_a_in_proj, m_a_conv_w=m_a_conv_w, m_a_conv_b=m_a_conv_b, m_a_dt_bias=m_a_dt_bias, m_a_A_log=m_a_A_log, m_a_D=m_a_D, m_a_gnorm_w=m_a_gnorm_w, m_a_out_proj=m_a_out_proj, m_kv_norm_w=m_kv_norm_w, m_w_kvf=m_w_kvf, m_b_f=m_b_f, m_k_norm_w=m_k_norm_w, m_b_norm_w=m_b_norm_w, m_w_q=m_w_q, m_q_norm_w=m_q_norm_w, m_w_o=m_w_o, m_ffn_norm_w=m_ffn_norm_w, m_w_gate_up=m_w_gate_up, m_w_down=m_w_down, v_a_norm_w=v_a_norm_w, v_a_in_proj=v_a_in_proj, v_a_conv_w=v_a_conv_w, v_a_conv_b=v_a_conv_b, v_a_dt_bias=v_a_dt_bias, v_a_A_log=v_a_A_log, v_a_D=v_a_D, v_a_gnorm_w=v_a_gnorm_w, v_a_out_proj=v_a_out_proj, v_kv_norm_w=v_kv_norm_w, v_w_kvf=v_w_kvf, v_b_f=v_b_f, v_k_norm_w=v_k_norm_w, v_b_norm_w=v_b_norm_w, v_w_q=v_w_q, v_q_norm_w=v_q_norm_w, v_w_o=v_w_o, v_ffn_norm_w=v_ffn_norm_w, v_w_gate_up=v_w_gate_up, v_w_down=v_w_down)
    weights = {n: given[n] for n in TWIN_WEIGHTS}
    shared = {n: given[n] for n in SHARED_INPUTS}
    per_example = {n: given[n] for n in ['x']}
    grad_fn = _jax.value_and_grad(_loss, argnums=(0, 1))

    def one_microbatch(ex, loss_target):
        ex = dict(ex)
        diff = ex.pop(TWIN_DIFF_INPUT)
        return grad_fn(weights, diff, {**shared, **ex}, loss_target)

    if N_MICROBATCH == 1:
        loss, (grad_w, grad_x) = one_microbatch(per_example, given["loss_target"])
    else:
        def body(carry, xs):
            loss_sum, grad_sum = carry
            l_k, (gw_k, gx_k) = one_microbatch(xs[0], xs[1])
            with _jax.named_scope("update"):
                return (loss_sum + l_k, _jax.tree.map(_jnp.add, grad_sum, gw_k)), gx_k

        init = (_jnp.zeros((), _jnp.float32), _jax.tree.map(_jnp.zeros_like, weights))
        (loss, grad_w), grad_x = _jax.lax.scan(body, init, (per_example, given["loss_target"]))
    with _jax.named_scope("update"):
        delta_w, new_m, new_v = {}, {}, {}
        for n in TWIN_WEIGHTS:
            delta_w[n], new_m[n], new_v[n] = _adamw(weights[n], grad_w[n], given["m_" + n], given["v_" + n])
    return (loss, grad_x, *[grad_w[n] for n in TWIN_WEIGHTS], *[delta_w[n] for n in TWIN_WEIGHTS],
            *[new_m[n] for n in TWIN_WEIGHTS], *[new_v[n] for n in TWIN_WEIGHTS])
```

```python
import functools
import math

import jax
import jax.numpy as jnp
from jax import lax
from jax.experimental import pallas as pl
from jax.experimental.pallas import tpu as pltpu

F32, BF16 = jnp.float32, jnp.bfloat16
EPS = 1e-6
SSM_HEAD_DIM = 64
SSM_GROUPS = 8
SSD_CHUNK = 128
ATT_HEAD_DIM = 128
LANES = 128
N_CHIPS = 4
NEG = -1e30
ADAM_LR, ADAM_B1, ADAM_B2, ADAM_EPS, ADAM_WD, ADAM_STEP = 0.001, 0.9, 0.999, 1e-08, 0.01, 10
VMEM_LIMIT_BYTES = 56 * 1024 * 1024
PACK_COLS = 1024
PACK_ROWS = 256

NN = ((1,), (0,))
NT = ((1,), (1,))
TN = ((0,), (0,))

WEIGHTS = ['a_norm_w', 'a_in_proj', 'a_conv_w', 'a_conv_b', 'a_dt_bias', 'a_A_log', 'a_D', 'a_gnorm_w', 'a_out_proj',
           'kv_norm_w', 'w_kvf', 'b_f', 'k_norm_w', 'b_norm_w', 'w_q', 'q_norm_w', 'w_o', 'ffn_norm_w', 'w_gate_up',
           'w_down']
BIG = {'a_in_proj': 2, 'a_out_proj': 1, 'w_kvf': 1, 'w_q': 1, 'w_o': 1, 'w_gate_up': 2, 'w_down': 1}
SMALL_SHARDED = {'a_norm_w': 1, 'a_conv_w': 2, 'a_conv_b': 1, 'a_gnorm_w': 1}
SMALL = [n for n in WEIGHTS if n not in BIG]


def _dot(a, b, dims):
    return lax.dot_general(a, b, (dims, ((), ())), preferred_element_type=F32)


def _params(sem=None):
    return pltpu.CompilerParams(dimension_semantics=sem, vmem_limit_bytes=VMEM_LIMIT_BYTES)


def _tile(dim, cap):
    for t in (1024, 512, 256, 128):
        if t <= cap and dim % t == 0:
            return t
    return dim


def _mm(a, b, mode, name, out_dtype=F32, add=None):
    if mode == 'nn':
        (m, k), n = a.shape, b.shape[1]
    elif mode == 'nt':
        (m, k), n = a.shape, b.shape[0]
    else:
        (k, m), n = a.shape, b.shape[1]
    tm, tn, tk = _tile(m, 1024), _tile(n, 1024), _tile(k, 512)
    nk = k // tk
    a_spec = pl.BlockSpec((tk, tm), lambda i, j, q: (q, i)) if mode == 'tn' else pl.BlockSpec((tm, tk), lambda i, j, q: (i, q))
    b_spec = pl.BlockSpec((tn, tk), lambda i, j, q: (j, q)) if mode == 'nt' else pl.BlockSpec((tk, tn), lambda i, j, q: (q, j))
    o_spec = pl.BlockSpec((tm, tn), lambda i, j, q: (i, j))
    dims = {'nn': NN, 'nt': NT, 'tn': TN}[mode]

    def kern(*refs):
        a_ref, b_ref = refs[0], refs[1]
        o_ref, acc = refs[-2], refs[-1]
        q = pl.program_id(2)

        @pl.when(q == 0)
        def _():
            acc[...] = jnp.zeros_like(acc)

        acc[...] += _dot(a_ref[...].astype(BF16), b_ref[...].astype(BF16), dims)

        @pl.when(q == nk - 1)
        def _():
            r = acc[...]
            if add is not None:
                r = r + refs[2][...]
            o_ref[...] = r.astype(o_ref.dtype)

    ins, specs = [a, b], [a_spec, b_spec]
    if add is not None:
        ins.append(add)
        specs.append(o_spec)
    return pl.pallas_call(
        kern, name=name, grid=(m // tm, n // tn, nk), in_specs=specs, out_specs=o_spec,
        out_shape=jax.ShapeDtypeStruct((m, n), out_dtype), scratch_shapes=[pltpu.VMEM((tm, tn), F32)],
        compiler_params=_params(("parallel", "parallel", "arbitrary")))(*ins)


def _rowwise(fn, name, rows, bcast=(), outs=(), accs=(), tm=256):
    rows = [r if isinstance(r, tuple) else (r, r.shape[1], 0) for r in rows]
    n_rows = rows[0][0].shape[0]
    tm = min(tm, n_rows)
    assert n_rows % tm == 0, (name, n_rows, tm)
    n_in, n_out = len(rows) + len(bcast), len(outs)
    in_specs = [pl.BlockSpec((tm, w), functools.partial(lambda i, cb: (i, cb), cb=cb)) for _, w, cb in rows]
    in_specs += [pl.BlockSpec(b.shape, lambda i: (0, 0)) for b in bcast]
    out_specs = [pl.BlockSpec((tm, w), lambda i: (i, 0)) for w, _ in outs]
    out_specs += [pl.BlockSpec(s, lambda i: (0, 0)) for s in accs]
    out_shape = [jax.ShapeDtypeStruct((n_rows, w), d) for w, d in outs] + [jax.ShapeDtypeStruct(s, F32) for s in accs]

    def kern(*refs):
        vals = fn(*[r[...] for r in refs[:n_in]])
        vals = vals if isinstance(vals, (tuple, list)) else (vals,)
        o_refs = refs[n_in:]
        for r, v in zip(o_refs[:n_out], vals[:n_out]):
            r[...] = v.astype(r.dtype)
        if accs:
            @pl.when(pl.program_id(0) == 0)
            def _():
                for r in o_refs[n_out:]:
                    r[...] = jnp.zeros_like(r)

            for r, v in zip(o_refs[n_out:], vals[n_out:]):
                r[...] += v

    return pl.pallas_call(
        kern, name=name, grid=(n_rows // tm,), in_specs=in_specs, out_specs=out_specs, out_shape=out_shape,
        compiler_params=_params(("arbitrary",)))(*[r[0] for r in rows], *bcast)


def _rms(x, w):
    xf = x.astype(F32)
    return xf * lax.rsqrt(jnp.mean(xf * xf, axis=-1, keepdims=True) + EPS) * w


def _rms_fwd(x, w, name, tm=256):
    return _rowwise(_rms, name, [x], [w], outs=[(x.shape[1], BF16)], tm=tm)[0]


def _rms_bwd(x, w, dy, name, extra=None, out_dtype=F32, tm=256):
    def fn(x, dy, *rest):
        _, vjp = jax.vjp(_rms, x, rest[-1])
        dx, dw = vjp(dy.astype(F32))
        if extra is not None:
            dx = dx + rest[0]
        return dx, dw

    rows = [x, dy] + ([extra] if extra is not None else [])
    return _rowwise(fn, name, rows, [w], outs=[(x.shape[1], out_dtype)], accs=[w.shape], tm=tm)


def _sigmoid(x):
    return 1.0 / (1.0 + jnp.exp(-x))


def _softplus(x):
    return jnp.maximum(x, 0.0) + jnp.log(1.0 + jnp.exp(-jnp.abs(x)))


def _swiglu_fwd(gu, d_ff, name):
    def fn(g, u):
        return g * _sigmoid(g) * u

    return _rowwise(fn, name, [(gu, d_ff, 0), (gu, d_ff, 1)], outs=[(d_ff, BF16)], tm=128)[0]


def _swiglu_bwd(gu, dact, d_ff, name):
    def fn(g, u, da):
        s = _sigmoid(g)
        dg = da * u * s * (1.0 + g * (1.0 - s))
        du = da * g * s
        return jnp.concatenate([dg, du], axis=1)

    return _rowwise(fn, name, [(gu, d_ff, 0), (gu, d_ff, 1), dact], outs=[(2 * d_ff, BF16)], tm=128)[0]


def _cumsum_rows(v, reverse=False):
    n = v.shape[0]
    row = lax.broadcasted_iota(jnp.int32, v.shape, 0)
    sh = 1
    while sh < n:
        if reverse:
            v = v + jnp.where(row < n - sh, pltpu.roll(v, n - sh, 0), 0.0)
        else:
            v = v + jnp.where(row >= sh, pltpu.roll(v, sh, 0), 0.0)
        sh *= 2
    return v


def _conv_fwd(u, w, b, name):
    s, c = u.shape
    kw = w.shape[0]
    tc = _tile(c, 128)

    def kern(u_ref, w_ref, b_ref, o_ref):
        uu = u_ref[...]
        row = lax.broadcasted_iota(jnp.int32, uu.shape, 0)
        acc = jnp.zeros_like(uu) + b_ref[...]
        for k in range(kw):
            sh = kw - 1 - k
            uk = uu if sh == 0 else jnp.where(row >= sh, pltpu.roll(uu, sh, 0), 0.0)
            acc = acc + w_ref[pl.ds(k, 1), :] * uk
        o_ref[...] = acc * _sigmoid(acc)

    return pl.pallas_call(
        kern, name=name, grid=(c // tc,),
        in_specs=[pl.BlockSpec((s, tc), lambda j: (0, j)), pl.BlockSpec((kw, tc), lambda j: (0, j)),
                  pl.BlockSpec((1, tc), lambda j: (0, j))],
        out_specs=pl.BlockSpec((s, tc), lambda j: (0, j)), out_shape=jax.ShapeDtypeStruct((s, c), F32),
        compiler_params=_params(("parallel",)))(u, w, b)


def _conv_bwd(u, w, b, dact, name):
    s, c = u.shape
    kw = w.shape[0]
    tc = _tile(c, 128)

    def kern(u_ref, w_ref, b_ref, d_ref, du_ref, dw_ref, db_ref):
        uu = u_ref[...]
        row = lax.broadcasted_iota(jnp.int32, uu.shape, 0)
        shifted = []
        acc = jnp.zeros_like(uu) + b_ref[...]
        for k in range(kw):
            sh = kw - 1 - k
            uk = uu if sh == 0 else jnp.where(row >= sh, pltpu.roll(uu, sh, 0), 0.0)
            shifted.append(uk)
            acc = acc + w_ref[pl.ds(k, 1), :] * uk
        sg = _sigmoid(acc)
        dacc = d_ref[...] * sg * (1.0 + acc * (1.0 - sg))
        db_ref[...] = jnp.sum(dacc, axis=0, keepdims=True)
        du = jnp.zeros_like(uu)
        for k in range(kw):
            sh = kw - 1 - k
            dw_ref[pl.ds(k, 1), :] = jnp.sum(dacc * shifted[k], axis=0, keepdims=True)
            dk = dacc if sh == 0 else jnp.where(row < s - sh, pltpu.roll(dacc, s - sh, 0), 0.0)
            du = du + w_ref[pl.ds(k, 1), :] * dk
        du_ref[...] = du.astype(du_ref.dtype)

    col = lambda j: (0, j)
    return pl.pallas_call(
        kern, name=name, grid=(c // tc,),
        in_specs=[pl.BlockSpec((s, tc), col), pl.BlockSpec((kw, tc), col), pl.BlockSpec((1, tc), col),
                  pl.BlockSpec((s, tc), col)],
        out_specs=[pl.BlockSpec((s, tc), col), pl.BlockSpec((kw, tc), col), pl.BlockSpec((1, tc), col)],
        out_shape=[jax.ShapeDtypeStruct((s, c), BF16), jax.ShapeDtypeStruct((kw, c), F32),
                   jax.ShapeDtypeStruct((1, c), F32)],
        compiler_params=_params(("parallel",)))(u, w, b, dact)


def _dt_fwd(dtraw, bias, a_log, name):
    def fn(raw, bias, a_log):
        dt = _softplus(raw + bias)
        return dt, _cumsum_rows(dt * (-jnp.exp(a_log)))

    return _rowwise(fn, name, [dtraw], [bias, a_log], outs=[(LANES, F32), (LANES, F32)], tm=SSD_CHUNK)


def _dt_bwd(dtraw, bias, a_log, ddt, dacum, name):
    def fn(raw, ddt, dac, bias, a_log):
        z = raw + bias
        dt = _softplus(z)
        a_neg = -jnp.exp(a_log)
        da = _cumsum_rows(dac, reverse=True)
        draw = (ddt + da * a_neg) * _sigmoid(z)
        return draw, jnp.sum(draw, axis=0, keepdims=True), jnp.sum(da * dt, axis=0, keepdims=True) * a_neg

    return _rowwise(fn, name, [dtraw, ddt, dacum], [bias, a_log], outs=[(LANES, BF16)],
                    accs=[(1, LANES), (1, LANES)], tm=SSD_CHUNK)


def _ssd_pieces(xs, dt, ac, kh):
    l, gw = xs.shape
    lane_w = lax.broadcasted_iota(jnp.int32, (l, gw), 1)
    lane_k = lax.broadcasted_iota(jnp.int32, (l, LANES), 1)
    head = [(lane_w >= k * SSM_HEAD_DIM) & (lane_w < (k + 1) * SSM_HEAD_DIM) for k in range(kh)]

    def col(blk, k):
        return jnp.sum(jnp.where(lane_k == k, blk, 0.0), axis=1, keepdims=True)

    def expand(blk):
        acc = jnp.zeros((l, gw), F32)
        for k in range(kh):
            acc = jnp.where(head[k], col(blk, k), acc)
        return acc

    def collapse(wide):
        acc = jnp.zeros((l, LANES), F32)
        for k in range(kh):
            acc = jnp.where(lane_k == k, jnp.sum(jnp.where(head[k], wide, 0.0), axis=1, keepdims=True), acc)
        return acc

    return head, lane_k, col, expand, collapse


def _ssd_specs(l, gw, n, n_xs_blocks):
    g_axis = SSM_GROUPS
    return dict(
        xs=lambda cm: pl.BlockSpec((l, gw), lambda g, c: (cm(c), g)),
        b=lambda cm: pl.BlockSpec((l, n), lambda g, c: (cm(c), n_xs_blocks + g)),
        c=lambda cm: pl.BlockSpec((l, n), lambda g, c: (cm(c), n_xs_blocks + g_axis + g)),
        col=lambda cm: pl.BlockSpec((None, l, LANES), lambda g, c: (g, cm(c), 0)),
        row=lambda cm: pl.BlockSpec((None, 8, l), lambda g, c: (g, 0, cm(c))),
        state=lambda cm: pl.BlockSpec((None, None, n, gw), lambda g, c: (cm(c), g, 0, 0)),
    )


def _ssd_fwd(xbc, dt_g, ac_g, act_g, d_inner, n, kh, name):
    s = xbc.shape[0]
    l, g_n = SSD_CHUNK, SSM_GROUPS
    gw, nc = d_inner // g_n, s // l
    sp = _ssd_specs(l, gw, n, d_inner // n)
    fwd = lambda c: c

    def kern(xs_ref, b_ref, c_ref, dt_ref, ac_ref, act_ref, y_ref, s0_ref, st):
        @pl.when(pl.program_id(1) == 0)
        def _():
            st[...] = jnp.zeros_like(st)

        s0 = st[...]
        s0_ref[...] = s0
        xs, ac = xs_ref[...], ac_ref[...]
        head, _, col, expand, _ = _ssd_pieces(xs, dt_ref[...], ac, kh)
        ace = expand(ac)
        x = xs * expand(dt_ref[...])
        xb, bb, cb_ = x.astype(BF16), b_ref[...].astype(BF16), c_ref[...].astype(BF16)
        cb = _dot(cb_, bb, NT)
        ri = lax.broadcasted_iota(jnp.int32, (l, l), 0)
        ci = lax.broadcasted_iota(jnp.int32, (l, l), 1)
        causal = ri >= ci
        y = _dot(cb_, s0.astype(BF16), NN) * jnp.exp(ace)
        for k in range(kh):
            seg = col(ac, k) - act_ref[pl.ds(k, 1), :]
            m = jnp.where(causal, cb * jnp.exp(jnp.where(causal, seg, 0.0)), 0.0)
            y = jnp.where(head[k], y + _dot(m.astype(BF16), xb, NN), y)
        y_ref[...] = y
        row_w = lax.broadcasted_iota(jnp.int32, (l, gw), 0)
        alast = jnp.sum(jnp.where(row_w == l - 1, ace, 0.0), axis=0, keepdims=True)
        st[...] = s0 * jnp.exp(alast) + _dot(bb, (jnp.exp(alast - ace) * x).astype(BF16), TN)

    return pl.pallas_call(
        kern, name=name, grid=(g_n, nc),
        in_specs=[sp['xs'](fwd), sp['b'](fwd), sp['c'](fwd), sp['col'](fwd), sp['col'](fwd), sp['row'](fwd)],
        out_specs=[pl.BlockSpec((l, gw), lambda g, c: (c, g)), sp['state'](fwd)],
        out_shape=[jax.ShapeDtypeStruct((s, d_inner), F32), jax.ShapeDtypeStruct((nc, g_n, n, gw), F32)],
        scratch_shapes=[pltpu.VMEM((n, gw), F32)],
        compiler_params=_params(("arbitrary", "arbitrary")))(xbc, xbc, xbc, dt_g, ac_g, act_g)


def _ssd_bwd(xbc, dt_g, ac_g, act_g, s0_all, dy, dxs_skip, d_inner, n, kh, name):
    s = xbc.shape[0]
    l, g_n = SSD_CHUNK, SSM_GROUPS
    gw, nc = d_inner // g_n, s // l
    sp = _ssd_specs(l, gw, n, d_inner // n)
    rev = lambda c: nc - 1 - c

    def kern(xs_ref, b_ref, c_ref, dt_ref, ac_ref, act_ref, s0_ref, dy_ref, skip_ref,
             dxs_ref, db_ref, dc_ref, ddt_ref, dacc_ref, dacr_ref, dst):
        @pl.when(pl.program_id(1) == 0)
        def _():
            dst[...] = jnp.zeros_like(dst)

        dsn = dst[...]
        s0 = s0_ref[...]
        xs, ac, dy = xs_ref[...], ac_ref[...], dy_ref[...]
        head, lane_k, col, expand, collapse = _ssd_pieces(xs, dt_ref[...], ac, kh)
        ace, dte = expand(ac), expand(dt_ref[...])
        x = xs * dte
        xb, bb, cb_ = x.astype(BF16), b_ref[...].astype(BF16), c_ref[...].astype(BF16)
        s0b, dsnb, dyb = s0.astype(BF16), dsn.astype(BF16), dy.astype(BF16)
        cb = _dot(cb_, bb, NT)
        ri = lax.broadcasted_iota(jnp.int32, (l, l), 0)
        ci = lax.broadcasted_iota(jnp.int32, (l, l), 1)
        causal = ri >= ci
        row_w = lax.broadcasted_iota(jnp.int32, (l, gw), 0)
        e = jnp.exp(ace)
        alast = jnp.sum(jnp.where(row_w == l - 1, ace, 0.0), axis=0, keepdims=True)
        gdec = jnp.exp(alast)
        wt = jnp.exp(alast - ace)
        cs = _dot(cb_, s0b, NN)
        dcs = (dy * e).astype(BF16)
        d_c = _dot(dcs, s0b, NT)
        ds_off = _dot(cb_, dcs, TN)
        dace = dy * cs * e
        dalast = jnp.sum(dsn * s0, axis=0, keepdims=True) * gdec
        z = wt * x
        dz = _dot(bb, dsnb, NN)
        d_b = _dot(z.astype(BF16), dsnb, NT)
        dx = dz * wt
        t = dz * z
        dalast = dalast + jnp.sum(t, axis=0, keepdims=True)
        dace = dace - t
        dcb = jnp.zeros((l, l), F32)
        dac_col = jnp.zeros((l, LANES), F32)
        for k in range(kh):
            seg = col(ac, k) - act_ref[pl.ds(k, 1), :]
            dk = jnp.exp(jnp.where(causal, seg, 0.0))
            mk = jnp.where(causal, cb * dk, 0.0)
            dx = jnp.where(head[k], dx + _dot(mk.astype(BF16), dyb, TN), dx)
            dmk = _dot(jnp.where(head[k], dy, 0.0).astype(BF16), xb, NT)
            dcb = dcb + jnp.where(causal, dmk * dk, 0.0)
            dseg = dmk * mk
            dac_col = jnp.where(lane_k == k, jnp.sum(dseg, axis=1, keepdims=True), dac_col)
            dacr_ref[pl.ds(k, 1), :] = -jnp.sum(dseg, axis=0, keepdims=True)
        for k in range(kh, 8):
            dacr_ref[pl.ds(k, 1), :] = jnp.zeros((1, l), F32)
        dcbb = dcb.astype(BF16)
        dc_ref[...] = d_c + _dot(dcbb, bb, NN)
        db_ref[...] = d_b + _dot(dcbb, cb_, TN)
        dace = jnp.where(row_w == l - 1, dace + dalast, dace)
        dacc_ref[...] = dac_col + collapse(dace)
        ddt_ref[...] = collapse(dx * xs)
        dxs_ref[...] = dx * dte + skip_ref[...]
        dst[...] = dsn * gdec + ds_off

    return pl.pallas_call(
        kern, name=name, grid=(g_n, nc),
        in_specs=[sp['xs'](rev), sp['b'](rev), sp['c'](rev), sp['col'](rev), sp['col'](rev), sp['row'](rev),
                  sp['state'](rev), pl.BlockSpec((l, gw), lambda g, c: (rev(c), g)),
                  pl.BlockSpec((l, gw), lambda g, c: (rev(c), g))],
        out_specs=[pl.BlockSpec((l, gw), lambda g, c: (rev(c), g)), pl.BlockSpec((l, n), lambda g, c: (rev(c), g)),
                   pl.BlockSpec((l, n), lambda g, c: (rev(c), g)), sp['col'](rev), sp['col'](rev), sp['row'](rev)],
        out_shape=[jax.ShapeDtypeStruct((s, d_inner), F32), jax.ShapeDtypeStruct((s, g_n * n), F32),
                   jax.ShapeDtypeStruct((s, g_n * n), F32), jax.ShapeDtypeStruct((g_n, s, LANES), F32),
                   jax.ShapeDtypeStruct((g_n, s, LANES), F32), jax.ShapeDtypeStruct((g_n, 8, s), F32)],
        scratch_shapes=[pltpu.VMEM((n, gw), F32)],
        compiler_params=_params(("arbitrary", "arbitrary")))(xbc, xbc, xbc, dt_g, ac_g, act_g, s0_all, dy, dxs_skip)


def _gate(y, xs, z, d_exp, gw):
    t = (y + xs * d_exp) * (z * _sigmoid(z))
    width = t.shape[1]
    gsz = width // SSM_GROUPS
    lane = lax.broadcasted_iota(jnp.int32, t.shape, 1)
    t2 = t * t
    scale = jnp.zeros_like(t)
    for g in range(SSM_GROUPS):
        in_g = (lane >= g * gsz) & (lane < (g + 1) * gsz)
        ms = jnp.sum(jnp.where(in_g, t2, 0.0), axis=1, keepdims=True) * (1.0 / gsz)
        scale = jnp.where(in_g, lax.rsqrt(ms + EPS), scale)
    return t * scale * gw


def _gate_fwd(y, xbc, z, d_exp, gw, d_inner, name):
    return _rowwise(_gate, name, [y, (xbc, d_inner, 0), z], [d_exp, gw], outs=[(d_inner, BF16)], tm=128)[0]


def _gate_bwd(y, xbc, z, d_exp, gw, dyn, d_inner, name):
    def fn(y, xs, z, dyn, d_exp, gw):
        _, vjp = jax.vjp(_gate, y, xs, z, d_exp, gw)
        return vjp(dyn)

    return _rowwise(fn, name, [y, (xbc, d_inner, 0), z, dyn], [d_exp, gw],
                    outs=[(d_inner, F32), (d_inner, F32), (d_inner, BF16)], accs=[d_exp.shape, gw.shape], tm=64)


def _forget_fwd(fraw, b_f, name):
    def kern(f_ref, b_ref, o_ref):
        o_ref[...] = _cumsum_rows(-_softplus(-(f_ref[...] + b_ref[...])))

    return pl.pallas_call(kern, name=name, out_shape=jax.ShapeDtypeStruct(fraw.shape, F32),
                          compiler_params=_params())(fraw, b_f)


def _forget_bwd(fraw, b_f, dcum, name):
    def kern(f_ref, b_ref, d_ref, df_ref, db_ref):
        df = _cumsum_rows(d_ref[...], reverse=True) * _sigmoid(-(f_ref[...] + b_ref[...]))
        df_ref[...] = df.astype(df_ref.dtype)
        db_ref[...] = jnp.sum(df, axis=0, keepdims=True)

    return pl.pallas_call(
        kern, name=name,
        out_shape=[jax.ShapeDtypeStruct(fraw.shape, BF16), jax.ShapeDtypeStruct((1, fraw.shape[1]), F32)],
        compiler_params=_params())(fraw, b_f, dcum)


ATT_BLOCK = 256


def _attn_fwd(q, k, v, cq_rep, ck, name):
    s, hd = q.shape
    h_n, d = hd // ATT_HEAD_DIM, ATT_HEAD_DIM
    tb = min(ATT_BLOCK, s)
    nb = s // tb
    scale = d ** -0.5

    def kern(q_ref, k_ref, v_ref, cq_ref, ck_ref, o_ref, lse_ref):
        i = pl.program_id(1)
        qq = q_ref[...]
        cq = jnp.max(cq_ref[...], axis=1, keepdims=True)
        rowpos = i * tb + lax.broadcasted_iota(jnp.int32, (tb, tb), 0)
        coli = lax.broadcasted_iota(jnp.int32, (tb, tb), 1)

        def body(j, carry):
            m, l_, acc = carry
            ks = pl.ds(pl.multiple_of(j * tb, tb), tb)
            sc = _dot(qq, k_ref[ks, :], NT) * scale + (cq - ck_ref[j])
            sc = jnp.where(j * tb + coli <= rowpos, sc, NEG)
            mn = jnp.maximum(m, jnp.max(sc, axis=1, keepdims=True))
            p = jnp.exp(sc - mn)
            alpha = jnp.exp(m - mn)
            l_ = alpha * l_ + jnp.sum(p, axis=1, keepdims=True)
            acc = alpha * acc + _dot(p.astype(BF16), v_ref[ks, :], NN)
            return mn, l_, acc

        init = (jnp.full((tb, 1), NEG, F32), jnp.zeros((tb, 1), F32), jnp.zeros((tb, d), F32))
        m, l_, acc = lax.fori_loop(0, i + 1, body, init)
        o_ref[...] = (acc / l_).astype(o_ref.dtype)
        lse_ref[...] = jnp.broadcast_to(m + jnp.log(l_), (tb, d))

    return pl.pallas_call(
        kern, name=name, grid=(h_n, nb),
        in_specs=[pl.BlockSpec((tb, d), lambda h, i: (i, h)), pl.BlockSpec((s, d), lambda h, i: (0, h)),
                  pl.BlockSpec((s, d), lambda h, i: (0, h)), pl.BlockSpec((tb, d), lambda h, i: (i, h)),
                  pl.BlockSpec((None, nb, 1, tb), lambda h, i: (h, 0, 0, 0))],
        out_specs=[pl.BlockSpec((tb, d), lambda h, i: (i, h)), pl.BlockSpec((tb, d), lambda h, i: (i, h))],
        out_shape=[jax.ShapeDtypeStruct((s, hd), BF16), jax.ShapeDtypeStruct((s, hd), F32)],
        compiler_params=_params(("parallel", "arbitrary")))(q, k, v, cq_rep, ck)


def _attn_bwd(q, k, v, do, lse_rep, delta_rep, cq_rep, ck, name):
    s, hd = q.shape
    h_n, d = hd // ATT_HEAD_DIM, ATT_HEAD_DIM
    tb = min(ATT_BLOCK, s)
    nb = s // tb
    scale = d ** -0.5

    def kern(q_ref, do_ref, k_ref, v_ref, lse_ref, dl_ref, cq_ref, ck_ref, dq_ref, dk_ref, dv_ref, dcq_ref, dck_ref):
        j = pl.program_id(1)

        @pl.when(j == 0)
        def _():
            dq_ref[...] = jnp.zeros_like(dq_ref)
            dcq_ref[...] = jnp.zeros_like(dcq_ref)

        kj, vj, ckj = k_ref[...], v_ref[...], ck_ref[...]
        colpos = j * tb + lax.broadcasted_iota(jnp.int32, (tb, tb), 1)
        rowi = lax.broadcasted_iota(jnp.int32, (tb, tb), 0)

        def body(i, carry):
            dk, dv, dck = carry
            rs = pl.ds(pl.multiple_of(i * tb, tb), tb)
            qi, doi = q_ref[rs, :], do_ref[rs, :]
            lse = jnp.max(lse_ref[rs, :], axis=1, keepdims=True)
            dl = jnp.max(dl_ref[rs, :], axis=1, keepdims=True)
            cq = jnp.max(cq_ref[rs, :], axis=1, keepdims=True)
            sc = _dot(qi, kj, NT) * scale + (cq - ckj)
            p = jnp.exp(jnp.where(colpos <= i * tb + rowi, sc - lse, NEG))
            dp = _dot(doi, vj, NT)
            ds = p * (dp - dl)
            dsb = ds.astype(BF16)
            dv = dv + _dot(p.astype(BF16), doi, TN)
            dk = dk + _dot(dsb, qi, TN)
            dq_ref[rs, :] += _dot(dsb, kj, NN) * scale
            dcq_ref[rs, :] += jnp.broadcast_to(jnp.sum(ds, axis=1, keepdims=True), (tb, d))
            return dk, dv, dck - jnp.sum(ds, axis=0, keepdims=True)

        init = (jnp.zeros((tb, d), F32), jnp.zeros((tb, d), F32), jnp.zeros((1, tb), F32))
        dk, dv, dck = lax.fori_loop(j, nb, body, init)
        dk_ref[...] = dk * scale
        dv_ref[...] = dv
        dck_ref[...] = dck

    whole = pl.BlockSpec((s, d), lambda h, j: (0, h))
    blk = pl.BlockSpec((tb, d), lambda h, j: (j, h))
    ckb = pl.BlockSpec((None, None, 1, tb), lambda h, j: (h, j, 0, 0))
    return pl.pallas_call(
        kern, name=name, grid=(h_n, nb),
        in_specs=[whole, whole, blk, blk, whole, whole, whole, ckb],
        out_specs=[whole, blk, blk, whole, ckb],
        out_shape=[jax.ShapeDtypeStruct((s, hd), F32), jax.ShapeDtypeStruct((s, hd), F32),
                   jax.ShapeDtypeStruct((s, hd), F32), jax.ShapeDtypeStruct((s, hd), F32),
                   jax.ShapeDtypeStruct((h_n, nb, 1, tb), F32)],
        compiler_params=_params(("arbitrary", "arbitrary")))(q, do, k, v, lse_rep, delta_rep, cq_rep, ck)


def _adamw(w, g, m, v, name):
    def fn(w, g, m, v):
        m = ADAM_B1 * m + (1.0 - ADAM_B1) * g
        v = ADAM_B2 * v + (1.0 - ADAM_B2) * (g * g)
        m_hat = m / (1.0 - ADAM_B1 ** ADAM_STEP)
        v_hat = v / (1.0 - ADAM_B2 ** ADAM_STEP)
        return -ADAM_LR * (m_hat / (jnp.sqrt(v_hat) + ADAM_EPS) + ADAM_WD * w), m, v

    cols = w.shape[1]
    tm = _tile(w.shape[0], 128) if w.shape[0] % 128 == 0 else w.shape[0]
    return _rowwise(fn, name, [w, g, m, v], outs=[(cols, F32)] * 3, tm=tm)


def _sum_arrays(arrs, out_dtype, name):
    def fn(*xs):
        acc = xs[0].astype(F32)
        for x in xs[1:]:
            acc = acc + x.astype(F32)
        return acc

    tm = PACK_ROWS if arrs[0].shape[0] % PACK_ROWS == 0 else arrs[0].shape[0]
    return _rowwise(fn, name, list(arrs), outs=[(arrs[0].shape[1], out_dtype)], tm=tm)[0]


FLIP_C = [(0, 0, 1)]
FLIP_XY = [(1, 0, 0), (0, 1, 0), (1, 1, 0)]
FLIP_ALL = [(fx, fy, fc) for fx in (0, 1) for fy in (0, 1) for fc in (0, 1) if (fx, fy, fc) != (0, 0, 0)]


def _chip(dev):
    return 2 * dev[0] + dev[1]


def _device(dev):
    return 4 * dev[0] + 2 * dev[1] + dev[2]


def _exchange(src, rels, n_dst, name, src_slot, dst_slot, own=False):
    n_rel = len(rels)
    blk = src.shape[1:]

    def body(src_ref, dst_ref, send_sems, recv_sems, own_sem):
        me = (lax.axis_index("x"), lax.axis_index("y"), lax.axis_index("c"))
        peers = [tuple(1 - m if f else m for m, f in zip(me, rel)) for rel in rels]

        def copy(k, sender, receiver):
            return pltpu.make_async_remote_copy(
                src_ref=src_ref.at[src_slot(sender, receiver)], dst_ref=dst_ref.at[dst_slot(sender)],
                send_sem=send_sems.at[k], recv_sem=recv_sems.at[k], device_id=receiver,
                device_id_type=pl.DeviceIdType.MESH)

        sends = [copy(k, me, peer) for k, peer in enumerate(peers)]
        for cp in sends:
            cp.start()
        if own:
            mine = pltpu.make_async_copy(src_ref.at[src_slot(me, me)], dst_ref.at[dst_slot(me)], own_sem)
            mine.start()
        for k, peer in enumerate(peers):
            copy(k, peer, me).wait_recv()
        for cp in sends:
            cp.wait_send()
        if own:
            mine.wait()

    return pl.pallas_call(
        body, name=name, out_shape=jax.ShapeDtypeStruct((n_dst,) + blk, src.dtype),
        in_specs=[pl.BlockSpec(memory_space=pl.ANY)], out_specs=pl.BlockSpec(memory_space=pl.ANY),
        scratch_shapes=[pltpu.SemaphoreType.DMA((n_rel,)), pltpu.SemaphoreType.DMA((n_rel,)),
                        pltpu.SemaphoreType.DMA(())])(src)


def _pack(parts, dtype, multiple):
    flat = jnp.concatenate([p.reshape(-1).astype(dtype) for p in parts])
    pad = (-flat.shape[0]) % multiple
    return jnp.pad(flat, (0, pad)) if pad else flat


def _unpack_shards(packs, names, shapes, axes):
    out, off = {}, 0
    for nme in names:
        sz = math.prod(shapes[nme])
        out[nme] = jnp.concatenate([packs[j, off:off + sz].reshape(shapes[nme]) for j in range(N_CHIPS)], axis=axes[nme])
        off += sz
    return out


def _shards_of(full, axis):
    sz = full.shape[axis] // N_CHIPS
    return [lax.slice_in_dim(full, j * sz, (j + 1) * sz, axis=axis) for j in range(N_CHIPS)]


def kernel(x, a_norm_w, a_in_proj, a_conv_w, a_conv_b, a_dt_bias, a_A_log, a_D, a_gnorm_w, a_out_proj, kv_norm_w, w_kvf, b_f, k_norm_w, b_norm_w, w_q, q_norm_w, w_o, ffn_norm_w, w_gate_up, w_down, loss_target, m_a_norm_w, m_a_in_proj, m_a_conv_w, m_a_conv_b, m_a_dt_bias, m_a_A_log, m_a_D, m_a_gnorm_w, m_a_out_proj, m_kv_norm_w, m_w_kvf, m_b_f, m_k_norm_w, m_b_norm_w, m_w_q, m_q_norm_w, m_w_o, m_ffn_norm_w, m_w_gate_up, m_w_down, v_a_norm_w, v_a_in_proj, v_a_conv_w, v_a_conv_b, v_a_dt_bias, v_a_A_log, v_a_D, v_a_gnorm_w, v_a_out_proj, v_kv_norm_w, v_w_kvf, v_b_f, v_k_norm_w, v_b_norm_w, v_w_q, v_q_norm_w, v_w_o, v_ffn_norm_w, v_w_gate_up, v_w_down):
    args = locals()
    w = {n: args[n] for n in WEIGHTS}
    mom = {n: args['m_' + n] for n in WEIGHTS}
    var = {n: args['v_' + n] for n in WEIGHTS}
    shapes = {n: w[n].shape for n in WEIGHTS}

    cx, cy, cc = lax.axis_index("x"), lax.axis_index("y"), lax.axis_index("c")
    my_chip = 2 * cx + cy

    xs_in = x[0]
    target = loss_target[0]
    s_len, d_model = xs_in.shape
    n_heads_ssm = a_dt_bias.shape[-1]
    d_inner = n_heads_ssm * SSM_HEAD_DIM
    d_xbc = a_conv_w.shape[-1] * N_CHIPS
    d_state = (d_xbc - d_inner) // (2 * SSM_GROUPS)
    kh = n_heads_ssm // SSM_GROUPS
    n_att = b_f.shape[0]
    d_att = n_att * ATT_HEAD_DIM
    d_ff = w_down.shape[1] * N_CHIPS

    pack_unit = 2 * PACK_ROWS * PACK_COLS
    small_names = list(SMALL_SHARDED)
    sp = _pack([w[n] for n in small_names], F32, PACK_COLS).reshape(1, -1, PACK_COLS)
    sp_all = _exchange(sp, FLIP_XY, N_CHIPS, "gather_small", lambda a, b: 0, _chip, own=True)
    full = _unpack_shards(sp_all.reshape(N_CHIPS, -1), small_names, shapes, SMALL_SHARDED)

    big_names = list(BIG)
    bp = _pack([w[n] for n in big_names], BF16, pack_unit).reshape(2, -1, PACK_COLS)
    half_rows = bp.shape[1]
    my_half = lax.dynamic_index_in_dim(bp, cc, 0, keepdims=True)
    mine = _exchange(my_half, FLIP_XY, N_CHIPS, "gather_big_ici", lambda a, b: 0, _chip, own=True)
    theirs = _exchange(mine.reshape(1, N_CHIPS * half_rows, PACK_COLS), FLIP_C, 1, "gather_big_d2d",
                       lambda a, b: 0, lambda a: 0)
    both = jnp.stack([mine, theirs.reshape(N_CHIPS, half_rows, PACK_COLS)])
    halves = [lax.dynamic_index_in_dim(both, jnp.where(cc == h, 0, 1), 0, keepdims=False) for h in (0, 1)]
    bp_all = jnp.stack(halves, axis=1).reshape(N_CHIPS, -1)
    full.update(_unpack_shards(bp_all, big_names, shapes, BIG))
    for n in WEIGHTS:
        if n not in full:
            full[n] = w[n]

    def pad_cols(a, width=LANES):
        return jnp.pad(a, ((0, 0), (0, width - a.shape[1])))

    w_in = full['a_in_proj'][0]
    w_z, w_xbc, w_dt = w_in[:, :d_inner], w_in[:, d_inner:d_inner + d_xbc], pad_cols(w_in[:, d_inner + d_xbc:])
    w_out = full['a_out_proj'][0]
    w_k, w_v, w_f = full['w_kvf'][:, :d_att], full['w_kvf'][:, d_att:2 * d_att], pad_cols(full['w_kvf'][:, 2 * d_att:])
    w_qm, w_om = full['w_q'][0], full['w_o'][0]
    w_gu, w_dn = full['w_gate_up'], full['w_down']
    conv_w, conv_b = full['a_conv_w'][0], full['a_conv_b']
    norm_a, gnorm = full['a_norm_w'], full['a_gnorm_w']
    dt_bias, a_log = pad_cols(a_dt_bias), pad_cols(a_A_log)
    d_exp = jnp.repeat(a_D, SSM_HEAD_DIM, axis=1)
    kv_nw, b_nw = kv_norm_w.reshape(1, -1), b_norm_w
    k_nw, q_nw = k_norm_w.reshape(1, -1), q_norm_w
    bf_pad = pad_cols(b_f.reshape(1, -1))
    ffn_nw = [ffn_norm_w[i:i + 1] for i in range(2)]

    def to_groups(a):
        g = a[:, :n_heads_ssm].reshape(s_len, SSM_GROUPS, kh).transpose(1, 0, 2)
        return jnp.pad(g, ((0, 0), (0, 0), (0, LANES - kh)))

    def to_groups_t(a):
        g = a[:, :n_heads_ssm].reshape(s_len, SSM_GROUPS, kh).transpose(1, 2, 0)
        return jnp.pad(g, ((0, 0), (0, 8 - kh), (0, 0)))

    def from_groups(col, row=None):
        a = col[:, :, :kh].transpose(1, 0, 2).reshape(s_len, n_heads_ssm)
        if row is not None:
            a = a + row[:, :kh, :].transpose(2, 0, 1).reshape(s_len, n_heads_ssm)
        return pad_cols(a)

    def heads_rows(a):
        return a.reshape(-1, ATT_HEAD_DIM)

    n1 = _rms_fwd(xs_in, norm_a, "norm_a")
    z = _mm(n1, w_z, 'nn', "in_z")
    xbc_raw = _mm(n1, w_xbc, 'nn', "in_xbc")
    dtraw = _mm(n1, w_dt, 'nn', "in_dt")
    xbc = _conv_fwd(xbc_raw, conv_w, conv_b, "conv")
    dt, acum = _dt_fwd(dtraw, dt_bias, a_log, "dt")
    dt_g, ac_g, act_g = to_groups(dt), to_groups(acum), to_groups_t(acum)
    y_ssd, states = _ssd_fwd(xbc, dt_g, ac_g, act_g, d_inner, d_state, kh, "ssd")
    yn = _gate_fwd(y_ssd, xbc, z, d_exp, gnorm, d_inner, "gate")
    h1 = _mm(yn, w_out, 'nn', "out_proj", add=xs_in)

    def ffn_fwd(h, i):
        nrm = _rms_fwd(h, ffn_nw[i], f"ffn{i}_norm")
        gu = _mm(nrm, w_gu[i], 'nn', f"ffn{i}_up")
        act = _swiglu_fwd(gu, d_ff, f"ffn{i}_act")
        return nrm, gu, act, _mm(act, w_dn[i], 'nn', f"ffn{i}_down", add=h)

    n2, gu0, act0, h2 = ffn_fwd(h1, 0)
    nkv = _rms_fwd(h2, kv_nw, "kv_norm")
    k_raw = _mm(nkv, w_k, 'nn', "proj_k")
    v_att = _mm(nkv, w_v, 'nn', "proj_v", out_dtype=BF16)
    f_raw = _mm(nkv, w_f, 'nn', "proj_f")
    k_att = _rms_fwd(heads_rows(k_raw), k_nw, "k_norm", tm=1024).reshape(s_len, d_att)
    cum = _forget_fwd(f_raw, bf_pad, "forget")
    cq_rep = jnp.repeat(cum[:, :n_att], ATT_HEAD_DIM, axis=1)
    tb = min(ATT_BLOCK, s_len)
    ck = cum[:, :n_att].T.reshape(n_att, s_len // tb, 1, tb)
    n3 = _rms_fwd(h2, b_nw, "b_norm")
    q_raw = _mm(n3, w_qm, 'nn', "proj_q")
    q_att = _rms_fwd(heads_rows(q_raw), q_nw, "q_norm", tm=1024).reshape(s_len, d_att)
    o_att, lse_rep = _attn_fwd(q_att, k_att, v_att, cq_rep, ck, "attn")
    h3 = _mm(o_att, w_om, 'nn', "proj_o", add=h2)
    n4, gu1, act1, h4 = ffn_fwd(h3, 1)

    def loss_fn(h, t):
        err = h - t
        sq = jnp.sum(jnp.sum(err * err, axis=1, keepdims=True), axis=0, keepdims=True)
        return err * (1.0 / d_model), sq * (0.5 / d_model)

    dh4, loss_part = _rowwise(loss_fn, "loss", [h4, target], outs=[(d_model, F32)], accs=[(1, 1)])

    def ffn_bwd(dh, h, nrm, gu, act, i):
        dact = _mm(dh, w_dn[i], 'nt', f"ffn{i}_down_dx")
        g_dn = _mm(act, dh, 'tn', f"ffn{i}_down_dw", out_dtype=BF16)
        dgu = _swiglu_bwd(gu, dact, d_ff, f"ffn{i}_act_bwd")
        dn = _mm(dgu, w_gu[i], 'nt', f"ffn{i}_up_dx")
        g_gu = _mm(nrm, dgu, 'tn', f"ffn{i}_up_dw", out_dtype=BF16)
        dh_new, g_nw = _rms_bwd(h, ffn_nw[i], dn, f"ffn{i}_norm_bwd", extra=dh)
        return dh_new, g_dn, g_gu, g_nw

    dh3, g_dn1, g_gu1, g_fnw1 = ffn_bwd(dh4, h3, n4, gu1, act1, 1)
    do_att = _mm(dh3, w_om, 'nt', "proj_o_dx", out_dtype=BF16)
    g_wo = _mm(o_att, dh3, 'tn', "proj_o_dw", out_dtype=BF16)

    def delta_fn(a, b):
        return jnp.broadcast_to(jnp.sum(a.astype(F32) * b.astype(F32), axis=1, keepdims=True), a.shape)

    delta_rep = _rowwise(delta_fn, "attn_delta", [heads_rows(do_att), heads_rows(o_att)],
                         outs=[(ATT_HEAD_DIM, F32)], tm=1024)[0].reshape(s_len, d_att)
    dq_att, dk_att, dv_att, dcq_rep, dck = _attn_bwd(q_att, k_att, v_att, do_att, lse_rep, delta_rep, cq_rep, ck,
                                                     "attn_bwd")
    dq_raw, g_qnw = _rms_bwd(heads_rows(q_raw), q_nw, heads_rows(dq_att), "q_norm_bwd", out_dtype=BF16, tm=1024)
    dq_raw = dq_raw.reshape(s_len, d_att)
    dn3 = _mm(dq_raw, w_qm, 'nt', "proj_q_dx")
    g_wq = _mm(n3, dq_raw, 'tn', "proj_q_dw", out_dtype=BF16)
    dh2, g_bnw = _rms_bwd(h2, b_nw, dn3, "b_norm_bwd", extra=dh3)
    dk_raw, g_knw = _rms_bwd(heads_rows(k_raw), k_nw, heads_rows(dk_att), "k_norm_bwd", out_dtype=BF16, tm=1024)
    dk_raw = dk_raw.reshape(s_len, d_att)
    dcum = pad_cols(dcq_rep.reshape(s_len, n_att, ATT_HEAD_DIM)[:, :, 0] + dck.reshape(n_att, s_len).T)
    df_raw, g_bf = _forget_bwd(f_raw, bf_pad, dcum, "forget_bwd")
    dnkv = _mm(dk_raw, w_k, 'nt', "proj_k_dx")
    dnkv = _mm(dv_att, w_v, 'nt', "proj_v_dx", add=dnkv)
    dnkv = _mm(df_raw, w_f, 'nt', "proj_f_dx", add=dnkv)
    g_wk = _mm(nkv, dk_raw, 'tn', "proj_k_dw", out_dtype=BF16)
    g_wv = _mm(nkv, dv_att, 'tn', "proj_v_dw", out_dtype=BF16)
    g_wf = _mm(nkv, df_raw, 'tn', "proj_f_dw", out_dtype=BF16)
    dh2, g_kvnw = _rms_bwd(h2, kv_nw, dnkv, "kv_norm_bwd", extra=dh2)
    dh1, g_dn0, g_gu0, g_fnw0 = ffn_bwd(dh2, h1, n2, gu0, act0, 0)

    dyn = _mm(dh1, w_out, 'nt', "out_proj_dx")
    g_wout = _mm(yn, dh1, 'tn', "out_proj_dw", out_dtype=BF16)
    dy_ssd, dxs_skip, dz, g_dexp, g_gnorm = _gate_bwd(y_ssd, xbc, z, d_exp, gnorm, dyn, d_inner, "gate_bwd")
    dxs, d_b, d_c, ddt_g, dacc_g, dacr_g = _ssd_bwd(xbc, dt_g, ac_g, act_g, states, dy_ssd, dxs_skip, d_inner,
                                                    d_state, kh, "ssd_bwd")
    dxbc_act = jnp.concatenate([dxs, d_b, d_c], axis=1)
    du, g_convw, g_convb = _conv_bwd(xbc_raw, conv_w, conv_b, dxbc_act, "conv_bwd")
    draw, g_dtb, g_alog = _dt_bwd(dtraw, dt_bias, a_log, from_groups(ddt_g), from_groups(dacc_g, dacr_g), "dt_bwd")
    dn1 = _mm(dz, w_z, 'nt', "in_z_dx")
    dn1 = _mm(du, w_xbc, 'nt', "in_xbc_dx", add=dn1)
    dn1 = _mm(draw, w_dt, 'nt', "in_dt_dx", add=dn1)
    g_wz = _mm(n1, dz, 'tn', "in_z_dw", out_dtype=BF16)
    g_wxbc = _mm(n1, du, 'tn', "in_xbc_dw", out_dtype=BF16)
    g_wdt = _mm(n1, draw, 'tn', "in_dt_dw", out_dtype=BF16)
    dx, g_norm_a = _rms_bwd(xs_in, norm_a, dn1, "norm_a_bwd", extra=dh1)

    n_f = n_att
    g_big = {
        'a_in_proj': jnp.concatenate([g_wz, g_wxbc, g_wdt[:, :n_heads_ssm]], axis=1)[None],
        'a_out_proj': g_wout[None],
        'w_kvf': jnp.concatenate([g_wk, g_wv, g_wf[:, :n_f]], axis=1),
        'w_q': g_wq[None], 'w_o': g_wo[None],
        'w_gate_up': jnp.stack([g_gu0, g_gu1]), 'w_down': jnp.stack([g_dn0, g_dn1]),
    }
    g_small = {
        'a_norm_w': g_norm_a, 'a_conv_w': g_convw[None], 'a_conv_b': g_convb,
        'a_dt_bias': g_dtb[:, :n_heads_ssm], 'a_A_log': g_alog[:, :n_heads_ssm],
        'a_D': g_dexp.reshape(n_heads_ssm, SSM_HEAD_DIM).sum(axis=1).reshape(1, -1), 'a_gnorm_w': g_gnorm,
        'kv_norm_w': g_kvnw.reshape(-1), 'b_f': g_bf[0, :n_f], 'k_norm_w': g_knw.reshape(-1), 'b_norm_w': g_bnw,
        'q_norm_w': g_qnw, 'ffn_norm_w': jnp.concatenate([g_fnw0, g_fnw1], axis=0),
    }

    sg = _pack([g_small[n] for n in SMALL] + [loss_part], F32, 8 * PACK_COLS).reshape(1, -1, PACK_COLS)
    sg_all = _exchange(sg, FLIP_ALL, 2 * N_CHIPS, "reduce_small", lambda a, b: 0, _device, own=True)
    sg_sum = _sum_arrays([sg_all[d] for d in range(2 * N_CHIPS)], F32, "reduce_small_sum").reshape(-1)
    red_small, off = {}, 0
    for n in SMALL:
        shp = g_small[n].shape
        red_small[n] = sg_sum[off:off + math.prod(shp)].reshape(shp)
        off += math.prod(shp)
    loss = sg_sum[off]

    per_chip = [[] for _ in range(N_CHIPS)]
    for n in big_names:
        for j, piece in enumerate(_shards_of(g_big[n], BIG[n])):
            per_chip[j].append(piece)
    gp = jnp.stack([_pack(parts, BF16, pack_unit) for parts in per_chip]).reshape(N_CHIPS, 2, half_rows, PACK_COLS)
    keep = lax.dynamic_index_in_dim(gp, cc, 1, keepdims=False).reshape(N_CHIPS * half_rows, PACK_COLS)
    give = lax.dynamic_index_in_dim(gp, 1 - cc, 1, keepdims=False).reshape(1, N_CHIPS * half_rows, PACK_COLS)
    got = _exchange(give, FLIP_C, 1, "reduce_big_d2d", lambda a, b: 0, lambda a: 0)
    chip_sum = _sum_arrays([keep, got[0]], BF16, "reduce_big_sum2").reshape(N_CHIPS, half_rows, PACK_COLS)
    parts = _exchange(chip_sum, FLIP_XY, N_CHIPS, "reduce_big_ici", lambda a, b: _chip(b), _chip, own=True)
    half_sum = _sum_arrays([parts[j] for j in range(N_CHIPS)], F32, "reduce_big_sum4")
    other = _exchange(half_sum[None], FLIP_C, 1, "reduce_big_back", lambda a, b: 0, lambda a: 0)
    pair = jnp.stack([half_sum, other[0]])
    g_flat = jnp.concatenate([lax.dynamic_index_in_dim(pair, jnp.where(cc == h, 0, 1), 0, keepdims=False)
                              for h in (0, 1)]).reshape(-1)

    grads, off = {}, 0
    for n in big_names:
        sz = math.prod(shapes[n])
        grads[n] = g_flat[off:off + sz].reshape(shapes[n])
        off += sz
    for n in SMALL:
        if n in SMALL_SHARDED:
            ax = SMALL_SHARDED[n]
            grads[n] = lax.dynamic_slice_in_dim(red_small[n], my_chip * shapes[n][ax], shapes[n][ax], axis=ax)
        else:
            grads[n] = red_small[n]

    delta, new_m, new_v = {}, {}, {}
    for n in big_names:
        two_d = (-1, shapes[n][-1])
        d_, m_, v_ = _adamw(w[n].reshape(two_d), grads[n].reshape(two_d), mom[n].reshape(two_d),
                            var[n].reshape(two_d), "adamw_" + n)
        delta[n], new_m[n], new_v[n] = d_.reshape(shapes[n]), m_.reshape(shapes[n]), v_.reshape(shapes[n])
    packed = [_pack([src[n] for n in SMALL], F32, 8 * LANES).reshape(-1, LANES) for src in (w, grads, mom, var)]
    small_out = _adamw(*packed, "adamw_small")
    for store, flat in zip((delta, new_m, new_v), small_out):
        flat, off = flat.reshape(-1), 0
        for n in SMALL:
            sz = math.prod(shapes[n])
            store[n] = flat[off:off + sz].reshape(shapes[n])
            off += sz

    return (loss, dx[None], *[grads[n] for n in WEIGHTS], *[delta[n] for n in WEIGHTS],
            *[new_m[n] for n in WEIGHTS], *[new_v[n] for n in WEIGHTS])
```

```python
import functools
import math

import jax
import jax.numpy as jnp
from jax import lax
from jax.experimental import pallas as pl
from jax.experimental.pallas import tpu as pltpu

F32, BF16 = jnp.float32, jnp.bfloat16
EPS = 1e-6
SSM_HEAD_DIM = 64
SSM_GROUPS = 8
SSD_CHUNK = 128
ATT_HEAD_DIM = 128
LANES = 128
N_CHIPS = 4
NEG = -1e30
ADAM_LR, ADAM_B1, ADAM_B2, ADAM_EPS, ADAM_WD, ADAM_STEP = 0.001, 0.9, 0.999, 1e-08, 0.01, 10
VMEM_LIMIT_BYTES = 56 * 1024 * 1024
PACK_COLS = 1024
PACK_ROWS = 256

NN = ((1,), (0,))
NT = ((1,), (1,))
TN = ((0,), (0,))

WEIGHTS = ['a_norm_w', 'a_in_proj', 'a_conv_w', 'a_conv_b', 'a_dt_bias', 'a_A_log', 'a_D', 'a_gnorm_w', 'a_out_proj',
           'kv_norm_w', 'w_kvf', 'b_f', 'k_norm_w', 'b_norm_w', 'w_q', 'q_norm_w', 'w_o', 'ffn_norm_w', 'w_gate_up',
           'w_down']
BIG = {'a_in_proj': 2, 'a_out_proj': 1, 'w_kvf': 1, 'w_q': 1, 'w_o': 1, 'w_gate_up': 2, 'w_down': 1}
SMALL_SHARDED = {'a_norm_w': 1, 'a_conv_w': 2, 'a_conv_b': 1, 'a_gnorm_w': 1}
SMALL = [n for n in WEIGHTS if n not in BIG]


def _dot(a, b, dims):
    return lax.dot_general(a, b, (dims, ((), ())), preferred_element_type=F32)


def _params(sem=None):
    return pltpu.CompilerParams(dimension_semantics=sem, vmem_limit_bytes=VMEM_LIMIT_BYTES)


def _tile(dim, cap):
    for t in (1408, 1024, 512, 256, 128):
        if t <= cap and dim % t == 0:
            return t
    return dim


def _mm(a, b, mode, name, out_dtype=F32, add=None, shards=False):
    if mode == 'nn':
        (m, k), n = a.shape, (b.shape[2] * N_CHIPS if shards else b.shape[1])
    elif mode == 'nt':
        (m, k), n = a.shape, (b.shape[1] if shards else b.shape[0])
    else:
        (k, m), n = a.shape, b.shape[1]
    per_chip = (k if mode == 'nt' else n) // N_CHIPS
    tm = _tile(m, 1024)
    tn = _tile(per_chip if shards and mode != 'nt' else n, 1408 if shards else 1024)
    tk = _tile(per_chip, 1408) if shards and mode == 'nt' else _tile(k, 512)
    nk = k // tk
    a_spec = pl.BlockSpec((tk, tm), lambda i, j, q: (q, i)) if mode == 'tn' else pl.BlockSpec((tm, tk), lambda i, j, q: (i, q))
    b_spec = pl.BlockSpec((tn, tk), lambda i, j, q: (j, q)) if mode == 'nt' else pl.BlockSpec((tk, tn), lambda i, j, q: (q, j))
    o_spec = pl.BlockSpec((tm, tn), lambda i, j, q: (i, j))
    out_struct = jax.ShapeDtypeStruct((m, n), out_dtype)
    if shards:
        per = per_chip // (tk if mode == 'nt' else tn)
        if mode == 'nn':
            b_spec = pl.BlockSpec((None, tk, tn), lambda i, j, q: (j // per, q, j % per))
        elif mode == 'nt':
            b_spec = pl.BlockSpec((None, tn, tk), lambda i, j, q: (q // per, j, q % per))
        else:
            out_struct = jax.ShapeDtypeStruct((N_CHIPS, m, per_chip), out_dtype)
    out_spec = pl.BlockSpec((None, tm, tn), lambda i, j, q: (j // per, i, j % per)) if shards and mode == 'tn' else o_spec
    dims = {'nn': NN, 'nt': NT, 'tn': TN}[mode]

    def kern(*refs):
        a_ref, b_ref = refs[0], refs[1]
        o_ref, acc = refs[-2], refs[-1]
        q = pl.program_id(2)

        @pl.when(q == 0)
        def _():
            acc[...] = jnp.zeros_like(acc)

        acc[...] += _dot(a_ref[...].astype(BF16), b_ref[...].astype(BF16), dims)

        @pl.when(q == nk - 1)
        def _():
            r = acc[...]
            if add is not None:
                r = r + refs[2][...]
            o_ref[...] = r.astype(o_ref.dtype)

    ins, specs = [a, b], [a_spec, b_spec]
    if add is not None:
        ins.append(add)
        specs.append(o_spec)
    return pl.pallas_call(
        kern, name=name, grid=(m // tm, n // tn, nk), in_specs=specs, out_specs=out_spec,
        out_shape=out_struct, scratch_shapes=[pltpu.VMEM((tm, tn), F32)],
        compiler_params=_params(("parallel", "parallel", "arbitrary")))(*ins)


def _rowwise(fn, name, rows, bcast=(), outs=(), accs=(), tm=256):
    rows = [r if isinstance(r, tuple) else (r, r.shape[1], 0) for r in rows]
    n_rows = rows[0][0].shape[0]
    tm = min(tm, n_rows)
    assert n_rows % tm == 0, (name, n_rows, tm)
    n_in, n_out = len(rows) + len(bcast), len(outs)
    in_specs = [pl.BlockSpec((tm, w), functools.partial(lambda i, cb: (i, cb), cb=cb)) for _, w, cb in rows]
    in_specs += [pl.BlockSpec(b.shape, lambda i: (0, 0)) for b in bcast]
    out_specs = [pl.BlockSpec((tm, w), lambda i: (i, 0)) for w, _ in outs]
    out_specs += [pl.BlockSpec(s, lambda i: (0, 0)) for s in accs]
    out_shape = [jax.ShapeDtypeStruct((n_rows, w), d) for w, d in outs] + [jax.ShapeDtypeStruct(s, F32) for s in accs]

    def kern(*refs):
        vals = fn(*[r[...] for r in refs[:n_in]])
        vals = vals if isinstance(vals, (tuple, list)) else (vals,)
        o_refs = refs[n_in:]
        for r, v in zip(o_refs[:n_out], vals[:n_out]):
            r[...] = v.astype(r.dtype)
        if accs:
            @pl.when(pl.program_id(0) == 0)
            def _():
                for r in o_refs[n_out:]:
                    r[...] = jnp.zeros_like(r)

            for r, v in zip(o_refs[n_out:], vals[n_out:]):
                r[...] += v

    return pl.pallas_call(
        kern, name=name, grid=(n_rows // tm,), in_specs=in_specs, out_specs=out_specs, out_shape=out_shape,
        compiler_params=_params(("arbitrary",)))(*[r[0] for r in rows], *bcast)


def _rms(x, w):
    xf = x.astype(F32)
    return xf * lax.rsqrt(jnp.mean(xf * xf, axis=-1, keepdims=True) + EPS) * w


def _rms_fwd(x, w, name, tm=256):
    return _rowwise(_rms, name, [x], [w], outs=[(x.shape[1], BF16)], tm=tm)[0]


def _rms_bwd(x, w, dy, name, extra=None, out_dtype=F32, tm=256):
    def fn(x, dy, *rest):
        _, vjp = jax.vjp(_rms, x, rest[-1])
        dx, dw = vjp(dy.astype(F32))
        if extra is not None:
            dx = dx + rest[0]
        return dx, dw

    rows = [x, dy] + ([extra] if extra is not None else [])
    return _rowwise(fn, name, rows, [w], outs=[(x.shape[1], out_dtype)], accs=[w.shape], tm=tm)


def _sigmoid(x):
    return 1.0 / (1.0 + jnp.exp(-x))


def _softplus(x):
    return jnp.maximum(x, 0.0) + jnp.log(1.0 + jnp.exp(-jnp.abs(x)))


def _swiglu_fwd(gu, d_ff, name):
    def fn(g, u):
        return g * _sigmoid(g) * u

    return _rowwise(fn, name, [(gu, d_ff, 0), (gu, d_ff, 1)], outs=[(d_ff, BF16)], tm=128)[0]


def _swiglu_bwd(gu, dact, d_ff, name):
    def fn(g, u, da):
        s = _sigmoid(g)
        dg = da * u * s * (1.0 + g * (1.0 - s))
        du = da * g * s
        return jnp.concatenate([dg, du], axis=1)

    return _rowwise(fn, name, [(gu, d_ff, 0), (gu, d_ff, 1), dact], outs=[(2 * d_ff, BF16)], tm=128)[0]


def _cumsum_rows(v, reverse=False):
    n = v.shape[0]
    row = lax.broadcasted_iota(jnp.int32, v.shape, 0)
    sh = 1
    while sh < n:
        if reverse:
            v = v + jnp.where(row < n - sh, pltpu.roll(v, n - sh, 0), 0.0)
        else:
            v = v + jnp.where(row >= sh, pltpu.roll(v, sh, 0), 0.0)
        sh *= 2
    return v


def _conv_fwd(u, w, b, name):
    s, c = u.shape
    kw = w.shape[0]
    tc = _tile(c, 128)

    def kern(u_ref, w_ref, b_ref, o_ref):
        uu = u_ref[...]
        row = lax.broadcasted_iota(jnp.int32, uu.shape, 0)
        acc = jnp.zeros_like(uu) + b_ref[...]
        for k in range(kw):
            sh = kw - 1 - k
            uk = uu if sh == 0 else jnp.where(row >= sh, pltpu.roll(uu, sh, 0), 0.0)
            acc = acc + w_ref[pl.ds(k, 1), :] * uk
        o_ref[...] = acc * _sigmoid(acc)

    return pl.pallas_call(
        kern, name=name, grid=(c // tc,),
        in_specs=[pl.BlockSpec((s, tc), lambda j: (0, j)), pl.BlockSpec((kw, tc), lambda j: (0, j)),
                  pl.BlockSpec((1, tc), lambda j: (0, j))],
        out_specs=pl.BlockSpec((s, tc), lambda j: (0, j)), out_shape=jax.ShapeDtypeStruct((s, c), F32),
        compiler_params=_params(("parallel",)))(u, w, b)


def _conv_bwd(u, w, b, dact, name):
    s, c = u.shape
    kw = w.shape[0]
    tc = _tile(c, 128)

    def kern(u_ref, w_ref, b_ref, d_ref, du_ref, dw_ref, db_ref):
        uu = u_ref[...]
        row = lax.broadcasted_iota(jnp.int32, uu.shape, 0)
        shifted = []
        acc = jnp.zeros_like(uu) + b_ref[...]
        for k in range(kw):
            sh = kw - 1 - k
            uk = uu if sh == 0 else jnp.where(row >= sh, pltpu.roll(uu, sh, 0), 0.0)
            shifted.append(uk)
            acc = acc + w_ref[pl.ds(k, 1), :] * uk
        sg = _sigmoid(acc)
        dacc = d_ref[...] * sg * (1.0 + acc * (1.0 - sg))
        db_ref[...] = jnp.sum(dacc, axis=0, keepdims=True)
        du = jnp.zeros_like(uu)
        for k in range(kw):
            sh = kw - 1 - k
            dw_ref[pl.ds(k, 1), :] = jnp.sum(dacc * shifted[k], axis=0, keepdims=True)
            dk = dacc if sh == 0 else jnp.where(row < s - sh, pltpu.roll(dacc, s - sh, 0), 0.0)
            du = du + w_ref[pl.ds(k, 1), :] * dk
        du_ref[...] = du.astype(du_ref.dtype)

    col = lambda j: (0, j)
    return pl.pallas_call(
        kern, name=name, grid=(c // tc,),
        in_specs=[pl.BlockSpec((s, tc), col), pl.BlockSpec((kw, tc), col), pl.BlockSpec((1, tc), col),
                  pl.BlockSpec((s, tc), col)],
        out_specs=[pl.BlockSpec((s, tc), col), pl.BlockSpec((kw, tc), col), pl.BlockSpec((1, tc), col)],
        out_shape=[jax.ShapeDtypeStruct((s, c), BF16), jax.ShapeDtypeStruct((kw, c), F32),
                   jax.ShapeDtypeStruct((1, c), F32)],
        compiler_params=_params(("parallel",)))(u, w, b, dact)


def _dt_fwd(dtraw, bias, a_log, name):
    def fn(raw, bias, a_log):
        dt = _softplus(raw + bias)
        return dt, _cumsum_rows(dt * (-jnp.exp(a_log)))

    return _rowwise(fn, name, [dtraw], [bias, a_log], outs=[(LANES, F32), (LANES, F32)], tm=SSD_CHUNK)


def _dt_bwd(dtraw, bias, a_log, ddt, dacum, name):
    def fn(raw, ddt, dac, bias, a_log):
        z = raw + bias
        dt = _softplus(z)
        a_neg = -jnp.exp(a_log)
        da = _cumsum_rows(dac, reverse=True)
        draw = (ddt + da * a_neg) * _sigmoid(z)
        return draw, jnp.sum(draw, axis=0, keepdims=True), jnp.sum(da * dt, axis=0, keepdims=True) * a_neg

    return _rowwise(fn, name, [dtraw, ddt, dacum], [bias, a_log], outs=[(LANES, BF16)],
                    accs=[(1, LANES), (1, LANES)], tm=SSD_CHUNK)


def _ssd_pieces(xs, dt, ac, kh):
    l, gw = xs.shape
    lane_w = lax.broadcasted_iota(jnp.int32, (l, gw), 1)
    lane_k = lax.broadcasted_iota(jnp.int32, (l, LANES), 1)
    head = [(lane_w >= k * SSM_HEAD_DIM) & (lane_w < (k + 1) * SSM_HEAD_DIM) for k in range(kh)]

    def col(blk, k):
        return jnp.sum(jnp.where(lane_k == k, blk, 0.0), axis=1, keepdims=True)

    def expand(blk):
        acc = jnp.zeros((l, gw), F32)
        for k in range(kh):
            acc = jnp.where(head[k], col(blk, k), acc)
        return acc

    def collapse(wide):
        acc = jnp.zeros((l, LANES), F32)
        for k in range(kh):
            acc = jnp.where(lane_k == k, jnp.sum(jnp.where(head[k], wide, 0.0), axis=1, keepdims=True), acc)
        return acc

    return head, lane_k, col, expand, collapse


def _ssd_specs(l, gw, n, n_xs_blocks):
    g_axis = SSM_GROUPS
    return dict(
        xs=lambda cm: pl.BlockSpec((l, gw), lambda g, c: (cm(c), g)),
        b=lambda cm: pl.BlockSpec((l, n), lambda g, c: (cm(c), n_xs_blocks + g)),
        c=lambda cm: pl.BlockSpec((l, n), lambda g, c: (cm(c), n_xs_blocks + g_axis + g)),
        col=lambda cm: pl.BlockSpec((None, l, LANES), lambda g, c: (g, cm(c), 0)),
        row=lambda cm: pl.BlockSpec((None, 8, l), lambda g, c: (g, 0, cm(c))),
        state=lambda cm: pl.BlockSpec((None, None, n, gw), lambda g, c: (cm(c), g, 0, 0)),
    )


def _ssd_fwd(xbc, dt_g, ac_g, act_g, d_inner, n, kh, name):
    s = xbc.shape[0]
    l, g_n = SSD_CHUNK, SSM_GROUPS
    gw, nc = d_inner // g_n, s // l
    sp = _ssd_specs(l, gw, n, d_inner // n)
    fwd = lambda c: c

    def kern(xs_ref, b_ref, c_ref, dt_ref, ac_ref, act_ref, y_ref, s0_ref, st):
        @pl.when(pl.program_id(1) == 0)
        def _():
            st[...] = jnp.zeros_like(st)

        s0 = st[...]
        s0_ref[...] = s0
        xs, ac = xs_ref[...], ac_ref[...]
        head, _, col, expand, _ = _ssd_pieces(xs, dt_ref[...], ac, kh)
        ace = expand(ac)
        x = xs * expand(dt_ref[...])
        xb, bb, cb_ = x.astype(BF16), b_ref[...].astype(BF16), c_ref[...].astype(BF16)
        cb = _dot(cb_, bb, NT)
        ri = lax.broadcasted_iota(jnp.int32, (l, l), 0)
        ci = lax.broadcasted_iota(jnp.int32, (l, l), 1)
        causal = ri >= ci
        y = _dot(cb_, s0.astype(BF16), NN) * jnp.exp(ace)
        for k in range(kh):
            seg = col(ac, k) - act_ref[pl.ds(k, 1), :]
            m = jnp.where(causal, cb * jnp.exp(jnp.where(causal, seg, 0.0)), 0.0)
            y = jnp.where(head[k], y + _dot(m.astype(BF16), xb, NN), y)
        y_ref[...] = y
        row_w = lax.broadcasted_iota(jnp.int32, (l, gw), 0)
        alast = jnp.sum(jnp.where(row_w == l - 1, ace, 0.0), axis=0, keepdims=True)
        st[...] = s0 * jnp.exp(alast) + _dot(bb, (jnp.exp(alast - ace) * x).astype(BF16), TN)

    return pl.pallas_call(
        kern, name=name, grid=(g_n, nc),
        in_specs=[sp['xs'](fwd), sp['b'](fwd), sp['c'](fwd), sp['col'](fwd), sp['col'](fwd), sp['row'](fwd)],
        out_specs=[pl.BlockSpec((l, gw), lambda g, c: (c, g)), sp['state'](fwd)],
        out_shape=[jax.ShapeDtypeStruct((s, d_inner), F32), jax.ShapeDtypeStruct((nc, g_n, n, gw), F32)],
        scratch_shapes=[pltpu.VMEM((n, gw), F32)],
        compiler_params=_params(("arbitrary", "arbitrary")))(xbc, xbc, xbc, dt_g, ac_g, act_g)


def _ssd_bwd(xbc, dt_g, ac_g, act_g, s0_all, dy, dxs_skip, d_inner, n, kh, name):
    s = xbc.shape[0]
    l, g_n = SSD_CHUNK, SSM_GROUPS
    gw, nc = d_inner // g_n, s // l
    sp = _ssd_specs(l, gw, n, d_inner // n)
    rev = lambda c: nc - 1 - c

    def kern(xs_ref, b_ref, c_ref, dt_ref, ac_ref, act_ref, s0_ref, dy_ref, skip_ref,
             dxs_ref, db_ref, dc_ref, ddt_ref, dacc_ref, dacr_ref, dst):
        @pl.when(pl.program_id(1) == 0)
        def _():
            dst[...] = jnp.zeros_like(dst)

        dsn = dst[...]
        s0 = s0_ref[...]
        xs, ac, dy = xs_ref[...], ac_ref[...], dy_ref[...]
        head, lane_k, col, expand, collapse = _ssd_pieces(xs, dt_ref[...], ac, kh)
        ace, dte = expand(ac), expand(dt_ref[...])
        x = xs * dte
        xb, bb, cb_ = x.astype(BF16), b_ref[...].astype(BF16), c_ref[...].astype(BF16)
        s0b, dsnb, dyb = s0.astype(BF16), dsn.astype(BF16), dy.astype(BF16)
        cb = _dot(cb_, bb, NT)
        ri = lax.broadcasted_iota(jnp.int32, (l, l), 0)
        ci = lax.broadcasted_iota(jnp.int32, (l, l), 1)
        causal = ri >= ci
        row_w = lax.broadcasted_iota(jnp.int32, (l, gw), 0)
        e = jnp.exp(ace)
        alast = jnp.sum(jnp.where(row_w == l - 1, ace, 0.0), axis=0, keepdims=True)
        gdec = jnp.exp(alast)
        wt = jnp.exp(alast - ace)
        cs = _dot(cb_, s0b, NN)
        dcs = (dy * e).astype(BF16)
        d_c = _dot(dcs, s0b, NT)
        ds_off = _dot(cb_, dcs, TN)
        dace = dy * cs * e
        dalast = jnp.sum(dsn * s0, axis=0, keepdims=True) * gdec
        z = wt * x
        dz = _dot(bb, dsnb, NN)
        d_b = _dot(z.astype(BF16), dsnb, NT)
        dx = dz * wt
        t = dz * z
        dalast = dalast + jnp.sum(t, axis=0, keepdims=True)
        dace = dace - t
        dcb = jnp.zeros((l, l), F32)
        dac_col = jnp.zeros((l, LANES), F32)
        for k in range(kh):
            seg = col(ac, k) - act_ref[pl.ds(k, 1), :]
            dk = jnp.exp(jnp.where(causal, seg, 0.0))
            mk = jnp.where(causal, cb * dk, 0.0)
            dx = jnp.where(head[k], dx + _dot(mk.astype(BF16), dyb, TN), dx)
            dmk = _dot(jnp.where(head[k], dy, 0.0).astype(BF16), xb, NT)
            dcb = dcb + jnp.where(causal, dmk * dk, 0.0)
            dseg = dmk * mk
            dac_col = jnp.where(lane_k == k, jnp.sum(dseg, axis=1, keepdims=True), dac_col)
            dacr_ref[pl.ds(k, 1), :] = -jnp.sum(dseg, axis=0, keepdims=True)
        for k in range(kh, 8):
            dacr_ref[pl.ds(k, 1), :] = jnp.zeros((1, l), F32)
        dcbb = dcb.astype(BF16)
        dc_ref[...] = d_c + _dot(dcbb, bb, NN)
        db_ref[...] = d_b + _dot(dcbb, cb_, TN)
        dace = jnp.where(row_w == l - 1, dace + dalast, dace)
        dacc_ref[...] = dac_col + collapse(dace)
        ddt_ref[...] = collapse(dx * xs)
        dxs_ref[...] = dx * dte + skip_ref[...]
        dst[...] = dsn * gdec + ds_off

    return pl.pallas_call(
        kern, name=name, grid=(g_n, nc),
        in_specs=[sp['xs'](rev), sp['b'](rev), sp['c'](rev), sp['col'](rev), sp['col'](rev), sp['row'](rev),
                  sp['state'](rev), pl.BlockSpec((l, gw), lambda g, c: (rev(c), g)),
                  pl.BlockSpec((l, gw), lambda g, c: (rev(c), g))],
        out_specs=[pl.BlockSpec((l, gw), lambda g, c: (rev(c), g)), pl.BlockSpec((l, n), lambda g, c: (rev(c), g)),
                   pl.BlockSpec((l, n), lambda g, c: (rev(c), g)), sp['col'](rev), sp['col'](rev), sp['row'](rev)],
        out_shape=[jax.ShapeDtypeStruct((s, d_inner), F32), jax.ShapeDtypeStruct((s, g_n * n), F32),
                   jax.ShapeDtypeStruct((s, g_n * n), F32), jax.ShapeDtypeStruct((g_n, s, LANES), F32),
                   jax.ShapeDtypeStruct((g_n, s, LANES), F32), jax.ShapeDtypeStruct((g_n, 8, s), F32)],
        scratch_shapes=[pltpu.VMEM((n, gw), F32)],
        compiler_params=_params(("arbitrary", "arbitrary")))(xbc, xbc, xbc, dt_g, ac_g, act_g, s0_all, dy, dxs_skip)


def _gate(y, xs, z, d_exp, gw):
    t = (y + xs * d_exp) * (z * _sigmoid(z))
    width = t.shape[1]
    gsz = width // SSM_GROUPS
    lane = lax.broadcasted_iota(jnp.int32, t.shape, 1)
    t2 = t * t
    scale = jnp.zeros_like(t)
    for g in range(SSM_GROUPS):
        in_g = (lane >= g * gsz) & (lane < (g + 1) * gsz)
        ms = jnp.sum(jnp.where(in_g, t2, 0.0), axis=1, keepdims=True) * (1.0 / gsz)
        scale = jnp.where(in_g, lax.rsqrt(ms + EPS), scale)
    return t * scale * gw


def _gate_fwd(y, xbc, z, d_exp, gw, d_inner, name):
    return _rowwise(_gate, name, [y, (xbc, d_inner, 0), z], [d_exp, gw], outs=[(d_inner, BF16)], tm=128)[0]


def _gate_bwd(y, xbc, z, d_exp, gw, dyn, d_inner, name):
    def fn(y, xs, z, dyn, d_exp, gw):
        _, vjp = jax.vjp(_gate, y, xs, z, d_exp, gw)
        return vjp(dyn)

    return _rowwise(fn, name, [y, (xbc, d_inner, 0), z, dyn], [d_exp, gw],
                    outs=[(d_inner, F32), (d_inner, F32), (d_inner, BF16)], accs=[d_exp.shape, gw.shape], tm=64)


def _forget_fwd(fraw, b_f, name):
    def kern(f_ref, b_ref, o_ref):
        o_ref[...] = _cumsum_rows(-_softplus(-(f_ref[...] + b_ref[...])))

    return pl.pallas_call(kern, name=name, out_shape=jax.ShapeDtypeStruct(fraw.shape, F32),
                          compiler_params=_params())(fraw, b_f)


def _forget_bwd(fraw, b_f, dcum, name):
    def kern(f_ref, b_ref, d_ref, df_ref, db_ref):
        df = _cumsum_rows(d_ref[...], reverse=True) * _sigmoid(-(f_ref[...] + b_ref[...]))
        df_ref[...] = df.astype(df_ref.dtype)
        db_ref[...] = jnp.sum(df, axis=0, keepdims=True)

    return pl.pallas_call(
        kern, name=name,
        out_shape=[jax.ShapeDtypeStruct(fraw.shape, BF16), jax.ShapeDtypeStruct((1, fraw.shape[1]), F32)],
        compiler_params=_params())(fraw, b_f, dcum)


ATT_BLOCK = 512


def _attn_fwd(q, k, v, cq_rep, ck, name):
    s, hd = q.shape
    h_n, d = hd // ATT_HEAD_DIM, ATT_HEAD_DIM
    tb = min(ATT_BLOCK, s)
    nb = s // tb
    scale = d ** -0.5

    def kern(q_ref, k_ref, v_ref, cq_ref, ck_ref, o_ref, lse_ref):
        i = pl.program_id(1)
        qq = q_ref[...]
        cq = jnp.max(cq_ref[...], axis=1, keepdims=True)
        rowpos = i * tb + lax.broadcasted_iota(jnp.int32, (tb, tb), 0)
        coli = lax.broadcasted_iota(jnp.int32, (tb, tb), 1)

        def step(j, carry, diagonal):
            m, l_, acc = carry
            ks = pl.ds(pl.multiple_of(j * tb, tb), tb)
            sc = _dot(qq, k_ref[ks, :], NT) * scale + (cq - ck_ref[j])
            if diagonal:
                sc = jnp.where(j * tb + coli <= rowpos, sc, NEG)
            mn = jnp.maximum(m, jnp.max(sc, axis=1, keepdims=True))
            p = jnp.exp(sc - mn)
            alpha = jnp.exp(m - mn)
            l_ = alpha * l_ + jnp.sum(p, axis=1, keepdims=True)
            acc = alpha * acc + _dot(p.astype(BF16), v_ref[ks, :], NN)
            return mn, l_, acc

        init = (jnp.full((tb, 1), NEG, F32), jnp.zeros((tb, 1), F32), jnp.zeros((tb, d), F32))
        below = lax.fori_loop(0, i, lambda j, carry: step(j, carry, False), init)
        m, l_, acc = step(i, below, True)
        o_ref[...] = (acc / l_).astype(o_ref.dtype)
        lse_ref[...] = jnp.broadcast_to(m + jnp.log(l_), (tb, d))

    return pl.pallas_call(
        kern, name=name, grid=(h_n, nb),
        in_specs=[pl.BlockSpec((tb, d), lambda h, i: (i, h)), pl.BlockSpec((s, d), lambda h, i: (0, h)),
                  pl.BlockSpec((s, d), lambda h, i: (0, h)), pl.BlockSpec((tb, d), lambda h, i: (i, h)),
                  pl.BlockSpec((None, nb, 1, tb), lambda h, i: (h, 0, 0, 0))],
        out_specs=[pl.BlockSpec((tb, d), lambda h, i: (i, h)), pl.BlockSpec((tb, d), lambda h, i: (i, h))],
        out_shape=[jax.ShapeDtypeStruct((s, hd), BF16), jax.ShapeDtypeStruct((s, hd), F32)],
        compiler_params=_params(("parallel", "arbitrary")))(q, k, v, cq_rep, ck)


def _attn_bwd(q, k, v, do, lse_rep, delta_rep, cq_rep, ck, name):
    s, hd = q.shape
    h_n, d = hd // ATT_HEAD_DIM, ATT_HEAD_DIM
    tb = min(ATT_BLOCK, s)
    nb = s // tb
    scale = d ** -0.5

    def kern(q_ref, do_ref, k_ref, v_ref, lse_ref, dl_ref, cq_ref, ck_ref, dq_ref, dk_ref, dv_ref, dcq_ref, dck_ref):
        j = pl.program_id(1)

        @pl.when(j == 0)
        def _():
            dq_ref[...] = jnp.zeros_like(dq_ref)
            dcq_ref[...] = jnp.zeros_like(dcq_ref)

        kj, vj, ckj = k_ref[...], v_ref[...], ck_ref[...]
        colpos = j * tb + lax.broadcasted_iota(jnp.int32, (tb, tb), 1)
        rowi = lax.broadcasted_iota(jnp.int32, (tb, tb), 0)

        def step(i, carry, diagonal):
            dk, dv, dck = carry
            rs = pl.ds(pl.multiple_of(i * tb, tb), tb)
            qi, doi = q_ref[rs, :], do_ref[rs, :]
            lse = jnp.max(lse_ref[rs, :], axis=1, keepdims=True)
            dl = jnp.max(dl_ref[rs, :], axis=1, keepdims=True)
            cq = jnp.max(cq_ref[rs, :], axis=1, keepdims=True)
            sc = _dot(qi, kj, NT) * scale + (cq - ckj) - lse
            p = jnp.exp(jnp.where(colpos <= i * tb + rowi, sc, NEG) if diagonal else sc)
            dp = _dot(doi, vj, NT)
            ds = p * (dp - dl)
            dsb = ds.astype(BF16)
            dv = dv + _dot(p.astype(BF16), doi, TN)
            dk = dk + _dot(dsb, qi, TN)
            dq_ref[rs, :] += _dot(dsb, kj, NN) * scale
            dcq_ref[rs, :] += jnp.broadcast_to(jnp.sum(ds, axis=1, keepdims=True), (tb, d))
            return dk, dv, dck - jnp.sum(ds, axis=0, keepdims=True)

        init = (jnp.zeros((tb, d), F32), jnp.zeros((tb, d), F32), jnp.zeros((1, tb), F32))
        dk, dv, dck = lax.fori_loop(j + 1, nb, lambda i, carry: step(i, carry, False), step(j, init, True))
        dk_ref[...] = dk * scale
        dv_ref[...] = dv
        dck_ref[...] = dck

    whole = pl.BlockSpec((s, d), lambda h, j: (0, h))
    blk = pl.BlockSpec((tb, d), lambda h, j: (j, h))
    ckb = pl.BlockSpec((None, None, 1, tb), lambda h, j: (h, j, 0, 0))
    return pl.pallas_call(
        kern, name=name, grid=(h_n, nb),
        in_specs=[whole, whole, blk, blk, whole, whole, whole, ckb],
        out_specs=[whole, blk, blk, whole, ckb],
        out_shape=[jax.ShapeDtypeStruct((s, hd), F32), jax.ShapeDtypeStruct((s, hd), F32),
                   jax.ShapeDtypeStruct((s, hd), F32), jax.ShapeDtypeStruct((s, hd), F32),
                   jax.ShapeDtypeStruct((h_n, nb, 1, tb), F32)],
        compiler_params=_params(("arbitrary", "arbitrary")))(q, do, k, v, lse_rep, delta_rep, cq_rep, ck)


def _adamw(w, g, m, v, name):
    def fn(w, g, m, v):
        m = ADAM_B1 * m + (1.0 - ADAM_B1) * g
        v = ADAM_B2 * v + (1.0 - ADAM_B2) * (g * g)
        m_hat = m / (1.0 - ADAM_B1 ** ADAM_STEP)
        v_hat = v / (1.0 - ADAM_B2 ** ADAM_STEP)
        return -ADAM_LR * (m_hat / (jnp.sqrt(v_hat) + ADAM_EPS) + ADAM_WD * w), m, v

    cols = w.shape[1]
    tm = _tile(w.shape[0], 128) if w.shape[0] % 128 == 0 else w.shape[0]
    return _rowwise(fn, name, [w, g, m, v], outs=[(cols, F32)] * 3, tm=tm)


def _sum_arrays(arrs, out_dtype, name):
    def fn(*xs):
        acc = xs[0].astype(F32)
        for x in xs[1:]:
            acc = acc + x.astype(F32)
        return acc

    tm = PACK_ROWS if arrs[0].shape[0] % PACK_ROWS == 0 else arrs[0].shape[0]
    return _rowwise(fn, name, list(arrs), outs=[(arrs[0].shape[1], out_dtype)], tm=tm)[0]


def _half_tile(rows):
    for t in (256, 176, 128, 64, 32, 16):
        if rows % t == 0:
            return t
    return rows


def _sum_half(g, got, core, name):
    n_chip, hr, cols = got.shape
    tm = _half_tile(hr)
    nt = hr // tm

    def kern(c_ref, g_ref, a_ref, o_ref):
        o_ref[...] = (g_ref[...].astype(F32) + a_ref[...].astype(F32)).astype(o_ref.dtype)

    grid_spec = pltpu.PrefetchScalarGridSpec(
        num_scalar_prefetch=1, grid=(n_chip * nt,),
        in_specs=[pl.BlockSpec((tm, cols), lambda r, c: (((r // nt) * 2 + c[0]) * nt + r % nt, 0)),
                  pl.BlockSpec((tm, cols), lambda r, c: (r, 0))],
        out_specs=pl.BlockSpec((tm, cols), lambda r, c: (r, 0)))
    out = pl.pallas_call(
        kern, name=name, grid_spec=grid_spec, out_shape=jax.ShapeDtypeStruct((n_chip * hr, cols), BF16),
        compiler_params=_params(("arbitrary",)))(core.reshape(1), g.reshape(-1, cols), got.reshape(-1, cols))
    return out.reshape(n_chip, hr, cols)


def _sum_lead(parts, name):
    n_chip, rows, cols = parts.shape
    tm = _half_tile(rows)

    def kern(p_ref, o_ref):
        acc = p_ref[0].astype(F32)
        for j in range(1, n_chip):
            acc = acc + p_ref[j].astype(F32)
        o_ref[...] = acc

    return pl.pallas_call(
        kern, name=name, grid=(rows // tm,), in_specs=[pl.BlockSpec((n_chip, tm, cols), lambda r: (0, r, 0))],
        out_specs=pl.BlockSpec((tm, cols), lambda r: (r, 0)), out_shape=jax.ShapeDtypeStruct((rows, cols), F32),
        compiler_params=_params(("parallel",)))(parts)


FLIP_C = [(0, 0, 1)]
FLIP_XY = [(1, 0, 0), (0, 1, 0), (1, 1, 0)]
FLIP_ALL = [(fx, fy, fc) for fx in (0, 1) for fy in (0, 1) for fc in (0, 1) if (fx, fy, fc) != (0, 0, 0)]


def _chip(dev):
    return 2 * dev[0] + dev[1]


def _device(dev):
    return 4 * dev[0] + 2 * dev[1] + dev[2]


def _exchange(name, srcs, dst_shapes, rels, src_view, dst_view, own_view=None):
    n, n_rel = len(srcs), len(rels)

    def body(*refs):
        src_refs, dst_refs = refs[:n], refs[n:2 * n]
        send_sems, recv_sems, own_sems = refs[2 * n:]
        me = (lax.axis_index("x"), lax.axis_index("y"), lax.axis_index("c"))
        peers = [tuple(1 - m if f else m for m, f in zip(me, rel)) for rel in rels]

        def copy(i, k, sender, receiver):
            return pltpu.make_async_remote_copy(
                src_ref=src_view(i, src_refs[i], sender, receiver), dst_ref=dst_view(i, dst_refs[i], sender),
                send_sem=send_sems.at[i * n_rel + k], recv_sem=recv_sems.at[i * n_rel + k], device_id=receiver,
                device_id_type=pl.DeviceIdType.MESH)

        sends = [copy(i, k, me, peer) for i in range(n) for k, peer in enumerate(peers)]
        for cp in sends:
            cp.start()
        mine = []
        if own_view is not None:
            for i in range(n):
                frm, to = own_view(i, src_refs[i], dst_refs[i], me)
                mine.append(pltpu.make_async_copy(frm, to, own_sems.at[i]))
                mine[-1].start()
        for i in range(n):
            for k, peer in enumerate(peers):
                copy(i, k, peer, me).wait_recv()
        for cp in sends:
            cp.wait_send()
        for cp in mine:
            cp.wait()

    any_spec = pl.BlockSpec(memory_space=pl.ANY)
    return pl.pallas_call(
        body, name=name, out_shape=[jax.ShapeDtypeStruct(s, d) for s, d in dst_shapes],
        in_specs=[any_spec] * n, out_specs=[any_spec] * n,
        scratch_shapes=[pltpu.SemaphoreType.DMA((n * n_rel,)), pltpu.SemaphoreType.DMA((n * n_rel,)),
                        pltpu.SemaphoreType.DMA((n,))])(*srcs)


def _gather_weights(srcs, name):
    n = len(srcs)
    halves = [s.shape[0] // 2 for s in srcs]

    def body(*refs):
        src_refs, dst_refs = refs[:n], refs[n:2 * n]
        ici_send, ici_recv, d2d_send, d2d_recv, own_sems = refs[2 * n:]
        x, y, c = lax.axis_index("x"), lax.axis_index("y"), lax.axis_index("c")
        sibling = (x, y, 1 - c)
        chips = [(1 - x, y), (x, 1 - y), (1 - x, 1 - y)]

        def rows(i, chip, core):
            return dst_refs[i].at[2 * chip[0] + chip[1], pl.ds(core * halves[i], halves[i]), :]

        def over_ici(i, k, src, chip_from, to):
            return pltpu.make_async_remote_copy(
                src_ref=src, dst_ref=rows(i, chip_from, c), send_sem=ici_send.at[3 * i + k],
                recv_sem=ici_recv.at[3 * i + k], device_id=to, device_id_type=pl.DeviceIdType.MESH)

        def over_d2d(i, k, core):
            return pltpu.make_async_remote_copy(
                src_ref=rows(i, chips[k], core), dst_ref=rows(i, chips[k], core), send_sem=d2d_send.at[3 * i + k],
                recv_sem=d2d_recv.at[3 * i + k], device_id=sibling, device_id_type=pl.DeviceIdType.MESH)

        mine = [pltpu.make_async_copy(src_refs[i], dst_refs[i].at[2 * x + y], own_sems.at[i]) for i in range(n)]
        for cp in mine:
            cp.start()
        sends = []
        for i in range(n):
            my_half = src_refs[i].at[pl.ds(c * halves[i], halves[i]), :]
            for k, chip in enumerate(chips):
                sends.append(over_ici(i, k, my_half, (x, y), (*chip, c)))
                sends[-1].start()
        passed = []
        for i in range(n):
            for k, chip in enumerate(chips):
                over_ici(i, k, rows(i, chip, c), chip, (x, y, c)).wait_recv()
                passed.append(over_d2d(i, k, c))
                passed[-1].start()
        for i in range(n):
            for k in range(len(chips)):
                over_d2d(i, k, 1 - c).wait_recv()
        for cp in sends + passed:
            cp.wait_send()
        for cp in mine:
            cp.wait()

    any_spec = pl.BlockSpec(memory_space=pl.ANY)
    return pl.pallas_call(
        body, name=name, out_shape=[jax.ShapeDtypeStruct((N_CHIPS,) + s.shape, s.dtype) for s in srcs],
        in_specs=[any_spec] * n, out_specs=[any_spec] * n,
        scratch_shapes=[pltpu.SemaphoreType.DMA((3 * n,))] * 4 + [pltpu.SemaphoreType.DMA((n,))])(*srcs)


def _pack(parts, dtype, multiple):
    flat = jnp.concatenate([p.reshape(-1).astype(dtype) for p in parts])
    pad = (-flat.shape[0]) % multiple
    return jnp.pad(flat, (0, pad)) if pad else flat


def _unpack_shards(packs, names, shapes, axes):
    out, off = {}, 0
    for nme in names:
        sz = math.prod(shapes[nme])
        out[nme] = jnp.concatenate([packs[j, off:off + sz].reshape(shapes[nme]) for j in range(N_CHIPS)], axis=axes[nme])
        off += sz
    return out


def _shards_of(full, axis):
    sz = full.shape[axis] // N_CHIPS
    return [lax.slice_in_dim(full, j * sz, (j + 1) * sz, axis=axis) for j in range(N_CHIPS)]


def kernel(x, a_norm_w, a_in_proj, a_conv_w, a_conv_b, a_dt_bias, a_A_log, a_D, a_gnorm_w, a_out_proj, kv_norm_w, w_kvf, b_f, k_norm_w, b_norm_w, w_q, q_norm_w, w_o, ffn_norm_w, w_gate_up, w_down, loss_target, m_a_norm_w, m_a_in_proj, m_a_conv_w, m_a_conv_b, m_a_dt_bias, m_a_A_log, m_a_D, m_a_gnorm_w, m_a_out_proj, m_kv_norm_w, m_w_kvf, m_b_f, m_k_norm_w, m_b_norm_w, m_w_q, m_q_norm_w, m_w_o, m_ffn_norm_w, m_w_gate_up, m_w_down, v_a_norm_w, v_a_in_proj, v_a_conv_w, v_a_conv_b, v_a_dt_bias, v_a_A_log, v_a_D, v_a_gnorm_w, v_a_out_proj, v_kv_norm_w, v_w_kvf, v_b_f, v_k_norm_w, v_b_norm_w, v_w_q, v_q_norm_w, v_w_o, v_ffn_norm_w, v_w_gate_up, v_w_down):
    args = locals()
    w = {n: args[n] for n in WEIGHTS}
    mom = {n: args['m_' + n] for n in WEIGHTS}
    var = {n: args['v_' + n] for n in WEIGHTS}
    shapes = {n: w[n].shape for n in WEIGHTS}

    cx, cy, cc = lax.axis_index("x"), lax.axis_index("y"), lax.axis_index("c")
    my_chip = 2 * cx + cy

    xs_in = x[0]
    target = loss_target[0]
    s_len, d_model = xs_in.shape
    n_heads_ssm = a_dt_bias.shape[-1]
    d_inner = n_heads_ssm * SSM_HEAD_DIM
    d_xbc = a_conv_w.shape[-1] * N_CHIPS
    d_state = (d_xbc - d_inner) // (2 * SSM_GROUPS)
    kh = n_heads_ssm // SSM_GROUPS
    n_att = b_f.shape[0]
    d_att = n_att * ATT_HEAD_DIM
    d_ff = w_down.shape[1] * N_CHIPS

    small_names = list(SMALL_SHARDED)
    sp = _pack([w[n] for n in small_names], F32, PACK_COLS).reshape(-1, PACK_COLS)
    sp_all = _exchange("gather_small", [sp], [((N_CHIPS,) + sp.shape, F32)], FLIP_XY,
                       lambda i, r, me, peer: r, lambda i, r, sender: r.at[_chip(sender)],
                       lambda i, s, d, me: (s, d.at[_chip(me)]))[0]
    full = _unpack_shards(sp_all.reshape(N_CHIPS, -1), small_names, shapes, SMALL_SHARDED)
    for n in SMALL:
        if n not in full:
            full[n] = w[n]

    big2d = {'in': a_in_proj[0], 'out': a_out_proj[0], 'gu0': w_gate_up[0], 'dn0': w_down[0], 'kvf': w_kvf,
             'q': w_q[0], 'o': w_o[0], 'gu1': w_gate_up[1], 'dn1': w_down[1]}
    big_keys = list(big2d)
    gathered = dict(zip(big_keys, _gather_weights([big2d[k].astype(BF16) for k in big_keys], "gather_big")))

    def pad_cols(a, width=LANES):
        return jnp.pad(a, ((0, 0), (0, width - a.shape[1])))

    w_in = jnp.concatenate([gathered['in'][j] for j in range(N_CHIPS)], axis=1)
    w_z, w_xbc, w_dt = w_in[:, :d_inner], w_in[:, d_inner:d_inner + d_xbc], pad_cols(w_in[:, d_inner + d_xbc:])
    w_out = gathered['out'].reshape(-1, d_model)
    w_kvf_full = jnp.concatenate([gathered['kvf'][j] for j in range(N_CHIPS)], axis=1)
    w_k, w_v, w_f = w_kvf_full[:, :d_att], w_kvf_full[:, d_att:2 * d_att], pad_cols(w_kvf_full[:, 2 * d_att:])
    w_qm, w_om = gathered['q'].reshape(-1, d_att), gathered['o'].reshape(-1, d_model)
    w_gu = [gathered['gu0'], gathered['gu1']]
    w_dn = [gathered['dn0'].reshape(-1, d_model), gathered['dn1'].reshape(-1, d_model)]
    conv_w, conv_b = full['a_conv_w'][0], full['a_conv_b']
    norm_a, gnorm = full['a_norm_w'], full['a_gnorm_w']
    dt_bias, a_log = pad_cols(a_dt_bias), pad_cols(a_A_log)
    d_exp = jnp.repeat(a_D, SSM_HEAD_DIM, axis=1)
    kv_nw, b_nw = kv_norm_w.reshape(1, -1), b_norm_w
    k_nw, q_nw = k_norm_w.reshape(1, -1), q_norm_w
    bf_pad = pad_cols(b_f.reshape(1, -1))
    ffn_nw = [ffn_norm_w[i:i + 1] for i in range(2)]

    def to_groups(a):
        g = a[:, :n_heads_ssm].reshape(s_len, SSM_GROUPS, kh).transpose(1, 0, 2)
        return jnp.pad(g, ((0, 0), (0, 0), (0, LANES - kh)))

    def to_groups_t(a):
        g = a[:, :n_heads_ssm].reshape(s_len, SSM_GROUPS, kh).transpose(1, 2, 0)
        return jnp.pad(g, ((0, 0), (0, 8 - kh), (0, 0)))

    def from_groups(col, row=None):
        a = col[:, :, :kh].transpose(1, 0, 2).reshape(s_len, n_heads_ssm)
        if row is not None:
            a = a + row[:, :kh, :].transpose(2, 0, 1).reshape(s_len, n_heads_ssm)
        return pad_cols(a)

    def heads_rows(a):
        return a.reshape(-1, ATT_HEAD_DIM)

    n1 = _rms_fwd(xs_in, norm_a, "norm_a")
    z = _mm(n1, w_z, 'nn', "in_z")
    xbc_raw = _mm(n1, w_xbc, 'nn', "in_xbc")
    dtraw = _mm(n1, w_dt, 'nn', "in_dt")
    xbc = _conv_fwd(xbc_raw, conv_w, conv_b, "conv")
    dt, acum = _dt_fwd(dtraw, dt_bias, a_log, "dt")
    dt_g, ac_g, act_g = to_groups(dt), to_groups(acum), to_groups_t(acum)
    y_ssd, states = _ssd_fwd(xbc, dt_g, ac_g, act_g, d_inner, d_state, kh, "ssd")
    yn = _gate_fwd(y_ssd, xbc, z, d_exp, gnorm, d_inner, "gate")
    h1 = _mm(yn, w_out, 'nn', "out_proj", add=xs_in)

    def ffn_fwd(h, i):
        nrm = _rms_fwd(h, ffn_nw[i], f"ffn{i}_norm")
        gu = _mm(nrm, w_gu[i], 'nn', f"ffn{i}_up", shards=True)
        act = _swiglu_fwd(gu, d_ff, f"ffn{i}_act")
        return nrm, gu, act, _mm(act, w_dn[i], 'nn', f"ffn{i}_down", add=h)

    n2, gu0, act0, h2 = ffn_fwd(h1, 0)
    nkv = _rms_fwd(h2, kv_nw, "kv_norm")
    k_raw = _mm(nkv, w_k, 'nn', "proj_k")
    v_att = _mm(nkv, w_v, 'nn', "proj_v", out_dtype=BF16)
    f_raw = _mm(nkv, w_f, 'nn', "proj_f")
    k_att = _rms_fwd(heads_rows(k_raw), k_nw, "k_norm", tm=1024).reshape(s_len, d_att)
    cum = _forget_fwd(f_raw, bf_pad, "forget")
    cq_rep = jnp.repeat(cum[:, :n_att], ATT_HEAD_DIM, axis=1)
    tb = min(ATT_BLOCK, s_len)
    ck = cum[:, :n_att].T.reshape(n_att, s_len // tb, 1, tb)
    n3 = _rms_fwd(h2, b_nw, "b_norm")
    q_raw = _mm(n3, w_qm, 'nn', "proj_q")
    q_att = _rms_fwd(heads_rows(q_raw), q_nw, "q_norm", tm=1024).reshape(s_len, d_att)
    o_att, lse_rep = _attn_fwd(q_att, k_att, v_att, cq_rep, ck, "attn")
    h3 = _mm(o_att, w_om, 'nn', "proj_o", add=h2)
    n4, gu1, act1, h4 = ffn_fwd(h3, 1)

    def loss_fn(h, t):
        err = h - t
        sq = jnp.sum(jnp.sum(err * err, axis=1, keepdims=True), axis=0, keepdims=True)
        return err * (1.0 / d_model), sq * (0.5 / d_model)

    dh4, loss_part = _rowwise(loss_fn, "loss", [h4, target], outs=[(d_model, F32)], accs=[(1, 1)])

    def ffn_bwd(dh, h, nrm, gu, act, i):
        dact = _mm(dh, w_dn[i], 'nt', f"ffn{i}_down_dx")
        g_dn = _mm(act, dh, 'tn', f"ffn{i}_down_dw", out_dtype=BF16)
        dgu = _swiglu_bwd(gu, dact, d_ff, f"ffn{i}_act_bwd")
        dn = _mm(dgu, w_gu[i], 'nt', f"ffn{i}_up_dx", shards=True)
        g_gu = _mm(nrm, dgu, 'tn', f"ffn{i}_up_dw", out_dtype=BF16, shards=True)
        dh_new, g_nw = _rms_bwd(h, ffn_nw[i], dn, f"ffn{i}_norm_bwd", extra=dh)
        return dh_new, g_dn, g_gu, g_nw

    dh3, g_dn1, g_gu1, g_fnw1 = ffn_bwd(dh4, h3, n4, gu1, act1, 1)
    do_att = _mm(dh3, w_om, 'nt', "proj_o_dx", out_dtype=BF16)
    g_wo = _mm(o_att, dh3, 'tn', "proj_o_dw", out_dtype=BF16)

    def delta_fn(a, b):
        return jnp.broadcast_to(jnp.sum(a.astype(F32) * b.astype(F32), axis=1, keepdims=True), a.shape)

    delta_rep = _rowwise(delta_fn, "attn_delta", [heads_rows(do_att), heads_rows(o_att)],
                         outs=[(ATT_HEAD_DIM, F32)], tm=1024)[0].reshape(s_len, d_att)
    dq_att, dk_att, dv_att, dcq_rep, dck = _attn_bwd(q_att, k_att, v_att, do_att, lse_rep, delta_rep, cq_rep, ck,
                                                     "attn_bwd")
    dq_raw, g_qnw = _rms_bwd(heads_rows(q_raw), q_nw, heads_rows(dq_att), "q_norm_bwd", out_dtype=BF16, tm=1024)
    dq_raw = dq_raw.reshape(s_len, d_att)
    dn3 = _mm(dq_raw, w_qm, 'nt', "proj_q_dx")
    g_wq = _mm(n3, dq_raw, 'tn', "proj_q_dw", out_dtype=BF16)
    dh2, g_bnw = _rms_bwd(h2, b_nw, dn3, "b_norm_bwd", extra=dh3)
    dk_raw, g_knw = _rms_bwd(heads_rows(k_raw), k_nw, heads_rows(dk_att), "k_norm_bwd", out_dtype=BF16, tm=1024)
    dk_raw = dk_raw.reshape(s_len, d_att)
    dcum = pad_cols(dcq_rep.reshape(s_len, n_att, ATT_HEAD_DIM)[:, :, 0] + dck.reshape(n_att, s_len).T)
    df_raw, g_bf = _forget_bwd(f_raw, bf_pad, dcum, "forget_bwd")
    dnkv = _mm(dk_raw, w_k, 'nt', "proj_k_dx")
    dnkv = _mm(dv_att, w_v, 'nt', "proj_v_dx", add=dnkv)
    dnkv = _mm(df_raw, w_f, 'nt', "proj_f_dx", add=dnkv)
    g_wk = _mm(nkv, dk_raw, 'tn', "proj_k_dw", out_dtype=BF16)
    g_wv = _mm(nkv, dv_att, 'tn', "proj_v_dw", out_dtype=BF16)
    g_wf = _mm(nkv, df_raw, 'tn', "proj_f_dw", out_dtype=BF16)
    dh2, g_kvnw = _rms_bwd(h2, kv_nw, dnkv, "kv_norm_bwd", extra=dh2)
    dh1, g_dn0, g_gu0, g_fnw0 = ffn_bwd(dh2, h1, n2, gu0, act0, 0)

    dyn = _mm(dh1, w_out, 'nt', "out_proj_dx")
    g_wout = _mm(yn, dh1, 'tn', "out_proj_dw", out_dtype=BF16)
    dy_ssd, dxs_skip, dz, g_dexp, g_gnorm = _gate_bwd(y_ssd, xbc, z, d_exp, gnorm, dyn, d_inner, "gate_bwd")
    dxs, d_b, d_c, ddt_g, dacc_g, dacr_g = _ssd_bwd(xbc, dt_g, ac_g, act_g, states, dy_ssd, dxs_skip, d_inner,
                                                    d_state, kh, "ssd_bwd")
    dxbc_act = jnp.concatenate([dxs, d_b, d_c], axis=1)
    du, g_convw, g_convb = _conv_bwd(xbc_raw, conv_w, conv_b, dxbc_act, "conv_bwd")
    draw, g_dtb, g_alog = _dt_bwd(dtraw, dt_bias, a_log, from_groups(ddt_g), from_groups(dacc_g, dacr_g), "dt_bwd")
    dn1 = _mm(dz, w_z, 'nt', "in_z_dx")
    dn1 = _mm(du, w_xbc, 'nt', "in_xbc_dx", add=dn1)
    dn1 = _mm(draw, w_dt, 'nt', "in_dt_dx", add=dn1)
    g_wz = _mm(n1, dz, 'tn', "in_z_dw", out_dtype=BF16)
    g_wxbc = _mm(n1, du, 'tn', "in_xbc_dw", out_dtype=BF16)
    g_wdt = _mm(n1, draw, 'tn', "in_dt_dw", out_dtype=BF16)
    dx, g_norm_a = _rms_bwd(xs_in, norm_a, dn1, "norm_a_bwd", extra=dh1)

    n_f = n_att

    def col_shards(parts):
        fullw = jnp.concatenate(parts, axis=1)
        return fullw.reshape(fullw.shape[0], N_CHIPS, -1).transpose(1, 0, 2)

    def row_shards(a):
        return a.reshape(N_CHIPS, -1, a.shape[1])

    g_big = {
        'in': col_shards([g_wz, g_wxbc, g_wdt[:, :n_heads_ssm]]), 'out': row_shards(g_wout),
        'gu0': g_gu0, 'dn0': row_shards(g_dn0), 'kvf': col_shards([g_wk, g_wv, g_wf[:, :n_f]]),
        'q': row_shards(g_wq), 'o': row_shards(g_wo), 'gu1': g_gu1, 'dn1': row_shards(g_dn1),
    }
    g_small = {
        'a_norm_w': g_norm_a, 'a_conv_w': g_convw[None], 'a_conv_b': g_convb,
        'a_dt_bias': g_dtb[:, :n_heads_ssm], 'a_A_log': g_alog[:, :n_heads_ssm],
        'a_D': g_dexp.reshape(n_heads_ssm, SSM_HEAD_DIM).sum(axis=1).reshape(1, -1), 'a_gnorm_w': g_gnorm,
        'kv_norm_w': g_kvnw.reshape(-1), 'b_f': g_bf[0, :n_f], 'k_norm_w': g_knw.reshape(-1), 'b_norm_w': g_bnw,
        'q_norm_w': g_qnw, 'ffn_norm_w': jnp.concatenate([g_fnw0, g_fnw1], axis=0),
    }

    sg = _pack([g_small[n] for n in SMALL] + [loss_part], F32, 8 * PACK_COLS).reshape(-1, PACK_COLS)
    sg_all = _exchange("reduce_small", [sg], [((2 * N_CHIPS,) + sg.shape, F32)], FLIP_ALL,
                       lambda i, r, me, peer: r, lambda i, r, sender: r.at[_device(sender)],
                       lambda i, s, d, me: (s, d.at[_device(me)]))[0]
    sg_sum = _sum_arrays([sg_all[d] for d in range(2 * N_CHIPS)], F32, "reduce_small_sum").reshape(-1)
    red_small, off = {}, 0
    for n in SMALL:
        shp = g_small[n].shape
        red_small[n] = sg_sum[off:off + math.prod(shp)].reshape(shp)
        off += math.prod(shp)
    loss = sg_sum[off]

    red_keys = big_keys[::-1]
    g_list = [g_big[k] for k in red_keys]
    hrs = [g.shape[1] // 2 for g in g_list]
    half_shapes = [((N_CHIPS, hr, g.shape[2]), BF16) for g, hr in zip(g_list, hrs)]
    got = _exchange("reduce_big_d2d", g_list, half_shapes, FLIP_C,
                    lambda i, r, me, peer: r.at[:, pl.ds((1 - me[2]) * hrs[i], hrs[i]), :], lambda i, r, sender: r)
    chip_sums = [_sum_half(g, a, cc, "reduce_big_sum2_" + k) for g, a, k in zip(g_list, got, red_keys)]
    parts = _exchange("reduce_big_ici", chip_sums, half_shapes, FLIP_XY,
                      lambda i, r, me, peer: r.at[_chip(peer)], lambda i, r, sender: r.at[_chip(sender)],
                      lambda i, s, d, me: (s.at[_chip(me)], d.at[_chip(me)]))
    half_sums = [_sum_lead(p, "reduce_big_sum4_" + k) for p, k in zip(parts, red_keys)]
    reduced = _exchange("reduce_big_back", half_sums, [((2 * hr, g.shape[2]), F32) for g, hr in zip(g_list, hrs)],
                        FLIP_C, lambda i, r, me, peer: r,
                        lambda i, r, sender: r.at[pl.ds(sender[2] * hrs[i], hrs[i]), :],
                        lambda i, s, d, me: (s, d.at[pl.ds(me[2] * hrs[i], hrs[i]), :]))
    red_big = dict(zip(red_keys, reduced))

    grads = {
        'a_in_proj': red_big['in'][None], 'a_out_proj': red_big['out'][None], 'w_kvf': red_big['kvf'],
        'w_q': red_big['q'][None], 'w_o': red_big['o'][None],
        'w_gate_up': jnp.stack([red_big['gu0'], red_big['gu1']]),
        'w_down': jnp.stack([red_big['dn0'], red_big['dn1']]),
    }
    big_names = list(BIG)
    for n in SMALL:
        if n in SMALL_SHARDED:
            ax = SMALL_SHARDED[n]
            grads[n] = lax.dynamic_slice_in_dim(red_small[n], my_chip * shapes[n][ax], shapes[n][ax], axis=ax)
        else:
            grads[n] = red_small[n]

    delta, new_m, new_v = {}, {}, {}
    for n in big_names:
        two_d = (-1, shapes[n][-1])
        d_, m_, v_ = _adamw(w[n].reshape(two_d), grads[n].reshape(two_d), mom[n].reshape(two_d),
                            var[n].reshape(two_d), "adamw_" + n)
        delta[n], new_m[n], new_v[n] = d_.reshape(shapes[n]), m_.reshape(shapes[n]), v_.reshape(shapes[n])
    packed = [_pack([src[n] for n in SMALL], F32, 8 * LANES).reshape(-1, LANES) for src in (w, grads, mom, var)]
    small_out = _adamw(*packed, "adamw_small")
    for store, flat in zip((delta, new_m, new_v), small_out):
        flat, off = flat.reshape(-1), 0
        for n in SMALL:
            sz = math.prod(shapes[n])
            store[n] = flat[off:off + sz].reshape(shapes[n])
            off += sz

    return (loss, dx[None], *[grads[n] for n in WEIGHTS], *[delta[n] for n in WEIGHTS],
            *[new_m[n] for n in WEIGHTS], *[new_v[n] for n in WEIGHTS])
```

```python
import functools
import math

import jax
import jax.numpy as jnp
from jax import lax
from jax.experimental import pallas as pl
from jax.experimental.pallas import tpu as pltpu

F32, BF16 = jnp.float32, jnp.bfloat16
EPS = 1e-6
SSM_HEAD_DIM = 64
SSM_GROUPS = 8
SSD_CHUNK = 128
ATT_HEAD_DIM = 128
LANES = 128
N_CHIPS = 4
NEG = -1e30
ADAM_LR, ADAM_B1, ADAM_B2, ADAM_EPS, ADAM_WD, ADAM_STEP = 0.001, 0.9, 0.999, 1e-08, 0.01, 10
VMEM_LIMIT_BYTES = 56 * 1024 * 1024
PACK_COLS = 1024
PACK_ROWS = 256

NN = ((1,), (0,))
NT = ((1,), (1,))
TN = ((0,), (0,))

WEIGHTS = ['a_norm_w', 'a_in_proj', 'a_conv_w', 'a_conv_b', 'a_dt_bias', 'a_A_log', 'a_D', 'a_gnorm_w', 'a_out_proj',
           'kv_norm_w', 'w_kvf', 'b_f', 'k_norm_w', 'b_norm_w', 'w_q', 'q_norm_w', 'w_o', 'ffn_norm_w', 'w_gate_up',
           'w_down']
BIG = {'a_in_proj': 2, 'a_out_proj': 1, 'w_kvf': 1, 'w_q': 1, 'w_o': 1, 'w_gate_up': 2, 'w_down': 1}
SMALL_SHARDED = {'a_norm_w': 1, 'a_conv_w': 2, 'a_conv_b': 1, 'a_gnorm_w': 1}
SMALL = [n for n in WEIGHTS if n not in BIG]


def _dot(a, b, dims):
    return lax.dot_general(a, b, (dims, ((), ())), preferred_element_type=F32)


def _params(sem=None):
    return pltpu.CompilerParams(dimension_semantics=sem, vmem_limit_bytes=VMEM_LIMIT_BYTES)


def _tile(dim, cap):
    for t in (1408, 1024, 512, 256, 128):
        if t <= cap and dim % t == 0:
            return t
    return dim


def _mm(a, b, mode, name, out_dtype=F32, add=None, shards=False):
    if mode == 'nn':
        (m, k), n = a.shape, (b.shape[2] * N_CHIPS if shards else b.shape[1])
    elif mode == 'nt':
        (m, k), n = a.shape, (b.shape[1] if shards else b.shape[0])
    else:
        (k, m), n = a.shape, b.shape[1]
    per_chip = (k if mode == 'nt' else n) // N_CHIPS
    tm = _tile(m, 1024)
    tn = _tile(per_chip if shards and mode != 'nt' else n, 1408 if shards else 1024)
    tk = _tile(per_chip, 1408) if shards and mode == 'nt' else _tile(k, 512)
    nk = k // tk
    a_spec = pl.BlockSpec((tk, tm), lambda i, j, q: (q, i)) if mode == 'tn' else pl.BlockSpec((tm, tk), lambda i, j, q: (i, q))
    b_spec = pl.BlockSpec((tn, tk), lambda i, j, q: (j, q)) if mode == 'nt' else pl.BlockSpec((tk, tn), lambda i, j, q: (q, j))
    o_spec = pl.BlockSpec((tm, tn), lambda i, j, q: (i, j))
    out_struct = jax.ShapeDtypeStruct((m, n), out_dtype)
    if shards:
        per = per_chip // (tk if mode == 'nt' else tn)
        if mode == 'nn':
            b_spec = pl.BlockSpec((None, tk, tn), lambda i, j, q: (j // per, q, j % per))
        elif mode == 'nt':
            b_spec = pl.BlockSpec((None, tn, tk), lambda i, j, q: (q // per, j, q % per))
        else:
            out_struct = jax.ShapeDtypeStruct((N_CHIPS, m, per_chip), out_dtype)
    out_spec = pl.BlockSpec((None, tm, tn), lambda i, j, q: (j // per, i, j % per)) if shards and mode == 'tn' else o_spec
    dims = {'nn': NN, 'nt': NT, 'tn': TN}[mode]

    def kern(*refs):
        a_ref, b_ref = refs[0], refs[1]
        o_ref, acc = refs[-2], refs[-1]
        q = pl.program_id(2)

        @pl.when(q == 0)
        def _():
            acc[...] = jnp.zeros_like(acc)

        acc[...] += _dot(a_ref[...].astype(BF16), b_ref[...].astype(BF16), dims)

        @pl.when(q == nk - 1)
        def _():
            r = acc[...]
            if add is not None:
                r = r + refs[2][...]
            o_ref[...] = r.astype(o_ref.dtype)

    ins, specs = [a, b], [a_spec, b_spec]
    if add is not None:
        ins.append(add)
        specs.append(o_spec)
    return pl.pallas_call(
        kern, name=name, grid=(m // tm, n // tn, nk), in_specs=specs, out_specs=out_spec,
        out_shape=out_struct, scratch_shapes=[pltpu.VMEM((tm, tn), F32)],
        compiler_params=_params(("parallel", "parallel", "arbitrary")))(*ins)


def _rowwise(fn, name, rows, bcast=(), outs=(), accs=(), tm=256):
    rows = [r if isinstance(r, tuple) else (r, r.shape[1], 0) for r in rows]
    n_rows = rows[0][0].shape[0]
    tm = min(tm, n_rows)
    assert n_rows % tm == 0, (name, n_rows, tm)
    n_in, n_out = len(rows) + len(bcast), len(outs)
    in_specs = [pl.BlockSpec((tm, w), functools.partial(lambda i, cb: (i, cb), cb=cb)) for _, w, cb in rows]
    in_specs += [pl.BlockSpec(b.shape, lambda i: (0, 0)) for b in bcast]
    out_specs = [pl.BlockSpec((tm, w), lambda i: (i, 0)) for w, _ in outs]
    out_specs += [pl.BlockSpec(s, lambda i: (0, 0)) for s in accs]
    out_shape = [jax.ShapeDtypeStruct((n_rows, w), d) for w, d in outs] + [jax.ShapeDtypeStruct(s, F32) for s in accs]

    def kern(*refs):
        vals = fn(*[r[...] for r in refs[:n_in]])
        vals = vals if isinstance(vals, (tuple, list)) else (vals,)
        o_refs = refs[n_in:]
        for r, v in zip(o_refs[:n_out], vals[:n_out]):
            r[...] = v.astype(r.dtype)
        if accs:
            @pl.when(pl.program_id(0) == 0)
            def _():
                for r in o_refs[n_out:]:
                    r[...] = jnp.zeros_like(r)

            for r, v in zip(o_refs[n_out:], vals[n_out:]):
                r[...] += v

    return pl.pallas_call(
        kern, name=name, grid=(n_rows // tm,), in_specs=in_specs, out_specs=out_specs, out_shape=out_shape,
        compiler_params=_params(("arbitrary",)))(*[r[0] for r in rows], *bcast)


def _rms(x, w):
    xf = x.astype(F32)
    return xf * lax.rsqrt(jnp.mean(xf * xf, axis=-1, keepdims=True) + EPS) * w


def _rms_fwd(x, w, name, tm=256):
    return _rowwise(_rms, name, [x], [w], outs=[(x.shape[1], BF16)], tm=tm)[0]


def _rms_bwd(x, w, dy, name, extra=None, out_dtype=F32, tm=256):
    def fn(x, dy, *rest):
        _, vjp = jax.vjp(_rms, x, rest[-1])
        dx, dw = vjp(dy.astype(F32))
        if extra is not None:
            dx = dx + rest[0]
        return dx, dw

    rows = [x, dy] + ([extra] if extra is not None else [])
    return _rowwise(fn, name, rows, [w], outs=[(x.shape[1], out_dtype)], accs=[w.shape], tm=tm)


def _sigmoid(x):
    return 1.0 / (1.0 + jnp.exp(-x))


def _softplus(x):
    return jnp.maximum(x, 0.0) + jnp.log(1.0 + jnp.exp(-jnp.abs(x)))


def _swiglu_fwd(gu, d_ff, name):
    def fn(g, u):
        return g * _sigmoid(g) * u

    return _rowwise(fn, name, [(gu, d_ff, 0), (gu, d_ff, 1)], outs=[(d_ff, BF16)], tm=128)[0]


def _swiglu_bwd(gu, dact, d_ff, name):
    def fn(g, u, da):
        s = _sigmoid(g)
        dg = da * u * s * (1.0 + g * (1.0 - s))
        du = da * g * s
        return jnp.concatenate([dg, du], axis=1)

    return _rowwise(fn, name, [(gu, d_ff, 0), (gu, d_ff, 1), dact], outs=[(2 * d_ff, BF16)], tm=128)[0]


def _cumsum_rows(v, reverse=False):
    n = v.shape[0]
    row = lax.broadcasted_iota(jnp.int32, v.shape, 0)
    sh = 1
    while sh < n:
        if reverse:
            v = v + jnp.where(row < n - sh, pltpu.roll(v, n - sh, 0), 0.0)
        else:
            v = v + jnp.where(row >= sh, pltpu.roll(v, sh, 0), 0.0)
        sh *= 2
    return v


def _conv_fwd(u, w, b, name):
    s, c = u.shape
    kw = w.shape[0]
    tc = _tile(c, 128)

    def kern(u_ref, w_ref, b_ref, o_ref):
        uu = u_ref[...]
        row = lax.broadcasted_iota(jnp.int32, uu.shape, 0)
        acc = jnp.zeros_like(uu) + b_ref[...]
        for k in range(kw):
            sh = kw - 1 - k
            uk = uu if sh == 0 else jnp.where(row >= sh, pltpu.roll(uu, sh, 0), 0.0)
            acc = acc + w_ref[pl.ds(k, 1), :] * uk
        o_ref[...] = acc * _sigmoid(acc)

    return pl.pallas_call(
        kern, name=name, grid=(c // tc,),
        in_specs=[pl.BlockSpec((s, tc), lambda j: (0, j)), pl.BlockSpec((kw, tc), lambda j: (0, j)),
                  pl.BlockSpec((1, tc), lambda j: (0, j))],
        out_specs=pl.BlockSpec((s, tc), lambda j: (0, j)), out_shape=jax.ShapeDtypeStruct((s, c), F32),
        compiler_params=_params(("parallel",)))(u, w, b)


def _conv_bwd(u, w, b, dact, name):
    s, c = u.shape
    kw = w.shape[0]
    tc = _tile(c, 128)

    def kern(u_ref, w_ref, b_ref, d_ref, du_ref, dw_ref, db_ref):
        uu = u_ref[...]
        row = lax.broadcasted_iota(jnp.int32, uu.shape, 0)
        shifted = []
        acc = jnp.zeros_like(uu) + b_ref[...]
        for k in range(kw):
            sh = kw - 1 - k
            uk = uu if sh == 0 else jnp.where(row >= sh, pltpu.roll(uu, sh, 0), 0.0)
            shifted.append(uk)
            acc = acc + w_ref[pl.ds(k, 1), :] * uk
        sg = _sigmoid(acc)
        dacc = d_ref[...] * sg * (1.0 + acc * (1.0 - sg))
        db_ref[...] = jnp.sum(dacc, axis=0, keepdims=True)
        du = jnp.zeros_like(uu)
        for k in range(kw):
            sh = kw - 1 - k
            dw_ref[pl.ds(k, 1), :] = jnp.sum(dacc * shifted[k], axis=0, keepdims=True)
            dk = dacc if sh == 0 else jnp.where(row < s - sh, pltpu.roll(dacc, s - sh, 0), 0.0)
            du = du + w_ref[pl.ds(k, 1), :] * dk
        du_ref[...] = du.astype(du_ref.dtype)

    col = lambda j: (0, j)
    return pl.pallas_call(
        kern, name=name, grid=(c // tc,),
        in_specs=[pl.BlockSpec((s, tc), col), pl.BlockSpec((kw, tc), col), pl.BlockSpec((1, tc), col),
                  pl.BlockSpec((s, tc), col)],
        out_specs=[pl.BlockSpec((s, tc), col), pl.BlockSpec((kw, tc), col), pl.BlockSpec((1, tc), col)],
        out_shape=[jax.ShapeDtypeStruct((s, c), BF16), jax.ShapeDtypeStruct((kw, c), F32),
                   jax.ShapeDtypeStruct((1, c), F32)],
        compiler_params=_params(("parallel",)))(u, w, b, dact)


def _dt_fwd(dtraw, bias, a_log, name):
    def fn(raw, bias, a_log):
        dt = _softplus(raw + bias)
        return dt, _cumsum_rows(dt * (-jnp.exp(a_log)))

    return _rowwise(fn, name, [dtraw], [bias, a_log], outs=[(LANES, F32), (LANES, F32)], tm=SSD_CHUNK)


def _dt_bwd(dtraw, bias, a_log, ddt, dacum, name):
    def fn(raw, ddt, dac, bias, a_log):
        z = raw + bias
        dt = _softplus(z)
        a_neg = -jnp.exp(a_log)
        da = _cumsum_rows(dac, reverse=True)
        draw = (ddt + da * a_neg) * _sigmoid(z)
        return draw, jnp.sum(draw, axis=0, keepdims=True), jnp.sum(da * dt, axis=0, keepdims=True) * a_neg

    return _rowwise(fn, name, [dtraw, ddt, dacum], [bias, a_log], outs=[(LANES, BF16)],
                    accs=[(1, LANES), (1, LANES)], tm=SSD_CHUNK)


def _ssd_pieces(xs, dt, ac, kh):
    l, gw = xs.shape
    lane_w = lax.broadcasted_iota(jnp.int32, (l, gw), 1)
    lane_k = lax.broadcasted_iota(jnp.int32, (l, LANES), 1)
    head = [(lane_w >= k * SSM_HEAD_DIM) & (lane_w < (k + 1) * SSM_HEAD_DIM) for k in range(kh)]

    def col(blk, k):
        return jnp.sum(jnp.where(lane_k == k, blk, 0.0), axis=1, keepdims=True)

    def expand(blk):
        acc = jnp.zeros((l, gw), F32)
        for k in range(kh):
            acc = jnp.where(head[k], col(blk, k), acc)
        return acc

    def collapse(wide):
        acc = jnp.zeros((l, LANES), F32)
        for k in range(kh):
            acc = jnp.where(lane_k == k, jnp.sum(jnp.where(head[k], wide, 0.0), axis=1, keepdims=True), acc)
        return acc

    return head, lane_k, col, expand, collapse


def _ssd_specs(l, gw, n, n_xs_blocks):
    g_axis = SSM_GROUPS
    return dict(
        xs=lambda cm: pl.BlockSpec((l, gw), lambda g, c: (cm(c), g)),
        b=lambda cm: pl.BlockSpec((l, n), lambda g, c: (cm(c), n_xs_blocks + g)),
        c=lambda cm: pl.BlockSpec((l, n), lambda g, c: (cm(c), n_xs_blocks + g_axis + g)),
        col=lambda cm: pl.BlockSpec((None, l, LANES), lambda g, c: (g, cm(c), 0)),
        row=lambda cm: pl.BlockSpec((None, 8, l), lambda g, c: (g, 0, cm(c))),
        state=lambda cm: pl.BlockSpec((None, None, n, gw), lambda g, c: (cm(c), g, 0, 0)),
    )


def _ssd_fwd(xbc, dt_g, ac_g, act_g, d_inner, n, kh, name):
    s = xbc.shape[0]
    l, g_n = SSD_CHUNK, SSM_GROUPS
    gw, nc = d_inner // g_n, s // l
    sp = _ssd_specs(l, gw, n, d_inner // n)
    fwd = lambda c: c

    def kern(xs_ref, b_ref, c_ref, dt_ref, ac_ref, act_ref, y_ref, s0_ref, st):
        @pl.when(pl.program_id(1) == 0)
        def _():
            st[...] = jnp.zeros_like(st)

        s0 = st[...]
        s0_ref[...] = s0
        xs, ac = xs_ref[...], ac_ref[...]
        head, _, col, expand, _ = _ssd_pieces(xs, dt_ref[...], ac, kh)
        ace = expand(ac)
        x = xs * expand(dt_ref[...])
        xb, bb, cb_ = x.astype(BF16), b_ref[...].astype(BF16), c_ref[...].astype(BF16)
        cb = _dot(cb_, bb, NT)
        ri = lax.broadcasted_iota(jnp.int32, (l, l), 0)
        ci = lax.broadcasted_iota(jnp.int32, (l, l), 1)
        causal = ri >= ci
        y = _dot(cb_, s0.astype(BF16), NN) * jnp.exp(ace)
        for k in range(kh):
            seg = col(ac, k) - act_ref[pl.ds(k, 1), :]
            m = jnp.where(causal, cb * jnp.exp(jnp.where(causal, seg, 0.0)), 0.0)
            y = jnp.where(head[k], y + _dot(m.astype(BF16), xb, NN), y)
        y_ref[...] = y
        row_w = lax.broadcasted_iota(jnp.int32, (l, gw), 0)
        alast = jnp.sum(jnp.where(row_w == l - 1, ace, 0.0), axis=0, keepdims=True)
        st[...] = s0 * jnp.exp(alast) + _dot(bb, (jnp.exp(alast - ace) * x).astype(BF16), TN)

    return pl.pallas_call(
        kern, name=name, grid=(g_n, nc),
        in_specs=[sp['xs'](fwd), sp['b'](fwd), sp['c'](fwd), sp['col'](fwd), sp['col'](fwd), sp['row'](fwd)],
        out_specs=[pl.BlockSpec((l, gw), lambda g, c: (c, g)), sp['state'](fwd)],
        out_shape=[jax.ShapeDtypeStruct((s, d_inner), F32), jax.ShapeDtypeStruct((nc, g_n, n, gw), F32)],
        scratch_shapes=[pltpu.VMEM((n, gw), F32)],
        compiler_params=_params(("arbitrary", "arbitrary")))(xbc, xbc, xbc, dt_g, ac_g, act_g)


def _ssd_bwd(xbc, dt_g, ac_g, act_g, s0_all, dy, dxs_skip, d_inner, n, kh, name):
    s = xbc.shape[0]
    l, g_n = SSD_CHUNK, SSM_GROUPS
    gw, nc = d_inner // g_n, s // l
    sp = _ssd_specs(l, gw, n, d_inner // n)
    rev = lambda c: nc - 1 - c

    def kern(xs_ref, b_ref, c_ref, dt_ref, ac_ref, act_ref, s0_ref, dy_ref, skip_ref,
             dxs_ref, db_ref, dc_ref, ddt_ref, dacc_ref, dacr_ref, dst):
        @pl.when(pl.program_id(1) == 0)
        def _():
            dst[...] = jnp.zeros_like(dst)

        dsn = dst[...]
        s0 = s0_ref[...]
        xs, ac, dy = xs_ref[...], ac_ref[...], dy_ref[...]
        head, lane_k, col, expand, collapse = _ssd_pieces(xs, dt_ref[...], ac, kh)
        ace, dte = expand(ac), expand(dt_ref[...])
        x = xs * dte
        xb, bb, cb_ = x.astype(BF16), b_ref[...].astype(BF16), c_ref[...].astype(BF16)
        s0b, dsnb, dyb = s0.astype(BF16), dsn.astype(BF16), dy.astype(BF16)
        cb = _dot(cb_, bb, NT)
        ri = lax.broadcasted_iota(jnp.int32, (l, l), 0)
        ci = lax.broadcasted_iota(jnp.int32, (l, l), 1)
        causal = ri >= ci
        row_w = lax.broadcasted_iota(jnp.int32, (l, gw), 0)
        e = jnp.exp(ace)
        alast = jnp.sum(jnp.where(row_w == l - 1, ace, 0.0), axis=0, keepdims=True)
        gdec = jnp.exp(alast)
        wt = jnp.exp(alast - ace)
        cs = _dot(cb_, s0b, NN)
        dcs = (dy * e).astype(BF16)
        d_c = _dot(dcs, s0b, NT)
        ds_off = _dot(cb_, dcs, TN)
        dace = dy * cs * e
        dalast = jnp.sum(dsn * s0, axis=0, keepdims=True) * gdec
        z = wt * x
        dz = _dot(bb, dsnb, NN)
        d_b = _dot(z.astype(BF16), dsnb, NT)
        dx = dz * wt
        t = dz * z
        dalast = dalast + jnp.sum(t, axis=0, keepdims=True)
        dace = dace - t
        dcb = jnp.zeros((l, l), F32)
        dac_col = jnp.zeros((l, LANES), F32)
        for k in range(kh):
            seg = col(ac, k) - act_ref[pl.ds(k, 1), :]
            dk = jnp.exp(jnp.where(causal, seg, 0.0))
            mk = jnp.where(causal, cb * dk, 0.0)
            dx = jnp.where(head[k], dx + _dot(mk.astype(BF16), dyb, TN), dx)
            dmk = _dot(jnp.where(head[k], dy, 0.0).astype(BF16), xb, NT)
            dcb = dcb + jnp.where(causal, dmk * dk, 0.0)
            dseg = dmk * mk
            dac_col = jnp.where(lane_k == k, jnp.sum(dseg, axis=1, keepdims=True), dac_col)
            dacr_ref[pl.ds(k, 1), :] = -jnp.sum(dseg, axis=0, keepdims=True)
        for k in range(kh, 8):
            dacr_ref[pl.ds(k, 1), :] = jnp.zeros((1, l), F32)
        dcbb = dcb.astype(BF16)
        dc_ref[...] = d_c + _dot(dcbb, bb, NN)
        db_ref[...] = d_b + _dot(dcbb, cb_, TN)
        dace = jnp.where(row_w == l - 1, dace + dalast, dace)
        dacc_ref[...] = dac_col + collapse(dace)
        ddt_ref[...] = collapse(dx * xs)
        dxs_ref[...] = dx * dte + skip_ref[...]
        dst[...] = dsn * gdec + ds_off

    return pl.pallas_call(
        kern, name=name, grid=(g_n, nc),
        in_specs=[sp['xs'](rev), sp['b'](rev), sp['c'](rev), sp['col'](rev), sp['col'](rev), sp['row'](rev),
                  sp['state'](rev), pl.BlockSpec((l, gw), lambda g, c: (rev(c), g)),
                  pl.BlockSpec((l, gw), lambda g, c: (rev(c), g))],
        out_specs=[pl.BlockSpec((l, gw), lambda g, c: (rev(c), g)), pl.BlockSpec((l, n), lambda g, c: (rev(c), g)),
                   pl.BlockSpec((l, n), lambda g, c: (rev(c), g)), sp['col'](rev), sp['col'](rev), sp['row'](rev)],
        out_shape=[jax.ShapeDtypeStruct((s, d_inner), F32), jax.ShapeDtypeStruct((s, g_n * n), F32),
                   jax.ShapeDtypeStruct((s, g_n * n), F32), jax.ShapeDtypeStruct((g_n, s, LANES), F32),
                   jax.ShapeDtypeStruct((g_n, s, LANES), F32), jax.ShapeDtypeStruct((g_n, 8, s), F32)],
        scratch_shapes=[pltpu.VMEM((n, gw), F32)],
        compiler_params=_params(("arbitrary", "arbitrary")))(xbc, xbc, xbc, dt_g, ac_g, act_g, s0_all, dy, dxs_skip)


def _gate(y, xs, z, d_exp, gw):
    t = (y + xs * d_exp) * (z * _sigmoid(z))
    width = t.shape[1]
    gsz = width // SSM_GROUPS
    lane = lax.broadcasted_iota(jnp.int32, t.shape, 1)
    t2 = t * t
    scale = jnp.zeros_like(t)
    for g in range(SSM_GROUPS):
        in_g = (lane >= g * gsz) & (lane < (g + 1) * gsz)
        ms = jnp.sum(jnp.where(in_g, t2, 0.0), axis=1, keepdims=True) * (1.0 / gsz)
        scale = jnp.where(in_g, lax.rsqrt(ms + EPS), scale)
    return t * scale * gw


def _gate_fwd(y, xbc, z, d_exp, gw, d_inner, name):
    return _rowwise(_gate, name, [y, (xbc, d_inner, 0), z], [d_exp, gw], outs=[(d_inner, BF16)], tm=128)[0]


def _gate_bwd(y, xbc, z, d_exp, gw, dyn, d_inner, name):
    def fn(y, xs, z, dyn, d_exp, gw):
        _, vjp = jax.vjp(_gate, y, xs, z, d_exp, gw)
        return vjp(dyn)

    return _rowwise(fn, name, [y, (xbc, d_inner, 0), z, dyn], [d_exp, gw],
                    outs=[(d_inner, F32), (d_inner, F32), (d_inner, BF16)], accs=[d_exp.shape, gw.shape], tm=64)


def _forget_fwd(fraw, b_f, name):
    def kern(f_ref, b_ref, o_ref):
        o_ref[...] = _cumsum_rows(-_softplus(-(f_ref[...] + b_ref[...])))

    return pl.pallas_call(kern, name=name, out_shape=jax.ShapeDtypeStruct(fraw.shape, F32),
                          compiler_params=_params())(fraw, b_f)


def _forget_bwd(fraw, b_f, dcum, name):
    def kern(f_ref, b_ref, d_ref, df_ref, db_ref):
        df = _cumsum_rows(d_ref[...], reverse=True) * _sigmoid(-(f_ref[...] + b_ref[...]))
        df_ref[...] = df.astype(df_ref.dtype)
        db_ref[...] = jnp.sum(df, axis=0, keepdims=True)

    return pl.pallas_call(
        kern, name=name,
        out_shape=[jax.ShapeDtypeStruct(fraw.shape, BF16), jax.ShapeDtypeStruct((1, fraw.shape[1]), F32)],
        compiler_params=_params())(fraw, b_f, dcum)


ATT_BLOCK = 512


def _attn_fwd(q, k, v, cq_rep, ck, name):
    s, hd = q.shape
    h_n, d = hd // ATT_HEAD_DIM, ATT_HEAD_DIM
    tb = min(ATT_BLOCK, s)
    nb = s // tb
    scale = d ** -0.5

    def kern(q_ref, k_ref, v_ref, cq_ref, ck_ref, o_ref, lse_ref):
        i = pl.program_id(1)
        qq = q_ref[...]
        cq = jnp.max(cq_ref[...], axis=1, keepdims=True)
        rowpos = i * tb + lax.broadcasted_iota(jnp.int32, (tb, tb), 0)
        coli = lax.broadcasted_iota(jnp.int32, (tb, tb), 1)

        def step(j, carry, diagonal):
            m, l_, acc = carry
            ks = pl.ds(pl.multiple_of(j * tb, tb), tb)
            sc = _dot(qq, k_ref[ks, :], NT) * scale + (cq - ck_ref[j])
            if diagonal:
                sc = jnp.where(j * tb + coli <= rowpos, sc, NEG)
            mn = jnp.maximum(m, jnp.max(sc, axis=1, keepdims=True))
            p = jnp.exp(sc - mn)
            alpha = jnp.exp(m - mn)
            l_ = alpha * l_ + jnp.sum(p, axis=1, keepdims=True)
            acc = alpha * acc + _dot(p.astype(BF16), v_ref[ks, :], NN)
            return mn, l_, acc

        init = (jnp.full((tb, 1), NEG, F32), jnp.zeros((tb, 1), F32), jnp.zeros((tb, d), F32))
        below = lax.fori_loop(0, i, lambda j, carry: step(j, carry, False), init)
        m, l_, acc = step(i, below, True)
        o_ref[...] = (acc / l_).astype(o_ref.dtype)
        lse_ref[...] = jnp.broadcast_to(m + jnp.log(l_), (tb, d))

    return pl.pallas_call(
        kern, name=name, grid=(h_n, nb),
        in_specs=[pl.BlockSpec((tb, d), lambda h, i: (i, h)), pl.BlockSpec((s, d), lambda h, i: (0, h)),
                  pl.BlockSpec((s, d), lambda h, i: (0, h)), pl.BlockSpec((tb, d), lambda h, i: (i, h)),
                  pl.BlockSpec((None, nb, 1, tb), lambda h, i: (h, 0, 0, 0))],
        out_specs=[pl.BlockSpec((tb, d), lambda h, i: (i, h)), pl.BlockSpec((tb, d), lambda h, i: (i, h))],
        out_shape=[jax.ShapeDtypeStruct((s, hd), BF16), jax.ShapeDtypeStruct((s, hd), F32)],
        compiler_params=_params(("parallel", "arbitrary")))(q, k, v, cq_rep, ck)


def _attn_bwd(q, k, v, do, lse_rep, delta_rep, cq_rep, ck, name):
    s, hd = q.shape
    h_n, d = hd // ATT_HEAD_DIM, ATT_HEAD_DIM
    tb = min(ATT_BLOCK, s)
    nb = s // tb
    scale = d ** -0.5

    def kern(q_ref, do_ref, k_ref, v_ref, lse_ref, dl_ref, cq_ref, ck_ref, dq_ref, dk_ref, dv_ref, dcq_ref, dck_ref):
        j = pl.program_id(1)

        @pl.when(j == 0)
        def _():
            dq_ref[...] = jnp.zeros_like(dq_ref)
            dcq_ref[...] = jnp.zeros_like(dcq_ref)

        kj, vj, ckj = k_ref[...], v_ref[...], ck_ref[...]
        colpos = j * tb + lax.broadcasted_iota(jnp.int32, (tb, tb), 1)
        rowi = lax.broadcasted_iota(jnp.int32, (tb, tb), 0)

        def step(i, carry, diagonal):
            dk, dv, dck = carry
            rs = pl.ds(pl.multiple_of(i * tb, tb), tb)
            qi, doi = q_ref[rs, :], do_ref[rs, :]
            lse = jnp.max(lse_ref[rs, :], axis=1, keepdims=True)
            dl = jnp.max(dl_ref[rs, :], axis=1, keepdims=True)
            cq = jnp.max(cq_ref[rs, :], axis=1, keepdims=True)
            sc = _dot(qi, kj, NT) * scale + (cq - ckj) - lse
            p = jnp.exp(jnp.where(colpos <= i * tb + rowi, sc, NEG) if diagonal else sc)
            dp = _dot(doi, vj, NT)
            ds = p * (dp - dl)
            dsb = ds.astype(BF16)
            dv = dv + _dot(p.astype(BF16), doi, TN)
            dk = dk + _dot(dsb, qi, TN)
            dq_ref[rs, :] += _dot(dsb, kj, NN) * scale
            dcq_ref[rs, :] += jnp.broadcast_to(jnp.sum(ds, axis=1, keepdims=True), (tb, d))
            return dk, dv, dck - jnp.sum(ds, axis=0, keepdims=True)

        init = (jnp.zeros((tb, d), F32), jnp.zeros((tb, d), F32), jnp.zeros((1, tb), F32))
        dk, dv, dck = lax.fori_loop(j + 1, nb, lambda i, carry: step(i, carry, False), step(j, init, True))
        dk_ref[...] = dk * scale
        dv_ref[...] = dv
        dck_ref[...] = dck

    whole = pl.BlockSpec((s, d), lambda h, j: (0, h))
    blk = pl.BlockSpec((tb, d), lambda h, j: (j, h))
    ckb = pl.BlockSpec((None, None, 1, tb), lambda h, j: (h, j, 0, 0))
    return pl.pallas_call(
        kern, name=name, grid=(h_n, nb),
        in_specs=[whole, whole, blk, blk, whole, whole, whole, ckb],
        out_specs=[whole, blk, blk, whole, ckb],
        out_shape=[jax.ShapeDtypeStruct((s, hd), F32), jax.ShapeDtypeStruct((s, hd), F32),
                   jax.ShapeDtypeStruct((s, hd), F32), jax.ShapeDtypeStruct((s, hd), F32),
                   jax.ShapeDtypeStruct((h_n, nb, 1, tb), F32)],
        compiler_params=_params(("arbitrary", "arbitrary")))(q, do, k, v, lse_rep, delta_rep, cq_rep, ck)


def _adamw(w, g, m, v, name):
    cols = w.shape[1]
    tm = _tile(w.shape[0], 128) if w.shape[0] % 128 == 0 else w.shape[0]
    return _rowwise(_adamw_math, name, [w, g, m, v], outs=[(cols, F32)] * 3, tm=tm)


def _sum_arrays(arrs, out_dtype, name):
    def fn(*xs):
        acc = xs[0].astype(F32)
        for x in xs[1:]:
            acc = acc + x.astype(F32)
        return acc

    tm = PACK_ROWS if arrs[0].shape[0] % PACK_ROWS == 0 else arrs[0].shape[0]
    return _rowwise(fn, name, list(arrs), outs=[(arrs[0].shape[1], out_dtype)], tm=tm)[0]


def _half_tile(rows):
    for t in (256, 176, 128, 64, 32, 16):
        if rows % t == 0:
            return t
    return rows


def _sum_half(g, got, core, name):
    n_chip, hr, cols = got.shape
    tm = _half_tile(hr)
    nt = hr // tm

    def kern(c_ref, g_ref, a_ref, o_ref):
        o_ref[...] = (g_ref[...].astype(F32) + a_ref[...].astype(F32)).astype(o_ref.dtype)

    grid_spec = pltpu.PrefetchScalarGridSpec(
        num_scalar_prefetch=1, grid=(n_chip * nt,),
        in_specs=[pl.BlockSpec((tm, cols), lambda r, c: (((r // nt) * 2 + c[0]) * nt + r % nt, 0)),
                  pl.BlockSpec((tm, cols), lambda r, c: (r, 0))],
        out_specs=pl.BlockSpec((tm, cols), lambda r, c: (r, 0)))
    out = pl.pallas_call(
        kern, name=name, grid_spec=grid_spec, out_shape=jax.ShapeDtypeStruct((n_chip * hr, cols), BF16),
        compiler_params=_params(("arbitrary",)))(core.reshape(1), g.reshape(-1, cols), got.reshape(-1, cols))
    return out.reshape(n_chip, hr, cols)


def _sum_parts(own, parts, chip, name):
    n_parts, hr, cols = parts.shape
    tm = _half_tile(hr)

    def kern(s_ref, t_ref, p_ref, o_ref):
        acc = t_ref[...].astype(F32)
        for j in range(n_parts):
            acc = acc + p_ref[j].astype(F32)
        o_ref[...] = acc

    grid_spec = pltpu.PrefetchScalarGridSpec(
        num_scalar_prefetch=1, grid=(hr // tm,),
        in_specs=[pl.BlockSpec((None, tm, cols), lambda r, s: (s[0], r, 0)),
                  pl.BlockSpec((n_parts, tm, cols), lambda r, s: (0, r, 0))],
        out_specs=pl.BlockSpec((tm, cols), lambda r, s: (r, 0)))
    return pl.pallas_call(
        kern, name=name, grid_spec=grid_spec, out_shape=jax.ShapeDtypeStruct((hr, cols), F32),
        compiler_params=_params(("arbitrary",)))(chip.reshape(1), own, parts)


def _adamw_math(w, g, m, v):
    m = ADAM_B1 * m + (1.0 - ADAM_B1) * g
    v = ADAM_B2 * v + (1.0 - ADAM_B2) * (g * g)
    m_hat = m / (1.0 - ADAM_B1 ** ADAM_STEP)
    v_hat = v / (1.0 - ADAM_B2 ** ADAM_STEP)
    return -ADAM_LR * (m_hat / (jnp.sqrt(v_hat) + ADAM_EPS) + ADAM_WD * w), m, v


def _adamw_big(w, m, v, mine, theirs, core, layer, prev, name):
    n_layers, rows, cols = w.shape
    hr = rows // 2
    tm = next(t for t in (128, 64, 32, 16, 8) if hr % t == 0)
    nt = hr // tm

    def kern(s_ref, w_ref, m_ref, v_ref, a_ref, b_ref, *rest):
        g_ref, d_ref, nm_ref, nv_ref = rest[-4:]
        g = jnp.where(pl.program_id(0) // nt == s_ref[0], a_ref[...], b_ref[...])
        g_ref[...] = g
        d_ref[...], nm_ref[...], nv_ref[...] = _adamw_math(w_ref[...], g, m_ref[...], v_ref[...])

    lyr = pl.BlockSpec((None, tm, cols), lambda r, s: (layer, r, 0))
    half = pl.BlockSpec((tm, cols), lambda r, s: (r % nt, 0))
    passed = [] if prev is None else list(prev)
    grid_spec = pltpu.PrefetchScalarGridSpec(
        num_scalar_prefetch=1, grid=(rows // tm,),
        in_specs=[lyr, lyr, lyr, half, half] + [pl.BlockSpec(memory_space=pl.ANY)] * len(passed), out_specs=[lyr] * 4)
    return pl.pallas_call(
        kern, name=name, grid_spec=grid_spec, out_shape=[jax.ShapeDtypeStruct(w.shape, F32)] * 4,
        input_output_aliases={6 + i: i for i in range(len(passed))},
        compiler_params=_params(("arbitrary",)))(core.reshape(1), w, m, v, mine, theirs, *passed)


def _cast_into_slot(src, slot, name):
    rows, cols = src.shape
    tm = _half_tile(rows)

    def kern(s_ref, x_ref, o_ref):
        o_ref[...] = x_ref[...].astype(o_ref.dtype)

    grid_spec = pltpu.PrefetchScalarGridSpec(
        num_scalar_prefetch=1, grid=(rows // tm,), in_specs=[pl.BlockSpec((tm, cols), lambda r, s: (r, 0))],
        out_specs=pl.BlockSpec((None, tm, cols), lambda r, s: (s[0], r, 0)))
    return pl.pallas_call(
        kern, name=name, grid_spec=grid_spec, out_shape=jax.ShapeDtypeStruct((N_CHIPS, rows, cols), BF16),
        compiler_params=_params(("arbitrary",)))(slot.reshape(1), src)


FLIP_C = [(0, 0, 1)]
FLIP_XY = [(1, 0, 0), (0, 1, 0), (1, 1, 0)]
FLIP_ALL = [(fx, fy, fc) for fx in (0, 1) for fy in (0, 1) for fc in (0, 1) if (fx, fy, fc) != (0, 0, 0)]


def _chip(dev):
    return 2 * dev[0] + dev[1]


def _device(dev):
    return 4 * dev[0] + 2 * dev[1] + dev[2]


def _exchange(name, srcs, dst_shapes, rels, src_view, dst_view, own_view=None, in_place=False):
    n, n_rel = len(srcs), len(rels)

    def body(*refs):
        src_refs, dst_refs = refs[:n], refs[n:2 * n]
        send_sems, recv_sems, own_sems = refs[2 * n:]
        me = (lax.axis_index("x"), lax.axis_index("y"), lax.axis_index("c"))
        peers = [tuple(1 - m if f else m for m, f in zip(me, rel)) for rel in rels]

        def copy(i, k, sender, receiver):
            return pltpu.make_async_remote_copy(
                src_ref=src_view(i, src_refs[i], sender, receiver), dst_ref=dst_view(i, dst_refs[i], sender, k),
                send_sem=send_sems.at[i * n_rel + k], recv_sem=recv_sems.at[i * n_rel + k], device_id=receiver,
                device_id_type=pl.DeviceIdType.MESH)

        sends = [copy(i, k, me, peer) for i in range(n) for k, peer in enumerate(peers)]
        for cp in sends:
            cp.start()
        mine = []
        if own_view is not None:
            for i in range(n):
                frm, to = own_view(i, src_refs[i], dst_refs[i], me)
                mine.append(pltpu.make_async_copy(frm, to, own_sems.at[i]))
                mine[-1].start()
        for i in range(n):
            for k, peer in enumerate(peers):
                copy(i, k, peer, me).wait_recv()
        for cp in sends:
            cp.wait_send()
        for cp in mine:
            cp.wait()

    any_spec = pl.BlockSpec(memory_space=pl.ANY)
    return pl.pallas_call(
        body, name=name, out_shape=[jax.ShapeDtypeStruct(s, d) for s, d in dst_shapes],
        in_specs=[any_spec] * n, out_specs=[any_spec] * n,
        input_output_aliases={i: i for i in range(n)} if in_place else {},
        scratch_shapes=[pltpu.SemaphoreType.DMA((n * n_rel,)), pltpu.SemaphoreType.DMA((n * n_rel,)),
                        pltpu.SemaphoreType.DMA((n,))])(*srcs)


def _gather_weights(bufs, name):
    n = len(bufs)
    halves = [b.shape[1] // 2 for b in bufs]

    def body(*refs):
        src_refs, dst_refs = refs[:n], refs[n:2 * n]
        ici_send, ici_recv, d2d_send, d2d_recv = refs[2 * n:]
        x, y, c = lax.axis_index("x"), lax.axis_index("y"), lax.axis_index("c")
        sibling = (x, y, 1 - c)
        chips = [(1 - x, y), (x, 1 - y), (1 - x, 1 - y)]

        def rows(i, chip, core):
            return dst_refs[i].at[2 * chip[0] + chip[1], pl.ds(core * halves[i], halves[i]), :]

        def over_ici(i, k, src, chip_from, to):
            return pltpu.make_async_remote_copy(
                src_ref=src, dst_ref=rows(i, chip_from, c), send_sem=ici_send.at[3 * i + k],
                recv_sem=ici_recv.at[3 * i + k], device_id=to, device_id_type=pl.DeviceIdType.MESH)

        def over_d2d(i, k, core):
            return pltpu.make_async_remote_copy(
                src_ref=rows(i, chips[k], core), dst_ref=rows(i, chips[k], core), send_sem=d2d_send.at[3 * i + k],
                recv_sem=d2d_recv.at[3 * i + k], device_id=sibling, device_id_type=pl.DeviceIdType.MESH)

        sends = []
        for i in range(n):
            my_half = src_refs[i].at[2 * x + y, pl.ds(c * halves[i], halves[i]), :]
            for k, chip in enumerate(chips):
                sends.append(over_ici(i, k, my_half, (x, y), (*chip, c)))
                sends[-1].start()
        passed = []
        for i in range(n):
            for k, chip in enumerate(chips):
                over_ici(i, k, rows(i, chip, c), chip, (x, y, c)).wait_recv()
                passed.append(over_d2d(i, k, c))
                passed[-1].start()
        for i in range(n):
            for k in range(len(chips)):
                over_d2d(i, k, 1 - c).wait_recv()
        for cp in sends + passed:
            cp.wait_send()

    any_spec = pl.BlockSpec(memory_space=pl.ANY)
    return pl.pallas_call(
        body, name=name, out_shape=[jax.ShapeDtypeStruct(b.shape, b.dtype) for b in bufs],
        in_specs=[any_spec] * n, out_specs=[any_spec] * n, input_output_aliases={i: i for i in range(n)},
        scratch_shapes=[pltpu.SemaphoreType.DMA((3 * n,))] * 4)(*bufs)


def _pack(parts, dtype, multiple):
    flat = jnp.concatenate([p.reshape(-1).astype(dtype) for p in parts])
    pad = (-flat.shape[0]) % multiple
    return jnp.pad(flat, (0, pad)) if pad else flat


def _unpack_shards(packs, names, shapes, axes):
    out, off = {}, 0
    for nme in names:
        sz = math.prod(shapes[nme])
        out[nme] = jnp.concatenate([packs[j, off:off + sz].reshape(shapes[nme]) for j in range(N_CHIPS)], axis=axes[nme])
        off += sz
    return out


def _shards_of(full, axis):
    sz = full.shape[axis] // N_CHIPS
    return [lax.slice_in_dim(full, j * sz, (j + 1) * sz, axis=axis) for j in range(N_CHIPS)]


def kernel(x, a_norm_w, a_in_proj, a_conv_w, a_conv_b, a_dt_bias, a_A_log, a_D, a_gnorm_w, a_out_proj, kv_norm_w, w_kvf, b_f, k_norm_w, b_norm_w, w_q, q_norm_w, w_o, ffn_norm_w, w_gate_up, w_down, loss_target, m_a_norm_w, m_a_in_proj, m_a_conv_w, m_a_conv_b, m_a_dt_bias, m_a_A_log, m_a_D, m_a_gnorm_w, m_a_out_proj, m_kv_norm_w, m_w_kvf, m_b_f, m_k_norm_w, m_b_norm_w, m_w_q, m_q_norm_w, m_w_o, m_ffn_norm_w, m_w_gate_up, m_w_down, v_a_norm_w, v_a_in_proj, v_a_conv_w, v_a_conv_b, v_a_dt_bias, v_a_A_log, v_a_D, v_a_gnorm_w, v_a_out_proj, v_kv_norm_w, v_w_kvf, v_b_f, v_k_norm_w, v_b_norm_w, v_w_q, v_q_norm_w, v_w_o, v_ffn_norm_w, v_w_gate_up, v_w_down):
    args = locals()
    w = {n: args[n] for n in WEIGHTS}
    mom = {n: args['m_' + n] for n in WEIGHTS}
    var = {n: args['v_' + n] for n in WEIGHTS}
    shapes = {n: w[n].shape for n in WEIGHTS}

    cx, cy, cc = lax.axis_index("x"), lax.axis_index("y"), lax.axis_index("c")
    my_chip = 2 * cx + cy

    xs_in = x[0]
    target = loss_target[0]
    s_len, d_model = xs_in.shape
    n_heads_ssm = a_dt_bias.shape[-1]
    d_inner = n_heads_ssm * SSM_HEAD_DIM
    d_xbc = a_conv_w.shape[-1] * N_CHIPS
    d_state = (d_xbc - d_inner) // (2 * SSM_GROUPS)
    kh = n_heads_ssm // SSM_GROUPS
    n_att = b_f.shape[0]
    d_att = n_att * ATT_HEAD_DIM
    d_ff = w_down.shape[1] * N_CHIPS

    small_names = list(SMALL_SHARDED)
    sp = _pack([w[n] for n in small_names], F32, PACK_COLS).reshape(-1, PACK_COLS)
    sp_all = _exchange("gather_small", [sp], [((N_CHIPS,) + sp.shape, F32)], FLIP_XY,
                       lambda i, r, me, peer: r, lambda i, r, sender, k: r.at[_chip(sender)],
                       lambda i, s, d, me: (s, d.at[_chip(me)]))[0]
    full = _unpack_shards(sp_all.reshape(N_CHIPS, -1), small_names, shapes, SMALL_SHARDED)
    for n in SMALL:
        if n not in full:
            full[n] = w[n]

    big2d = {'in': a_in_proj[0], 'out': a_out_proj[0], 'gu0': w_gate_up[0], 'dn0': w_down[0], 'kvf': w_kvf,
             'q': w_q[0], 'o': w_o[0], 'gu1': w_gate_up[1], 'dn1': w_down[1]}
    big_keys = list(big2d)
    slotted = [_cast_into_slot(big2d[k], my_chip, "cast_" + k) for k in big_keys]
    gathered = dict(zip(big_keys, _gather_weights(slotted, "gather_big")))

    def pad_cols(a, width=LANES):
        return jnp.pad(a, ((0, 0), (0, width - a.shape[1])))

    w_in = jnp.concatenate([gathered['in'][j] for j in range(N_CHIPS)], axis=1)
    w_z, w_xbc, w_dt = w_in[:, :d_inner], w_in[:, d_inner:d_inner + d_xbc], pad_cols(w_in[:, d_inner + d_xbc:])
    w_out = gathered['out'].reshape(-1, d_model)
    w_kvf_full = jnp.concatenate([gathered['kvf'][j] for j in range(N_CHIPS)], axis=1)
    w_k, w_v, w_f = w_kvf_full[:, :d_att], w_kvf_full[:, d_att:2 * d_att], pad_cols(w_kvf_full[:, 2 * d_att:])
    w_qm, w_om = gathered['q'].reshape(-1, d_att), gathered['o'].reshape(-1, d_model)
    w_gu = [gathered['gu0'], gathered['gu1']]
    w_dn = [gathered['dn0'].reshape(-1, d_model), gathered['dn1'].reshape(-1, d_model)]
    conv_w, conv_b = full['a_conv_w'][0], full['a_conv_b']
    norm_a, gnorm = full['a_norm_w'], full['a_gnorm_w']
    dt_bias, a_log = pad_cols(a_dt_bias), pad_cols(a_A_log)
    d_exp = jnp.repeat(a_D, SSM_HEAD_DIM, axis=1)
    kv_nw, b_nw = kv_norm_w.reshape(1, -1), b_norm_w
    k_nw, q_nw = k_norm_w.reshape(1, -1), q_norm_w
    bf_pad = pad_cols(b_f.reshape(1, -1))
    ffn_nw = [ffn_norm_w[i:i + 1] for i in range(2)]

    def to_groups(a):
        g = a[:, :n_heads_ssm].reshape(s_len, SSM_GROUPS, kh).transpose(1, 0, 2)
        return jnp.pad(g, ((0, 0), (0, 0), (0, LANES - kh)))

    def to_groups_t(a):
        g = a[:, :n_heads_ssm].reshape(s_len, SSM_GROUPS, kh).transpose(1, 2, 0)
        return jnp.pad(g, ((0, 0), (0, 8 - kh), (0, 0)))

    def from_groups(col, row=None):
        a = col[:, :, :kh].transpose(1, 0, 2).reshape(s_len, n_heads_ssm)
        if row is not None:
            a = a + row[:, :kh, :].transpose(2, 0, 1).reshape(s_len, n_heads_ssm)
        return pad_cols(a)

    def heads_rows(a):
        return a.reshape(-1, ATT_HEAD_DIM)

    n1 = _rms_fwd(xs_in, norm_a, "norm_a")
    z = _mm(n1, w_z, 'nn', "in_z")
    xbc_raw = _mm(n1, w_xbc, 'nn', "in_xbc")
    dtraw = _mm(n1, w_dt, 'nn', "in_dt")
    xbc = _conv_fwd(xbc_raw, conv_w, conv_b, "conv")
    dt, acum = _dt_fwd(dtraw, dt_bias, a_log, "dt")
    dt_g, ac_g, act_g = to_groups(dt), to_groups(acum), to_groups_t(acum)
    y_ssd, states = _ssd_fwd(xbc, dt_g, ac_g, act_g, d_inner, d_state, kh, "ssd")
    yn = _gate_fwd(y_ssd, xbc, z, d_exp, gnorm, d_inner, "gate")
    h1 = _mm(yn, w_out, 'nn', "out_proj", add=xs_in)

    def ffn_fwd(h, i):
        nrm = _rms_fwd(h, ffn_nw[i], f"ffn{i}_norm")
        gu = _mm(nrm, w_gu[i], 'nn', f"ffn{i}_up", shards=True)
        act = _swiglu_fwd(gu, d_ff, f"ffn{i}_act")
        return nrm, gu, act, _mm(act, w_dn[i], 'nn', f"ffn{i}_down", add=h)

    n2, gu0, act0, h2 = ffn_fwd(h1, 0)
    nkv = _rms_fwd(h2, kv_nw, "kv_norm")
    k_raw = _mm(nkv, w_k, 'nn', "proj_k")
    v_att = _mm(nkv, w_v, 'nn', "proj_v", out_dtype=BF16)
    f_raw = _mm(nkv, w_f, 'nn', "proj_f")
    k_att = _rms_fwd(heads_rows(k_raw), k_nw, "k_norm", tm=1024).reshape(s_len, d_att)
    cum = _forget_fwd(f_raw, bf_pad, "forget")
    cq_rep = jnp.repeat(cum[:, :n_att], ATT_HEAD_DIM, axis=1)
    tb = min(ATT_BLOCK, s_len)
    ck = cum[:, :n_att].T.reshape(n_att, s_len // tb, 1, tb)
    n3 = _rms_fwd(h2, b_nw, "b_norm")
    q_raw = _mm(n3, w_qm, 'nn', "proj_q")
    q_att = _rms_fwd(heads_rows(q_raw), q_nw, "q_norm", tm=1024).reshape(s_len, d_att)
    o_att, lse_rep = _attn_fwd(q_att, k_att, v_att, cq_rep, ck, "attn")
    h3 = _mm(o_att, w_om, 'nn', "proj_o", add=h2)
    n4, gu1, act1, h4 = ffn_fwd(h3, 1)

    def loss_fn(h, t):
        err = h - t
        sq = jnp.sum(jnp.sum(err * err, axis=1, keepdims=True), axis=0, keepdims=True)
        return err * (1.0 / d_model), sq * (0.5 / d_model)

    dh4, loss_part = _rowwise(loss_fn, "loss", [h4, target], outs=[(d_model, F32)], accs=[(1, 1)])

    def ffn_bwd(dh, h, nrm, gu, act, i):
        dact = _mm(dh, w_dn[i], 'nt', f"ffn{i}_down_dx")
        g_dn = _mm(act, dh, 'tn', f"ffn{i}_down_dw", out_dtype=BF16)
        dgu = _swiglu_bwd(gu, dact, d_ff, f"ffn{i}_act_bwd")
        dn = _mm(dgu, w_gu[i], 'nt', f"ffn{i}_up_dx", shards=True)
        g_gu = _mm(nrm, dgu, 'tn', f"ffn{i}_up_dw", out_dtype=BF16, shards=True)
        dh_new, g_nw = _rms_bwd(h, ffn_nw[i], dn, f"ffn{i}_norm_bwd", extra=dh)
        return dh_new, g_dn, g_gu, g_nw

    dh3, g_dn1, g_gu1, g_fnw1 = ffn_bwd(dh4, h3, n4, gu1, act1, 1)
    do_att = _mm(dh3, w_om, 'nt', "proj_o_dx", out_dtype=BF16)
    g_wo = _mm(o_att, dh3, 'tn', "proj_o_dw", out_dtype=BF16)

    def delta_fn(a, b):
        return jnp.broadcast_to(jnp.sum(a.astype(F32) * b.astype(F32), axis=1, keepdims=True), a.shape)

    delta_rep = _rowwise(delta_fn, "attn_delta", [heads_rows(do_att), heads_rows(o_att)],
                         outs=[(ATT_HEAD_DIM, F32)], tm=1024)[0].reshape(s_len, d_att)
    dq_att, dk_att, dv_att, dcq_rep, dck = _attn_bwd(q_att, k_att, v_att, do_att, lse_rep, delta_rep, cq_rep, ck,
                                                     "attn_bwd")
    dq_raw, g_qnw = _rms_bwd(heads_rows(q_raw), q_nw, heads_rows(dq_att), "q_norm_bwd", out_dtype=BF16, tm=1024)
    dq_raw = dq_raw.reshape(s_len, d_att)
    dn3 = _mm(dq_raw, w_qm, 'nt', "proj_q_dx")
    g_wq = _mm(n3, dq_raw, 'tn', "proj_q_dw", out_dtype=BF16)
    dh2, g_bnw = _rms_bwd(h2, b_nw, dn3, "b_norm_bwd", extra=dh3)
    dk_raw, g_knw = _rms_bwd(heads_rows(k_raw), k_nw, heads_rows(dk_att), "k_norm_bwd", out_dtype=BF16, tm=1024)
    dk_raw = dk_raw.reshape(s_len, d_att)
    dcum = pad_cols(dcq_rep.reshape(s_len, n_att, ATT_HEAD_DIM)[:, :, 0] + dck.reshape(n_att, s_len).T)
    df_raw, g_bf = _forget_bwd(f_raw, bf_pad, dcum, "forget_bwd")
    dnkv = _mm(dk_raw, w_k, 'nt', "proj_k_dx")
    dnkv = _mm(dv_att, w_v, 'nt', "proj_v_dx", add=dnkv)
    dnkv = _mm(df_raw, w_f, 'nt', "proj_f_dx", add=dnkv)
    g_wk = _mm(nkv, dk_raw, 'tn', "proj_k_dw", out_dtype=BF16)
    g_wv = _mm(nkv, dv_att, 'tn', "proj_v_dw", out_dtype=BF16)
    g_wf = _mm(nkv, df_raw, 'tn', "proj_f_dw", out_dtype=BF16)
    dh2, g_kvnw = _rms_bwd(h2, kv_nw, dnkv, "kv_norm_bwd", extra=dh2)
    dh1, g_dn0, g_gu0, g_fnw0 = ffn_bwd(dh2, h1, n2, gu0, act0, 0)

    dyn = _mm(dh1, w_out, 'nt', "out_proj_dx")
    g_wout = _mm(yn, dh1, 'tn', "out_proj_dw", out_dtype=BF16)
    dy_ssd, dxs_skip, dz, g_dexp, g_gnorm = _gate_bwd(y_ssd, xbc, z, d_exp, gnorm, dyn, d_inner, "gate_bwd")
    dxs, d_b, d_c, ddt_g, dacc_g, dacr_g = _ssd_bwd(xbc, dt_g, ac_g, act_g, states, dy_ssd, dxs_skip, d_inner,
                                                    d_state, kh, "ssd_bwd")
    dxbc_act = jnp.concatenate([dxs, d_b, d_c], axis=1)
    du, g_convw, g_convb = _conv_bwd(xbc_raw, conv_w, conv_b, dxbc_act, "conv_bwd")
    draw, g_dtb, g_alog = _dt_bwd(dtraw, dt_bias, a_log, from_groups(ddt_g), from_groups(dacc_g, dacr_g), "dt_bwd")
    dn1 = _mm(dz, w_z, 'nt', "in_z_dx")
    dn1 = _mm(du, w_xbc, 'nt', "in_xbc_dx", add=dn1)
    dn1 = _mm(draw, w_dt, 'nt', "in_dt_dx", add=dn1)
    g_wz = _mm(n1, dz, 'tn', "in_z_dw", out_dtype=BF16)
    g_wxbc = _mm(n1, du, 'tn', "in_xbc_dw", out_dtype=BF16)
    g_wdt = _mm(n1, draw, 'tn', "in_dt_dw", out_dtype=BF16)
    dx, g_norm_a = _rms_bwd(xs_in, norm_a, dn1, "norm_a_bwd", extra=dh1)

    n_f = n_att

    def col_shards(parts):
        fullw = jnp.concatenate(parts, axis=1)
        return fullw.reshape(fullw.shape[0], N_CHIPS, -1).transpose(1, 0, 2)

    def row_shards(a):
        return a.reshape(N_CHIPS, -1, a.shape[1])

    g_big = {
        'in': col_shards([g_wz, g_wxbc, g_wdt[:, :n_heads_ssm]]), 'out': row_shards(g_wout),
        'gu0': g_gu0, 'dn0': row_shards(g_dn0), 'kvf': col_shards([g_wk, g_wv, g_wf[:, :n_f]]),
        'q': row_shards(g_wq), 'o': row_shards(g_wo), 'gu1': g_gu1, 'dn1': row_shards(g_dn1),
    }
    g_small = {
        'a_norm_w': g_norm_a, 'a_conv_w': g_convw[None], 'a_conv_b': g_convb,
        'a_dt_bias': g_dtb[:, :n_heads_ssm], 'a_A_log': g_alog[:, :n_heads_ssm],
        'a_D': g_dexp.reshape(n_heads_ssm, SSM_HEAD_DIM).sum(axis=1).reshape(1, -1), 'a_gnorm_w': g_gnorm,
        'kv_norm_w': g_kvnw.reshape(-1), 'b_f': g_bf[0, :n_f], 'k_norm_w': g_knw.reshape(-1), 'b_norm_w': g_bnw,
        'q_norm_w': g_qnw, 'ffn_norm_w': jnp.concatenate([g_fnw0, g_fnw1], axis=0),
    }

    sg = _pack([g_small[n] for n in SMALL] + [loss_part], F32, 8 * PACK_COLS).reshape(-1, PACK_COLS)
    sg_all = _exchange("reduce_small", [sg], [((2 * N_CHIPS,) + sg.shape, F32)], FLIP_ALL,
                       lambda i, r, me, peer: r, lambda i, r, sender, k: r.at[_device(sender)],
                       lambda i, s, d, me: (s, d.at[_device(me)]))[0]
    sg_sum = _sum_arrays([sg_all[d] for d in range(2 * N_CHIPS)], F32, "reduce_small_sum").reshape(-1)
    red_small, off = {}, 0
    for n in SMALL:
        shp = g_small[n].shape
        red_small[n] = sg_sum[off:off + math.prod(shp)].reshape(shp)
        off += math.prod(shp)
    loss = sg_sum[off]

    red_keys = big_keys[::-1]
    g_list = [g_big[k] for k in red_keys]
    hrs = [g.shape[1] // 2 for g in g_list]
    half_shapes = [((N_CHIPS, hr, g.shape[2]), BF16) for g, hr in zip(g_list, hrs)]
    got = _exchange("reduce_big_d2d", g_list, half_shapes, FLIP_C,
                    lambda i, r, me, peer: r.at[:, pl.ds((1 - me[2]) * hrs[i], hrs[i]), :],
                    lambda i, r, sender, k: r)
    chip_sums = [_sum_half(g, a, cc, "reduce_big_sum2_" + k) for g, a, k in zip(g_list, got, red_keys)]
    parts = _exchange("reduce_big_ici", chip_sums,
                      [((len(FLIP_XY), hr, g.shape[2]), BF16) for g, hr in zip(g_list, hrs)], FLIP_XY,
                      lambda i, r, me, peer: r.at[_chip(peer)], lambda i, r, sender, k: r.at[k])
    half_sums = [_sum_parts(t, p, my_chip, "reduce_big_sum4_" + k) for t, p, k in zip(chip_sums, parts, red_keys)]
    others = _exchange("reduce_big_back", half_sums, [(h.shape, F32) for h in half_sums], FLIP_C,
                       lambda i, r, me, peer: r, lambda i, r, sender, k: r)
    mine_of, theirs_of = dict(zip(red_keys, half_sums)), dict(zip(red_keys, others))

    grads, delta, new_m, new_v = {}, {}, {}, {}
    layers_of = {'a_in_proj': ['in'], 'a_out_proj': ['out'], 'w_kvf': ['kvf'], 'w_q': ['q'], 'w_o': ['o'],
                 'w_gate_up': ['gu0', 'gu1'], 'w_down': ['dn0', 'dn1']}
    for n, keys in layers_of.items():
        three_d = (len(keys),) + tuple(shapes[n][-2:])
        res = None
        for layer, k in enumerate(keys):
            res = _adamw_big(w[n].reshape(three_d), mom[n].reshape(three_d), var[n].reshape(three_d), mine_of[k],
                             theirs_of[k], cc, layer, res, "adamw_" + k)
        grads[n], delta[n], new_m[n], new_v[n] = [r.reshape(shapes[n]) for r in res]
    for n in SMALL:
        if n in SMALL_SHARDED:
            ax = SMALL_SHARDED[n]
            grads[n] = lax.dynamic_slice_in_dim(red_small[n], my_chip * shapes[n][ax], shapes[n][ax], axis=ax)
        else:
            grads[n] = red_small[n]

    packed = [_pack([src[n] for n in SMALL], F32, 8 * LANES).reshape(-1, LANES) for src in (w, grads, mom, var)]
    small_out = _adamw(*packed, "adamw_small")
    for store, flat in zip((delta, new_m, new_v), small_out):
        flat, off = flat.reshape(-1), 0
        for n in SMALL:
            sz = math.prod(shapes[n])
            store[n] = flat[off:off + sz].reshape(shapes[n])
            off += sz

    return (loss, dx[None], *[grads[n] for n in WEIGHTS], *[delta[n] for n in WEIGHTS],
            *[new_m[n] for n in WEIGHTS], *[new_v[n] for n in WEIGHTS])
```

```python
import functools
import math

import jax
import jax.numpy as jnp
from jax import lax
from jax.experimental import pallas as pl
from jax.experimental.pallas import tpu as pltpu

F32, BF16 = jnp.float32, jnp.bfloat16
EPS = 1e-6
SSM_HEAD_DIM = 64
SSM_GROUPS = 8
SSD_CHUNK = 128
ATT_HEAD_DIM = 128
LANES = 128
N_CHIPS = 4
NEG = -1e30
ADAM_LR, ADAM_B1, ADAM_B2, ADAM_EPS, ADAM_WD, ADAM_STEP = 0.001, 0.9, 0.999, 1e-08, 0.01, 10
VMEM_LIMIT_BYTES = 56 * 1024 * 1024
PACK_COLS = 1024
PACK_ROWS = 256

NN = ((1,), (0,))
NT = ((1,), (1,))
TN = ((0,), (0,))

WEIGHTS = ['a_norm_w', 'a_in_proj', 'a_conv_w', 'a_conv_b', 'a_dt_bias', 'a_A_log', 'a_D', 'a_gnorm_w', 'a_out_proj',
           'kv_norm_w', 'w_kvf', 'b_f', 'k_norm_w', 'b_norm_w', 'w_q', 'q_norm_w', 'w_o', 'ffn_norm_w', 'w_gate_up',
           'w_down']
BIG = {'a_in_proj': 2, 'a_out_proj': 1, 'w_kvf': 1, 'w_q': 1, 'w_o': 1, 'w_gate_up': 2, 'w_down': 1}
SMALL_SHARDED = {'a_norm_w': 1, 'a_conv_w': 2, 'a_conv_b': 1, 'a_gnorm_w': 1}
SMALL = [n for n in WEIGHTS if n not in BIG]


def _dot(a, b, dims):
    return lax.dot_general(a, b, (dims, ((), ())), preferred_element_type=F32)


def _params(sem=None):
    return pltpu.CompilerParams(dimension_semantics=sem, vmem_limit_bytes=VMEM_LIMIT_BYTES)


def _tile(dim, cap):
    for t in (1408, 1024, 512, 256, 128):
        if t <= cap and dim % t == 0:
            return t
    return dim


class _Beside:
    def __init__(self, operands, results, sem_sizes, start, finish):
        self.operands, self.results, self.sem_sizes = list(operands), list(results), list(sem_sizes)
        self.start, self.finish = start, finish


def _host(kern, beside, *, name, grid, in_specs, out_specs, out_shape, scratch_shapes, semantics, args):
    single = not isinstance(out_shape, (list, tuple))
    out_specs = [out_specs] if single else list(out_specs)
    out_shape = [out_shape] if single else list(out_shape)
    if beside is None:
        res = pl.pallas_call(kern, name=name, grid=grid, in_specs=in_specs, out_specs=out_specs, out_shape=out_shape,
                             scratch_shapes=scratch_shapes, compiler_params=_params(semantics))(*args)
        return (res[0] if single else res), []
    n_in, n_out, n_scr = len(in_specs), len(out_specs), len(scratch_shapes)
    nb_in, nb_out = len(beside.operands), len(beside.results)

    def body(*refs):
        ins, b_ins = refs[:n_in], refs[n_in:n_in + nb_in]
        outs = refs[n_in + nb_in:n_in + nb_in + n_out]
        b_outs = refs[n_in + nb_in + n_out:n_in + nb_in + n_out + nb_out]
        rest = refs[n_in + nb_in + n_out + nb_out:]
        scr, sems = rest[:n_scr], rest[n_scr:]
        ids = [pl.program_id(a) for a in range(len(grid))]
        first = functools.reduce(jnp.logical_and, [i == 0 for i in ids])
        last = functools.reduce(jnp.logical_and, [i == g - 1 for i, g in zip(ids, grid)])

        @pl.when(first)
        def _():
            beside.start(b_ins, b_outs, sems)

        kern(*ins, *outs, *scr)

        @pl.when(last)
        def _():
            beside.finish(b_ins, b_outs, sems)

    any_spec = pl.BlockSpec(memory_space=pl.ANY)
    res = pl.pallas_call(
        body, name=name, grid=grid, in_specs=list(in_specs) + [any_spec] * nb_in,
        out_specs=out_specs + [any_spec] * nb_out,
        out_shape=out_shape + [jax.ShapeDtypeStruct(s, d) for s, d, _ in beside.results],
        input_output_aliases={n_in + op: n_out + r for r, (_, _, op) in enumerate(beside.results) if op is not None},
        scratch_shapes=list(scratch_shapes) + [pltpu.SemaphoreType.DMA((k,)) for k in beside.sem_sizes],
        compiler_params=_params(("arbitrary",) * len(grid)))(*args, *beside.operands)
    mine = res[:n_out]
    return (mine[0] if single else mine), list(res[n_out:])


def _alone(beside, name):
    return _host(lambda: None, beside, name=name, grid=(1,), in_specs=[], out_specs=[], out_shape=[],
                 scratch_shapes=[], semantics=("arbitrary",), args=[])[1]


def _mm(a, b, mode, name, out_dtype=F32, add=None, shards=False, beside=None):
    if mode == 'nn':
        (m, k), n = a.shape, (b.shape[2] * N_CHIPS if shards else b.shape[1])
    elif mode == 'nt':
        (m, k), n = a.shape, (b.shape[1] if shards else b.shape[0])
    else:
        (k, m), n = a.shape, b.shape[1]
    per_chip = (k if mode == 'nt' else n) // N_CHIPS
    tm = _tile(m, 1024)
    tn = _tile(per_chip if shards and mode != 'nt' else n, 1408 if shards else 1024)
    tk = _tile(per_chip, 1408) if shards and mode == 'nt' else _tile(k, 512)
    nk = k // tk
    a_spec = pl.BlockSpec((tk, tm), lambda i, j, q: (q, i)) if mode == 'tn' else pl.BlockSpec((tm, tk), lambda i, j, q: (i, q))
    b_spec = pl.BlockSpec((tn, tk), lambda i, j, q: (j, q)) if mode == 'nt' else pl.BlockSpec((tk, tn), lambda i, j, q: (q, j))
    o_spec = pl.BlockSpec((tm, tn), lambda i, j, q: (i, j))
    out_struct = jax.ShapeDtypeStruct((m, n), out_dtype)
    if shards:
        per = per_chip // (tk if mode == 'nt' else tn)
        if mode == 'nn':
            b_spec = pl.BlockSpec((None, tk, tn), lambda i, j, q: (j // per, q, j % per))
        elif mode == 'nt':
            b_spec = pl.BlockSpec((None, tn, tk), lambda i, j, q: (q // per, j, q % per))
        else:
            out_struct = jax.ShapeDtypeStruct((N_CHIPS, m, per_chip), out_dtype)
    out_spec = pl.BlockSpec((None, tm, tn), lambda i, j, q: (j // per, i, j % per)) if shards and mode == 'tn' else o_spec
    dims = {'nn': NN, 'nt': NT, 'tn': TN}[mode]

    def kern(*refs):
        a_ref, b_ref = refs[0], refs[1]
        o_ref, acc = refs[-2], refs[-1]
        q = pl.program_id(2)

        @pl.when(q == 0)
        def _():
            acc[...] = jnp.zeros_like(acc)

        acc[...] += _dot(a_ref[...].astype(BF16), b_ref[...].astype(BF16), dims)

        @pl.when(q == nk - 1)
        def _():
            r = acc[...]
            if add is not None:
                r = r + refs[2][...]
            o_ref[...] = r.astype(o_ref.dtype)

    ins, specs = [a, b], [a_spec, b_spec]
    if add is not None:
        ins.append(add)
        specs.append(o_spec)
    res, extra = _host(kern, beside, name=name, grid=(m // tm, n // tn, nk), in_specs=specs, out_specs=out_spec,
                       out_shape=out_struct, scratch_shapes=[pltpu.VMEM((tm, tn), F32)],
                       semantics=("parallel", "parallel", "arbitrary"), args=ins)
    return res if beside is None else (res, extra)


def _rowwise(fn, name, rows, bcast=(), outs=(), accs=(), tm=256):
    rows = [r if isinstance(r, tuple) else (r, r.shape[1], 0) for r in rows]
    n_rows = rows[0][0].shape[0]
    tm = min(tm, n_rows)
    assert n_rows % tm == 0, (name, n_rows, tm)
    n_in, n_out = len(rows) + len(bcast), len(outs)
    in_specs = [pl.BlockSpec((tm, w), functools.partial(lambda i, cb: (i, cb), cb=cb)) for _, w, cb in rows]
    in_specs += [pl.BlockSpec(b.shape, lambda i: (0, 0)) for b in bcast]
    out_specs = [pl.BlockSpec((tm, w), lambda i: (i, 0)) for w, _ in outs]
    out_specs += [pl.BlockSpec(s, lambda i: (0, 0)) for s in accs]
    out_shape = [jax.ShapeDtypeStruct((n_rows, w), d) for w, d in outs] + [jax.ShapeDtypeStruct(s, F32) for s in accs]

    def kern(*refs):
        vals = fn(*[r[...] for r in refs[:n_in]])
        vals = vals if isinstance(vals, (tuple, list)) else (vals,)
        o_refs = refs[n_in:]
        for r, v in zip(o_refs[:n_out], vals[:n_out]):
            r[...] = v.astype(r.dtype)
        if accs:
            @pl.when(pl.program_id(0) == 0)
            def _():
                for r in o_refs[n_out:]:
                    r[...] = jnp.zeros_like(r)

            for r, v in zip(o_refs[n_out:], vals[n_out:]):
                r[...] += v

    return pl.pallas_call(
        kern, name=name, grid=(n_rows // tm,), in_specs=in_specs, out_specs=out_specs, out_shape=out_shape,
        compiler_params=_params(("arbitrary",)))(*[r[0] for r in rows], *bcast)


def _rms(x, w):
    xf = x.astype(F32)
    return xf * lax.rsqrt(jnp.mean(xf * xf, axis=-1, keepdims=True) + EPS) * w


def _rms_fwd(x, w, name, tm=256):
    return _rowwise(_rms, name, [x], [w], outs=[(x.shape[1], BF16)], tm=tm)[0]


def _rms_bwd(x, w, dy, name, extra=None, out_dtype=F32, tm=256):
    def fn(x, dy, *rest):
        _, vjp = jax.vjp(_rms, x, rest[-1])
        dx, dw = vjp(dy.astype(F32))
        if extra is not None:
            dx = dx + rest[0]
        return dx, dw

    rows = [x, dy] + ([extra] if extra is not None else [])
    return _rowwise(fn, name, rows, [w], outs=[(x.shape[1], out_dtype)], accs=[w.shape], tm=tm)


def _sigmoid(x):
    return 1.0 / (1.0 + jnp.exp(-x))


def _softplus(x):
    return jnp.maximum(x, 0.0) + jnp.log(1.0 + jnp.exp(-jnp.abs(x)))


def _swiglu_fwd(gu, d_ff, name):
    def fn(g, u):
        return g * _sigmoid(g) * u

    return _rowwise(fn, name, [(gu, d_ff, 0), (gu, d_ff, 1)], outs=[(d_ff, BF16)], tm=128)[0]


def _swiglu_bwd(gu, dact, d_ff, name):
    def fn(g, u, da):
        s = _sigmoid(g)
        dg = da * u * s * (1.0 + g * (1.0 - s))
        du = da * g * s
        return jnp.concatenate([dg, du], axis=1)

    return _rowwise(fn, name, [(gu, d_ff, 0), (gu, d_ff, 1), dact], outs=[(2 * d_ff, BF16)], tm=128)[0]


def _cumsum_rows(v, reverse=False):
    n = v.shape[0]
    row = lax.broadcasted_iota(jnp.int32, v.shape, 0)
    sh = 1
    while sh < n:
        if reverse:
            v = v + jnp.where(row < n - sh, pltpu.roll(v, n - sh, 0), 0.0)
        else:
            v = v + jnp.where(row >= sh, pltpu.roll(v, sh, 0), 0.0)
        sh *= 2
    return v


def _conv_fwd(u, w, b, name):
    s, c = u.shape
    kw = w.shape[0]
    tc = _tile(c, 128)

    def kern(u_ref, w_ref, b_ref, o_ref):
        uu = u_ref[...]
        row = lax.broadcasted_iota(jnp.int32, uu.shape, 0)
        acc = jnp.zeros_like(uu) + b_ref[...]
        for k in range(kw):
            sh = kw - 1 - k
            uk = uu if sh == 0 else jnp.where(row >= sh, pltpu.roll(uu, sh, 0), 0.0)
            acc = acc + w_ref[pl.ds(k, 1), :] * uk
        o_ref[...] = acc * _sigmoid(acc)

    return pl.pallas_call(
        kern, name=name, grid=(c // tc,),
        in_specs=[pl.BlockSpec((s, tc), lambda j: (0, j)), pl.BlockSpec((kw, tc), lambda j: (0, j)),
                  pl.BlockSpec((1, tc), lambda j: (0, j))],
        out_specs=pl.BlockSpec((s, tc), lambda j: (0, j)), out_shape=jax.ShapeDtypeStruct((s, c), F32),
        compiler_params=_params(("parallel",)))(u, w, b)


def _conv_bwd(u, w, b, dact, name):
    s, c = u.shape
    kw = w.shape[0]
    tc = _tile(c, 128)

    def kern(u_ref, w_ref, b_ref, d_ref, du_ref, dw_ref, db_ref):
        uu = u_ref[...]
        row = lax.broadcasted_iota(jnp.int32, uu.shape, 0)
        shifted = []
        acc = jnp.zeros_like(uu) + b_ref[...]
        for k in range(kw):
            sh = kw - 1 - k
            uk = uu if sh == 0 else jnp.where(row >= sh, pltpu.roll(uu, sh, 0), 0.0)
            shifted.append(uk)
            acc = acc + w_ref[pl.ds(k, 1), :] * uk
        sg = _sigmoid(acc)
        dacc = d_ref[...] * sg * (1.0 + acc * (1.0 - sg))
        db_ref[...] = jnp.sum(dacc, axis=0, keepdims=True)
        du = jnp.zeros_like(uu)
        for k in range(kw):
            sh = kw - 1 - k
            dw_ref[pl.ds(k, 1), :] = jnp.sum(dacc * shifted[k], axis=0, keepdims=True)
            dk = dacc if sh == 0 else jnp.where(row < s - sh, pltpu.roll(dacc, s - sh, 0), 0.0)
            du = du + w_ref[pl.ds(k, 1), :] * dk
        du_ref[...] = du.astype(du_ref.dtype)

    col = lambda j: (0, j)
    return pl.pallas_call(
        kern, name=name, grid=(c // tc,),
        in_specs=[pl.BlockSpec((s, tc), col), pl.BlockSpec((kw, tc), col), pl.BlockSpec((1, tc), col),
                  pl.BlockSpec((s, tc), col)],
        out_specs=[pl.BlockSpec((s, tc), col), pl.BlockSpec((kw, tc), col), pl.BlockSpec((1, tc), col)],
        out_shape=[jax.ShapeDtypeStruct((s, c), BF16), jax.ShapeDtypeStruct((kw, c), F32),
                   jax.ShapeDtypeStruct((1, c), F32)],
        compiler_params=_params(("parallel",)))(u, w, b, dact)


def _dt_fwd(dtraw, bias, a_log, name):
    def fn(raw, bias, a_log):
        dt = _softplus(raw + bias)
        return dt, _cumsum_rows(dt * (-jnp.exp(a_log)))

    return _rowwise(fn, name, [dtraw], [bias, a_log], outs=[(LANES, F32), (LANES, F32)], tm=SSD_CHUNK)


def _dt_bwd(dtraw, bias, a_log, ddt, dacum, name):
    def fn(raw, ddt, dac, bias, a_log):
        z = raw + bias
        dt = _softplus(z)
        a_neg = -jnp.exp(a_log)
        da = _cumsum_rows(dac, reverse=True)
        draw = (ddt + da * a_neg) * _sigmoid(z)
        return draw, jnp.sum(draw, axis=0, keepdims=True), jnp.sum(da * dt, axis=0, keepdims=True) * a_neg

    return _rowwise(fn, name, [dtraw, ddt, dacum], [bias, a_log], outs=[(LANES, BF16)],
                    accs=[(1, LANES), (1, LANES)], tm=SSD_CHUNK)


def _ssd_pieces(xs, dt, ac, kh):
    l, gw = xs.shape
    lane_w = lax.broadcasted_iota(jnp.int32, (l, gw), 1)
    lane_k = lax.broadcasted_iota(jnp.int32, (l, LANES), 1)
    head = [(lane_w >= k * SSM_HEAD_DIM) & (lane_w < (k + 1) * SSM_HEAD_DIM) for k in range(kh)]

    def col(blk, k):
        return jnp.sum(jnp.where(lane_k == k, blk, 0.0), axis=1, keepdims=True)

    def expand(blk):
        acc = jnp.zeros((l, gw), F32)
        for k in range(kh):
            acc = jnp.where(head[k], col(blk, k), acc)
        return acc

    def collapse(wide):
        acc = jnp.zeros((l, LANES), F32)
        for k in range(kh):
            acc = jnp.where(lane_k == k, jnp.sum(jnp.where(head[k], wide, 0.0), axis=1, keepdims=True), acc)
        return acc

    return head, lane_k, col, expand, collapse


def _ssd_specs(l, gw, n, n_xs_blocks):
    g_axis = SSM_GROUPS
    return dict(
        xs=lambda cm: pl.BlockSpec((l, gw), lambda g, c: (cm(c), g)),
        b=lambda cm: pl.BlockSpec((l, n), lambda g, c: (cm(c), n_xs_blocks + g)),
        c=lambda cm: pl.BlockSpec((l, n), lambda g, c: (cm(c), n_xs_blocks + g_axis + g)),
        col=lambda cm: pl.BlockSpec((None, l, LANES), lambda g, c: (g, cm(c), 0)),
        row=lambda cm: pl.BlockSpec((None, 8, l), lambda g, c: (g, 0, cm(c))),
        state=lambda cm: pl.BlockSpec((None, None, n, gw), lambda g, c: (cm(c), g, 0, 0)),
    )


def _ssd_fwd(xbc, dt_g, ac_g, act_g, d_inner, n, kh, name, beside=None):
    s = xbc.shape[0]
    l, g_n = SSD_CHUNK, SSM_GROUPS
    gw, nc = d_inner // g_n, s // l
    sp = _ssd_specs(l, gw, n, d_inner // n)
    fwd = lambda c: c

    def kern(xs_ref, b_ref, c_ref, dt_ref, ac_ref, act_ref, y_ref, s0_ref, st):
        @pl.when(pl.program_id(1) == 0)
        def _():
            st[...] = jnp.zeros_like(st)

        s0 = st[...]
        s0_ref[...] = s0
        xs, ac = xs_ref[...], ac_ref[...]
        head, _, col, expand, _ = _ssd_pieces(xs, dt_ref[...], ac, kh)
        ace = expand(ac)
        x = xs * expand(dt_ref[...])
        xb, bb, cb_ = x.astype(BF16), b_ref[...].astype(BF16), c_ref[...].astype(BF16)
        cb = _dot(cb_, bb, NT)
        ri = lax.broadcasted_iota(jnp.int32, (l, l), 0)
        ci = lax.broadcasted_iota(jnp.int32, (l, l), 1)
        causal = ri >= ci
        y = _dot(cb_, s0.astype(BF16), NN) * jnp.exp(ace)
        for k in range(kh):
            seg = col(ac, k) - act_ref[pl.ds(k, 1), :]
            m = jnp.where(causal, cb * jnp.exp(jnp.where(causal, seg, 0.0)), 0.0)
            y = jnp.where(head[k], y + _dot(m.astype(BF16), xb, NN), y)
        y_ref[...] = y
        row_w = lax.broadcasted_iota(jnp.int32, (l, gw), 0)
        alast = jnp.sum(jnp.where(row_w == l - 1, ace, 0.0), axis=0, keepdims=True)
        st[...] = s0 * jnp.exp(alast) + _dot(bb, (jnp.exp(alast - ace) * x).astype(BF16), TN)

    return _host(
        kern, beside, name=name, grid=(g_n, nc),
        in_specs=[sp['xs'](fwd), sp['b'](fwd), sp['c'](fwd), sp['col'](fwd), sp['col'](fwd), sp['row'](fwd)],
        out_specs=[pl.BlockSpec((l, gw), lambda g, c: (c, g)), sp['state'](fwd)],
        out_shape=[jax.ShapeDtypeStruct((s, d_inner), F32), jax.ShapeDtypeStruct((nc, g_n, n, gw), F32)],
        scratch_shapes=[pltpu.VMEM((n, gw), F32)], semantics=("arbitrary", "arbitrary"),
        args=[xbc, xbc, xbc, dt_g, ac_g, act_g])


def _ssd_bwd(xbc, dt_g, ac_g, act_g, s0_all, dy, dxs_skip, d_inner, n, kh, name, beside=None):
    s = xbc.shape[0]
    l, g_n = SSD_CHUNK, SSM_GROUPS
    gw, nc = d_inner // g_n, s // l
    sp = _ssd_specs(l, gw, n, d_inner // n)
    rev = lambda c: nc - 1 - c

    def kern(xs_ref, b_ref, c_ref, dt_ref, ac_ref, act_ref, s0_ref, dy_ref, skip_ref,
             dxs_ref, db_ref, dc_ref, ddt_ref, dacc_ref, dacr_ref, dst):
        @pl.when(pl.program_id(1) == 0)
        def _():
            dst[...] = jnp.zeros_like(dst)

        dsn = dst[...]
        s0 = s0_ref[...]
        xs, ac, dy = xs_ref[...], ac_ref[...], dy_ref[...]
        head, lane_k, col, expand, collapse = _ssd_pieces(xs, dt_ref[...], ac, kh)
        ace, dte = expand(ac), expand(dt_ref[...])
        x = xs * dte
        xb, bb, cb_ = x.astype(BF16), b_ref[...].astype(BF16), c_ref[...].astype(BF16)
        s0b, dsnb, dyb = s0.astype(BF16), dsn.astype(BF16), dy.astype(BF16)
        cb = _dot(cb_, bb, NT)
        ri = lax.broadcasted_iota(jnp.int32, (l, l), 0)
        ci = lax.broadcasted_iota(jnp.int32, (l, l), 1)
        causal = ri >= ci
        row_w = lax.broadcasted_iota(jnp.int32, (l, gw), 0)
        e = jnp.exp(ace)
        alast = jnp.sum(jnp.where(row_w == l - 1, ace, 0.0), axis=0, keepdims=True)
        gdec = jnp.exp(alast)
        wt = jnp.exp(alast - ace)
        cs = _dot(cb_, s0b, NN)
        dcs = (dy * e).astype(BF16)
        d_c = _dot(dcs, s0b, NT)
        ds_off = _dot(cb_, dcs, TN)
        dace = dy * cs * e
        dalast = jnp.sum(dsn * s0, axis=0, keepdims=True) * gdec
        z = wt * x
        dz = _dot(bb, dsnb, NN)
        d_b = _dot(z.astype(BF16), dsnb, NT)
        dx = dz * wt
        t = dz * z
        dalast = dalast + jnp.sum(t, axis=0, keepdims=True)
        dace = dace - t
        dcb = jnp.zeros((l, l), F32)
        dac_col = jnp.zeros((l, LANES), F32)
        for k in range(kh):
            seg = col(ac, k) - act_ref[pl.ds(k, 1), :]
            dk = jnp.exp(jnp.where(causal, seg, 0.0))
            mk = jnp.where(causal, cb * dk, 0.0)
            dx = jnp.where(head[k], dx + _dot(mk.astype(BF16), dyb, TN), dx)
            dmk = _dot(jnp.where(head[k], dy, 0.0).astype(BF16), xb, NT)
            dcb = dcb + jnp.where(causal, dmk * dk, 0.0)
            dseg = dmk * mk
            dac_col = jnp.where(lane_k == k, jnp.sum(dseg, axis=1, keepdims=True), dac_col)
            dacr_ref[pl.ds(k, 1), :] = -jnp.sum(dseg, axis=0, keepdims=True)
        for k in range(kh, 8):
            dacr_ref[pl.ds(k, 1), :] = jnp.zeros((1, l), F32)
        dcbb = dcb.astype(BF16)
        dc_ref[...] = d_c + _dot(dcbb, bb, NN)
        db_ref[...] = d_b + _dot(dcbb, cb_, TN)
        dace = jnp.where(row_w == l - 1, dace + dalast, dace)
        dacc_ref[...] = dac_col + collapse(dace)
        ddt_ref[...] = collapse(dx * xs)
        dxs_ref[...] = dx * dte + skip_ref[...]
        dst[...] = dsn * gdec + ds_off

    return _host(
        kern, beside, name=name, grid=(g_n, nc),
        in_specs=[sp['xs'](rev), sp['b'](rev), sp['c'](rev), sp['col'](rev), sp['col'](rev), sp['row'](rev),
                  sp['state'](rev), pl.BlockSpec((l, gw), lambda g, c: (rev(c), g)),
                  pl.BlockSpec((l, gw), lambda g, c: (rev(c), g))],
        out_specs=[pl.BlockSpec((l, gw), lambda g, c: (rev(c), g)), pl.BlockSpec((l, n), lambda g, c: (rev(c), g)),
                   pl.BlockSpec((l, n), lambda g, c: (rev(c), g)), sp['col'](rev), sp['col'](rev), sp['row'](rev)],
        out_shape=[jax.ShapeDtypeStruct((s, d_inner), F32), jax.ShapeDtypeStruct((s, g_n * n), F32),
                   jax.ShapeDtypeStruct((s, g_n * n), F32), jax.ShapeDtypeStruct((g_n, s, LANES), F32),
                   jax.ShapeDtypeStruct((g_n, s, LANES), F32), jax.ShapeDtypeStruct((g_n, 8, s), F32)],
        scratch_shapes=[pltpu.VMEM((n, gw), F32)], semantics=("arbitrary", "arbitrary"),
        args=[xbc, xbc, xbc, dt_g, ac_g, act_g, s0_all, dy, dxs_skip])


def _gate(y, xs, z, d_exp, gw):
    t = (y + xs * d_exp) * (z * _sigmoid(z))
    width = t.shape[1]
    gsz = width // SSM_GROUPS
    lane = lax.broadcasted_iota(jnp.int32, t.shape, 1)
    t2 = t * t
    scale = jnp.zeros_like(t)
    for g in range(SSM_GROUPS):
        in_g = (lane >= g * gsz) & (lane < (g + 1) * gsz)
        ms = jnp.sum(jnp.where(in_g, t2, 0.0), axis=1, keepdims=True) * (1.0 / gsz)
        scale = jnp.where(in_g, lax.rsqrt(ms + EPS), scale)
    return t * scale * gw


def _gate_fwd(y, xbc, z, d_exp, gw, d_inner, name):
    return _rowwise(_gate, name, [y, (xbc, d_inner, 0), z], [d_exp, gw], outs=[(d_inner, BF16)], tm=128)[0]


def _gate_bwd(y, xbc, z, d_exp, gw, dyn, d_inner, name):
    def fn(y, xs, z, dyn, d_exp, gw):
        _, vjp = jax.vjp(_gate, y, xs, z, d_exp, gw)
        return vjp(dyn)

    return _rowwise(fn, name, [y, (xbc, d_inner, 0), z, dyn], [d_exp, gw],
                    outs=[(d_inner, F32), (d_inner, F32), (d_inner, BF16)], accs=[d_exp.shape, gw.shape], tm=64)


def _forget_fwd(fraw, b_f, name):
    def kern(f_ref, b_ref, o_ref):
        o_ref[...] = _cumsum_rows(-_softplus(-(f_ref[...] + b_ref[...])))

    return pl.pallas_call(kern, name=name, out_shape=jax.ShapeDtypeStruct(fraw.shape, F32),
                          compiler_params=_params())(fraw, b_f)


def _forget_bwd(fraw, b_f, dcum, name):
    def kern(f_ref, b_ref, d_ref, df_ref, db_ref):
        df = _cumsum_rows(d_ref[...], reverse=True) * _sigmoid(-(f_ref[...] + b_ref[...]))
        df_ref[...] = df.astype(df_ref.dtype)
        db_ref[...] = jnp.sum(df, axis=0, keepdims=True)

    return pl.pallas_call(
        kern, name=name,
        out_shape=[jax.ShapeDtypeStruct(fraw.shape, BF16), jax.ShapeDtypeStruct((1, fraw.shape[1]), F32)],
        compiler_params=_params())(fraw, b_f, dcum)


ATT_BLOCK = 512


def _attn_fwd(q, k, v, cq_rep, ck, name, beside=None):
    s, hd = q.shape
    h_n, d = hd // ATT_HEAD_DIM, ATT_HEAD_DIM
    tb = min(ATT_BLOCK, s)
    nb = s // tb
    scale = d ** -0.5

    def kern(q_ref, k_ref, v_ref, cq_ref, ck_ref, o_ref, lse_ref):
        i = pl.program_id(1)
        qq = q_ref[...]
        cq = jnp.max(cq_ref[...], axis=1, keepdims=True)
        rowpos = i * tb + lax.broadcasted_iota(jnp.int32, (tb, tb), 0)
        coli = lax.broadcasted_iota(jnp.int32, (tb, tb), 1)

        def step(j, carry, diagonal):
            m, l_, acc = carry
            ks = pl.ds(pl.multiple_of(j * tb, tb), tb)
            sc = _dot(qq, k_ref[ks, :], NT) * scale + (cq - ck_ref[j])
            if diagonal:
                sc = jnp.where(j * tb + coli <= rowpos, sc, NEG)
            mn = jnp.maximum(m, jnp.max(sc, axis=1, keepdims=True))
            p = jnp.exp(sc - mn)
            alpha = jnp.exp(m - mn)
            l_ = alpha * l_ + jnp.sum(p, axis=1, keepdims=True)
            acc = alpha * acc + _dot(p.astype(BF16), v_ref[ks, :], NN)
            return mn, l_, acc

        init = (jnp.full((tb, 1), NEG, F32), jnp.zeros((tb, 1), F32), jnp.zeros((tb, d), F32))
        below = lax.fori_loop(0, i, lambda j, carry: step(j, carry, False), init)
        m, l_, acc = step(i, below, True)
        o_ref[...] = (acc / l_).astype(o_ref.dtype)
        lse_ref[...] = jnp.broadcast_to(m + jnp.log(l_), (tb, d))

    return _host(
        kern, beside, name=name, grid=(h_n, nb),
        in_specs=[pl.BlockSpec((tb, d), lambda h, i: (i, h)), pl.BlockSpec((s, d), lambda h, i: (0, h)),
                  pl.BlockSpec((s, d), lambda h, i: (0, h)), pl.BlockSpec((tb, d), lambda h, i: (i, h)),
                  pl.BlockSpec((None, nb, 1, tb), lambda h, i: (h, 0, 0, 0))],
        out_specs=[pl.BlockSpec((tb, d), lambda h, i: (i, h)), pl.BlockSpec((tb, d), lambda h, i: (i, h))],
        out_shape=[jax.ShapeDtypeStruct((s, hd), BF16), jax.ShapeDtypeStruct((s, hd), F32)],
        scratch_shapes=[], semantics=("parallel", "arbitrary"), args=[q, k, v, cq_rep, ck])


def _attn_bwd(q, k, v, do, lse_rep, delta_rep, cq_rep, ck, name, beside=None):
    s, hd = q.shape
    h_n, d = hd // ATT_HEAD_DIM, ATT_HEAD_DIM
    tb = min(ATT_BLOCK, s)
    nb = s // tb
    scale = d ** -0.5

    def kern(q_ref, do_ref, k_ref, v_ref, lse_ref, dl_ref, cq_ref, ck_ref, dq_ref, dk_ref, dv_ref, dcq_ref, dck_ref):
        j = pl.program_id(1)

        @pl.when(j == 0)
        def _():
            dq_ref[...] = jnp.zeros_like(dq_ref)
            dcq_ref[...] = jnp.zeros_like(dcq_ref)

        kj, vj, ckj = k_ref[...], v_ref[...], ck_ref[...]
        colpos = j * tb + lax.broadcasted_iota(jnp.int32, (tb, tb), 1)
        rowi = lax.broadcasted_iota(jnp.int32, (tb, tb), 0)

        def step(i, carry, diagonal):
            dk, dv, dck = carry
            rs = pl.ds(pl.multiple_of(i * tb, tb), tb)
            qi, doi = q_ref[rs, :], do_ref[rs, :]
            lse = jnp.max(lse_ref[rs, :], axis=1, keepdims=True)
            dl = jnp.max(dl_ref[rs, :], axis=1, keepdims=True)
            cq = jnp.max(cq_ref[rs, :], axis=1, keepdims=True)
            sc = _dot(qi, kj, NT) * scale + (cq - ckj) - lse
            p = jnp.exp(jnp.where(colpos <= i * tb + rowi, sc, NEG) if diagonal else sc)
            dp = _dot(doi, vj, NT)
            ds = p * (dp - dl)
            dsb = ds.astype(BF16)
            dv = dv + _dot(p.astype(BF16), doi, TN)
            dk = dk + _dot(dsb, qi, TN)
            dq_ref[rs, :] += _dot(dsb, kj, NN) * scale
            dcq_ref[rs, :] += jnp.broadcast_to(jnp.sum(ds, axis=1, keepdims=True), (tb, d))
            return dk, dv, dck - jnp.sum(ds, axis=0, keepdims=True)

        init = (jnp.zeros((tb, d), F32), jnp.zeros((tb, d), F32), jnp.zeros((1, tb), F32))
        dk, dv, dck = lax.fori_loop(j + 1, nb, lambda i, carry: step(i, carry, False), step(j, init, True))
        dk_ref[...] = dk * scale
        dv_ref[...] = dv
        dck_ref[...] = dck

    whole = pl.BlockSpec((s, d), lambda h, j: (0, h))
    blk = pl.BlockSpec((tb, d), lambda h, j: (j, h))
    ckb = pl.BlockSpec((None, None, 1, tb), lambda h, j: (h, j, 0, 0))
    return _host(
        kern, beside, name=name, grid=(h_n, nb),
        in_specs=[whole, whole, blk, blk, whole, whole, whole, ckb],
        out_specs=[whole, blk, blk, whole, ckb],
        out_shape=[jax.ShapeDtypeStruct((s, hd), F32), jax.ShapeDtypeStruct((s, hd), F32),
                   jax.ShapeDtypeStruct((s, hd), F32), jax.ShapeDtypeStruct((s, hd), F32),
                   jax.ShapeDtypeStruct((h_n, nb, 1, tb), F32)],
        scratch_shapes=[], semantics=("arbitrary", "arbitrary"), args=[q, do, k, v, lse_rep, delta_rep, cq_rep, ck])


def _adamw(w, g, m, v, name):
    cols = w.shape[1]
    tm = _tile(w.shape[0], 128) if w.shape[0] % 128 == 0 else w.shape[0]
    return _rowwise(_adamw_math, name, [w, g, m, v], outs=[(cols, F32)] * 3, tm=tm)


def _sum_arrays(arrs, out_dtype, name):
    def fn(*xs):
        acc = xs[0].astype(F32)
        for x in xs[1:]:
            acc = acc + x.astype(F32)
        return acc

    tm = PACK_ROWS if arrs[0].shape[0] % PACK_ROWS == 0 else arrs[0].shape[0]
    return _rowwise(fn, name, list(arrs), outs=[(arrs[0].shape[1], out_dtype)], tm=tm)[0]


def _half_tile(rows):
    for t in (256, 176, 128, 64, 32, 16):
        if rows % t == 0:
            return t
    return rows


def _sum_half(g, got, core, name):
    n_chip, hr, cols = got.shape
    tm = _half_tile(hr)
    nt = hr // tm

    def kern(c_ref, g_ref, a_ref, o_ref):
        o_ref[...] = (g_ref[...].astype(F32) + a_ref[...].astype(F32)).astype(o_ref.dtype)

    grid_spec = pltpu.PrefetchScalarGridSpec(
        num_scalar_prefetch=1, grid=(n_chip * nt,),
        in_specs=[pl.BlockSpec((tm, cols), lambda r, c: (((r // nt) * 2 + c[0]) * nt + r % nt, 0)),
                  pl.BlockSpec((tm, cols), lambda r, c: (r, 0))],
        out_specs=pl.BlockSpec((tm, cols), lambda r, c: (r, 0)))
    out = pl.pallas_call(
        kern, name=name, grid_spec=grid_spec, out_shape=jax.ShapeDtypeStruct((n_chip * hr, cols), BF16),
        compiler_params=_params(("arbitrary",)))(core.reshape(1), g.reshape(-1, cols), got.reshape(-1, cols))
    return out.reshape(n_chip, hr, cols)


def _sum_parts(own, parts, chip, name):
    n_parts, hr, cols = parts.shape
    tm = _half_tile(hr)

    def kern(s_ref, t_ref, p_ref, o_ref):
        acc = t_ref[...].astype(F32)
        for j in range(n_parts):
            acc = acc + p_ref[j].astype(F32)
        o_ref[...] = acc

    grid_spec = pltpu.PrefetchScalarGridSpec(
        num_scalar_prefetch=1, grid=(hr // tm,),
        in_specs=[pl.BlockSpec((None, tm, cols), lambda r, s: (s[0], r, 0)),
                  pl.BlockSpec((n_parts, tm, cols), lambda r, s: (0, r, 0))],
        out_specs=pl.BlockSpec((tm, cols), lambda r, s: (r, 0)))
    return pl.pallas_call(
        kern, name=name, grid_spec=grid_spec, out_shape=jax.ShapeDtypeStruct((hr, cols), F32),
        compiler_params=_params(("arbitrary",)))(chip.reshape(1), own, parts)


def _adamw_math(w, g, m, v):
    m = ADAM_B1 * m + (1.0 - ADAM_B1) * g
    v = ADAM_B2 * v + (1.0 - ADAM_B2) * (g * g)
    m_hat = m / (1.0 - ADAM_B1 ** ADAM_STEP)
    v_hat = v / (1.0 - ADAM_B2 ** ADAM_STEP)
    return -ADAM_LR * (m_hat / (jnp.sqrt(v_hat) + ADAM_EPS) + ADAM_WD * w), m, v


def _adamw_big(w, m, v, mine, theirs, core, layer, prev, name):
    n_layers, rows, cols = w.shape
    hr = rows // 2
    tm = next(t for t in (128, 64, 32, 16, 8) if hr % t == 0)
    nt = hr // tm

    def kern(s_ref, w_ref, m_ref, v_ref, a_ref, b_ref, *rest):
        g_ref, d_ref, nm_ref, nv_ref = rest[-4:]
        g = jnp.where(pl.program_id(0) // nt == s_ref[0], a_ref[...], b_ref[...])
        g_ref[...] = g
        d_ref[...], nm_ref[...], nv_ref[...] = _adamw_math(w_ref[...], g, m_ref[...], v_ref[...])

    lyr = pl.BlockSpec((None, tm, cols), lambda r, s: (layer, r, 0))
    half = pl.BlockSpec((tm, cols), lambda r, s: (r % nt, 0))
    passed = [] if prev is None else list(prev)
    grid_spec = pltpu.PrefetchScalarGridSpec(
        num_scalar_prefetch=1, grid=(rows // tm,),
        in_specs=[lyr, lyr, lyr, half, half] + [pl.BlockSpec(memory_space=pl.ANY)] * len(passed), out_specs=[lyr] * 4)
    return pl.pallas_call(
        kern, name=name, grid_spec=grid_spec, out_shape=[jax.ShapeDtypeStruct(w.shape, F32)] * 4,
        input_output_aliases={6 + i: i for i in range(len(passed))},
        compiler_params=_params(("arbitrary",)))(core.reshape(1), w, m, v, mine, theirs, *passed)


def _cast_into_slot(src, slot, name):
    rows, cols = src.shape
    tm = _half_tile(rows)

    def kern(s_ref, x_ref, o_ref):
        o_ref[...] = x_ref[...].astype(o_ref.dtype)

    grid_spec = pltpu.PrefetchScalarGridSpec(
        num_scalar_prefetch=1, grid=(rows // tm,), in_specs=[pl.BlockSpec((tm, cols), lambda r, s: (r, 0))],
        out_specs=pl.BlockSpec((None, tm, cols), lambda r, s: (s[0], r, 0)))
    return pl.pallas_call(
        kern, name=name, grid_spec=grid_spec, out_shape=jax.ShapeDtypeStruct((N_CHIPS, rows, cols), BF16),
        compiler_params=_params(("arbitrary",)))(slot.reshape(1), src)


FLIP_C = [(0, 0, 1)]
FLIP_XY = [(1, 0, 0), (0, 1, 0), (1, 1, 0)]
FLIP_ALL = [(fx, fy, fc) for fx in (0, 1) for fy in (0, 1) for fc in (0, 1) if (fx, fy, fc) != (0, 0, 0)]


def _chip(dev):
    return 2 * dev[0] + dev[1]


def _device(dev):
    return 4 * dev[0] + 2 * dev[1] + dev[2]


def _exchange(name, srcs, dst_shapes, rels, src_view, dst_view, own_view=None, in_place=False):
    n, n_rel = len(srcs), len(rels)

    def body(*refs):
        src_refs, dst_refs = refs[:n], refs[n:2 * n]
        send_sems, recv_sems, own_sems = refs[2 * n:]
        me = (lax.axis_index("x"), lax.axis_index("y"), lax.axis_index("c"))
        peers = [tuple(1 - m if f else m for m, f in zip(me, rel)) for rel in rels]

        def copy(i, k, sender, receiver):
            return pltpu.make_async_remote_copy(
                src_ref=src_view(i, src_refs[i], sender, receiver), dst_ref=dst_view(i, dst_refs[i], sender, k),
                send_sem=send_sems.at[i * n_rel + k], recv_sem=recv_sems.at[i * n_rel + k], device_id=receiver,
                device_id_type=pl.DeviceIdType.MESH)

        sends = [copy(i, k, me, peer) for i in range(n) for k, peer in enumerate(peers)]
        for cp in sends:
            cp.start()
        mine = []
        if own_view is not None:
            for i in range(n):
                frm, to = own_view(i, src_refs[i], dst_refs[i], me)
                mine.append(pltpu.make_async_copy(frm, to, own_sems.at[i]))
                mine[-1].start()
        for i in range(n):
            for k, peer in enumerate(peers):
                copy(i, k, peer, me).wait_recv()
        for cp in sends:
            cp.wait_send()
        for cp in mine:
            cp.wait()

    any_spec = pl.BlockSpec(memory_space=pl.ANY)
    return pl.pallas_call(
        body, name=name, out_shape=[jax.ShapeDtypeStruct(s, d) for s, d in dst_shapes],
        in_specs=[any_spec] * n, out_specs=[any_spec] * n,
        input_output_aliases={i: i for i in range(n)} if in_place else {},
        scratch_shapes=[pltpu.SemaphoreType.DMA((n * n_rel,)), pltpu.SemaphoreType.DMA((n * n_rel,)),
                        pltpu.SemaphoreType.DMA((n,))])(*srcs)


def _gather_plan(bufs):
    n = len(bufs)
    halves = [b.shape[1] // 2 for b in bufs]

    def tools(src_refs, dst_refs, sems):
        ici_send, ici_recv, d2d_send, d2d_recv = sems
        x, y, c = lax.axis_index("x"), lax.axis_index("y"), lax.axis_index("c")
        sibling = (x, y, 1 - c)
        chips = [(1 - x, y), (x, 1 - y), (1 - x, 1 - y)]

        def rows(i, chip, core):
            return dst_refs[i].at[2 * chip[0] + chip[1], pl.ds(core * halves[i], halves[i]), :]

        def over_ici(i, k, src, chip_from, to):
            return pltpu.make_async_remote_copy(
                src_ref=src, dst_ref=rows(i, chip_from, c), send_sem=ici_send.at[3 * i + k],
                recv_sem=ici_recv.at[3 * i + k], device_id=to, device_id_type=pl.DeviceIdType.MESH)

        def over_d2d(i, k, core):
            return pltpu.make_async_remote_copy(
                src_ref=rows(i, chips[k], core), dst_ref=rows(i, chips[k], core), send_sem=d2d_send.at[3 * i + k],
                recv_sem=d2d_recv.at[3 * i + k], device_id=sibling, device_id_type=pl.DeviceIdType.MESH)

        def my_send(i, k):
            my_half = src_refs[i].at[2 * x + y, pl.ds(c * halves[i], halves[i]), :]
            return over_ici(i, k, my_half, (x, y), (*chips[k], c))

        def my_arrival(i, k):
            return over_ici(i, k, rows(i, chips[k], c), chips[k], (x, y, c))

        return c, over_d2d, my_send, my_arrival

    def start(src_refs, dst_refs, sems):
        _, _, my_send, _ = tools(src_refs, dst_refs, sems)
        for i in range(n):
            for k in range(3):
                my_send(i, k).start()

    def finish(src_refs, dst_refs, sems):
        c, over_d2d, my_send, my_arrival = tools(src_refs, dst_refs, sems)
        passed = []
        for i in range(n):
            for k in range(3):
                my_arrival(i, k).wait_recv()
                passed.append(over_d2d(i, k, c))
                passed[-1].start()
        for i in range(n):
            for k in range(3):
                over_d2d(i, k, 1 - c).wait_recv()
        for i in range(n):
            for k in range(3):
                my_send(i, k).wait_send()
        for cp in passed:
            cp.wait_send()

    return _Beside(bufs, [(b.shape, b.dtype, i) for i, b in enumerate(bufs)], [3 * n] * 4, start, finish)


def _scatter_plan(sums):
    n = len(sums)

    def copy(i, k, src_refs, dst_refs, sems, sender, receiver):
        return pltpu.make_async_remote_copy(
            src_ref=src_refs[i].at[_chip(receiver)], dst_ref=dst_refs[i].at[k], send_sem=sems[0].at[3 * i + k],
            recv_sem=sems[1].at[3 * i + k], device_id=receiver, device_id_type=pl.DeviceIdType.MESH)

    def each(fn, src_refs, dst_refs, sems, outgoing):
        me = (lax.axis_index("x"), lax.axis_index("y"), lax.axis_index("c"))
        for i in range(n):
            for k, rel in enumerate(FLIP_XY):
                peer = tuple(1 - m if f else m for m, f in zip(me, rel))
                fn(copy(i, k, src_refs, dst_refs, sems, *((me, peer) if outgoing else (peer, me))))

    def start(src_refs, dst_refs, sems):
        each(lambda cp: cp.start(), src_refs, dst_refs, sems, True)

    def finish(src_refs, dst_refs, sems):
        each(lambda cp: cp.wait_recv(), src_refs, dst_refs, sems, False)
        each(lambda cp: cp.wait_send(), src_refs, dst_refs, sems, True)

    return _Beside(sums, [((3,) + s.shape[1:], s.dtype, None) for s in sums], [3 * n] * 2, start, finish)


def _pack(parts, dtype, multiple):
    flat = jnp.concatenate([p.reshape(-1).astype(dtype) for p in parts])
    pad = (-flat.shape[0]) % multiple
    return jnp.pad(flat, (0, pad)) if pad else flat


def _unpack_shards(packs, names, shapes, axes):
    out, off = {}, 0
    for nme in names:
        sz = math.prod(shapes[nme])
        out[nme] = jnp.concatenate([packs[j, off:off + sz].reshape(shapes[nme]) for j in range(N_CHIPS)], axis=axes[nme])
        off += sz
    return out


def _shards_of(full, axis):
    sz = full.shape[axis] // N_CHIPS
    return [lax.slice_in_dim(full, j * sz, (j + 1) * sz, axis=axis) for j in range(N_CHIPS)]


def kernel(x, a_norm_w, a_in_proj, a_conv_w, a_conv_b, a_dt_bias, a_A_log, a_D, a_gnorm_w, a_out_proj, kv_norm_w, w_kvf, b_f, k_norm_w, b_norm_w, w_q, q_norm_w, w_o, ffn_norm_w, w_gate_up, w_down, loss_target, m_a_norm_w, m_a_in_proj, m_a_conv_w, m_a_conv_b, m_a_dt_bias, m_a_A_log, m_a_D, m_a_gnorm_w, m_a_out_proj, m_kv_norm_w, m_w_kvf, m_b_f, m_k_norm_w, m_b_norm_w, m_w_q, m_q_norm_w, m_w_o, m_ffn_norm_w, m_w_gate_up, m_w_down, v_a_norm_w, v_a_in_proj, v_a_conv_w, v_a_conv_b, v_a_dt_bias, v_a_A_log, v_a_D, v_a_gnorm_w, v_a_out_proj, v_kv_norm_w, v_w_kvf, v_b_f, v_k_norm_w, v_b_norm_w, v_w_q, v_q_norm_w, v_w_o, v_ffn_norm_w, v_w_gate_up, v_w_down):
    args = locals()
    w = {n: args[n] for n in WEIGHTS}
    mom = {n: args['m_' + n] for n in WEIGHTS}
    var = {n: args['v_' + n] for n in WEIGHTS}
    shapes = {n: w[n].shape for n in WEIGHTS}

    cx, cy, cc = lax.axis_index("x"), lax.axis_index("y"), lax.axis_index("c")
    my_chip = 2 * cx + cy

    xs_in = x[0]
    target = loss_target[0]
    s_len, d_model = xs_in.shape
    n_heads_ssm = a_dt_bias.shape[-1]
    d_inner = n_heads_ssm * SSM_HEAD_DIM
    d_xbc = a_conv_w.shape[-1] * N_CHIPS
    d_state = (d_xbc - d_inner) // (2 * SSM_GROUPS)
    kh = n_heads_ssm // SSM_GROUPS
    n_att = b_f.shape[0]
    d_att = n_att * ATT_HEAD_DIM
    d_ff = w_down.shape[1] * N_CHIPS

    small_names = list(SMALL_SHARDED)
    sp = _pack([w[n] for n in small_names], F32, PACK_COLS).reshape(-1, PACK_COLS)
    sp_all = _exchange("gather_small", [sp], [((N_CHIPS,) + sp.shape, F32)], FLIP_XY,
                       lambda i, r, me, peer: r, lambda i, r, sender, k: r.at[_chip(sender)],
                       lambda i, s, d, me: (s, d.at[_chip(me)]))[0]
    full = _unpack_shards(sp_all.reshape(N_CHIPS, -1), small_names, shapes, SMALL_SHARDED)
    for n in SMALL:
        if n not in full:
            full[n] = w[n]

    big2d = {'in': a_in_proj[0], 'out': a_out_proj[0], 'gu0': w_gate_up[0], 'dn0': w_down[0], 'kvf': w_kvf,
             'q': w_q[0], 'o': w_o[0], 'gu1': w_gate_up[1], 'dn1': w_down[1]}
    big_keys = list(big2d)
    slot = {k: _cast_into_slot(big2d[k], my_chip, "cast_" + k) for k in big_keys}
    gathered = {}

    def gather(keys):
        return _gather_plan([slot[k] for k in keys])

    def pad_cols(a, width=LANES):
        return jnp.pad(a, ((0, 0), (0, width - a.shape[1])))

    gathered['in'], = _alone(gather(['in']), "gather_in")
    w_in = jnp.concatenate([gathered['in'][j] for j in range(N_CHIPS)], axis=1)
    w_z, w_xbc, w_dt = w_in[:, :d_inner], w_in[:, d_inner:d_inner + d_xbc], pad_cols(w_in[:, d_inner + d_xbc:])
    conv_w, conv_b = full['a_conv_w'][0], full['a_conv_b']
    norm_a, gnorm = full['a_norm_w'], full['a_gnorm_w']
    dt_bias, a_log = pad_cols(a_dt_bias), pad_cols(a_A_log)
    d_exp = jnp.repeat(a_D, SSM_HEAD_DIM, axis=1)
    kv_nw, b_nw = kv_norm_w.reshape(1, -1), b_norm_w
    k_nw, q_nw = k_norm_w.reshape(1, -1), q_norm_w
    bf_pad = pad_cols(b_f.reshape(1, -1))
    ffn_nw = [ffn_norm_w[i:i + 1] for i in range(2)]

    def to_groups(a):
        g = a[:, :n_heads_ssm].reshape(s_len, SSM_GROUPS, kh).transpose(1, 0, 2)
        return jnp.pad(g, ((0, 0), (0, 0), (0, LANES - kh)))

    def to_groups_t(a):
        g = a[:, :n_heads_ssm].reshape(s_len, SSM_GROUPS, kh).transpose(1, 2, 0)
        return jnp.pad(g, ((0, 0), (0, 8 - kh), (0, 0)))

    def from_groups(col, row=None):
        a = col[:, :, :kh].transpose(1, 0, 2).reshape(s_len, n_heads_ssm)
        if row is not None:
            a = a + row[:, :kh, :].transpose(2, 0, 1).reshape(s_len, n_heads_ssm)
        return pad_cols(a)

    def heads_rows(a):
        return a.reshape(-1, ATT_HEAD_DIM)

    n1 = _rms_fwd(xs_in, norm_a, "norm_a")
    z = _mm(n1, w_z, 'nn', "in_z")
    xbc_raw, (gathered['out'],) = _mm(n1, w_xbc, 'nn', "in_xbc", beside=gather(['out']))
    dtraw = _mm(n1, w_dt, 'nn', "in_dt")
    xbc = _conv_fwd(xbc_raw, conv_w, conv_b, "conv")
    dt, acum = _dt_fwd(dtraw, dt_bias, a_log, "dt")
    dt_g, ac_g, act_g = to_groups(dt), to_groups(acum), to_groups_t(acum)
    (y_ssd, states), (gathered['gu0'],) = _ssd_fwd(xbc, dt_g, ac_g, act_g, d_inner, d_state, kh, "ssd",
                                                   beside=gather(['gu0']))
    yn = _gate_fwd(y_ssd, xbc, z, d_exp, gnorm, d_inner, "gate")
    w_out = gathered['out'].reshape(-1, d_model)
    h1 = _mm(yn, w_out, 'nn', "out_proj", add=xs_in)

    w_gu, w_dn = {}, {}

    def ffn_fwd(h, i, also):
        nrm = _rms_fwd(h, ffn_nw[i], f"ffn{i}_norm")
        w_gu[i] = gathered[f'gu{i}']
        gu = _mm(nrm, w_gu[i], 'nn', f"ffn{i}_up", shards=True, beside=gather(also) if also else None)
        if also:
            gu, arrived = gu
            gathered.update(zip(also, arrived))
        act = _swiglu_fwd(gu, d_ff, f"ffn{i}_act")
        w_dn[i] = gathered[f'dn{i}'].reshape(-1, d_model)
        return nrm, gu, act, _mm(act, w_dn[i], 'nn', f"ffn{i}_down", add=h)

    n2, gu0, act0, h2 = ffn_fwd(h1, 0, ['dn0', 'kvf', 'q', 'o'])
    w_kvf_full = jnp.concatenate([gathered['kvf'][j] for j in range(N_CHIPS)], axis=1)
    w_k, w_v, w_f = w_kvf_full[:, :d_att], w_kvf_full[:, d_att:2 * d_att], pad_cols(w_kvf_full[:, 2 * d_att:])
    w_qm, w_om = gathered['q'].reshape(-1, d_att), gathered['o'].reshape(-1, d_model)
    nkv = _rms_fwd(h2, kv_nw, "kv_norm")
    k_raw = _mm(nkv, w_k, 'nn', "proj_k")
    v_att = _mm(nkv, w_v, 'nn', "proj_v", out_dtype=BF16)
    f_raw = _mm(nkv, w_f, 'nn', "proj_f")
    k_att = _rms_fwd(heads_rows(k_raw), k_nw, "k_norm", tm=1024).reshape(s_len, d_att)
    cum = _forget_fwd(f_raw, bf_pad, "forget")
    cq_rep = jnp.repeat(cum[:, :n_att], ATT_HEAD_DIM, axis=1)
    tb = min(ATT_BLOCK, s_len)
    ck = cum[:, :n_att].T.reshape(n_att, s_len // tb, 1, tb)
    n3 = _rms_fwd(h2, b_nw, "b_norm")
    q_raw = _mm(n3, w_qm, 'nn', "proj_q")
    q_att = _rms_fwd(heads_rows(q_raw), q_nw, "q_norm", tm=1024).reshape(s_len, d_att)
    (o_att, lse_rep), arrived = _attn_fwd(q_att, k_att, v_att, cq_rep, ck, "attn", beside=gather(['gu1', 'dn1']))
    gathered.update(zip(['gu1', 'dn1'], arrived))
    h3 = _mm(o_att, w_om, 'nn', "proj_o", add=h2)
    n4, gu1, act1, h4 = ffn_fwd(h3, 1, [])

    def loss_fn(h, t):
        err = h - t
        sq = jnp.sum(jnp.sum(err * err, axis=1, keepdims=True), axis=0, keepdims=True)
        return err * (1.0 / d_model), sq * (0.5 / d_model)

    dh4, loss_part = _rowwise(loss_fn, "loss", [h4, target], outs=[(d_model, F32)], accs=[(1, 1)])

    n_f = n_att
    g_big, chip_sums, parts = {}, {}, {}

    def col_shards(pieces):
        fullw = jnp.concatenate(pieces, axis=1)
        return fullw.reshape(fullw.shape[0], N_CHIPS, -1).transpose(1, 0, 2)

    def row_shards(a):
        return a.reshape(N_CHIPS, -1, a.shape[1])

    def scatter(keys, tag):
        g_list = [g_big[k] for k in keys]
        hrs = [g.shape[1] // 2 for g in g_list]
        got = _exchange("reduce_d2d_" + tag, g_list,
                        [((N_CHIPS, hr, g.shape[2]), BF16) for g, hr in zip(g_list, hrs)], FLIP_C,
                        lambda i, r, me, peer: r.at[:, pl.ds((1 - me[2]) * hrs[i], hrs[i]), :],
                        lambda i, r, sender, k: r)
        for k, g, a in zip(keys, g_list, got):
            chip_sums[k] = _sum_half(g, a, cc, "reduce_sum2_" + k)
        return _scatter_plan([chip_sums[k] for k in keys])

    def ffn_bwd(dh, h, nrm, gu, act, i, plan, plan_keys):
        dact = _mm(dh, w_dn[i], 'nt', f"ffn{i}_down_dx", beside=plan)
        if plan is not None:
            dact, arrived = dact
            parts.update(zip(plan_keys, arrived))
        g_dn = _mm(act, dh, 'tn', f"ffn{i}_down_dw", out_dtype=BF16)
        dgu = _swiglu_bwd(gu, dact, d_ff, f"ffn{i}_act_bwd")
        dn = _mm(dgu, w_gu[i], 'nt', f"ffn{i}_up_dx", shards=True)
        g_gu = _mm(nrm, dgu, 'tn', f"ffn{i}_up_dw", out_dtype=BF16, shards=True)
        dh_new, g_nw = _rms_bwd(h, ffn_nw[i], dn, f"ffn{i}_norm_bwd", extra=dh)
        return dh_new, g_dn, g_gu, g_nw

    dh3, g_dn1, g_gu1, g_fnw1 = ffn_bwd(dh4, h3, n4, gu1, act1, 1, None, [])
    g_big.update(gu1=g_gu1, dn1=row_shards(g_dn1))
    plan_ffn1 = scatter(['gu1', 'dn1'], "ffn1")
    do_att = _mm(dh3, w_om, 'nt', "proj_o_dx", out_dtype=BF16)
    g_wo = _mm(o_att, dh3, 'tn', "proj_o_dw", out_dtype=BF16)

    def delta_fn(a, b):
        return jnp.broadcast_to(jnp.sum(a.astype(F32) * b.astype(F32), axis=1, keepdims=True), a.shape)

    delta_rep = _rowwise(delta_fn, "attn_delta", [heads_rows(do_att), heads_rows(o_att)],
                         outs=[(ATT_HEAD_DIM, F32)], tm=1024)[0].reshape(s_len, d_att)
    (dq_att, dk_att, dv_att, dcq_rep, dck), arrived = _attn_bwd(q_att, k_att, v_att, do_att, lse_rep, delta_rep,
                                                                cq_rep, ck, "attn_bwd", beside=plan_ffn1)
    parts.update(zip(['gu1', 'dn1'], arrived))
    dq_raw, g_qnw = _rms_bwd(heads_rows(q_raw), q_nw, heads_rows(dq_att), "q_norm_bwd", out_dtype=BF16, tm=1024)
    dq_raw = dq_raw.reshape(s_len, d_att)
    dn3 = _mm(dq_raw, w_qm, 'nt', "proj_q_dx")
    g_wq = _mm(n3, dq_raw, 'tn', "proj_q_dw", out_dtype=BF16)
    dh2, g_bnw = _rms_bwd(h2, b_nw, dn3, "b_norm_bwd", extra=dh3)
    dk_raw, g_knw = _rms_bwd(heads_rows(k_raw), k_nw, heads_rows(dk_att), "k_norm_bwd", out_dtype=BF16, tm=1024)
    dk_raw = dk_raw.reshape(s_len, d_att)
    dcum = pad_cols(dcq_rep.reshape(s_len, n_att, ATT_HEAD_DIM)[:, :, 0] + dck.reshape(n_att, s_len).T)
    df_raw, g_bf = _forget_bwd(f_raw, bf_pad, dcum, "forget_bwd")
    dnkv = _mm(dk_raw, w_k, 'nt', "proj_k_dx")
    dnkv = _mm(dv_att, w_v, 'nt', "proj_v_dx", add=dnkv)
    dnkv = _mm(df_raw, w_f, 'nt', "proj_f_dx", add=dnkv)
    g_wk = _mm(nkv, dk_raw, 'tn', "proj_k_dw", out_dtype=BF16)
    g_wv = _mm(nkv, dv_att, 'tn', "proj_v_dw", out_dtype=BF16)
    g_wf = _mm(nkv, df_raw, 'tn', "proj_f_dw", out_dtype=BF16)
    dh2, g_kvnw = _rms_bwd(h2, kv_nw, dnkv, "kv_norm_bwd", extra=dh2)
    g_big.update(o=row_shards(g_wo), q=row_shards(g_wq), kvf=col_shards([g_wk, g_wv, g_wf[:, :n_f]]))
    att_keys = ['o', 'q', 'kvf']
    dh1, g_dn0, g_gu0, g_fnw0 = ffn_bwd(dh2, h1, n2, gu0, act0, 0, scatter(att_keys, "att"), att_keys)
    g_big.update(gu0=g_gu0, dn0=row_shards(g_dn0))
    plan_ffn0 = scatter(['gu0', 'dn0'], "ffn0")

    dyn = _mm(dh1, w_out, 'nt', "out_proj_dx")
    g_wout = _mm(yn, dh1, 'tn', "out_proj_dw", out_dtype=BF16)
    dy_ssd, dxs_skip, dz, g_dexp, g_gnorm = _gate_bwd(y_ssd, xbc, z, d_exp, gnorm, dyn, d_inner, "gate_bwd")
    (dxs, d_b, d_c, ddt_g, dacc_g, dacr_g), arrived = _ssd_bwd(xbc, dt_g, ac_g, act_g, states, dy_ssd, dxs_skip,
                                                               d_inner, d_state, kh, "ssd_bwd", beside=plan_ffn0)
    parts.update(zip(['gu0', 'dn0'], arrived))
    g_big.update(out=row_shards(g_wout))
    plan_out = scatter(['out'], "out")
    dxbc_act = jnp.concatenate([dxs, d_b, d_c], axis=1)
    du, g_convw, g_convb = _conv_bwd(xbc_raw, conv_w, conv_b, dxbc_act, "conv_bwd")
    draw, g_dtb, g_alog = _dt_bwd(dtraw, dt_bias, a_log, from_groups(ddt_g), from_groups(dacc_g, dacr_g), "dt_bwd")
    dn1 = _mm(dz, w_z, 'nt', "in_z_dx")
    dn1, (parts['out'],) = _mm(du, w_xbc, 'nt', "in_xbc_dx", add=dn1, beside=plan_out)
    dn1 = _mm(draw, w_dt, 'nt', "in_dt_dx", add=dn1)
    g_wz = _mm(n1, dz, 'tn', "in_z_dw", out_dtype=BF16)
    g_wxbc = _mm(n1, du, 'tn', "in_xbc_dw", out_dtype=BF16)
    g_wdt = _mm(n1, draw, 'tn', "in_dt_dw", out_dtype=BF16)
    dx, g_norm_a = _rms_bwd(xs_in, norm_a, dn1, "norm_a_bwd", extra=dh1)
    g_big.update({'in': col_shards([g_wz, g_wxbc, g_wdt[:, :n_heads_ssm]])})
    parts['in'], = _alone(scatter(['in'], "in"), "reduce_ici_in")

    g_small = {
        'a_norm_w': g_norm_a, 'a_conv_w': g_convw[None], 'a_conv_b': g_convb,
        'a_dt_bias': g_dtb[:, :n_heads_ssm], 'a_A_log': g_alog[:, :n_heads_ssm],
        'a_D': g_dexp.reshape(n_heads_ssm, SSM_HEAD_DIM).sum(axis=1).reshape(1, -1), 'a_gnorm_w': g_gnorm,
        'kv_norm_w': g_kvnw.reshape(-1), 'b_f': g_bf[0, :n_f], 'k_norm_w': g_knw.reshape(-1), 'b_norm_w': g_bnw,
        'q_norm_w': g_qnw, 'ffn_norm_w': jnp.concatenate([g_fnw0, g_fnw1], axis=0),
    }

    sg = _pack([g_small[n] for n in SMALL] + [loss_part], F32, 8 * PACK_COLS).reshape(-1, PACK_COLS)
    sg_all = _exchange("reduce_small", [sg], [((2 * N_CHIPS,) + sg.shape, F32)], FLIP_ALL,
                       lambda i, r, me, peer: r, lambda i, r, sender, k: r.at[_device(sender)],
                       lambda i, s, d, me: (s, d.at[_device(me)]))[0]
    sg_sum = _sum_arrays([sg_all[d] for d in range(2 * N_CHIPS)], F32, "reduce_small_sum").reshape(-1)
    red_small, off = {}, 0
    for n in SMALL:
        shp = g_small[n].shape
        red_small[n] = sg_sum[off:off + math.prod(shp)].reshape(shp)
        off += math.prod(shp)
    loss = sg_sum[off]

    red_keys = big_keys[::-1]
    half_sums = [_sum_parts(chip_sums[k], parts[k], my_chip, "reduce_sum4_" + k) for k in red_keys]
    others = _exchange("reduce_back", half_sums, [(h.shape, F32) for h in half_sums], FLIP_C,
                       lambda i, r, me, peer: r, lambda i, r, sender, k: r)
    mine_of, theirs_of = dict(zip(red_keys, half_sums)), dict(zip(red_keys, others))

    grads, delta, new_m, new_v = {}, {}, {}, {}
    layers_of = {'a_in_proj': ['in'], 'a_out_proj': ['out'], 'w_kvf': ['kvf'], 'w_q': ['q'], 'w_o': ['o'],
                 'w_gate_up': ['gu0', 'gu1'], 'w_down': ['dn0', 'dn1']}
    for n, keys in layers_of.items():
        three_d = (len(keys),) + tuple(shapes[n][-2:])
        res = None
        for layer, k in enumerate(keys):
            res = _adamw_big(w[n].reshape(three_d), mom[n].reshape(three_d), var[n].reshape(three_d), mine_of[k],
                             theirs_of[k], cc, layer, res, "adamw_" + k)
        grads[n], delta[n], new_m[n], new_v[n] = [r.reshape(shapes[n]) for r in res]
    for n in SMALL:
        if n in SMALL_SHARDED:
            ax = SMALL_SHARDED[n]
            grads[n] = lax.dynamic_slice_in_dim(red_small[n], my_chip * shapes[n][ax], shapes[n][ax], axis=ax)
        else:
            grads[n] = red_small[n]

    packed = [_pack([src[n] for n in SMALL], F32, 8 * LANES).reshape(-1, LANES) for src in (w, grads, mom, var)]
    small_out = _adamw(*packed, "adamw_small")
    for store, flat in zip((delta, new_m, new_v), small_out):
        flat, off = flat.reshape(-1), 0
        for n in SMALL:
            sz = math.prod(shapes[n])
            store[n] = flat[off:off + sz].reshape(shapes[n])
            off += sz

    return (loss, dx[None], *[grads[n] for n in WEIGHTS], *[delta[n] for n in WEIGHTS],
            *[new_m[n] for n in WEIGHTS], *[new_v[n] for n in WEIGHTS])
```

```python
import functools
import math

import jax
import jax.numpy as jnp
from jax import lax
from jax.experimental import pallas as pl
from jax.experimental.pallas import tpu as pltpu

F32, BF16 = jnp.float32, jnp.bfloat16
EPS = 1e-6
SSM_HEAD_DIM = 64
SSM_GROUPS = 8
SSD_CHUNK = 128
ATT_HEAD_DIM = 128
LANES = 128
N_CHIPS = 4
NEG = -1e30
ADAM_LR, ADAM_B1, ADAM_B2, ADAM_EPS, ADAM_WD, ADAM_STEP = 0.001, 0.9, 0.999, 1e-08, 0.01, 10
VMEM_LIMIT_BYTES = 56 * 1024 * 1024
PACK_COLS = 1024
PACK_ROWS = 256
MM_K_TILES = (2816, 2048, 1408, 1024, 512, 256, 128)
MM_OPERAND_BYTES = 12 * 1024 * 1024

NN = ((1,), (0,))
NT = ((1,), (1,))
TN = ((0,), (0,))

WEIGHTS = ['a_norm_w', 'a_in_proj', 'a_conv_w', 'a_conv_b', 'a_dt_bias', 'a_A_log', 'a_D', 'a_gnorm_w', 'a_out_proj',
           'kv_norm_w', 'w_kvf', 'b_f', 'k_norm_w', 'b_norm_w', 'w_q', 'q_norm_w', 'w_o', 'ffn_norm_w', 'w_gate_up',
           'w_down']
BIG = {'a_in_proj': 2, 'a_out_proj': 1, 'w_kvf': 1, 'w_q': 1, 'w_o': 1, 'w_gate_up': 2, 'w_down': 1}
SMALL_SHARDED = {'a_norm_w': 1, 'a_conv_w': 2, 'a_conv_b': 1, 'a_gnorm_w': 1}
SMALL = [n for n in WEIGHTS if n not in BIG]


def _dot(a, b, dims):
    return lax.dot_general(a, b, (dims, ((), ())), preferred_element_type=F32)


def _params(sem=None):
    return pltpu.CompilerParams(dimension_semantics=sem, vmem_limit_bytes=VMEM_LIMIT_BYTES)


def _tile(dim, cap):
    for t in (1408, 1024, 512, 256, 128):
        if t <= cap and dim % t == 0:
            return t
    return dim


class _Beside:
    def __init__(self, operands, results, sem_sizes, start, finish):
        self.operands, self.results, self.sem_sizes = list(operands), list(results), list(sem_sizes)
        self.start, self.finish = start, finish


def _host(kern, beside, *, name, grid, in_specs, out_specs, out_shape, scratch_shapes, semantics, args):
    single = not isinstance(out_shape, (list, tuple))
    out_specs = [out_specs] if single else list(out_specs)
    out_shape = [out_shape] if single else list(out_shape)
    if beside is None:
        res = pl.pallas_call(kern, name=name, grid=grid, in_specs=in_specs, out_specs=out_specs, out_shape=out_shape,
                             scratch_shapes=scratch_shapes, compiler_params=_params(semantics))(*args)
        return (res[0] if single else res), []
    n_in, n_out, n_scr = len(in_specs), len(out_specs), len(scratch_shapes)
    nb_in, nb_out = len(beside.operands), len(beside.results)

    def body(*refs):
        ins, b_ins = refs[:n_in], refs[n_in:n_in + nb_in]
        outs = refs[n_in + nb_in:n_in + nb_in + n_out]
        b_outs = refs[n_in + nb_in + n_out:n_in + nb_in + n_out + nb_out]
        rest = refs[n_in + nb_in + n_out + nb_out:]
        scr, sems = rest[:n_scr], rest[n_scr:]
        ids = [pl.program_id(a) for a in range(len(grid))]
        first = functools.reduce(jnp.logical_and, [i == 0 for i in ids])
        last = functools.reduce(jnp.logical_and, [i == g - 1 for i, g in zip(ids, grid)])

        @pl.when(first)
        def _():
            beside.start(b_ins, b_outs, sems)

        kern(*ins, *outs, *scr)

        @pl.when(last)
        def _():
            beside.finish(b_ins, b_outs, sems)

    any_spec = pl.BlockSpec(memory_space=pl.ANY)
    res = pl.pallas_call(
        body, name=name, grid=grid, in_specs=list(in_specs) + [any_spec] * nb_in,
        out_specs=out_specs + [any_spec] * nb_out,
        out_shape=out_shape + [jax.ShapeDtypeStruct(s, d) for s, d, _ in beside.results],
        input_output_aliases={n_in + op: n_out + r for r, (_, _, op) in enumerate(beside.results) if op is not None},
        scratch_shapes=list(scratch_shapes) + [pltpu.SemaphoreType.DMA((k,)) for k in beside.sem_sizes],
        compiler_params=_params(("arbitrary",) * len(grid)))(*args, *beside.operands)
    mine = res[:n_out]
    return (mine[0] if single else mine), list(res[n_out:])


def _alone(beside, name):
    return _host(lambda: None, beside, name=name, grid=(1,), in_specs=[], out_specs=[], out_shape=[],
                 scratch_shapes=[], semantics=("arbitrary",), args=[])[1]


def _mm(a, b, mode, name, out_dtype=F32, add=None, shards=False, beside=None):
    if mode == 'nn':
        (m, k), n = a.shape, (b.shape[2] * N_CHIPS if shards else b.shape[1])
    elif mode == 'nt':
        (m, k), n = a.shape, (b.shape[1] if shards else b.shape[0])
    else:
        (k, m), n = a.shape, b.shape[1]
    per_chip = (k if mode == 'nt' else n) // N_CHIPS
    tm = _tile(m, 1024)
    tn = _tile(per_chip if shards and mode != 'nt' else n, 1408 if shards else 1024)
    k_dim = per_chip if shards and mode == 'nt' else k
    a_bytes, b_bytes = jnp.dtype(a.dtype).itemsize, jnp.dtype(b.dtype).itemsize
    tk = next((t for t in MM_K_TILES if k_dim % t == 0 and t * (tm * a_bytes + tn * b_bytes) <= MM_OPERAND_BYTES), k_dim)
    nk = k // tk
    in_place = nk > 1 and out_dtype == F32
    a_spec = pl.BlockSpec((tk, tm), lambda i, j, q: (q, i)) if mode == 'tn' else pl.BlockSpec((tm, tk), lambda i, j, q: (i, q))
    b_spec = pl.BlockSpec((tn, tk), lambda i, j, q: (j, q)) if mode == 'nt' else pl.BlockSpec((tk, tn), lambda i, j, q: (q, j))
    o_spec = pl.BlockSpec((tm, tn), lambda i, j, q: (i, j))
    out_struct = jax.ShapeDtypeStruct((m, n), out_dtype)
    if shards:
        per = per_chip // (tk if mode == 'nt' else tn)
        if mode == 'nn':
            b_spec = pl.BlockSpec((None, tk, tn), lambda i, j, q: (j // per, q, j % per))
        elif mode == 'nt':
            b_spec = pl.BlockSpec((None, tn, tk), lambda i, j, q: (q // per, j, q % per))
        else:
            out_struct = jax.ShapeDtypeStruct((N_CHIPS, m, per_chip), out_dtype)
    out_spec = pl.BlockSpec((None, tm, tn), lambda i, j, q: (j // per, i, j % per)) if shards and mode == 'tn' else o_spec
    dims = {'nn': NN, 'nt': NT, 'tn': TN}[mode]

    n_ins = 3 if add is not None else 2

    def kern(*refs):
        a_ref, b_ref, o_ref = refs[0], refs[1], refs[n_ins]
        acc = o_ref if in_place or nk == 1 else refs[n_ins + 1]
        q = pl.program_id(2)
        part = _dot(a_ref[...].astype(BF16), b_ref[...].astype(BF16), dims)

        def first():
            return part if add is None else part + refs[2][...]

        if nk == 1:
            o_ref[...] = first().astype(o_ref.dtype)
            return

        @pl.when(q == 0)
        def _():
            acc[...] = first()

        @pl.when(q > 0)
        def _():
            acc[...] += part

        if not in_place:
            @pl.when(q == nk - 1)
            def _():
                o_ref[...] = acc[...].astype(o_ref.dtype)

    ins, specs = [a, b], [a_spec, b_spec]
    if add is not None:
        ins.append(add)
        specs.append(o_spec)
    scratch = [] if in_place or nk == 1 else [pltpu.VMEM((tm, tn), F32)]
    res, extra = _host(kern, beside, name=name, grid=(m // tm, n // tn, nk), in_specs=specs, out_specs=out_spec,
                       out_shape=out_struct, scratch_shapes=scratch,
                       semantics=("parallel", "parallel", "arbitrary"), args=ins)
    return res if beside is None else (res, extra)


def _rowwise(fn, name, rows, bcast=(), outs=(), accs=(), tm=256):
    rows = [r if isinstance(r, tuple) else (r, r.shape[1], 0) for r in rows]
    n_rows = rows[0][0].shape[0]
    tm = min(tm, n_rows)
    assert n_rows % tm == 0, (name, n_rows, tm)
    n_in, n_out = len(rows) + len(bcast), len(outs)
    in_specs = [pl.BlockSpec((tm, w), functools.partial(lambda i, cb: (i, cb), cb=cb)) for _, w, cb in rows]
    in_specs += [pl.BlockSpec(b.shape, lambda i: (0, 0)) for b in bcast]
    out_specs = [pl.BlockSpec((tm, w), lambda i: (i, 0)) for w, _ in outs]
    out_specs += [pl.BlockSpec(s, lambda i: (0, 0)) for s in accs]
    out_shape = [jax.ShapeDtypeStruct((n_rows, w), d) for w, d in outs] + [jax.ShapeDtypeStruct(s, F32) for s in accs]

    def kern(*refs):
        vals = fn(*[r[...] for r in refs[:n_in]])
        vals = vals if isinstance(vals, (tuple, list)) else (vals,)
        o_refs = refs[n_in:]
        for r, v in zip(o_refs[:n_out], vals[:n_out]):
            r[...] = v.astype(r.dtype)
        if accs:
            @pl.when(pl.program_id(0) == 0)
            def _():
                for r in o_refs[n_out:]:
                    r[...] = jnp.zeros_like(r)

            for r, v in zip(o_refs[n_out:], vals[n_out:]):
                r[...] += v

    return pl.pallas_call(
        kern, name=name, grid=(n_rows // tm,), in_specs=in_specs, out_specs=out_specs, out_shape=out_shape,
        compiler_params=_params(("arbitrary",)))(*[r[0] for r in rows], *bcast)


def _rms(x, w):
    xf = x.astype(F32)
    return xf * lax.rsqrt(jnp.mean(xf * xf, axis=-1, keepdims=True) + EPS) * w


def _lane_groups(width, group):
    return [slice(g * group, (g + 1) * group) for g in range(width // group)]


def _rms_fwd(x, w, name, tm=256):
    def fn(x, w):
        return jnp.concatenate([_rms(x[:, g], w) for g in _lane_groups(x.shape[1], w.shape[1])], axis=1)

    return _rowwise(fn, name, [x], [w], outs=[(x.shape[1], BF16)], tm=tm)[0]


def _rms_bwd(x, w, dy, name, extra=None, out_dtype=F32, tm=256, copy_bf16=False):
    def fn(x, dy, *rest):
        w = rest[-1]
        dxs, dw = [], jnp.zeros(w.shape, F32)
        for g in _lane_groups(x.shape[1], w.shape[1]):
            _, vjp = jax.vjp(_rms, x[:, g], w)
            dx_g, dw_g = vjp(dy[:, g].astype(F32))
            dxs.append(dx_g)
            dw = dw + dw_g
        dx = jnp.concatenate(dxs, axis=1)
        if extra is not None:
            dx = dx + rest[0]
        return (dx, dx, dw) if copy_bf16 else (dx, dw)

    rows = [x, dy] + ([extra] if extra is not None else [])
    outs = [(x.shape[1], out_dtype)] + ([(x.shape[1], BF16)] if copy_bf16 else [])
    return _rowwise(fn, name, rows, [w], outs=outs, accs=[w.shape], tm=tm)


def _sigmoid(x):
    return 1.0 / (1.0 + jnp.exp(-x))


def _softplus(x):
    return jnp.maximum(x, 0.0) + jnp.log(1.0 + jnp.exp(-jnp.abs(x)))


def _swiglu_fwd(gu, d_ff, name):
    def fn(g, u):
        return g * _sigmoid(g) * u

    return _rowwise(fn, name, [(gu, d_ff, 0), (gu, d_ff, 1)], outs=[(d_ff, BF16)], tm=128)[0]


def _swiglu_bwd(gu, dact, d_ff, name):
    def fn(g, u, da):
        s = _sigmoid(g)
        dg = da * u * s * (1.0 + g * (1.0 - s))
        du = da * g * s
        return jnp.concatenate([dg, du], axis=1)

    return _rowwise(fn, name, [(gu, d_ff, 0), (gu, d_ff, 1), dact], outs=[(2 * d_ff, BF16)], tm=128)[0]


def _cumsum_rows(v, reverse=False):
    n = v.shape[0]
    row = lax.broadcasted_iota(jnp.int32, v.shape, 0)
    sh = 1
    while sh < n:
        if reverse:
            v = v + jnp.where(row < n - sh, pltpu.roll(v, n - sh, 0), 0.0)
        else:
            v = v + jnp.where(row >= sh, pltpu.roll(v, sh, 0), 0.0)
        sh *= 2
    return v


def _conv_fwd(u, w, b, name):
    s, c = u.shape
    kw = w.shape[0]
    tc = _tile(c, 128)

    def kern(u_ref, w_ref, b_ref, o_ref):
        uu = u_ref[...]
        row = lax.broadcasted_iota(jnp.int32, uu.shape, 0)
        acc = jnp.zeros_like(uu) + b_ref[...]
        for k in range(kw):
            sh = kw - 1 - k
            uk = uu if sh == 0 else jnp.where(row >= sh, pltpu.roll(uu, sh, 0), 0.0)
            acc = acc + w_ref[pl.ds(k, 1), :] * uk
        o_ref[...] = acc * _sigmoid(acc)

    return pl.pallas_call(
        kern, name=name, grid=(c // tc,),
        in_specs=[pl.BlockSpec((s, tc), lambda j: (0, j)), pl.BlockSpec((kw, tc), lambda j: (0, j)),
                  pl.BlockSpec((1, tc), lambda j: (0, j))],
        out_specs=pl.BlockSpec((s, tc), lambda j: (0, j)), out_shape=jax.ShapeDtypeStruct((s, c), F32),
        compiler_params=_params(("parallel",)))(u, w, b)


def _conv_bwd(u, w, b, dact, name):
    s, c = u.shape
    kw = w.shape[0]
    tc = _tile(c, 128)

    def kern(u_ref, w_ref, b_ref, d_ref, du_ref, dw_ref, db_ref):
        uu = u_ref[...]
        row = lax.broadcasted_iota(jnp.int32, uu.shape, 0)
        shifted = []
        acc = jnp.zeros_like(uu) + b_ref[...]
        for k in range(kw):
            sh = kw - 1 - k
            uk = uu if sh == 0 else jnp.where(row >= sh, pltpu.roll(uu, sh, 0), 0.0)
            shifted.append(uk)
            acc = acc + w_ref[pl.ds(k, 1), :] * uk
        sg = _sigmoid(acc)
        dacc = d_ref[...] * sg * (1.0 + acc * (1.0 - sg))
        db_ref[...] = jnp.sum(dacc, axis=0, keepdims=True)
        du = jnp.zeros_like(uu)
        for k in range(kw):
            sh = kw - 1 - k
            dw_ref[pl.ds(k, 1), :] = jnp.sum(dacc * shifted[k], axis=0, keepdims=True)
            dk = dacc if sh == 0 else jnp.where(row < s - sh, pltpu.roll(dacc, s - sh, 0), 0.0)
            du = du + w_ref[pl.ds(k, 1), :] * dk
        du_ref[...] = du.astype(du_ref.dtype)

    col = lambda j: (0, j)
    return pl.pallas_call(
        kern, name=name, grid=(c // tc,),
        in_specs=[pl.BlockSpec((s, tc), col), pl.BlockSpec((kw, tc), col), pl.BlockSpec((1, tc), col),
                  pl.BlockSpec((s, tc), col)],
        out_specs=[pl.BlockSpec((s, tc), col), pl.BlockSpec((kw, tc), col), pl.BlockSpec((1, tc), col)],
        out_shape=[jax.ShapeDtypeStruct((s, c), BF16), jax.ShapeDtypeStruct((kw, c), F32),
                   jax.ShapeDtypeStruct((1, c), F32)],
        compiler_params=_params(("parallel",)))(u, w, b, dact)


def _dt_fwd(dtraw, bias, a_log, name):
    def fn(raw, bias, a_log):
        dt = _softplus(raw + bias)
        return dt, _cumsum_rows(dt * (-jnp.exp(a_log)))

    return _rowwise(fn, name, [dtraw], [bias, a_log], outs=[(LANES, F32), (LANES, F32)], tm=SSD_CHUNK)


def _dt_bwd(dtraw, bias, a_log, ddt, dacum, name):
    def fn(raw, ddt, dac, bias, a_log):
        z = raw + bias
        dt = _softplus(z)
        a_neg = -jnp.exp(a_log)
        da = _cumsum_rows(dac, reverse=True)
        draw = (ddt + da * a_neg) * _sigmoid(z)
        return draw, jnp.sum(draw, axis=0, keepdims=True), jnp.sum(da * dt, axis=0, keepdims=True) * a_neg

    return _rowwise(fn, name, [dtraw, ddt, dacum], [bias, a_log], outs=[(LANES, BF16)],
                    accs=[(1, LANES), (1, LANES)], tm=SSD_CHUNK)


def _ssd_pieces(xs, dt, ac, kh):
    l, gw = xs.shape
    lane_w = lax.broadcasted_iota(jnp.int32, (l, gw), 1)
    lane_k = lax.broadcasted_iota(jnp.int32, (l, LANES), 1)
    head = [(lane_w >= k * SSM_HEAD_DIM) & (lane_w < (k + 1) * SSM_HEAD_DIM) for k in range(kh)]

    def col(blk, k):
        return jnp.sum(jnp.where(lane_k == k, blk, 0.0), axis=1, keepdims=True)

    def expand(blk):
        acc = jnp.zeros((l, gw), F32)
        for k in range(kh):
            acc = jnp.where(head[k], col(blk, k), acc)
        return acc

    def collapse(wide):
        acc = jnp.zeros((l, LANES), F32)
        for k in range(kh):
            acc = jnp.where(lane_k == k, jnp.sum(jnp.where(head[k], wide, 0.0), axis=1, keepdims=True), acc)
        return acc

    return head, lane_k, col, expand, collapse


def _ssd_specs(l, gw, n, n_xs_blocks):
    g_axis = SSM_GROUPS
    return dict(
        xs=lambda cm: pl.BlockSpec((l, gw), lambda g, c: (cm(c), g)),
        b=lambda cm: pl.BlockSpec((l, n), lambda g, c: (cm(c), n_xs_blocks + g)),
        c=lambda cm: pl.BlockSpec((l, n), lambda g, c: (cm(c), n_xs_blocks + g_axis + g)),
        col=lambda cm: pl.BlockSpec((None, l, LANES), lambda g, c: (g, cm(c), 0)),
        row=lambda cm: pl.BlockSpec((None, 8, l), lambda g, c: (g, 0, cm(c))),
        state=lambda cm: pl.BlockSpec((None, None, n, gw), lambda g, c: (cm(c), g, 0, 0)),
    )


def _ssd_fwd(xbc, dt_g, ac_g, act_g, d_inner, n, kh, name, beside=None):
    s = xbc.shape[0]
    l, g_n = SSD_CHUNK, SSM_GROUPS
    gw, nc = d_inner // g_n, s // l
    sp = _ssd_specs(l, gw, n, d_inner // n)
    fwd = lambda c: c

    def kern(xs_ref, b_ref, c_ref, dt_ref, ac_ref, act_ref, y_ref, s0_ref, st):
        @pl.when(pl.program_id(1) == 0)
        def _():
            st[...] = jnp.zeros_like(st)

        s0 = st[...]
        s0_ref[...] = s0
        xs, ac = xs_ref[...], ac_ref[...]
        head, _, col, expand, _ = _ssd_pieces(xs, dt_ref[...], ac, kh)
        ace = expand(ac)
        x = xs * expand(dt_ref[...])
        xb, bb, cb_ = x.astype(BF16), b_ref[...].astype(BF16), c_ref[...].astype(BF16)
        cb = _dot(cb_, bb, NT)
        ri = lax.broadcasted_iota(jnp.int32, (l, l), 0)
        ci = lax.broadcasted_iota(jnp.int32, (l, l), 1)
        causal = ri >= ci
        y = _dot(cb_, s0.astype(BF16), NN) * jnp.exp(ace)
        for k in range(kh):
            seg = col(ac, k) - act_ref[pl.ds(k, 1), :]
            m = jnp.where(causal, cb * jnp.exp(jnp.where(causal, seg, 0.0)), 0.0)
            y = jnp.where(head[k], y + _dot(m.astype(BF16), xb, NN), y)
        y_ref[...] = y
        row_w = lax.broadcasted_iota(jnp.int32, (l, gw), 0)
        alast = jnp.sum(jnp.where(row_w == l - 1, ace, 0.0), axis=0, keepdims=True)
        st[...] = s0 * jnp.exp(alast) + _dot(bb, (jnp.exp(alast - ace) * x).astype(BF16), TN)

    return _host(
        kern, beside, name=name, grid=(g_n, nc),
        in_specs=[sp['xs'](fwd), sp['b'](fwd), sp['c'](fwd), sp['col'](fwd), sp['col'](fwd), sp['row'](fwd)],
        out_specs=[pl.BlockSpec((l, gw), lambda g, c: (c, g)), sp['state'](fwd)],
        out_shape=[jax.ShapeDtypeStruct((s, d_inner), F32), jax.ShapeDtypeStruct((nc, g_n, n, gw), F32)],
        scratch_shapes=[pltpu.VMEM((n, gw), F32)], semantics=("arbitrary", "arbitrary"),
        args=[xbc, xbc, xbc, dt_g, ac_g, act_g])


def _ssd_bwd(xbc, dt_g, ac_g, act_g, s0_all, dy, dxs_skip, d_inner, n, kh, name, beside=None):
    s = xbc.shape[0]
    l, g_n = SSD_CHUNK, SSM_GROUPS
    gw, nc = d_inner // g_n, s // l
    sp = _ssd_specs(l, gw, n, d_inner // n)
    rev = lambda c: nc - 1 - c

    def kern(xs_ref, b_ref, c_ref, dt_ref, ac_ref, act_ref, s0_ref, dy_ref, skip_ref,
             dxs_ref, db_ref, dc_ref, ddt_ref, dacc_ref, dacr_ref, dst):
        @pl.when(pl.program_id(1) == 0)
        def _():
            dst[...] = jnp.zeros_like(dst)

        dsn = dst[...]
        s0 = s0_ref[...]
        xs, ac, dy = xs_ref[...], ac_ref[...], dy_ref[...]
        head, lane_k, col, expand, collapse = _ssd_pieces(xs, dt_ref[...], ac, kh)
        ace, dte = expand(ac), expand(dt_ref[...])
        x = xs * dte
        xb, bb, cb_ = x.astype(BF16), b_ref[...].astype(BF16), c_ref[...].astype(BF16)
        s0b, dsnb, dyb = s0.astype(BF16), dsn.astype(BF16), dy.astype(BF16)
        cb = _dot(cb_, bb, NT)
        ri = lax.broadcasted_iota(jnp.int32, (l, l), 0)
        ci = lax.broadcasted_iota(jnp.int32, (l, l), 1)
        causal = ri >= ci
        row_w = lax.broadcasted_iota(jnp.int32, (l, gw), 0)
        e = jnp.exp(ace)
        alast = jnp.sum(jnp.where(row_w == l - 1, ace, 0.0), axis=0, keepdims=True)
        gdec = jnp.exp(alast)
        wt = jnp.exp(alast - ace)
        cs = _dot(cb_, s0b, NN)
        dcs = (dy * e).astype(BF16)
        d_c = _dot(dcs, s0b, NT)
        ds_off = _dot(cb_, dcs, TN)
        dace = dy * cs * e
        dalast = jnp.sum(dsn * s0, axis=0, keepdims=True) * gdec
        z = wt * x
        dz = _dot(bb, dsnb, NN)
        d_b = _dot(z.astype(BF16), dsnb, NT)
        dx = dz * wt
        t = dz * z
        dalast = dalast + jnp.sum(t, axis=0, keepdims=True)
        dace = dace - t
        dcb = jnp.zeros((l, l), F32)
        dac_col = jnp.zeros((l, LANES), F32)
        for k in range(kh):
            seg = col(ac, k) - act_ref[pl.ds(k, 1), :]
            dk = jnp.exp(jnp.where(causal, seg, 0.0))
            mk = jnp.where(causal, cb * dk, 0.0)
            dx = jnp.where(head[k], dx + _dot(mk.astype(BF16), dyb, TN), dx)
            dmk = _dot(jnp.where(head[k], dy, 0.0).astype(BF16), xb, NT)
            dcb = dcb + jnp.where(causal, dmk * dk, 0.0)
            dseg = dmk * mk
            dac_col = jnp.where(lane_k == k, jnp.sum(dseg, axis=1, keepdims=True), dac_col)
            dacr_ref[pl.ds(k, 1), :] = -jnp.sum(dseg, axis=0, keepdims=True)
        for k in range(kh, 8):
            dacr_ref[pl.ds(k, 1), :] = jnp.zeros((1, l), F32)
        dcbb = dcb.astype(BF16)
        dc_ref[...] = d_c + _dot(dcbb, bb, NN)
        db_ref[...] = d_b + _dot(dcbb, cb_, TN)
        dace = jnp.where(row_w == l - 1, dace + dalast, dace)
        dacc_ref[...] = dac_col + collapse(dace)
        ddt_ref[...] = collapse(dx * xs)
        dxs_ref[...] = dx * dte + skip_ref[...]
        dst[...] = dsn * gdec + ds_off

    return _host(
        kern, beside, name=name, grid=(g_n, nc),
        in_specs=[sp['xs'](rev), sp['b'](rev), sp['c'](rev), sp['col'](rev), sp['col'](rev), sp['row'](rev),
                  sp['state'](rev), pl.BlockSpec((l, gw), lambda g, c: (rev(c), g)),
                  pl.BlockSpec((l, gw), lambda g, c: (rev(c), g))],
        out_specs=[pl.BlockSpec((l, gw), lambda g, c: (rev(c), g)), pl.BlockSpec((l, n), lambda g, c: (rev(c), g)),
                   pl.BlockSpec((l, n), lambda g, c: (rev(c), g)), sp['col'](rev), sp['col'](rev), sp['row'](rev)],
        out_shape=[jax.ShapeDtypeStruct((s, d_inner), F32), jax.ShapeDtypeStruct((s, g_n * n), F32),
                   jax.ShapeDtypeStruct((s, g_n * n), F32), jax.ShapeDtypeStruct((g_n, s, LANES), F32),
                   jax.ShapeDtypeStruct((g_n, s, LANES), F32), jax.ShapeDtypeStruct((g_n, 8, s), F32)],
        scratch_shapes=[pltpu.VMEM((n, gw), F32)], semantics=("arbitrary", "arbitrary"),
        args=[xbc, xbc, xbc, dt_g, ac_g, act_g, s0_all, dy, dxs_skip])


def _gate(y, xs, z, d_exp, gw):
    t = (y + xs * d_exp) * (z * _sigmoid(z))
    width = t.shape[1]
    gsz = width // SSM_GROUPS
    lane = lax.broadcasted_iota(jnp.int32, t.shape, 1)
    t2 = t * t
    scale = jnp.zeros_like(t)
    for g in range(SSM_GROUPS):
        in_g = (lane >= g * gsz) & (lane < (g + 1) * gsz)
        ms = jnp.sum(jnp.where(in_g, t2, 0.0), axis=1, keepdims=True) * (1.0 / gsz)
        scale = jnp.where(in_g, lax.rsqrt(ms + EPS), scale)
    return t * scale * gw


def _gate_fwd(y, xbc, z, d_exp, gw, d_inner, name):
    return _rowwise(_gate, name, [y, (xbc, d_inner, 0), z], [d_exp, gw], outs=[(d_inner, BF16)], tm=128)[0]


def _gate_bwd(y, xbc, z, d_exp, gw, dyn, d_inner, name):
    def fn(y, xs, z, dyn, d_exp, gw):
        _, vjp = jax.vjp(_gate, y, xs, z, d_exp, gw)
        return vjp(dyn)

    return _rowwise(fn, name, [y, (xbc, d_inner, 0), z, dyn], [d_exp, gw],
                    outs=[(d_inner, F32), (d_inner, F32), (d_inner, BF16)], accs=[d_exp.shape, gw.shape], tm=64)


def _forget_fwd(fraw, b_f, name):
    def kern(f_ref, b_ref, o_ref):
        o_ref[...] = _cumsum_rows(-_softplus(-(f_ref[...] + b_ref[...])))

    return pl.pallas_call(kern, name=name, out_shape=jax.ShapeDtypeStruct(fraw.shape, F32),
                          compiler_params=_params())(fraw, b_f)


def _forget_bwd(fraw, b_f, dcum, name):
    def kern(f_ref, b_ref, d_ref, df_ref, db_ref):
        df = _cumsum_rows(d_ref[...], reverse=True) * _sigmoid(-(f_ref[...] + b_ref[...]))
        df_ref[...] = df.astype(df_ref.dtype)
        db_ref[...] = jnp.sum(df, axis=0, keepdims=True)

    return pl.pallas_call(
        kern, name=name,
        out_shape=[jax.ShapeDtypeStruct(fraw.shape, BF16), jax.ShapeDtypeStruct((1, fraw.shape[1]), F32)],
        compiler_params=_params())(fraw, b_f, dcum)


ATT_BLOCK = 512


def _attn_fwd(q, k, v, cq_rep, ck, name, beside=None):
    s, hd = q.shape
    h_n, d = hd // ATT_HEAD_DIM, ATT_HEAD_DIM
    tb = min(ATT_BLOCK, s)
    nb = s // tb
    scale = d ** -0.5

    def kern(q_ref, k_ref, v_ref, cq_ref, ck_ref, o_ref, lse_ref):
        i = pl.program_id(1)
        qq = q_ref[...]
        cq = jnp.max(cq_ref[...], axis=1, keepdims=True)
        rowpos = i * tb + lax.broadcasted_iota(jnp.int32, (tb, tb), 0)
        coli = lax.broadcasted_iota(jnp.int32, (tb, tb), 1)

        def step(j, carry, diagonal):
            m, l_, acc = carry
            ks = pl.ds(pl.multiple_of(j * tb, tb), tb)
            sc = _dot(qq, k_ref[ks, :], NT) * scale + (cq - ck_ref[j])
            if diagonal:
                sc = jnp.where(j * tb + coli <= rowpos, sc, NEG)
            mn = jnp.maximum(m, jnp.max(sc, axis=1, keepdims=True))
            p = jnp.exp(sc - mn)
            alpha = jnp.exp(m - mn)
            l_ = alpha * l_ + jnp.sum(p, axis=1, keepdims=True)
            acc = alpha * acc + _dot(p.astype(BF16), v_ref[ks, :], NN)
            return mn, l_, acc

        init = (jnp.full((tb, 1), NEG, F32), jnp.zeros((tb, 1), F32), jnp.zeros((tb, d), F32))
        below = lax.fori_loop(0, i, lambda j, carry: step(j, carry, False), init)
        m, l_, acc = step(i, below, True)
        o_ref[...] = (acc / l_).astype(o_ref.dtype)
        lse_ref[...] = jnp.broadcast_to(m + jnp.log(l_), (tb, d))

    return _host(
        kern, beside, name=name, grid=(h_n, nb),
        in_specs=[pl.BlockSpec((tb, d), lambda h, i: (i, h)), pl.BlockSpec((s, d), lambda h, i: (0, h)),
                  pl.BlockSpec((s, d), lambda h, i: (0, h)), pl.BlockSpec((tb, d), lambda h, i: (i, h)),
                  pl.BlockSpec((None, nb, 1, tb), lambda h, i: (h, 0, 0, 0))],
        out_specs=[pl.BlockSpec((tb, d), lambda h, i: (i, h)), pl.BlockSpec((tb, d), lambda h, i: (i, h))],
        out_shape=[jax.ShapeDtypeStruct((s, hd), BF16), jax.ShapeDtypeStruct((s, hd), F32)],
        scratch_shapes=[], semantics=("parallel", "arbitrary"), args=[q, k, v, cq_rep, ck])


def _attn_bwd(q, k, v, do, lse_rep, delta_rep, cq_rep, ck, name, beside=None):
    s, hd = q.shape
    h_n, d = hd // ATT_HEAD_DIM, ATT_HEAD_DIM
    tb = min(ATT_BLOCK, s)
    nb = s // tb
    scale = d ** -0.5

    def kern(q_ref, do_ref, k_ref, v_ref, lse_ref, dl_ref, cq_ref, ck_ref, dq_ref, dk_ref, dv_ref, dcq_ref, dck_ref):
        j = pl.program_id(1)

        @pl.when(j == 0)
        def _():
            dq_ref[...] = jnp.zeros_like(dq_ref)
            dcq_ref[...] = jnp.zeros_like(dcq_ref)

        kj, vj, ckj = k_ref[...], v_ref[...], ck_ref[...]
        colpos = j * tb + lax.broadcasted_iota(jnp.int32, (tb, tb), 1)
        rowi = lax.broadcasted_iota(jnp.int32, (tb, tb), 0)

        def step(i, carry, diagonal):
            dk, dv, dck = carry
            rs = pl.ds(pl.multiple_of(i * tb, tb), tb)
            qi, doi = q_ref[rs, :], do_ref[rs, :]
            lse = jnp.max(lse_ref[rs, :], axis=1, keepdims=True)
            dl = jnp.max(dl_ref[rs, :], axis=1, keepdims=True)
            cq = jnp.max(cq_ref[rs, :], axis=1, keepdims=True)
            sc = _dot(qi, kj, NT) * scale + (cq - ckj) - lse
            p = jnp.exp(jnp.where(colpos <= i * tb + rowi, sc, NEG) if diagonal else sc)
            dp = _dot(doi, vj, NT)
            ds = p * (dp - dl)
            dsb = ds.astype(BF16)
            dv = dv + _dot(p.astype(BF16), doi, TN)
            dk = dk + _dot(dsb, qi, TN)
            dq_ref[rs, :] += _dot(dsb, kj, NN) * scale
            dcq_ref[rs, :] += jnp.broadcast_to(jnp.sum(ds, axis=1, keepdims=True), (tb, d))
            return dk, dv, dck - jnp.sum(ds, axis=0, keepdims=True)

        init = (jnp.zeros((tb, d), F32), jnp.zeros((tb, d), F32), jnp.zeros((1, tb), F32))
        dk, dv, dck = lax.fori_loop(j + 1, nb, lambda i, carry: step(i, carry, False), step(j, init, True))
        dk_ref[...] = dk * scale
        dv_ref[...] = dv.astype(dv_ref.dtype)
        dck_ref[...] = dck

    whole = pl.BlockSpec((s, d), lambda h, j: (0, h))
    blk = pl.BlockSpec((tb, d), lambda h, j: (j, h))
    ckb = pl.BlockSpec((None, None, 1, tb), lambda h, j: (h, j, 0, 0))
    return _host(
        kern, beside, name=name, grid=(h_n, nb),
        in_specs=[whole, whole, blk, blk, whole, whole, whole, ckb],
        out_specs=[whole, blk, blk, whole, ckb],
        out_shape=[jax.ShapeDtypeStruct((s, hd), F32), jax.ShapeDtypeStruct((s, hd), F32),
                   jax.ShapeDtypeStruct((s, hd), BF16), jax.ShapeDtypeStruct((s, hd), F32),
                   jax.ShapeDtypeStruct((h_n, nb, 1, tb), F32)],
        scratch_shapes=[], semantics=("arbitrary", "arbitrary"), args=[q, do, k, v, lse_rep, delta_rep, cq_rep, ck])


def _adamw(w, g, m, v, name):
    cols = w.shape[1]
    tm = _tile(w.shape[0], 128) if w.shape[0] % 128 == 0 else w.shape[0]
    return _rowwise(_adamw_math, name, [w, g, m, v], outs=[(cols, F32)] * 3, tm=tm)


def _sum_arrays(arrs, out_dtype, name):
    def fn(*xs):
        acc = xs[0].astype(F32)
        for x in xs[1:]:
            acc = acc + x.astype(F32)
        return acc

    tm = PACK_ROWS if arrs[0].shape[0] % PACK_ROWS == 0 else arrs[0].shape[0]
    return _rowwise(fn, name, list(arrs), outs=[(arrs[0].shape[1], out_dtype)], tm=tm)[0]


def _half_tile(rows):
    for t in (256, 176, 128, 64, 32, 16):
        if rows % t == 0:
            return t
    return rows


def _sum_half(g, got, core, name):
    n_chip, hr, cols = got.shape
    tm = _half_tile(hr)
    nt = hr // tm

    def kern(c_ref, g_ref, a_ref, o_ref):
        o_ref[...] = (g_ref[...].astype(F32) + a_ref[...].astype(F32)).astype(o_ref.dtype)

    grid_spec = pltpu.PrefetchScalarGridSpec(
        num_scalar_prefetch=1, grid=(n_chip * nt,),
        in_specs=[pl.BlockSpec((tm, cols), lambda r, c: (((r // nt) * 2 + c[0]) * nt + r % nt, 0)),
                  pl.BlockSpec((tm, cols), lambda r, c: (r, 0))],
        out_specs=pl.BlockSpec((tm, cols), lambda r, c: (r, 0)))
    out = pl.pallas_call(
        kern, name=name, grid_spec=grid_spec, out_shape=jax.ShapeDtypeStruct((n_chip * hr, cols), BF16),
        compiler_params=_params(("arbitrary",)))(core.reshape(1), g.reshape(-1, cols), got.reshape(-1, cols))
    return out.reshape(n_chip, hr, cols)


def _sum_parts(own, parts, chip, name):
    n_parts, hr, cols = parts.shape
    tm = _half_tile(hr)

    def kern(s_ref, t_ref, p_ref, o_ref):
        acc = t_ref[...].astype(F32)
        for j in range(n_parts):
            acc = acc + p_ref[j].astype(F32)
        o_ref[...] = acc

    grid_spec = pltpu.PrefetchScalarGridSpec(
        num_scalar_prefetch=1, grid=(hr // tm,),
        in_specs=[pl.BlockSpec((None, tm, cols), lambda r, s: (s[0], r, 0)),
                  pl.BlockSpec((n_parts, tm, cols), lambda r, s: (0, r, 0))],
        out_specs=pl.BlockSpec((tm, cols), lambda r, s: (r, 0)))
    return pl.pallas_call(
        kern, name=name, grid_spec=grid_spec, out_shape=jax.ShapeDtypeStruct((hr, cols), F32),
        compiler_params=_params(("arbitrary",)))(chip.reshape(1), own, parts)


def _adamw_math(w, g, m, v):
    m = ADAM_B1 * m + (1.0 - ADAM_B1) * g
    v = ADAM_B2 * v + (1.0 - ADAM_B2) * (g * g)
    m_hat = m / (1.0 - ADAM_B1 ** ADAM_STEP)
    v_hat = v / (1.0 - ADAM_B2 ** ADAM_STEP)
    return -ADAM_LR * (m_hat / (jnp.sqrt(v_hat) + ADAM_EPS) + ADAM_WD * w), m, v


def _adamw_big(w, m, v, mine, theirs, core, layer, prev, name):
    n_layers, rows, cols = w.shape
    hr = rows // 2
    tm = next(t for t in (128, 64, 32, 16, 8) if hr % t == 0)
    nt = hr // tm

    def kern(s_ref, w_ref, m_ref, v_ref, a_ref, b_ref, *rest):
        g_ref, d_ref, nm_ref, nv_ref = rest[-4:]
        g = jnp.where(pl.program_id(0) // nt == s_ref[0], a_ref[...], b_ref[...])
        g_ref[...] = g
        d_ref[...], nm_ref[...], nv_ref[...] = _adamw_math(w_ref[...], g, m_ref[...], v_ref[...])

    lyr = pl.BlockSpec((None, tm, cols), lambda r, s: (layer, r, 0))
    half = pl.BlockSpec((tm, cols), lambda r, s: (r % nt, 0))
    passed = [] if prev is None else list(prev)
    grid_spec = pltpu.PrefetchScalarGridSpec(
        num_scalar_prefetch=1, grid=(rows // tm,),
        in_specs=[lyr, lyr, lyr, half, half] + [pl.BlockSpec(memory_space=pl.ANY)] * len(passed), out_specs=[lyr] * 4)
    return pl.pallas_call(
        kern, name=name, grid_spec=grid_spec, out_shape=[jax.ShapeDtypeStruct(w.shape, F32)] * 4,
        input_output_aliases={6 + i: i for i in range(len(passed))},
        compiler_params=_params(("arbitrary",)))(core.reshape(1), w, m, v, mine, theirs, *passed)


def _cast_into_slot(src, slot, name):
    rows, cols = src.shape
    tm = _half_tile(rows)

    def kern(s_ref, x_ref, o_ref):
        o_ref[...] = x_ref[...].astype(o_ref.dtype)

    grid_spec = pltpu.PrefetchScalarGridSpec(
        num_scalar_prefetch=1, grid=(rows // tm,), in_specs=[pl.BlockSpec((tm, cols), lambda r, s: (r, 0))],
        out_specs=pl.BlockSpec((None, tm, cols), lambda r, s: (s[0], r, 0)))
    return pl.pallas_call(
        kern, name=name, grid_spec=grid_spec, out_shape=jax.ShapeDtypeStruct((N_CHIPS, rows, cols), BF16),
        compiler_params=_params(("arbitrary",)))(slot.reshape(1), src)


FLIP_C = [(0, 0, 1)]
FLIP_XY = [(1, 0, 0), (0, 1, 0), (1, 1, 0)]
FLIP_ALL = [(fx, fy, fc) for fx in (0, 1) for fy in (0, 1) for fc in (0, 1) if (fx, fy, fc) != (0, 0, 0)]


def _chip(dev):
    return 2 * dev[0] + dev[1]


def _device(dev):
    return 4 * dev[0] + 2 * dev[1] + dev[2]


def _exchange(name, srcs, dst_shapes, rels, src_view, dst_view, own_view=None, in_place=False):
    n, n_rel = len(srcs), len(rels)

    def body(*refs):
        src_refs, dst_refs = refs[:n], refs[n:2 * n]
        send_sems, recv_sems, own_sems = refs[2 * n:]
        me = (lax.axis_index("x"), lax.axis_index("y"), lax.axis_index("c"))
        peers = [tuple(1 - m if f else m for m, f in zip(me, rel)) for rel in rels]

        def copy(i, k, sender, receiver):
            return pltpu.make_async_remote_copy(
                src_ref=src_view(i, src_refs[i], sender, receiver), dst_ref=dst_view(i, dst_refs[i], sender, k),
                send_sem=send_sems.at[i * n_rel + k], recv_sem=recv_sems.at[i * n_rel + k], device_id=receiver,
                device_id_type=pl.DeviceIdType.MESH)

        sends = [copy(i, k, me, peer) for i in range(n) for k, peer in enumerate(peers)]
        for cp in sends:
            cp.start()
        mine = []
        if own_view is not None:
            for i in range(n):
                frm, to = own_view(i, src_refs[i], dst_refs[i], me)
                mine.append(pltpu.make_async_copy(frm, to, own_sems.at[i]))
                mine[-1].start()
        for i in range(n):
            for k, peer in enumerate(peers):
                copy(i, k, peer, me).wait_recv()
        for cp in sends:
            cp.wait_send()
        for cp in mine:
            cp.wait()

    any_spec = pl.BlockSpec(memory_space=pl.ANY)
    return pl.pallas_call(
        body, name=name, out_shape=[jax.ShapeDtypeStruct(s, d) for s, d in dst_shapes],
        in_specs=[any_spec] * n, out_specs=[any_spec] * n,
        input_output_aliases={i: i for i in range(n)} if in_place else {},
        scratch_shapes=[pltpu.SemaphoreType.DMA((n * n_rel,)), pltpu.SemaphoreType.DMA((n * n_rel,)),
                        pltpu.SemaphoreType.DMA((n,))])(*srcs)


def _gather_plan(bufs):
    n = len(bufs)
    halves = [b.shape[1] // 2 for b in bufs]

    def tools(src_refs, dst_refs, sems):
        ici_send, ici_recv, d2d_send, d2d_recv = sems
        x, y, c = lax.axis_index("x"), lax.axis_index("y"), lax.axis_index("c")
        sibling = (x, y, 1 - c)
        chips = [(1 - x, y), (x, 1 - y), (1 - x, 1 - y)]

        def rows(i, chip, core):
            return dst_refs[i].at[2 * chip[0] + chip[1], pl.ds(core * halves[i], halves[i]), :]

        def over_ici(i, k, src, chip_from, to):
            return pltpu.make_async_remote_copy(
                src_ref=src, dst_ref=rows(i, chip_from, c), send_sem=ici_send.at[3 * i + k],
                recv_sem=ici_recv.at[3 * i + k], device_id=to, device_id_type=pl.DeviceIdType.MESH)

        def over_d2d(i, k, core):
            return pltpu.make_async_remote_copy(
                src_ref=rows(i, chips[k], core), dst_ref=rows(i, chips[k], core), send_sem=d2d_send.at[3 * i + k],
                recv_sem=d2d_recv.at[3 * i + k], device_id=sibling, device_id_type=pl.DeviceIdType.MESH)

        def my_send(i, k):
            my_half = src_refs[i].at[2 * x + y, pl.ds(c * halves[i], halves[i]), :]
            return over_ici(i, k, my_half, (x, y), (*chips[k], c))

        def my_arrival(i, k):
            return over_ici(i, k, rows(i, chips[k], c), chips[k], (x, y, c))

        return c, over_d2d, my_send, my_arrival

    def start(src_refs, dst_refs, sems):
        _, _, my_send, _ = tools(src_refs, dst_refs, sems)
        for i in range(n):
            for k in range(3):
                my_send(i, k).start()

    def finish(src_refs, dst_refs, sems):
        c, over_d2d, my_send, my_arrival = tools(src_refs, dst_refs, sems)
        passed = []
        for i in range(n):
            for k in range(3):
                my_arrival(i, k).wait_recv()
                passed.append(over_d2d(i, k, c))
                passed[-1].start()
        for i in range(n):
            for k in range(3):
                over_d2d(i, k, 1 - c).wait_recv()
        for i in range(n):
            for k in range(3):
                my_send(i, k).wait_send()
        for cp in passed:
            cp.wait_send()

    return _Beside(bufs, [(b.shape, b.dtype, i) for i, b in enumerate(bufs)], [3 * n] * 4, start, finish)


def _scatter_plan(sums):
    n = len(sums)

    def copy(i, k, src_refs, dst_refs, sems, sender, receiver):
        return pltpu.make_async_remote_copy(
            src_ref=src_refs[i].at[_chip(receiver)], dst_ref=dst_refs[i].at[k], send_sem=sems[0].at[3 * i + k],
            recv_sem=sems[1].at[3 * i + k], device_id=receiver, device_id_type=pl.DeviceIdType.MESH)

    def each(fn, src_refs, dst_refs, sems, outgoing):
        me = (lax.axis_index("x"), lax.axis_index("y"), lax.axis_index("c"))
        for i in range(n):
            for k, rel in enumerate(FLIP_XY):
                peer = tuple(1 - m if f else m for m, f in zip(me, rel))
                fn(copy(i, k, src_refs, dst_refs, sems, *((me, peer) if outgoing else (peer, me))))

    def start(src_refs, dst_refs, sems):
        each(lambda cp: cp.start(), src_refs, dst_refs, sems, True)

    def finish(src_refs, dst_refs, sems):
        each(lambda cp: cp.wait_recv(), src_refs, dst_refs, sems, False)
        each(lambda cp: cp.wait_send(), src_refs, dst_refs, sems, True)

    return _Beside(sums, [((3,) + s.shape[1:], s.dtype, None) for s in sums], [3 * n] * 2, start, finish)


def _pack(parts, dtype, multiple):
    flat = jnp.concatenate([p.reshape(-1).astype(dtype) for p in parts])
    pad = (-flat.shape[0]) % multiple
    return jnp.pad(flat, (0, pad)) if pad else flat


def _unpack_shards(packs, names, shapes, axes):
    out, off = {}, 0
    for nme in names:
        sz = math.prod(shapes[nme])
        out[nme] = jnp.concatenate([packs[j, off:off + sz].reshape(shapes[nme]) for j in range(N_CHIPS)], axis=axes[nme])
        off += sz
    return out


def _shards_of(full, axis):
    sz = full.shape[axis] // N_CHIPS
    return [lax.slice_in_dim(full, j * sz, (j + 1) * sz, axis=axis) for j in range(N_CHIPS)]


def kernel(x, a_norm_w, a_in_proj, a_conv_w, a_conv_b, a_dt_bias, a_A_log, a_D, a_gnorm_w, a_out_proj, kv_norm_w, w_kvf, b_f, k_norm_w, b_norm_w, w_q, q_norm_w, w_o, ffn_norm_w, w_gate_up, w_down, loss_target, m_a_norm_w, m_a_in_proj, m_a_conv_w, m_a_conv_b, m_a_dt_bias, m_a_A_log, m_a_D, m_a_gnorm_w, m_a_out_proj, m_kv_norm_w, m_w_kvf, m_b_f, m_k_norm_w, m_b_norm_w, m_w_q, m_q_norm_w, m_w_o, m_ffn_norm_w, m_w_gate_up, m_w_down, v_a_norm_w, v_a_in_proj, v_a_conv_w, v_a_conv_b, v_a_dt_bias, v_a_A_log, v_a_D, v_a_gnorm_w, v_a_out_proj, v_kv_norm_w, v_w_kvf, v_b_f, v_k_norm_w, v_b_norm_w, v_w_q, v_q_norm_w, v_w_o, v_ffn_norm_w, v_w_gate_up, v_w_down):
    args = locals()
    w = {n: args[n] for n in WEIGHTS}
    mom = {n: args['m_' + n] for n in WEIGHTS}
    var = {n: args['v_' + n] for n in WEIGHTS}
    shapes = {n: w[n].shape for n in WEIGHTS}

    cx, cy, cc = lax.axis_index("x"), lax.axis_index("y"), lax.axis_index("c")
    my_chip = 2 * cx + cy

    xs_in = x[0]
    target = loss_target[0]
    s_len, d_model = xs_in.shape
    n_heads_ssm = a_dt_bias.shape[-1]
    d_inner = n_heads_ssm * SSM_HEAD_DIM
    d_xbc = a_conv_w.shape[-1] * N_CHIPS
    d_state = (d_xbc - d_inner) // (2 * SSM_GROUPS)
    kh = n_heads_ssm // SSM_GROUPS
    n_att = b_f.shape[0]
    d_att = n_att * ATT_HEAD_DIM
    d_ff = w_down.shape[1] * N_CHIPS

    small_names = list(SMALL_SHARDED)
    sp = _pack([w[n] for n in small_names], F32, PACK_COLS).reshape(-1, PACK_COLS)
    sp_all = _exchange("gather_small", [sp], [((N_CHIPS,) + sp.shape, F32)], FLIP_XY,
                       lambda i, r, me, peer: r, lambda i, r, sender, k: r.at[_chip(sender)],
                       lambda i, s, d, me: (s, d.at[_chip(me)]))[0]
    full = _unpack_shards(sp_all.reshape(N_CHIPS, -1), small_names, shapes, SMALL_SHARDED)
    for n in SMALL:
        if n not in full:
            full[n] = w[n]

    big2d = {'in': a_in_proj[0], 'out': a_out_proj[0], 'gu0': w_gate_up[0], 'dn0': w_down[0], 'kvf': w_kvf,
             'q': w_q[0], 'o': w_o[0], 'gu1': w_gate_up[1], 'dn1': w_down[1]}
    big_keys = list(big2d)
    slot = {k: _cast_into_slot(big2d[k], my_chip, "cast_" + k) for k in big_keys}
    gathered = {}

    def gather(keys):
        return _gather_plan([slot[k] for k in keys])

    def pad_cols(a, width=LANES):
        return jnp.pad(a, ((0, 0), (0, width - a.shape[1])))

    gathered['in'], = _alone(gather(['in']), "gather_in")
    w_in = jnp.concatenate([gathered['in'][j] for j in range(N_CHIPS)], axis=1)
    w_z, w_xbc, w_dt = w_in[:, :d_inner], w_in[:, d_inner:d_inner + d_xbc], pad_cols(w_in[:, d_inner + d_xbc:])
    conv_w, conv_b = full['a_conv_w'][0], full['a_conv_b']
    norm_a, gnorm = full['a_norm_w'], full['a_gnorm_w']
    dt_bias, a_log = pad_cols(a_dt_bias), pad_cols(a_A_log)
    d_exp = jnp.repeat(a_D, SSM_HEAD_DIM, axis=1)
    kv_nw, b_nw = kv_norm_w.reshape(1, -1), b_norm_w
    k_nw, q_nw = k_norm_w.reshape(1, -1), q_norm_w
    bf_pad = pad_cols(b_f.reshape(1, -1))
    ffn_nw = [ffn_norm_w[i:i + 1] for i in range(2)]

    def to_groups(a):
        g = a[:, :n_heads_ssm].reshape(s_len, SSM_GROUPS, kh).transpose(1, 0, 2)
        return jnp.pad(g, ((0, 0), (0, 0), (0, LANES - kh)))

    def to_groups_t(a):
        g = a[:, :n_heads_ssm].reshape(s_len, SSM_GROUPS, kh).transpose(1, 2, 0)
        return jnp.pad(g, ((0, 0), (0, 8 - kh), (0, 0)))

    def from_groups(col, row=None):
        a = col[:, :, :kh].transpose(1, 0, 2).reshape(s_len, n_heads_ssm)
        if row is not None:
            a = a + row[:, :kh, :].transpose(2, 0, 1).reshape(s_len, n_heads_ssm)
        return pad_cols(a)

    n1 = _rms_fwd(xs_in, norm_a, "norm_a")
    z = _mm(n1, w_z, 'nn', "in_z")
    xbc_raw, (gathered['out'],) = _mm(n1, w_xbc, 'nn', "in_xbc", beside=gather(['out']))
    dtraw = _mm(n1, w_dt, 'nn', "in_dt")
    xbc = _conv_fwd(xbc_raw, conv_w, conv_b, "conv")
    dt, acum = _dt_fwd(dtraw, dt_bias, a_log, "dt")
    dt_g, ac_g, act_g = to_groups(dt), to_groups(acum), to_groups_t(acum)
    (y_ssd, states), (gathered['gu0'],) = _ssd_fwd(xbc, dt_g, ac_g, act_g, d_inner, d_state, kh, "ssd",
                                                   beside=gather(['gu0']))
    yn = _gate_fwd(y_ssd, xbc, z, d_exp, gnorm, d_inner, "gate")
    w_out = gathered['out'].reshape(-1, d_model)
    h1 = _mm(yn, w_out, 'nn', "out_proj", add=xs_in)

    w_gu, w_dn = {}, {}

    def ffn_fwd(h, i, also):
        nrm = _rms_fwd(h, ffn_nw[i], f"ffn{i}_norm")
        w_gu[i] = gathered[f'gu{i}']
        gu = _mm(nrm, w_gu[i], 'nn', f"ffn{i}_up", shards=True, beside=gather(also) if also else None)
        if also:
            gu, arrived = gu
            gathered.update(zip(also, arrived))
        act = _swiglu_fwd(gu, d_ff, f"ffn{i}_act")
        w_dn[i] = gathered[f'dn{i}'].reshape(-1, d_model)
        return nrm, gu, act, _mm(act, w_dn[i], 'nn', f"ffn{i}_down", add=h)

    n2, gu0, act0, h2 = ffn_fwd(h1, 0, ['dn0', 'kvf', 'q', 'o'])
    w_kvf_full = jnp.concatenate([gathered['kvf'][j] for j in range(N_CHIPS)], axis=1)
    w_k, w_v, w_f = w_kvf_full[:, :d_att], w_kvf_full[:, d_att:2 * d_att], pad_cols(w_kvf_full[:, 2 * d_att:])
    w_qm, w_om = gathered['q'].reshape(-1, d_att), gathered['o'].reshape(-1, d_model)
    nkv = _rms_fwd(h2, kv_nw, "kv_norm")
    k_raw = _mm(nkv, w_k, 'nn', "proj_k")
    v_att = _mm(nkv, w_v, 'nn', "proj_v", out_dtype=BF16)
    f_raw = _mm(nkv, w_f, 'nn', "proj_f")
    k_att = _rms_fwd(k_raw, k_nw, "k_norm")
    cum = _forget_fwd(f_raw, bf_pad, "forget")
    cq_rep = jnp.repeat(cum[:, :n_att], ATT_HEAD_DIM, axis=1)
    tb = min(ATT_BLOCK, s_len)
    ck = cum[:, :n_att].T.reshape(n_att, s_len // tb, 1, tb)
    n3 = _rms_fwd(h2, b_nw, "b_norm")
    q_raw = _mm(n3, w_qm, 'nn', "proj_q")
    q_att = _rms_fwd(q_raw, q_nw, "q_norm")
    (o_att, lse_rep), arrived = _attn_fwd(q_att, k_att, v_att, cq_rep, ck, "attn", beside=gather(['gu1', 'dn1']))
    gathered.update(zip(['gu1', 'dn1'], arrived))
    h3 = _mm(o_att, w_om, 'nn', "proj_o", add=h2)
    n4, gu1, act1, h4 = ffn_fwd(h3, 1, [])

    def loss_fn(h, t):
        err = h - t
        sq = jnp.sum(jnp.sum(err * err, axis=1, keepdims=True), axis=0, keepdims=True)
        return err * (1.0 / d_model), err * (1.0 / d_model), sq * (0.5 / d_model)

    dh4, dh4_b, loss_part = _rowwise(loss_fn, "loss", [h4, target], outs=[(d_model, F32), (d_model, BF16)],
                                     accs=[(1, 1)])

    n_f = n_att
    g_big, chip_sums, parts = {}, {}, {}

    def col_shards(pieces):
        fullw = jnp.concatenate(pieces, axis=1)
        return fullw.reshape(fullw.shape[0], N_CHIPS, -1).transpose(1, 0, 2)

    def row_shards(a):
        return a.reshape(N_CHIPS, -1, a.shape[1])

    def scatter(keys, tag):
        g_list = [g_big[k] for k in keys]
        hrs = [g.shape[1] // 2 for g in g_list]
        got = _exchange("reduce_d2d_" + tag, g_list,
                        [((N_CHIPS, hr, g.shape[2]), BF16) for g, hr in zip(g_list, hrs)], FLIP_C,
                        lambda i, r, me, peer: r.at[:, pl.ds((1 - me[2]) * hrs[i], hrs[i]), :],
                        lambda i, r, sender, k: r)
        for k, g, a in zip(keys, g_list, got):
            chip_sums[k] = _sum_half(g, a, cc, "reduce_sum2_" + k)
        return _scatter_plan([chip_sums[k] for k in keys])

    def ffn_bwd(dh, dh_b, h, nrm, gu, act, i, plan, plan_keys):
        dact = _mm(dh_b, w_dn[i], 'nt', f"ffn{i}_down_dx", beside=plan)
        if plan is not None:
            dact, arrived = dact
            parts.update(zip(plan_keys, arrived))
        g_dn = _mm(act, dh_b, 'tn', f"ffn{i}_down_dw", out_dtype=BF16)
        dgu = _swiglu_bwd(gu, dact, d_ff, f"ffn{i}_act_bwd")
        dn = _mm(dgu, w_gu[i], 'nt', f"ffn{i}_up_dx", shards=True)
        g_gu = _mm(nrm, dgu, 'tn', f"ffn{i}_up_dw", out_dtype=BF16, shards=True)
        dh_new, dh_new_b, g_nw = _rms_bwd(h, ffn_nw[i], dn, f"ffn{i}_norm_bwd", extra=dh, copy_bf16=True)
        return dh_new, dh_new_b, g_dn, g_gu, g_nw

    dh3, dh3_b, g_dn1, g_gu1, g_fnw1 = ffn_bwd(dh4, dh4_b, h3, n4, gu1, act1, 1, None, [])
    g_big.update(gu1=g_gu1, dn1=row_shards(g_dn1))
    plan_ffn1 = scatter(['gu1', 'dn1'], "ffn1")
    do_att = _mm(dh3_b, w_om, 'nt', "proj_o_dx", out_dtype=BF16)
    g_wo = _mm(o_att, dh3_b, 'tn', "proj_o_dw", out_dtype=BF16)

    def delta_fn(a, b):
        prod = a.astype(F32) * b.astype(F32)
        return jnp.concatenate([jnp.broadcast_to(jnp.sum(prod[:, g], axis=1, keepdims=True), (a.shape[0], ATT_HEAD_DIM))
                                for g in _lane_groups(a.shape[1], ATT_HEAD_DIM)], axis=1)

    delta_rep = _rowwise(delta_fn, "attn_delta", [do_att, o_att], outs=[(d_att, F32)])[0]
    (dq_att, dk_att, dv_att, dcq_rep, dck), arrived = _attn_bwd(q_att, k_att, v_att, do_att, lse_rep, delta_rep,
                                                                cq_rep, ck, "attn_bwd", beside=plan_ffn1)
    parts.update(zip(['gu1', 'dn1'], arrived))
    dq_raw, g_qnw = _rms_bwd(q_raw, q_nw, dq_att, "q_norm_bwd", out_dtype=BF16)
    dn3 = _mm(dq_raw, w_qm, 'nt', "proj_q_dx")
    g_wq = _mm(n3, dq_raw, 'tn', "proj_q_dw", out_dtype=BF16)
    dh2, g_bnw = _rms_bwd(h2, b_nw, dn3, "b_norm_bwd", extra=dh3)
    dk_raw, g_knw = _rms_bwd(k_raw, k_nw, dk_att, "k_norm_bwd", out_dtype=BF16)
    dcum = pad_cols(dcq_rep.reshape(s_len, n_att, ATT_HEAD_DIM)[:, :, 0] + dck.reshape(n_att, s_len).T)
    df_raw, g_bf = _forget_bwd(f_raw, bf_pad, dcum, "forget_bwd")
    dnkv = _mm(dk_raw, w_k, 'nt', "proj_k_dx")
    dnkv = _mm(dv_att, w_v, 'nt', "proj_v_dx", add=dnkv)
    dnkv = _mm(df_raw, w_f, 'nt', "proj_f_dx", add=dnkv)
    g_wk = _mm(nkv, dk_raw, 'tn', "proj_k_dw", out_dtype=BF16)
    g_wv = _mm(nkv, dv_att, 'tn', "proj_v_dw", out_dtype=BF16)
    g_wf = _mm(nkv, df_raw, 'tn', "proj_f_dw", out_dtype=BF16)
    dh2, dh2_b, g_kvnw = _rms_bwd(h2, kv_nw, dnkv, "kv_norm_bwd", extra=dh2, copy_bf16=True)
    g_big.update(o=row_shards(g_wo), q=row_shards(g_wq), kvf=col_shards([g_wk, g_wv, g_wf[:, :n_f]]))
    att_keys = ['o', 'q', 'kvf']
    dh1, dh1_b, g_dn0, g_gu0, g_fnw0 = ffn_bwd(dh2, dh2_b, h1, n2, gu0, act0, 0, scatter(att_keys, "att"), att_keys)
    g_big.update(gu0=g_gu0, dn0=row_shards(g_dn0))
    plan_ffn0 = scatter(['gu0', 'dn0'], "ffn0")

    dyn = _mm(dh1_b, w_out, 'nt', "out_proj_dx")
    g_wout = _mm(yn, dh1_b, 'tn', "out_proj_dw", out_dtype=BF16)
    dy_ssd, dxs_skip, dz, g_dexp, g_gnorm = _gate_bwd(y_ssd, xbc, z, d_exp, gnorm, dyn, d_inner, "gate_bwd")
    (dxs, d_b, d_c, ddt_g, dacc_g, dacr_g), arrived = _ssd_bwd(xbc, dt_g, ac_g, act_g, states, dy_ssd, dxs_skip,
                                                               d_inner, d_state, kh, "ssd_bwd", beside=plan_ffn0)
    parts.update(zip(['gu0', 'dn0'], arrived))
    g_big.update(out=row_shards(g_wout))
    plan_out = scatter(['out'], "out")
    dxbc_act = jnp.concatenate([dxs, d_b, d_c], axis=1)
    du, g_convw, g_convb = _conv_bwd(xbc_raw, conv_w, conv_b, dxbc_act, "conv_bwd")
    draw, g_dtb, g_alog = _dt_bwd(dtraw, dt_bias, a_log, from_groups(ddt_g), from_groups(dacc_g, dacr_g), "dt_bwd")
    dn1 = _mm(dz, w_z, 'nt', "in_z_dx")
    dn1, (parts['out'],) = _mm(du, w_xbc, 'nt', "in_xbc_dx", add=dn1, beside=plan_out)
    dn1 = _mm(draw, w_dt, 'nt', "in_dt_dx", add=dn1)
    g_wz = _mm(n1, dz, 'tn', "in_z_dw", out_dtype=BF16)
    g_wxbc = _mm(n1, du, 'tn', "in_xbc_dw", out_dtype=BF16)
    g_wdt = _mm(n1, draw, 'tn', "in_dt_dw", out_dtype=BF16)
    dx, g_norm_a = _rms_bwd(xs_in, norm_a, dn1, "norm_a_bwd", extra=dh1)
    g_big.update({'in': col_shards([g_wz, g_wxbc, g_wdt[:, :n_heads_ssm]])})
    parts['in'], = _alone(scatter(['in'], "in"), "reduce_ici_in")

    g_small = {
        'a_norm_w': g_norm_a, 'a_conv_w': g_convw[None], 'a_conv_b': g_convb,
        'a_dt_bias': g_dtb[:, :n_heads_ssm], 'a_A_log': g_alog[:, :n_heads_ssm],
        'a_D': g_dexp.reshape(n_heads_ssm, SSM_HEAD_DIM).sum(axis=1).reshape(1, -1), 'a_gnorm_w': g_gnorm,
        'kv_norm_w': g_kvnw.reshape(-1), 'b_f': g_bf[0, :n_f], 'k_norm_w': g_knw.reshape(-1), 'b_norm_w': g_bnw,
        'q_norm_w': g_qnw, 'ffn_norm_w': jnp.concatenate([g_fnw0, g_fnw1], axis=0),
    }

    sg = _pack([g_small[n] for n in SMALL] + [loss_part], F32, 8 * PACK_COLS).reshape(-1, PACK_COLS)
    sg_all = _exchange("reduce_small", [sg], [((2 * N_CHIPS,) + sg.shape, F32)], FLIP_ALL,
                       lambda i, r, me, peer: r, lambda i, r, sender, k: r.at[_device(sender)],
                       lambda i, s, d, me: (s, d.at[_device(me)]))[0]
    sg_sum = _sum_arrays([sg_all[d] for d in range(2 * N_CHIPS)], F32, "reduce_small_sum").reshape(-1)
    red_small, off = {}, 0
    for n in SMALL:
        shp = g_small[n].shape
        red_small[n] = sg_sum[off:off + math.prod(shp)].reshape(shp)
        off += math.prod(shp)
    loss = sg_sum[off]

    red_keys = big_keys[::-1]
    half_sums = [_sum_parts(chip_sums[k], parts[k], my_chip, "reduce_sum4_" + k) for k in red_keys]
    others = _exchange("reduce_back", half_sums, [(h.shape, F32) for h in half_sums], FLIP_C,
                       lambda i, r, me, peer: r, lambda i, r, sender, k: r)
    mine_of, theirs_of = dict(zip(red_keys, half_sums)), dict(zip(red_keys, others))

    grads, delta, new_m, new_v = {}, {}, {}, {}
    layers_of = {'a_in_proj': ['in'], 'a_out_proj': ['out'], 'w_kvf': ['kvf'], 'w_q': ['q'], 'w_o': ['o'],
                 'w_gate_up': ['gu0', 'gu1'], 'w_down': ['dn0', 'dn1']}
    for n, keys in layers_of.items():
        three_d = (len(keys),) + tuple(shapes[n][-2:])
        res = None
        for layer, k in enumerate(keys):
            res = _adamw_big(w[n].reshape(three_d), mom[n].reshape(three_d), var[n].reshape(three_d), mine_of[k],
                             theirs_of[k], cc, layer, res, "adamw_" + k)
        grads[n], delta[n], new_m[n], new_v[n] = [r.reshape(shapes[n]) for r in res]
    for n in SMALL:
        if n in SMALL_SHARDED:
            ax = SMALL_SHARDED[n]
            grads[n] = lax.dynamic_slice_in_dim(red_small[n], my_chip * shapes[n][ax], shapes[n][ax], axis=ax)
        else:
            grads[n] = red_small[n]

    packed = [_pack([src[n] for n in SMALL], F32, 8 * LANES).reshape(-1, LANES) for src in (w, grads, mom, var)]
    small_out = _adamw(*packed, "adamw_small")
    for store, flat in zip((delta, new_m, new_v), small_out):
        flat, off = flat.reshape(-1), 0
        for n in SMALL:
            sz = math.prod(shapes[n])
            store[n] = flat[off:off + sz].reshape(shapes[n])
            off += sz

    return (loss, dx[None], *[grads[n] for n in WEIGHTS], *[delta[n] for n in WEIGHTS],
            *[new_m[n] for n in WEIGHTS], *[new_v[n] for n in WEIGHTS])
```

```python
import functools
import math

import jax
import jax.numpy as jnp
from jax import lax
from jax.experimental import pallas as pl
from jax.experimental.pallas import tpu as pltpu

F32, BF16 = jnp.float32, jnp.bfloat16
EPS = 1e-6
SSM_HEAD_DIM = 64
SSM_GROUPS = 8
SSD_CHUNK = 128
ATT_HEAD_DIM = 128
LANES = 128
N_CHIPS = 4
NEG = -1e30
ADAM_LR, ADAM_B1, ADAM_B2, ADAM_EPS, ADAM_WD, ADAM_STEP = 0.001, 0.9, 0.999, 1e-08, 0.01, 10
VMEM_LIMIT_BYTES = 56 * 1024 * 1024
PACK_COLS = 1024
PACK_ROWS = 256
MM_K_TILES = (2816, 2048, 1408, 1024, 512, 256, 128)
MM_OPERAND_BYTES = 12 * 1024 * 1024

NN = ((1,), (0,))
NT = ((1,), (1,))
TN = ((0,), (0,))

WEIGHTS = ['a_norm_w', 'a_in_proj', 'a_conv_w', 'a_conv_b', 'a_dt_bias', 'a_A_log', 'a_D', 'a_gnorm_w', 'a_out_proj',
           'kv_norm_w', 'w_kvf', 'b_f', 'k_norm_w', 'b_norm_w', 'w_q', 'q_norm_w', 'w_o', 'ffn_norm_w', 'w_gate_up',
           'w_down']
BIG = {'a_in_proj': 2, 'a_out_proj': 1, 'w_kvf': 1, 'w_q': 1, 'w_o': 1, 'w_gate_up': 2, 'w_down': 1}
SMALL_SHARDED = {'a_norm_w': 1, 'a_conv_w': 2, 'a_conv_b': 1, 'a_gnorm_w': 1}
SMALL = [n for n in WEIGHTS if n not in BIG]


def _dot(a, b, dims):
    return lax.dot_general(a, b, (dims, ((), ())), preferred_element_type=F32)


def _params(sem=None):
    return pltpu.CompilerParams(dimension_semantics=sem, vmem_limit_bytes=VMEM_LIMIT_BYTES)


def _tile(dim, cap):
    for t in (1408, 1024, 512, 256, 128):
        if t <= cap and dim % t == 0:
            return t
    return dim


class _Beside:
    def __init__(self, operands, results, sem_sizes, start, finish):
        self.operands, self.results, self.sem_sizes = list(operands), list(results), list(sem_sizes)
        self.start, self.finish = start, finish


def _host(kern, beside, *, name, grid, in_specs, out_specs, out_shape, scratch_shapes, semantics, args):
    single = not isinstance(out_shape, (list, tuple))
    out_specs = [out_specs] if single else list(out_specs)
    out_shape = [out_shape] if single else list(out_shape)
    if beside is None:
        res = pl.pallas_call(kern, name=name, grid=grid, in_specs=in_specs, out_specs=out_specs, out_shape=out_shape,
                             scratch_shapes=scratch_shapes, compiler_params=_params(semantics))(*args)
        return (res[0] if single else res), []
    n_in, n_out, n_scr = len(in_specs), len(out_specs), len(scratch_shapes)
    nb_in, nb_out = len(beside.operands), len(beside.results)

    def body(*refs):
        ins, b_ins = refs[:n_in], refs[n_in:n_in + nb_in]
        outs = refs[n_in + nb_in:n_in + nb_in + n_out]
        b_outs = refs[n_in + nb_in + n_out:n_in + nb_in + n_out + nb_out]
        rest = refs[n_in + nb_in + n_out + nb_out:]
        scr, sems = rest[:n_scr], rest[n_scr:]
        ids = [pl.program_id(a) for a in range(len(grid))]
        first = functools.reduce(jnp.logical_and, [i == 0 for i in ids])
        last = functools.reduce(jnp.logical_and, [i == g - 1 for i, g in zip(ids, grid)])

        @pl.when(first)
        def _():
            beside.start(b_ins, b_outs, sems)

        kern(*ins, *outs, *scr)

        @pl.when(last)
        def _():
            beside.finish(b_ins, b_outs, sems)

    any_spec = pl.BlockSpec(memory_space=pl.ANY)
    res = pl.pallas_call(
        body, name=name, grid=grid, in_specs=list(in_specs) + [any_spec] * nb_in,
        out_specs=out_specs + [any_spec] * nb_out,
        out_shape=out_shape + [jax.ShapeDtypeStruct(s, d) for s, d, _ in beside.results],
        input_output_aliases={n_in + op: n_out + r for r, (_, _, op) in enumerate(beside.results) if op is not None},
        scratch_shapes=list(scratch_shapes) + [pltpu.SemaphoreType.DMA((k,)) for k in beside.sem_sizes],
        compiler_params=_params(("arbitrary",) * len(grid)))(*args, *beside.operands)
    mine = res[:n_out]
    return (mine[0] if single else mine), list(res[n_out:])


def _alone(beside, name):
    return _host(lambda: None, beside, name=name, grid=(1,), in_specs=[], out_specs=[], out_shape=[],
                 scratch_shapes=[], semantics=("arbitrary",), args=[])[1]


def _mm(a, b, mode, name, out_dtype=F32, add=None, shards=False, beside=None):
    if mode == 'nn':
        (m, k), n = a.shape, (b.shape[2] * N_CHIPS if shards else b.shape[1])
    elif mode == 'nt':
        (m, k), n = a.shape, (b.shape[1] if shards else b.shape[0])
    else:
        (k, m), n = a.shape, b.shape[1]
    per_chip = (k if mode == 'nt' else n) // N_CHIPS
    tm = _tile(m, 1024)
    tn = _tile(per_chip if shards and mode != 'nt' else n, 1408 if shards else 1024)
    k_dim = per_chip if shards and mode == 'nt' else k
    a_bytes, b_bytes = jnp.dtype(a.dtype).itemsize, jnp.dtype(b.dtype).itemsize
    tk = next((t for t in MM_K_TILES if k_dim % t == 0 and t * (tm * a_bytes + tn * b_bytes) <= MM_OPERAND_BYTES), k_dim)
    nk = k // tk
    in_place = nk > 1 and out_dtype == F32
    a_spec = pl.BlockSpec((tk, tm), lambda i, j, q: (q, i)) if mode == 'tn' else pl.BlockSpec((tm, tk), lambda i, j, q: (i, q))
    b_spec = pl.BlockSpec((tn, tk), lambda i, j, q: (j, q)) if mode == 'nt' else pl.BlockSpec((tk, tn), lambda i, j, q: (q, j))
    o_spec = pl.BlockSpec((tm, tn), lambda i, j, q: (i, j))
    out_struct = jax.ShapeDtypeStruct((m, n), out_dtype)
    if shards:
        per = per_chip // (tk if mode == 'nt' else tn)
        if mode == 'nn':
            b_spec = pl.BlockSpec((None, tk, tn), lambda i, j, q: (j // per, q, j % per))
        elif mode == 'nt':
            b_spec = pl.BlockSpec((None, tn, tk), lambda i, j, q: (q // per, j, q % per))
        else:
            out_struct = jax.ShapeDtypeStruct((N_CHIPS, m, per_chip), out_dtype)
    out_spec = pl.BlockSpec((None, tm, tn), lambda i, j, q: (j // per, i, j % per)) if shards and mode == 'tn' else o_spec
    dims = {'nn': NN, 'nt': NT, 'tn': TN}[mode]

    n_ins = 3 if add is not None else 2

    def kern(*refs):
        a_ref, b_ref, o_ref = refs[0], refs[1], refs[n_ins]
        acc = o_ref if in_place or nk == 1 else refs[n_ins + 1]
        q = pl.program_id(2)
        part = _dot(a_ref[...].astype(BF16), b_ref[...].astype(BF16), dims)

        def first():
            return part if add is None else part + refs[2][...]

        if nk == 1:
            o_ref[...] = first().astype(o_ref.dtype)
            return

        @pl.when(q == 0)
        def _():
            acc[...] = first()

        @pl.when(q > 0)
        def _():
            acc[...] += part

        if not in_place:
            @pl.when(q == nk - 1)
            def _():
                o_ref[...] = acc[...].astype(o_ref.dtype)

    ins, specs = [a, b], [a_spec, b_spec]
    if add is not None:
        ins.append(add)
        specs.append(o_spec)
    scratch = [] if in_place or nk == 1 else [pltpu.VMEM((tm, tn), F32)]
    res, extra = _host(kern, beside, name=name, grid=(m // tm, n // tn, nk), in_specs=specs, out_specs=out_spec,
                       out_shape=out_struct, scratch_shapes=scratch,
                       semantics=("parallel", "parallel", "arbitrary"), args=ins)
    return res if beside is None else (res, extra)


def _rowwise(fn, name, rows, bcast=(), outs=(), accs=(), tm=256, beside=None):
    rows = [r if isinstance(r, tuple) else (r, r.shape[1], 0) for r in rows]
    n_rows = rows[0][0].shape[0]
    tm = min(tm, n_rows)
    assert n_rows % tm == 0, (name, n_rows, tm)
    n_in, n_out = len(rows) + len(bcast), len(outs)
    in_specs = [pl.BlockSpec((tm, w), functools.partial(lambda i, cb: (i, cb), cb=cb)) for _, w, cb in rows]
    in_specs += [pl.BlockSpec(b.shape, lambda i: (0, 0)) for b in bcast]
    out_specs = [pl.BlockSpec((tm, w), lambda i: (i, 0)) for w, _ in outs]
    out_specs += [pl.BlockSpec(s, lambda i: (0, 0)) for s in accs]
    out_shape = [jax.ShapeDtypeStruct((n_rows, w), d) for w, d in outs] + [jax.ShapeDtypeStruct(s, F32) for s in accs]

    def kern(*refs):
        vals = fn(*[r[...] for r in refs[:n_in]])
        vals = vals if isinstance(vals, (tuple, list)) else (vals,)
        o_refs = refs[n_in:]
        for r, v in zip(o_refs[:n_out], vals[:n_out]):
            r[...] = v.astype(r.dtype)
        if accs:
            @pl.when(pl.program_id(0) == 0)
            def _():
                for r in o_refs[n_out:]:
                    r[...] = jnp.zeros_like(r)

            for r, v in zip(o_refs[n_out:], vals[n_out:]):
                r[...] += v

    res, extra = _host(kern, beside, name=name, grid=(n_rows // tm,), in_specs=in_specs, out_specs=out_specs,
                       out_shape=out_shape, scratch_shapes=[], semantics=("arbitrary",),
                       args=[r[0] for r in rows] + list(bcast))
    return res if beside is None else (res, extra)


def _rms(x, w):
    xf = x.astype(F32)
    return xf * lax.rsqrt(jnp.mean(xf * xf, axis=-1, keepdims=True) + EPS) * w


def _lane_groups(width, group):
    return [slice(g * group, (g + 1) * group) for g in range(width // group)]


def _rms_fwd(x, w, name, tm=256):
    def fn(x, w):
        return jnp.concatenate([_rms(x[:, g], w) for g in _lane_groups(x.shape[1], w.shape[1])], axis=1)

    return _rowwise(fn, name, [x], [w], outs=[(x.shape[1], BF16)], tm=tm)[0]


def _rms_bwd(x, w, dy, name, extra=None, out_dtype=F32, tm=256, copy_bf16=False):
    def fn(x, dy, *rest):
        w = rest[-1]
        dxs, dw = [], jnp.zeros(w.shape, F32)
        for g in _lane_groups(x.shape[1], w.shape[1]):
            _, vjp = jax.vjp(_rms, x[:, g], w)
            dx_g, dw_g = vjp(dy[:, g].astype(F32))
            dxs.append(dx_g)
            dw = dw + dw_g
        dx = jnp.concatenate(dxs, axis=1)
        if extra is not None:
            dx = dx + rest[0]
        return (dx, dx, dw) if copy_bf16 else (dx, dw)

    rows = [x, dy] + ([extra] if extra is not None else [])
    outs = [(x.shape[1], out_dtype)] + ([(x.shape[1], BF16)] if copy_bf16 else [])
    return _rowwise(fn, name, rows, [w], outs=outs, accs=[w.shape], tm=tm)


def _sigmoid(x):
    return 1.0 / (1.0 + jnp.exp(-x))


def _softplus(x):
    return jnp.maximum(x, 0.0) + jnp.log(1.0 + jnp.exp(-jnp.abs(x)))


def _swiglu_fwd(gu, d_ff, name):
    def fn(g, u):
        return g * _sigmoid(g) * u

    return _rowwise(fn, name, [(gu, d_ff, 0), (gu, d_ff, 1)], outs=[(d_ff, BF16)], tm=128)[0]


def _swiglu_bwd(gu, dact, d_ff, name):
    def fn(g, u, da):
        s = _sigmoid(g)
        dg = da * u * s * (1.0 + g * (1.0 - s))
        du = da * g * s
        return jnp.concatenate([dg, du], axis=1)

    return _rowwise(fn, name, [(gu, d_ff, 0), (gu, d_ff, 1), dact], outs=[(2 * d_ff, BF16)], tm=128)[0]


def _cumsum_rows(v, reverse=False):
    n = v.shape[0]
    row = lax.broadcasted_iota(jnp.int32, v.shape, 0)
    sh = 1
    while sh < n:
        if reverse:
            v = v + jnp.where(row < n - sh, pltpu.roll(v, n - sh, 0), 0.0)
        else:
            v = v + jnp.where(row >= sh, pltpu.roll(v, sh, 0), 0.0)
        sh *= 2
    return v


def _conv_fwd(u, w, b, name):
    s, c = u.shape
    kw = w.shape[0]
    tc = _tile(c, 128)

    def kern(u_ref, w_ref, b_ref, o_ref):
        uu = u_ref[...]
        row = lax.broadcasted_iota(jnp.int32, uu.shape, 0)
        acc = jnp.zeros_like(uu) + b_ref[...]
        for k in range(kw):
            sh = kw - 1 - k
            uk = uu if sh == 0 else jnp.where(row >= sh, pltpu.roll(uu, sh, 0), 0.0)
            acc = acc + w_ref[pl.ds(k, 1), :] * uk
        o_ref[...] = acc * _sigmoid(acc)

    return pl.pallas_call(
        kern, name=name, grid=(c // tc,),
        in_specs=[pl.BlockSpec((s, tc), lambda j: (0, j)), pl.BlockSpec((kw, tc), lambda j: (0, j)),
                  pl.BlockSpec((1, tc), lambda j: (0, j))],
        out_specs=pl.BlockSpec((s, tc), lambda j: (0, j)), out_shape=jax.ShapeDtypeStruct((s, c), F32),
        compiler_params=_params(("parallel",)))(u, w, b)


def _conv_bwd(u, w, b, dact, name, beside=None):
    s, c = u.shape
    kw = w.shape[0]
    tc = _tile(c, 128)

    def kern(u_ref, w_ref, b_ref, d_ref, du_ref, dw_ref, db_ref):
        uu = u_ref[...]
        row = lax.broadcasted_iota(jnp.int32, uu.shape, 0)
        shifted = []
        acc = jnp.zeros_like(uu) + b_ref[...]
        for k in range(kw):
            sh = kw - 1 - k
            uk = uu if sh == 0 else jnp.where(row >= sh, pltpu.roll(uu, sh, 0), 0.0)
            shifted.append(uk)
            acc = acc + w_ref[pl.ds(k, 1), :] * uk
        sg = _sigmoid(acc)
        dacc = d_ref[...] * sg * (1.0 + acc * (1.0 - sg))
        db_ref[...] = jnp.sum(dacc, axis=0, keepdims=True)
        du = jnp.zeros_like(uu)
        for k in range(kw):
            sh = kw - 1 - k
            dw_ref[pl.ds(k, 1), :] = jnp.sum(dacc * shifted[k], axis=0, keepdims=True)
            dk = dacc if sh == 0 else jnp.where(row < s - sh, pltpu.roll(dacc, s - sh, 0), 0.0)
            du = du + w_ref[pl.ds(k, 1), :] * dk
        du_ref[...] = du.astype(du_ref.dtype)

    col = lambda j: (0, j)
    return _host(
        kern, beside, name=name, grid=(c // tc,),
        in_specs=[pl.BlockSpec((s, tc), col), pl.BlockSpec((kw, tc), col), pl.BlockSpec((1, tc), col),
                  pl.BlockSpec((s, tc), col)],
        out_specs=[pl.BlockSpec((s, tc), col), pl.BlockSpec((kw, tc), col), pl.BlockSpec((1, tc), col)],
        out_shape=[jax.ShapeDtypeStruct((s, c), BF16), jax.ShapeDtypeStruct((kw, c), F32),
                   jax.ShapeDtypeStruct((1, c), F32)],
        scratch_shapes=[], semantics=("parallel",), args=[u, w, b, dact])


def _dt_fwd(dtraw, bias, a_log, name):
    def fn(raw, bias, a_log):
        dt = _softplus(raw + bias)
        return dt, _cumsum_rows(dt * (-jnp.exp(a_log)))

    return _rowwise(fn, name, [dtraw], [bias, a_log], outs=[(LANES, F32), (LANES, F32)], tm=SSD_CHUNK)


def _dt_bwd(dtraw, bias, a_log, ddt, dacum, name):
    def fn(raw, ddt, dac, bias, a_log):
        z = raw + bias
        dt = _softplus(z)
        a_neg = -jnp.exp(a_log)
        da = _cumsum_rows(dac, reverse=True)
        draw = (ddt + da * a_neg) * _sigmoid(z)
        return draw, jnp.sum(draw, axis=0, keepdims=True), jnp.sum(da * dt, axis=0, keepdims=True) * a_neg

    return _rowwise(fn, name, [dtraw, ddt, dacum], [bias, a_log], outs=[(LANES, BF16)],
                    accs=[(1, LANES), (1, LANES)], tm=SSD_CHUNK)


def _ssd_pieces(xs, dt, ac, kh):
    l, gw = xs.shape
    lane_w = lax.broadcasted_iota(jnp.int32, (l, gw), 1)
    lane_k = lax.broadcasted_iota(jnp.int32, (l, LANES), 1)
    head = [(lane_w >= k * SSM_HEAD_DIM) & (lane_w < (k + 1) * SSM_HEAD_DIM) for k in range(kh)]

    def col(blk, k):
        return jnp.sum(jnp.where(lane_k == k, blk, 0.0), axis=1, keepdims=True)

    def expand(blk):
        acc = jnp.zeros((l, gw), F32)
        for k in range(kh):
            acc = jnp.where(head[k], col(blk, k), acc)
        return acc

    def collapse(wide):
        acc = jnp.zeros((l, LANES), F32)
        for k in range(kh):
            acc = jnp.where(lane_k == k, jnp.sum(jnp.where(head[k], wide, 0.0), axis=1, keepdims=True), acc)
        return acc

    return head, lane_k, col, expand, collapse


def _ssd_specs(l, gw, n, n_xs_blocks):
    g_axis = SSM_GROUPS
    return dict(
        xs=lambda cm: pl.BlockSpec((l, gw), lambda g, c: (cm(c), g)),
        b=lambda cm: pl.BlockSpec((l, n), lambda g, c: (cm(c), n_xs_blocks + g)),
        c=lambda cm: pl.BlockSpec((l, n), lambda g, c: (cm(c), n_xs_blocks + g_axis + g)),
        col=lambda cm: pl.BlockSpec((None, l, LANES), lambda g, c: (g, cm(c), 0)),
        row=lambda cm: pl.BlockSpec((None, 8, l), lambda g, c: (g, 0, cm(c))),
        state=lambda cm: pl.BlockSpec((None, None, n, gw), lambda g, c: (cm(c), g, 0, 0)),
    )


def _ssd_fwd(xbc, dt_g, ac_g, act_g, d_inner, n, kh, name, beside=None):
    s = xbc.shape[0]
    l, g_n = SSD_CHUNK, SSM_GROUPS
    gw, nc = d_inner // g_n, s // l
    sp = _ssd_specs(l, gw, n, d_inner // n)
    fwd = lambda c: c

    def kern(xs_ref, b_ref, c_ref, dt_ref, ac_ref, act_ref, y_ref, s0_ref, st):
        @pl.when(pl.program_id(1) == 0)
        def _():
            st[...] = jnp.zeros_like(st)

        s0 = st[...]
        s0_ref[...] = s0
        xs, ac = xs_ref[...], ac_ref[...]
        head, _, col, expand, _ = _ssd_pieces(xs, dt_ref[...], ac, kh)
        ace = expand(ac)
        x = xs * expand(dt_ref[...])
        xb, bb, cb_ = x.astype(BF16), b_ref[...].astype(BF16), c_ref[...].astype(BF16)
        cb = _dot(cb_, bb, NT)
        ri = lax.broadcasted_iota(jnp.int32, (l, l), 0)
        ci = lax.broadcasted_iota(jnp.int32, (l, l), 1)
        causal = ri >= ci
        y = _dot(cb_, s0.astype(BF16), NN) * jnp.exp(ace)
        for k in range(kh):
            seg = col(ac, k) - act_ref[pl.ds(k, 1), :]
            m = jnp.where(causal, cb * jnp.exp(jnp.where(causal, seg, 0.0)), 0.0)
            y = jnp.where(head[k], y + _dot(m.astype(BF16), xb, NN), y)
        y_ref[...] = y
        row_w = lax.broadcasted_iota(jnp.int32, (l, gw), 0)
        alast = jnp.sum(jnp.where(row_w == l - 1, ace, 0.0), axis=0, keepdims=True)
        st[...] = s0 * jnp.exp(alast) + _dot(bb, (jnp.exp(alast - ace) * x).astype(BF16), TN)

    return _host(
        kern, beside, name=name, grid=(g_n, nc),
        in_specs=[sp['xs'](fwd), sp['b'](fwd), sp['c'](fwd), sp['col'](fwd), sp['col'](fwd), sp['row'](fwd)],
        out_specs=[pl.BlockSpec((l, gw), lambda g, c: (c, g)), sp['state'](fwd)],
        out_shape=[jax.ShapeDtypeStruct((s, d_inner), F32), jax.ShapeDtypeStruct((nc, g_n, n, gw), F32)],
        scratch_shapes=[pltpu.VMEM((n, gw), F32)], semantics=("arbitrary", "arbitrary"),
        args=[xbc, xbc, xbc, dt_g, ac_g, act_g])


def _ssd_bwd(xbc, dt_g, ac_g, act_g, s0_all, dy, dxs_skip, d_inner, n, kh, name, beside=None):
    s = xbc.shape[0]
    l, g_n = SSD_CHUNK, SSM_GROUPS
    gw, nc = d_inner // g_n, s // l
    sp = _ssd_specs(l, gw, n, d_inner // n)
    rev = lambda c: nc - 1 - c

    def kern(xs_ref, b_ref, c_ref, dt_ref, ac_ref, act_ref, s0_ref, dy_ref, skip_ref,
             dxs_ref, db_ref, dc_ref, ddt_ref, dacc_ref, dacr_ref, dst):
        @pl.when(pl.program_id(1) == 0)
        def _():
            dst[...] = jnp.zeros_like(dst)

        dsn = dst[...]
        s0 = s0_ref[...]
        xs, ac, dy = xs_ref[...], ac_ref[...], dy_ref[...]
        head, lane_k, col, expand, collapse = _ssd_pieces(xs, dt_ref[...], ac, kh)
        ace, dte = expand(ac), expand(dt_ref[...])
        x = xs * dte
        xb, bb, cb_ = x.astype(BF16), b_ref[...].astype(BF16), c_ref[...].astype(BF16)
        s0b, dsnb, dyb = s0.astype(BF16), dsn.astype(BF16), dy.astype(BF16)
        cb = _dot(cb_, bb, NT)
        ri = lax.broadcasted_iota(jnp.int32, (l, l), 0)
        ci = lax.broadcasted_iota(jnp.int32, (l, l), 1)
        causal = ri >= ci
        row_w = lax.broadcasted_iota(jnp.int32, (l, gw), 0)
        e = jnp.exp(ace)
        alast = jnp.sum(jnp.where(row_w == l - 1, ace, 0.0), axis=0, keepdims=True)
        gdec = jnp.exp(alast)
        wt = jnp.exp(alast - ace)
        cs = _dot(cb_, s0b, NN)
        dcs = (dy * e).astype(BF16)
        d_c = _dot(dcs, s0b, NT)
        ds_off = _dot(cb_, dcs, TN)
        dace = dy * cs * e
        dalast = jnp.sum(dsn * s0, axis=0, keepdims=True) * gdec
        z = wt * x
        dz = _dot(bb, dsnb, NN)
        d_b = _dot(z.astype(BF16), dsnb, NT)
        dx = dz * wt
        t = dz * z
        dalast = dalast + jnp.sum(t, axis=0, keepdims=True)
        dace = dace - t
        dcb = jnp.zeros((l, l), F32)
        dac_col = jnp.zeros((l, LANES), F32)
        for k in range(kh):
            seg = col(ac, k) - act_ref[pl.ds(k, 1), :]
            dk = jnp.exp(jnp.where(causal, seg, 0.0))
            mk = jnp.where(causal, cb * dk, 0.0)
            dx = jnp.where(head[k], dx + _dot(mk.astype(BF16), dyb, TN), dx)
            dmk = _dot(jnp.where(head[k], dy, 0.0).astype(BF16), xb, NT)
            dcb = dcb + jnp.where(causal, dmk * dk, 0.0)
            dseg = dmk * mk
            dac_col = jnp.where(lane_k == k, jnp.sum(dseg, axis=1, keepdims=True), dac_col)
            dacr_ref[pl.ds(k, 1), :] = -jnp.sum(dseg, axis=0, keepdims=True)
        for k in range(kh, 8):
            dacr_ref[pl.ds(k, 1), :] = jnp.zeros((1, l), F32)
        dcbb = dcb.astype(BF16)
        dc_ref[...] = d_c + _dot(dcbb, bb, NN)
        db_ref[...] = d_b + _dot(dcbb, cb_, TN)
        dace = jnp.where(row_w == l - 1, dace + dalast, dace)
        dacc_ref[...] = dac_col + collapse(dace)
        ddt_ref[...] = collapse(dx * xs)
        dxs_ref[...] = dx * dte + skip_ref[...]
        dst[...] = dsn * gdec + ds_off

    return _host(
        kern, beside, name=name, grid=(g_n, nc),
        in_specs=[sp['xs'](rev), sp['b'](rev), sp['c'](rev), sp['col'](rev), sp['col'](rev), sp['row'](rev),
                  sp['state'](rev), pl.BlockSpec((l, gw), lambda g, c: (rev(c), g)),
                  pl.BlockSpec((l, gw), lambda g, c: (rev(c), g))],
        out_specs=[pl.BlockSpec((l, gw), lambda g, c: (rev(c), g)), pl.BlockSpec((l, n), lambda g, c: (rev(c), g)),
                   pl.BlockSpec((l, n), lambda g, c: (rev(c), g)), sp['col'](rev), sp['col'](rev), sp['row'](rev)],
        out_shape=[jax.ShapeDtypeStruct((s, d_inner), F32), jax.ShapeDtypeStruct((s, g_n * n), F32),
                   jax.ShapeDtypeStruct((s, g_n * n), F32), jax.ShapeDtypeStruct((g_n, s, LANES), F32),
                   jax.ShapeDtypeStruct((g_n, s, LANES), F32), jax.ShapeDtypeStruct((g_n, 8, s), F32)],
        scratch_shapes=[pltpu.VMEM((n, gw), F32)], semantics=("arbitrary", "arbitrary"),
        args=[xbc, xbc, xbc, dt_g, ac_g, act_g, s0_all, dy, dxs_skip])


def _gate(y, xs, z, d_exp, gw):
    t = (y + xs * d_exp) * (z * _sigmoid(z))
    width = t.shape[1]
    gsz = width // SSM_GROUPS
    lane = lax.broadcasted_iota(jnp.int32, t.shape, 1)
    t2 = t * t
    scale = jnp.zeros_like(t)
    for g in range(SSM_GROUPS):
        in_g = (lane >= g * gsz) & (lane < (g + 1) * gsz)
        ms = jnp.sum(jnp.where(in_g, t2, 0.0), axis=1, keepdims=True) * (1.0 / gsz)
        scale = jnp.where(in_g, lax.rsqrt(ms + EPS), scale)
    return t * scale * gw


def _gate_fwd(y, xbc, z, d_exp, gw, d_inner, name, beside=None):
    res = _rowwise(_gate, name, [y, (xbc, d_inner, 0), z], [d_exp, gw], outs=[(d_inner, BF16)], tm=128, beside=beside)
    return res[0] if beside is None else (res[0][0], res[1])


def _gate_bwd(y, xbc, z, d_exp, gw, dyn, d_inner, name):
    def fn(y, xs, z, dyn, d_exp, gw):
        _, vjp = jax.vjp(_gate, y, xs, z, d_exp, gw)
        return vjp(dyn)

    return _rowwise(fn, name, [y, (xbc, d_inner, 0), z, dyn], [d_exp, gw],
                    outs=[(d_inner, F32), (d_inner, F32), (d_inner, BF16)], accs=[d_exp.shape, gw.shape], tm=64)


def _forget_fwd(fraw, b_f, name):
    def kern(f_ref, b_ref, o_ref):
        o_ref[...] = _cumsum_rows(-_softplus(-(f_ref[...] + b_ref[...])))

    return pl.pallas_call(kern, name=name, out_shape=jax.ShapeDtypeStruct(fraw.shape, F32),
                          compiler_params=_params())(fraw, b_f)


def _forget_bwd(fraw, b_f, dcum, name):
    def kern(f_ref, b_ref, d_ref, df_ref, db_ref):
        df = _cumsum_rows(d_ref[...], reverse=True) * _sigmoid(-(f_ref[...] + b_ref[...]))
        df_ref[...] = df.astype(df_ref.dtype)
        db_ref[...] = jnp.sum(df, axis=0, keepdims=True)

    return pl.pallas_call(
        kern, name=name,
        out_shape=[jax.ShapeDtypeStruct(fraw.shape, BF16), jax.ShapeDtypeStruct((1, fraw.shape[1]), F32)],
        compiler_params=_params())(fraw, b_f, dcum)


ATT_BLOCK = 512


def _attn_fwd(q, k, v, cq_rep, ck, name, beside=None):
    s, hd = q.shape
    h_n, d = hd // ATT_HEAD_DIM, ATT_HEAD_DIM
    tb = min(ATT_BLOCK, s)
    nb = s // tb
    scale = d ** -0.5

    def kern(q_ref, k_ref, v_ref, cq_ref, ck_ref, o_ref, lse_ref):
        i = pl.program_id(1)
        qq = q_ref[...]
        cq = jnp.max(cq_ref[...], axis=1, keepdims=True)
        rowpos = i * tb + lax.broadcasted_iota(jnp.int32, (tb, tb), 0)
        coli = lax.broadcasted_iota(jnp.int32, (tb, tb), 1)

        def step(j, carry, diagonal):
            m, l_, acc = carry
            ks = pl.ds(pl.multiple_of(j * tb, tb), tb)
            sc = _dot(qq, k_ref[ks, :], NT) * scale + (cq - ck_ref[j])
            if diagonal:
                sc = jnp.where(j * tb + coli <= rowpos, sc, NEG)
            mn = jnp.maximum(m, jnp.max(sc, axis=1, keepdims=True))
            p = jnp.exp(sc - mn)
            alpha = jnp.exp(m - mn)
            l_ = alpha * l_ + jnp.sum(p, axis=1, keepdims=True)
            acc = alpha * acc + _dot(p.astype(BF16), v_ref[ks, :], NN)
            return mn, l_, acc

        init = (jnp.full((tb, 1), NEG, F32), jnp.zeros((tb, 1), F32), jnp.zeros((tb, d), F32))
        below = lax.fori_loop(0, i, lambda j, carry: step(j, carry, False), init)
        m, l_, acc = step(i, below, True)
        o_ref[...] = (acc / l_).astype(o_ref.dtype)
        lse_ref[...] = jnp.broadcast_to(m + jnp.log(l_), (tb, d))

    return _host(
        kern, beside, name=name, grid=(h_n, nb),
        in_specs=[pl.BlockSpec((tb, d), lambda h, i: (i, h)), pl.BlockSpec((s, d), lambda h, i: (0, h)),
                  pl.BlockSpec((s, d), lambda h, i: (0, h)), pl.BlockSpec((tb, d), lambda h, i: (i, h)),
                  pl.BlockSpec((None, nb, 1, tb), lambda h, i: (h, 0, 0, 0))],
        out_specs=[pl.BlockSpec((tb, d), lambda h, i: (i, h)), pl.BlockSpec((tb, d), lambda h, i: (i, h))],
        out_shape=[jax.ShapeDtypeStruct((s, hd), BF16), jax.ShapeDtypeStruct((s, hd), F32)],
        scratch_shapes=[], semantics=("parallel", "arbitrary"), args=[q, k, v, cq_rep, ck])


def _attn_bwd(q, k, v, do, lse_rep, delta_rep, cq_rep, ck, name, beside=None):
    s, hd = q.shape
    h_n, d = hd // ATT_HEAD_DIM, ATT_HEAD_DIM
    tb = min(ATT_BLOCK, s)
    nb = s // tb
    scale = d ** -0.5

    def kern(q_ref, do_ref, k_ref, v_ref, lse_ref, dl_ref, cq_ref, ck_ref, dq_ref, dk_ref, dv_ref, dcq_ref, dck_ref):
        j = pl.program_id(1)

        @pl.when(j == 0)
        def _():
            dq_ref[...] = jnp.zeros_like(dq_ref)
            dcq_ref[...] = jnp.zeros_like(dcq_ref)

        kj, vj, ckj = k_ref[...], v_ref[...], ck_ref[...]
        colpos = j * tb + lax.broadcasted_iota(jnp.int32, (tb, tb), 1)
        rowi = lax.broadcasted_iota(jnp.int32, (tb, tb), 0)

        def step(i, carry, diagonal):
            dk, dv, dck = carry
            rs = pl.ds(pl.multiple_of(i * tb, tb), tb)
            qi, doi = q_ref[rs, :], do_ref[rs, :]
            lse = jnp.max(lse_ref[rs, :], axis=1, keepdims=True)
            dl = jnp.max(dl_ref[rs, :], axis=1, keepdims=True)
            cq = jnp.max(cq_ref[rs, :], axis=1, keepdims=True)
            sc = _dot(qi, kj, NT) * scale + ((cq - lse) - ckj)
            p = jnp.exp(jnp.where(colpos <= i * tb + rowi, sc, NEG) if diagonal else sc)
            dp = _dot(doi, vj, NT)
            ds = p * (dp - dl)
            dsb = ds.astype(BF16)
            dv = dv + _dot(p.astype(BF16), doi, TN)
            dk = dk + _dot(dsb, qi, TN)
            dq_ref[rs, :] += _dot(dsb, kj, NN) * scale
            dcq_ref[rs, :] += jnp.broadcast_to(jnp.sum(ds, axis=1, keepdims=True), (tb, d))
            return dk, dv, dck - jnp.sum(ds, axis=0, keepdims=True)

        init = (jnp.zeros((tb, d), F32), jnp.zeros((tb, d), F32), jnp.zeros((1, tb), F32))
        dk, dv, dck = lax.fori_loop(j + 1, nb, lambda i, carry: step(i, carry, False), step(j, init, True))
        dk_ref[...] = dk * scale
        dv_ref[...] = dv.astype(dv_ref.dtype)
        dck_ref[...] = dck

    whole = pl.BlockSpec((s, d), lambda h, j: (0, h))
    blk = pl.BlockSpec((tb, d), lambda h, j: (j, h))
    ckb = pl.BlockSpec((None, None, 1, tb), lambda h, j: (h, j, 0, 0))
    return _host(
        kern, beside, name=name, grid=(h_n, nb),
        in_specs=[whole, whole, blk, blk, whole, whole, whole, ckb],
        out_specs=[whole, blk, blk, whole, ckb],
        out_shape=[jax.ShapeDtypeStruct((s, hd), F32), jax.ShapeDtypeStruct((s, hd), F32),
                   jax.ShapeDtypeStruct((s, hd), BF16), jax.ShapeDtypeStruct((s, hd), F32),
                   jax.ShapeDtypeStruct((h_n, nb, 1, tb), F32)],
        scratch_shapes=[], semantics=("arbitrary", "arbitrary"), args=[q, do, k, v, lse_rep, delta_rep, cq_rep, ck])


def _adamw(w, g, m, v, name):
    cols = w.shape[1]
    tm = _tile(w.shape[0], 128) if w.shape[0] % 128 == 0 else w.shape[0]
    return _rowwise(_adamw_math, name, [w, g, m, v], outs=[(cols, F32)] * 3, tm=tm)


def _sum_arrays(arrs, out_dtype, name):
    def fn(*xs):
        acc = xs[0].astype(F32)
        for x in xs[1:]:
            acc = acc + x.astype(F32)
        return acc

    tm = PACK_ROWS if arrs[0].shape[0] % PACK_ROWS == 0 else arrs[0].shape[0]
    return _rowwise(fn, name, list(arrs), outs=[(arrs[0].shape[1], out_dtype)], tm=tm)[0]


def _half_tile(rows):
    for t in (256, 176, 128, 64, 32, 16):
        if rows % t == 0:
            return t
    return rows


def _sum_half(g, got, core, name):
    n_chip, hr, cols = got.shape
    tm = _half_tile(hr)
    nt = hr // tm

    def kern(c_ref, g_ref, a_ref, o_ref):
        o_ref[...] = (g_ref[...].astype(F32) + a_ref[...].astype(F32)).astype(o_ref.dtype)

    grid_spec = pltpu.PrefetchScalarGridSpec(
        num_scalar_prefetch=1, grid=(n_chip * nt,),
        in_specs=[pl.BlockSpec((tm, cols), lambda r, c: (((r // nt) * 2 + c[0]) * nt + r % nt, 0)),
                  pl.BlockSpec((tm, cols), lambda r, c: (r, 0))],
        out_specs=pl.BlockSpec((tm, cols), lambda r, c: (r, 0)))
    out = pl.pallas_call(
        kern, name=name, grid_spec=grid_spec, out_shape=jax.ShapeDtypeStruct((n_chip * hr, cols), BF16),
        compiler_params=_params(("arbitrary",)))(core.reshape(1), g.reshape(-1, cols), got.reshape(-1, cols))
    return out.reshape(n_chip, hr, cols)


def _sum_parts(own, parts, chip, name):
    n_parts, hr, cols = parts.shape
    tm = _half_tile(hr)

    def kern(s_ref, t_ref, p_ref, o_ref):
        acc = t_ref[...].astype(F32)
        for j in range(n_parts):
            acc = acc + p_ref[j].astype(F32)
        o_ref[...] = acc

    grid_spec = pltpu.PrefetchScalarGridSpec(
        num_scalar_prefetch=1, grid=(hr // tm,),
        in_specs=[pl.BlockSpec((None, tm, cols), lambda r, s: (s[0], r, 0)),
                  pl.BlockSpec((n_parts, tm, cols), lambda r, s: (0, r, 0))],
        out_specs=pl.BlockSpec((tm, cols), lambda r, s: (r, 0)))
    return pl.pallas_call(
        kern, name=name, grid_spec=grid_spec, out_shape=jax.ShapeDtypeStruct((hr, cols), F32),
        compiler_params=_params(("arbitrary",)))(chip.reshape(1), own, parts)


def _adamw_math(w, g, m, v):
    m = ADAM_B1 * m + (1.0 - ADAM_B1) * g
    v = ADAM_B2 * v + (1.0 - ADAM_B2) * (g * g)
    m_hat = m / (1.0 - ADAM_B1 ** ADAM_STEP)
    v_hat = v / (1.0 - ADAM_B2 ** ADAM_STEP)
    return -ADAM_LR * (m_hat / (jnp.sqrt(v_hat) + ADAM_EPS) + ADAM_WD * w), m, v


def _adamw_big(w, m, v, mine, theirs, core, layer, prev, name):
    n_layers, rows, cols = w.shape
    hr = rows // 2
    tm = next(t for t in (128, 64, 32, 16, 8) if hr % t == 0)
    nt = hr // tm

    def kern(s_ref, w_ref, m_ref, v_ref, a_ref, b_ref, *rest):
        g_ref, d_ref, nm_ref, nv_ref = rest[-4:]
        g = jnp.where(pl.program_id(0) // nt == s_ref[0], a_ref[...], b_ref[...])
        g_ref[...] = g
        d_ref[...], nm_ref[...], nv_ref[...] = _adamw_math(w_ref[...], g, m_ref[...], v_ref[...])

    lyr = pl.BlockSpec((None, tm, cols), lambda r, s: (layer, r, 0))
    half = pl.BlockSpec((tm, cols), lambda r, s: (r % nt, 0))
    passed = [] if prev is None else list(prev)
    grid_spec = pltpu.PrefetchScalarGridSpec(
        num_scalar_prefetch=1, grid=(rows // tm,),
        in_specs=[lyr, lyr, lyr, half, half] + [pl.BlockSpec(memory_space=pl.ANY)] * len(passed), out_specs=[lyr] * 4)
    return pl.pallas_call(
        kern, name=name, grid_spec=grid_spec, out_shape=[jax.ShapeDtypeStruct(w.shape, F32)] * 4,
        input_output_aliases={6 + i: i for i in range(len(passed))},
        compiler_params=_params(("arbitrary",)))(core.reshape(1), w, m, v, mine, theirs, *passed)


def _cast_into_slot(src, slot, name):
    rows, cols = src.shape
    tm = _half_tile(rows)

    def kern(s_ref, x_ref, o_ref):
        o_ref[...] = x_ref[...].astype(o_ref.dtype)

    grid_spec = pltpu.PrefetchScalarGridSpec(
        num_scalar_prefetch=1, grid=(rows // tm,), in_specs=[pl.BlockSpec((tm, cols), lambda r, s: (r, 0))],
        out_specs=pl.BlockSpec((None, tm, cols), lambda r, s: (s[0], r, 0)))
    return pl.pallas_call(
        kern, name=name, grid_spec=grid_spec, out_shape=jax.ShapeDtypeStruct((N_CHIPS, rows, cols), BF16),
        compiler_params=_params(("arbitrary",)))(slot.reshape(1), src)


FLIP_C = [(0, 0, 1)]
FLIP_XY = [(1, 0, 0), (0, 1, 0), (1, 1, 0)]
FLIP_ALL = [(fx, fy, fc) for fx in (0, 1) for fy in (0, 1) for fc in (0, 1) if (fx, fy, fc) != (0, 0, 0)]


def _chip(dev):
    return 2 * dev[0] + dev[1]


def _device(dev):
    return 4 * dev[0] + 2 * dev[1] + dev[2]


def _exchange(name, srcs, dst_shapes, rels, src_view, dst_view, own_view=None, in_place=False):
    n, n_rel = len(srcs), len(rels)

    def body(*refs):
        src_refs, dst_refs = refs[:n], refs[n:2 * n]
        send_sems, recv_sems, own_sems = refs[2 * n:]
        me = (lax.axis_index("x"), lax.axis_index("y"), lax.axis_index("c"))
        peers = [tuple(1 - m if f else m for m, f in zip(me, rel)) for rel in rels]

        def copy(i, k, sender, receiver):
            return pltpu.make_async_remote_copy(
                src_ref=src_view(i, src_refs[i], sender, receiver), dst_ref=dst_view(i, dst_refs[i], sender, k),
                send_sem=send_sems.at[i * n_rel + k], recv_sem=recv_sems.at[i * n_rel + k], device_id=receiver,
                device_id_type=pl.DeviceIdType.MESH)

        sends = [copy(i, k, me, peer) for i in range(n) for k, peer in enumerate(peers)]
        for cp in sends:
            cp.start()
        mine = []
        if own_view is not None:
            for i in range(n):
                frm, to = own_view(i, src_refs[i], dst_refs[i], me)
                mine.append(pltpu.make_async_copy(frm, to, own_sems.at[i]))
                mine[-1].start()
        for i in range(n):
            for k, peer in enumerate(peers):
                copy(i, k, peer, me).wait_recv()
        for cp in sends:
            cp.wait_send()
        for cp in mine:
            cp.wait()

    any_spec = pl.BlockSpec(memory_space=pl.ANY)
    return pl.pallas_call(
        body, name=name, out_shape=[jax.ShapeDtypeStruct(s, d) for s, d in dst_shapes],
        in_specs=[any_spec] * n, out_specs=[any_spec] * n,
        input_output_aliases={i: i for i in range(n)} if in_place else {},
        scratch_shapes=[pltpu.SemaphoreType.DMA((n * n_rel,)), pltpu.SemaphoreType.DMA((n * n_rel,)),
                        pltpu.SemaphoreType.DMA((n,))])(*srcs)


def _gather_plan(bufs):
    n = len(bufs)
    halves = [b.shape[1] // 2 for b in bufs]

    def tools(src_refs, dst_refs, sems):
        ici_send, ici_recv, d2d_send, d2d_recv = sems
        x, y, c = lax.axis_index("x"), lax.axis_index("y"), lax.axis_index("c")
        sibling = (x, y, 1 - c)
        chips = [(1 - x, y), (x, 1 - y), (1 - x, 1 - y)]

        def rows(i, chip, core):
            return dst_refs[i].at[2 * chip[0] + chip[1], pl.ds(core * halves[i], halves[i]), :]

        def over_ici(i, k, src, chip_from, to):
            return pltpu.make_async_remote_copy(
                src_ref=src, dst_ref=rows(i, chip_from, c), send_sem=ici_send.at[3 * i + k],
                recv_sem=ici_recv.at[3 * i + k], device_id=to, device_id_type=pl.DeviceIdType.MESH)

        def over_d2d(i, k, core):
            return pltpu.make_async_remote_copy(
                src_ref=rows(i, chips[k], core), dst_ref=rows(i, chips[k], core), send_sem=d2d_send.at[3 * i + k],
                recv_sem=d2d_recv.at[3 * i + k], device_id=sibling, device_id_type=pl.DeviceIdType.MESH)

        def my_send(i, k):
            my_half = src_refs[i].at[2 * x + y, pl.ds(c * halves[i], halves[i]), :]
            return over_ici(i, k, my_half, (x, y), (*chips[k], c))

        def my_arrival(i, k):
            return over_ici(i, k, rows(i, chips[k], c), chips[k], (x, y, c))

        return c, over_d2d, my_send, my_arrival

    def start(src_refs, dst_refs, sems):
        _, _, my_send, _ = tools(src_refs, dst_refs, sems)
        for i in range(n):
            for k in range(3):
                my_send(i, k).start()

    def finish(src_refs, dst_refs, sems):
        c, over_d2d, my_send, my_arrival = tools(src_refs, dst_refs, sems)
        passed = []
        for i in range(n):
            for k in range(3):
                my_arrival(i, k).wait_recv()
                passed.append(over_d2d(i, k, c))
                passed[-1].start()
        for i in range(n):
            for k in range(3):
                over_d2d(i, k, 1 - c).wait_recv()
        for i in range(n):
            for k in range(3):
                my_send(i, k).wait_send()
        for cp in passed:
            cp.wait_send()

    return _Beside(bufs, [(b.shape, b.dtype, i) for i, b in enumerate(bufs)], [3 * n] * 4, start, finish)


def _scatter_plan(sums):
    n = len(sums)

    def copy(i, k, src_refs, dst_refs, sems, sender, receiver):
        return pltpu.make_async_remote_copy(
            src_ref=src_refs[i].at[_chip(receiver)], dst_ref=dst_refs[i].at[k], send_sem=sems[0].at[3 * i + k],
            recv_sem=sems[1].at[3 * i + k], device_id=receiver, device_id_type=pl.DeviceIdType.MESH)

    def each(fn, src_refs, dst_refs, sems, outgoing):
        me = (lax.axis_index("x"), lax.axis_index("y"), lax.axis_index("c"))
        for i in range(n):
            for k, rel in enumerate(FLIP_XY):
                peer = tuple(1 - m if f else m for m, f in zip(me, rel))
                fn(copy(i, k, src_refs, dst_refs, sems, *((me, peer) if outgoing else (peer, me))))

    def start(src_refs, dst_refs, sems):
        each(lambda cp: cp.start(), src_refs, dst_refs, sems, True)

    def finish(src_refs, dst_refs, sems):
        each(lambda cp: cp.wait_recv(), src_refs, dst_refs, sems, False)
        each(lambda cp: cp.wait_send(), src_refs, dst_refs, sems, True)

    return _Beside(sums, [((3,) + s.shape[1:], s.dtype, None) for s in sums], [3 * n] * 2, start, finish)


def _pack(parts, dtype, multiple):
    flat = jnp.concatenate([p.reshape(-1).astype(dtype) for p in parts])
    pad = (-flat.shape[0]) % multiple
    return jnp.pad(flat, (0, pad)) if pad else flat


def _unpack_shards(packs, names, shapes, axes):
    out, off = {}, 0
    for nme in names:
        sz = math.prod(shapes[nme])
        out[nme] = jnp.concatenate([packs[j, off:off + sz].reshape(shapes[nme]) for j in range(N_CHIPS)], axis=axes[nme])
        off += sz
    return out


def _shards_of(full, axis):
    sz = full.shape[axis] // N_CHIPS
    return [lax.slice_in_dim(full, j * sz, (j + 1) * sz, axis=axis) for j in range(N_CHIPS)]


def kernel(x, a_norm_w, a_in_proj, a_conv_w, a_conv_b, a_dt_bias, a_A_log, a_D, a_gnorm_w, a_out_proj, kv_norm_w, w_kvf, b_f, k_norm_w, b_norm_w, w_q, q_norm_w, w_o, ffn_norm_w, w_gate_up, w_down, loss_target, m_a_norm_w, m_a_in_proj, m_a_conv_w, m_a_conv_b, m_a_dt_bias, m_a_A_log, m_a_D, m_a_gnorm_w, m_a_out_proj, m_kv_norm_w, m_w_kvf, m_b_f, m_k_norm_w, m_b_norm_w, m_w_q, m_q_norm_w, m_w_o, m_ffn_norm_w, m_w_gate_up, m_w_down, v_a_norm_w, v_a_in_proj, v_a_conv_w, v_a_conv_b, v_a_dt_bias, v_a_A_log, v_a_D, v_a_gnorm_w, v_a_out_proj, v_kv_norm_w, v_w_kvf, v_b_f, v_k_norm_w, v_b_norm_w, v_w_q, v_q_norm_w, v_w_o, v_ffn_norm_w, v_w_gate_up, v_w_down):
    args = locals()
    w = {n: args[n] for n in WEIGHTS}
    mom = {n: args['m_' + n] for n in WEIGHTS}
    var = {n: args['v_' + n] for n in WEIGHTS}
    shapes = {n: w[n].shape for n in WEIGHTS}

    cx, cy, cc = lax.axis_index("x"), lax.axis_index("y"), lax.axis_index("c")
    my_chip = 2 * cx + cy

    xs_in = x[0]
    target = loss_target[0]
    s_len, d_model = xs_in.shape
    n_heads_ssm = a_dt_bias.shape[-1]
    d_inner = n_heads_ssm * SSM_HEAD_DIM
    d_xbc = a_conv_w.shape[-1] * N_CHIPS
    d_state = (d_xbc - d_inner) // (2 * SSM_GROUPS)
    kh = n_heads_ssm // SSM_GROUPS
    n_att = b_f.shape[0]
    d_att = n_att * ATT_HEAD_DIM
    d_ff = w_down.shape[1] * N_CHIPS

    small_names = list(SMALL_SHARDED)
    sp = _pack([w[n] for n in small_names], F32, PACK_COLS).reshape(-1, PACK_COLS)
    sp_all = _exchange("gather_small", [sp], [((N_CHIPS,) + sp.shape, F32)], FLIP_XY,
                       lambda i, r, me, peer: r, lambda i, r, sender, k: r.at[_chip(sender)],
                       lambda i, s, d, me: (s, d.at[_chip(me)]))[0]
    full = _unpack_shards(sp_all.reshape(N_CHIPS, -1), small_names, shapes, SMALL_SHARDED)
    for n in SMALL:
        if n not in full:
            full[n] = w[n]

    big2d = {'in': a_in_proj[0], 'out': a_out_proj[0], 'gu0': w_gate_up[0], 'dn0': w_down[0], 'kvf': w_kvf,
             'q': w_q[0], 'o': w_o[0], 'gu1': w_gate_up[1], 'dn1': w_down[1]}
    big_keys = list(big2d)
    slot = {k: _cast_into_slot(big2d[k], my_chip, "cast_" + k) for k in big_keys}
    gathered = {}

    def gather(keys):
        return _gather_plan([slot[k] for k in keys])

    def pad_cols(a, width=LANES):
        return jnp.pad(a, ((0, 0), (0, width - a.shape[1])))

    gathered['in'], = _alone(gather(['in']), "gather_in")
    w_in = jnp.concatenate([gathered['in'][j] for j in range(N_CHIPS)], axis=1)
    w_z, w_xbc, w_dt = w_in[:, :d_inner], w_in[:, d_inner:d_inner + d_xbc], pad_cols(w_in[:, d_inner + d_xbc:])
    conv_w, conv_b = full['a_conv_w'][0], full['a_conv_b']
    norm_a, gnorm = full['a_norm_w'], full['a_gnorm_w']
    dt_bias, a_log = pad_cols(a_dt_bias), pad_cols(a_A_log)
    d_exp = jnp.repeat(a_D, SSM_HEAD_DIM, axis=1)
    kv_nw, b_nw = kv_norm_w.reshape(1, -1), b_norm_w
    k_nw, q_nw = k_norm_w.reshape(1, -1), q_norm_w
    bf_pad = pad_cols(b_f.reshape(1, -1))
    ffn_nw = [ffn_norm_w[i:i + 1] for i in range(2)]

    def to_groups(a):
        g = a[:, :n_heads_ssm].reshape(s_len, SSM_GROUPS, kh).transpose(1, 0, 2)
        return jnp.pad(g, ((0, 0), (0, 0), (0, LANES - kh)))

    def to_groups_t(a):
        g = a[:, :n_heads_ssm].reshape(s_len, SSM_GROUPS, kh).transpose(1, 2, 0)
        return jnp.pad(g, ((0, 0), (0, 8 - kh), (0, 0)))

    def from_groups(col, row=None):
        a = col[:, :, :kh].transpose(1, 0, 2).reshape(s_len, n_heads_ssm)
        if row is not None:
            a = a + row[:, :kh, :].transpose(2, 0, 1).reshape(s_len, n_heads_ssm)
        return pad_cols(a)

    n1 = _rms_fwd(xs_in, norm_a, "norm_a")
    z = _mm(n1, w_z, 'nn', "in_z")
    xbc_raw, (gathered['out'],) = _mm(n1, w_xbc, 'nn', "in_xbc", beside=gather(['out']))
    dtraw = _mm(n1, w_dt, 'nn', "in_dt")
    xbc = _conv_fwd(xbc_raw, conv_w, conv_b, "conv")
    dt, acum = _dt_fwd(dtraw, dt_bias, a_log, "dt")
    dt_g, ac_g, act_g = to_groups(dt), to_groups(acum), to_groups_t(acum)
    (y_ssd, states), (gathered['gu0'],) = _ssd_fwd(xbc, dt_g, ac_g, act_g, d_inner, d_state, kh, "ssd",
                                                   beside=gather(['gu0']))
    yn, (gathered['q'],) = _gate_fwd(y_ssd, xbc, z, d_exp, gnorm, d_inner, "gate", beside=gather(['q']))
    w_out = gathered['out'].reshape(-1, d_model)
    h1, (gathered['o'],) = _mm(yn, w_out, 'nn', "out_proj", add=xs_in, beside=gather(['o']))

    w_gu, w_dn = {}, {}

    def ffn_fwd(h, i, also):
        nrm = _rms_fwd(h, ffn_nw[i], f"ffn{i}_norm")
        w_gu[i] = gathered[f'gu{i}']
        gu = _mm(nrm, w_gu[i], 'nn', f"ffn{i}_up", shards=True, beside=gather(also) if also else None)
        if also:
            gu, arrived = gu
            gathered.update(zip(also, arrived))
        act = _swiglu_fwd(gu, d_ff, f"ffn{i}_act")
        w_dn[i] = gathered[f'dn{i}'].reshape(-1, d_model)
        return nrm, gu, act, _mm(act, w_dn[i], 'nn', f"ffn{i}_down", add=h)

    n2, gu0, act0, h2 = ffn_fwd(h1, 0, ['dn0', 'kvf'])
    w_kvf_full = jnp.concatenate([gathered['kvf'][j] for j in range(N_CHIPS)], axis=1)
    w_k, w_v, w_f = w_kvf_full[:, :d_att], w_kvf_full[:, d_att:2 * d_att], pad_cols(w_kvf_full[:, 2 * d_att:])
    w_qm, w_om = gathered['q'].reshape(-1, d_att), gathered['o'].reshape(-1, d_model)
    nkv = _rms_fwd(h2, kv_nw, "kv_norm")
    k_raw = _mm(nkv, w_k, 'nn', "proj_k")
    v_att = _mm(nkv, w_v, 'nn', "proj_v", out_dtype=BF16)
    f_raw = _mm(nkv, w_f, 'nn', "proj_f")
    k_att = _rms_fwd(k_raw, k_nw, "k_norm")
    cum = _forget_fwd(f_raw, bf_pad, "forget")
    cq_rep = jnp.repeat(cum[:, :n_att], ATT_HEAD_DIM, axis=1)
    tb = min(ATT_BLOCK, s_len)
    ck = cum[:, :n_att].T.reshape(n_att, s_len // tb, 1, tb)
    n3 = _rms_fwd(h2, b_nw, "b_norm")
    q_raw = _mm(n3, w_qm, 'nn', "proj_q")
    q_att = _rms_fwd(q_raw, q_nw, "q_norm")
    (o_att, lse_rep), arrived = _attn_fwd(q_att, k_att, v_att, cq_rep, ck, "attn", beside=gather(['gu1', 'dn1']))
    gathered.update(zip(['gu1', 'dn1'], arrived))
    h3 = _mm(o_att, w_om, 'nn', "proj_o", add=h2)
    n4, gu1, act1, h4 = ffn_fwd(h3, 1, [])

    def loss_fn(h, t):
        err = h - t
        sq = jnp.sum(jnp.sum(err * err, axis=1, keepdims=True), axis=0, keepdims=True)
        return err * (1.0 / d_model), err * (1.0 / d_model), sq * (0.5 / d_model)

    dh4, dh4_b, loss_part = _rowwise(loss_fn, "loss", [h4, target], outs=[(d_model, F32), (d_model, BF16)],
                                     accs=[(1, 1)])

    n_f = n_att
    g_big, chip_sums, parts = {}, {}, {}

    def col_shards(pieces):
        fullw = jnp.concatenate(pieces, axis=1)
        return fullw.reshape(fullw.shape[0], N_CHIPS, -1).transpose(1, 0, 2)

    def row_shards(a):
        return a.reshape(N_CHIPS, -1, a.shape[1])

    def scatter(keys, tag):
        g_list = [g_big[k] for k in keys]
        hrs = [g.shape[1] // 2 for g in g_list]
        got = _exchange("reduce_d2d_" + tag, g_list,
                        [((N_CHIPS, hr, g.shape[2]), BF16) for g, hr in zip(g_list, hrs)], FLIP_C,
                        lambda i, r, me, peer: r.at[:, pl.ds((1 - me[2]) * hrs[i], hrs[i]), :],
                        lambda i, r, sender, k: r)
        for k, g, a in zip(keys, g_list, got):
            chip_sums[k] = _sum_half(g, a, cc, "reduce_sum2_" + k)
        return _scatter_plan([chip_sums[k] for k in keys])

    def ffn_bwd(dh, dh_b, h, nrm, gu, act, i, hosted):
        dact = _mm(dh_b, w_dn[i], 'nt', f"ffn{i}_down_dx", beside=scatter(hosted[:1], f"a{i}") if hosted else None)
        g_dn = _mm(act, dh_b, 'tn', f"ffn{i}_down_dw", out_dtype=BF16,
                   beside=scatter(hosted[1:], f"b{i}") if hosted else None)
        if hosted:
            (dact, first), (g_dn, second) = dact, g_dn
            parts.update(zip(hosted, first + second))
        dgu = _swiglu_bwd(gu, dact, d_ff, f"ffn{i}_act_bwd")
        dn = _mm(dgu, w_gu[i], 'nt', f"ffn{i}_up_dx", shards=True)
        g_gu = _mm(nrm, dgu, 'tn', f"ffn{i}_up_dw", out_dtype=BF16, shards=True)
        dh_new, dh_new_b, g_nw = _rms_bwd(h, ffn_nw[i], dn, f"ffn{i}_norm_bwd", extra=dh, copy_bf16=True)
        return dh_new, dh_new_b, g_dn, g_gu, g_nw

    dh3, dh3_b, g_dn1, g_gu1, g_fnw1 = ffn_bwd(dh4, dh4_b, h3, n4, gu1, act1, 1, [])
    do_att = _mm(dh3_b, w_om, 'nt', "proj_o_dx", out_dtype=BF16)
    g_wo = _mm(o_att, dh3_b, 'tn', "proj_o_dw", out_dtype=BF16)
    g_big.update(gu1=g_gu1, dn1=row_shards(g_dn1), o=row_shards(g_wo))
    late_keys = ['gu1', 'dn1', 'o']
    plan_late = scatter(late_keys, "late")

    def delta_fn(a, b):
        prod = a.astype(F32) * b.astype(F32)
        return jnp.concatenate([jnp.broadcast_to(jnp.sum(prod[:, g], axis=1, keepdims=True), (a.shape[0], ATT_HEAD_DIM))
                                for g in _lane_groups(a.shape[1], ATT_HEAD_DIM)], axis=1)

    delta_rep = _rowwise(delta_fn, "attn_delta", [do_att, o_att], outs=[(d_att, F32)])[0]
    (dq_att, dk_att, dv_att, dcq_rep, dck), arrived = _attn_bwd(q_att, k_att, v_att, do_att, lse_rep, delta_rep,
                                                                cq_rep, ck, "attn_bwd", beside=plan_late)
    parts.update(zip(late_keys, arrived))
    dq_raw, g_qnw = _rms_bwd(q_raw, q_nw, dq_att, "q_norm_bwd", out_dtype=BF16)
    dn3 = _mm(dq_raw, w_qm, 'nt', "proj_q_dx")
    g_wq = _mm(n3, dq_raw, 'tn', "proj_q_dw", out_dtype=BF16)
    dh2, g_bnw = _rms_bwd(h2, b_nw, dn3, "b_norm_bwd", extra=dh3)
    dk_raw, g_knw = _rms_bwd(k_raw, k_nw, dk_att, "k_norm_bwd", out_dtype=BF16)
    dcum = pad_cols(dcq_rep.reshape(s_len, n_att, ATT_HEAD_DIM)[:, :, 0] + dck.reshape(n_att, s_len).T)
    df_raw, g_bf = _forget_bwd(f_raw, bf_pad, dcum, "forget_bwd")
    dnkv = _mm(dk_raw, w_k, 'nt', "proj_k_dx")
    dnkv = _mm(dv_att, w_v, 'nt', "proj_v_dx", add=dnkv)
    dnkv = _mm(df_raw, w_f, 'nt', "proj_f_dx", add=dnkv)
    g_wk = _mm(nkv, dk_raw, 'tn', "proj_k_dw", out_dtype=BF16)
    g_wv = _mm(nkv, dv_att, 'tn', "proj_v_dw", out_dtype=BF16)
    g_wf = _mm(nkv, df_raw, 'tn', "proj_f_dw", out_dtype=BF16)
    dh2, dh2_b, g_kvnw = _rms_bwd(h2, kv_nw, dnkv, "kv_norm_bwd", extra=dh2, copy_bf16=True)
    g_big.update(q=row_shards(g_wq), kvf=col_shards([g_wk, g_wv, g_wf[:, :n_f]]))
    dh1, dh1_b, g_dn0, g_gu0, g_fnw0 = ffn_bwd(dh2, dh2_b, h1, n2, gu0, act0, 0, ['q', 'kvf'])
    g_big.update(gu0=g_gu0, dn0=row_shards(g_dn0))
    plan_ffn0 = scatter(['gu0', 'dn0'], "ffn0")

    dyn = _mm(dh1_b, w_out, 'nt', "out_proj_dx")
    g_wout = _mm(yn, dh1_b, 'tn', "out_proj_dw", out_dtype=BF16)
    dy_ssd, dxs_skip, dz, g_dexp, g_gnorm = _gate_bwd(y_ssd, xbc, z, d_exp, gnorm, dyn, d_inner, "gate_bwd")
    (dxs, d_b, d_c, ddt_g, dacc_g, dacr_g), arrived = _ssd_bwd(xbc, dt_g, ac_g, act_g, states, dy_ssd, dxs_skip,
                                                               d_inner, d_state, kh, "ssd_bwd", beside=plan_ffn0)
    parts.update(zip(['gu0', 'dn0'], arrived))
    g_big.update(out=row_shards(g_wout))
    dxbc_act = jnp.concatenate([dxs, d_b, d_c], axis=1)
    (du, g_convw, g_convb), (parts['out'],) = _conv_bwd(xbc_raw, conv_w, conv_b, dxbc_act, "conv_bwd",
                                                        beside=scatter(['out'], "out"))
    draw, g_dtb, g_alog = _dt_bwd(dtraw, dt_bias, a_log, from_groups(ddt_g), from_groups(dacc_g, dacr_g), "dt_bwd")
    g_wz = _mm(n1, dz, 'tn', "in_z_dw", out_dtype=BF16)
    g_wxbc = _mm(n1, du, 'tn', "in_xbc_dw", out_dtype=BF16)
    g_wdt = _mm(n1, draw, 'tn', "in_dt_dw", out_dtype=BF16)
    g_in = jnp.concatenate([g_wz, g_wxbc, g_wdt[:, :n_heads_ssm]], axis=1)
    g_in = g_in.reshape(2, d_model // 2, N_CHIPS, -1).transpose(0, 2, 1, 3)
    g_big.update(in0=g_in[0], in1=g_in[1])
    dn1, (parts['in0'],) = _mm(du, w_xbc, 'nt', "in_xbc_dx", beside=scatter(['in0'], "in0"))
    dn1, (parts['in1'],) = _mm(dz, w_z, 'nt', "in_z_dx", add=dn1, beside=scatter(['in1'], "in1"))
    dn1 = _mm(draw, w_dt, 'nt', "in_dt_dx", add=dn1)
    dx, g_norm_a = _rms_bwd(xs_in, norm_a, dn1, "norm_a_bwd", extra=dh1)

    g_small = {
        'a_norm_w': g_norm_a, 'a_conv_w': g_convw[None], 'a_conv_b': g_convb,
        'a_dt_bias': g_dtb[:, :n_heads_ssm], 'a_A_log': g_alog[:, :n_heads_ssm],
        'a_D': g_dexp.reshape(n_heads_ssm, SSM_HEAD_DIM).sum(axis=1).reshape(1, -1), 'a_gnorm_w': g_gnorm,
        'kv_norm_w': g_kvnw.reshape(-1), 'b_f': g_bf[0, :n_f], 'k_norm_w': g_knw.reshape(-1), 'b_norm_w': g_bnw,
        'q_norm_w': g_qnw, 'ffn_norm_w': jnp.concatenate([g_fnw0, g_fnw1], axis=0),
    }

    sg = _pack([g_small[n] for n in SMALL] + [loss_part], F32, 8 * PACK_COLS).reshape(-1, PACK_COLS)
    sg_all = _exchange("reduce_small", [sg], [((2 * N_CHIPS,) + sg.shape, F32)], FLIP_ALL,
                       lambda i, r, me, peer: r, lambda i, r, sender, k: r.at[_device(sender)],
                       lambda i, s, d, me: (s, d.at[_device(me)]))[0]
    sg_sum = _sum_arrays([sg_all[d] for d in range(2 * N_CHIPS)], F32, "reduce_small_sum").reshape(-1)
    red_small, off = {}, 0
    for n in SMALL:
        shp = g_small[n].shape
        red_small[n] = sg_sum[off:off + math.prod(shp)].reshape(shp)
        off += math.prod(shp)
    loss = sg_sum[off]

    red_keys = list(parts)
    half_sums = [_sum_parts(chip_sums[k], parts[k], my_chip, "reduce_sum4_" + k) for k in red_keys]
    others = _exchange("reduce_back", half_sums, [(h.shape, F32) for h in half_sums], FLIP_C,
                       lambda i, r, me, peer: r, lambda i, r, sender, k: r)
    mine_of, theirs_of = dict(zip(red_keys, half_sums)), dict(zip(red_keys, others))

    grads, delta, new_m, new_v = {}, {}, {}, {}
    layers_of = {'a_in_proj': ['in0', 'in1'], 'a_out_proj': ['out'], 'w_kvf': ['kvf'], 'w_q': ['q'], 'w_o': ['o'],
                 'w_gate_up': ['gu0', 'gu1'], 'w_down': ['dn0', 'dn1']}
    for n, keys in layers_of.items():
        three_d = (len(keys), math.prod(shapes[n][:-1]) // len(keys), shapes[n][-1])
        res = None
        for layer, k in enumerate(keys):
            res = _adamw_big(w[n].reshape(three_d), mom[n].reshape(three_d), var[n].reshape(three_d), mine_of[k],
                             theirs_of[k], cc, layer, res, "adamw_" + k)
        grads[n], delta[n], new_m[n], new_v[n] = [r.reshape(shapes[n]) for r in res]
    for n in SMALL:
        if n in SMALL_SHARDED:
            ax = SMALL_SHARDED[n]
            grads[n] = lax.dynamic_slice_in_dim(red_small[n], my_chip * shapes[n][ax], shapes[n][ax], axis=ax)
        else:
            grads[n] = red_small[n]

    packed = [_pack([src[n] for n in SMALL], F32, 8 * LANES).reshape(-1, LANES) for src in (w, grads, mom, var)]
    small_out = _adamw(*packed, "adamw_small")
    for store, flat in zip((delta, new_m, new_v), small_out):
        flat, off = flat.reshape(-1), 0
        for n in SMALL:
            sz = math.prod(shapes[n])
            store[n] = flat[off:off + sz].reshape(shapes[n])
            off += sz

    return (loss, dx[None], *[grads[n] for n in WEIGHTS], *[delta[n] for n in WEIGHTS],
            *[new_m[n] for n in WEIGHTS], *[new_v[n] for n in WEIGHTS])
```

```python
import functools
import math

import jax
import jax.numpy as jnp
from jax import lax
from jax.experimental import pallas as pl
from jax.experimental.pallas import tpu as pltpu

F32, BF16 = jnp.float32, jnp.bfloat16
EPS = 1e-6
SSM_HEAD_DIM = 64
SSM_GROUPS = 8
SSD_CHUNK = 128
ATT_HEAD_DIM = 128
LANES = 128
N_CHIPS = 4
NEG = -1e30
ADAM_LR, ADAM_B1, ADAM_B2, ADAM_EPS, ADAM_WD, ADAM_STEP = 0.001, 0.9, 0.999, 1e-08, 0.01, 10
VMEM_LIMIT_BYTES = 56 * 1024 * 1024
PACK_COLS = 1024
PACK_ROWS = 256
MM_K_TILES = (2816, 2048, 1408, 1024, 512, 256, 128)
MM_OPERAND_BYTES = 12 * 1024 * 1024

NN = ((1,), (0,))
NT = ((1,), (1,))
TN = ((0,), (0,))

WEIGHTS = ['a_norm_w', 'a_in_proj', 'a_conv_w', 'a_conv_b', 'a_dt_bias', 'a_A_log', 'a_D', 'a_gnorm_w', 'a_out_proj',
           'kv_norm_w', 'w_kvf', 'b_f', 'k_norm_w', 'b_norm_w', 'w_q', 'q_norm_w', 'w_o', 'ffn_norm_w', 'w_gate_up',
           'w_down']
BIG = {'a_in_proj': 2, 'a_out_proj': 1, 'w_kvf': 1, 'w_q': 1, 'w_o': 1, 'w_gate_up': 2, 'w_down': 1}
SMALL_SHARDED = {'a_norm_w': 1, 'a_conv_w': 2, 'a_conv_b': 1, 'a_gnorm_w': 1}
SMALL = [n for n in WEIGHTS if n not in BIG]


def _dot(a, b, dims):
    return lax.dot_general(a, b, (dims, ((), ())), preferred_element_type=F32)


def _params(sem=None):
    return pltpu.CompilerParams(dimension_semantics=sem, vmem_limit_bytes=VMEM_LIMIT_BYTES)


def _tile(dim, cap):
    for t in (1408, 1024, 512, 256, 128):
        if t <= cap and dim % t == 0:
            return t
    return dim


class _Beside:
    def __init__(self, operands, results, sem_sizes, start, finish):
        self.operands, self.results, self.sem_sizes = list(operands), list(results), list(sem_sizes)
        self.start, self.finish = start, finish


def _host(kern, beside, *, name, grid, in_specs, out_specs, out_shape, scratch_shapes, semantics, args, prefetch=()):
    single = not isinstance(out_shape, (list, tuple))
    out_specs = [out_specs] if single else list(out_specs)
    out_shape = [out_shape] if single else list(out_shape)
    n_pre = len(prefetch)

    def call(body, in_specs, out_specs, out_shape, scratch_shapes, semantics, aliases, args):
        if not n_pre:
            return pl.pallas_call(body, name=name, grid=grid, in_specs=list(in_specs), out_specs=list(out_specs),
                                  out_shape=out_shape, scratch_shapes=list(scratch_shapes),
                                  input_output_aliases=aliases, compiler_params=_params(semantics))(*args)
        spec = pltpu.PrefetchScalarGridSpec(num_scalar_prefetch=n_pre, grid=grid, in_specs=list(in_specs),
                                            out_specs=list(out_specs), scratch_shapes=list(scratch_shapes))
        return pl.pallas_call(body, name=name, grid_spec=spec, out_shape=out_shape,
                              input_output_aliases={n_pre + i: o for i, o in aliases.items()},
                              compiler_params=_params(semantics))(*prefetch, *args)

    if beside is None:
        res = call(kern, in_specs, out_specs, out_shape, scratch_shapes, semantics, {}, args)
        return (res[0] if single else res), []
    n_in, n_out, n_scr = len(in_specs), len(out_specs), len(scratch_shapes)
    nb_in, nb_out = len(beside.operands), len(beside.results)

    def body(*refs):
        pre, refs = refs[:n_pre], refs[n_pre:]
        ins, b_ins = refs[:n_in], refs[n_in:n_in + nb_in]
        outs = refs[n_in + nb_in:n_in + nb_in + n_out]
        b_outs = refs[n_in + nb_in + n_out:n_in + nb_in + n_out + nb_out]
        rest = refs[n_in + nb_in + n_out + nb_out:]
        scr, sems = rest[:n_scr], rest[n_scr:]
        ids = [pl.program_id(a) for a in range(len(grid))]
        first = functools.reduce(jnp.logical_and, [i == 0 for i in ids])
        last = functools.reduce(jnp.logical_and, [i == g - 1 for i, g in zip(ids, grid)])

        @pl.when(first)
        def _():
            beside.start(b_ins, b_outs, sems)

        kern(*pre, *ins, *outs, *scr)

        @pl.when(last)
        def _():
            beside.finish(b_ins, b_outs, sems)

    any_spec = pl.BlockSpec(memory_space=pl.ANY)
    res = call(body, list(in_specs) + [any_spec] * nb_in, out_specs + [any_spec] * nb_out,
               out_shape + [jax.ShapeDtypeStruct(s, d) for s, d, _ in beside.results],
               list(scratch_shapes) + [pltpu.SemaphoreType.DMA((k,)) for k in beside.sem_sizes],
               ("arbitrary",) * len(grid),
               {n_in + op: n_out + r for r, (_, _, op) in enumerate(beside.results) if op is not None},
               list(args) + list(beside.operands))
    mine = res[:n_out]
    return (mine[0] if single else mine), list(res[n_out:])


def _alone(beside, name):
    return _host(lambda: None, beside, name=name, grid=(1,), in_specs=[], out_specs=[], out_shape=[],
                 scratch_shapes=[], semantics=("arbitrary",), args=[])[1]


def _mm(a, b, mode, name, out_dtype=F32, add=None, shards=False, beside=None):
    if mode == 'nn':
        (m, k), n = a.shape, (b.shape[2] * N_CHIPS if shards else b.shape[1])
    elif mode == 'nt':
        (m, k), n = a.shape, (b.shape[1] if shards else b.shape[0])
    else:
        (k, m), n = a.shape, b.shape[1]
    per_chip = (k if mode == 'nt' else n) // N_CHIPS
    tm = _tile(m, 1024)
    tn = _tile(per_chip if shards and mode != 'nt' else n, 1408 if shards else 1024)
    k_dim = per_chip if shards and mode == 'nt' else k
    a_bytes, b_bytes = jnp.dtype(a.dtype).itemsize, jnp.dtype(b.dtype).itemsize
    tk = next((t for t in MM_K_TILES if k_dim % t == 0 and t * (tm * a_bytes + tn * b_bytes) <= MM_OPERAND_BYTES), k_dim)
    nk = k // tk
    in_place = nk > 1 and out_dtype == F32
    a_spec = pl.BlockSpec((tk, tm), lambda i, j, q: (q, i)) if mode == 'tn' else pl.BlockSpec((tm, tk), lambda i, j, q: (i, q))
    b_spec = pl.BlockSpec((tn, tk), lambda i, j, q: (j, q)) if mode == 'nt' else pl.BlockSpec((tk, tn), lambda i, j, q: (q, j))
    o_spec = pl.BlockSpec((tm, tn), lambda i, j, q: (i, j))
    out_struct = jax.ShapeDtypeStruct((m, n), out_dtype)
    if shards:
        per = per_chip // (tk if mode == 'nt' else tn)
        if mode == 'nn':
            b_spec = pl.BlockSpec((None, tk, tn), lambda i, j, q: (j // per, q, j % per))
        elif mode == 'nt':
            b_spec = pl.BlockSpec((None, tn, tk), lambda i, j, q: (q // per, j, q % per))
        else:
            out_struct = jax.ShapeDtypeStruct((N_CHIPS, m, per_chip), out_dtype)
    out_spec = pl.BlockSpec((None, tm, tn), lambda i, j, q: (j // per, i, j % per)) if shards and mode == 'tn' else o_spec
    dims = {'nn': NN, 'nt': NT, 'tn': TN}[mode]

    n_ins = 3 if add is not None else 2

    def kern(*refs):
        a_ref, b_ref, o_ref = refs[0], refs[1], refs[n_ins]
        acc = o_ref if in_place or nk == 1 else refs[n_ins + 1]
        q = pl.program_id(2)
        part = _dot(a_ref[...].astype(BF16), b_ref[...].astype(BF16), dims)

        def first():
            return part if add is None else part + refs[2][...]

        if nk == 1:
            o_ref[...] = first().astype(o_ref.dtype)
            return

        @pl.when(q == 0)
        def _():
            acc[...] = first()

        @pl.when(q > 0)
        def _():
            acc[...] += part

        if not in_place:
            @pl.when(q == nk - 1)
            def _():
                o_ref[...] = acc[...].astype(o_ref.dtype)

    ins, specs = [a, b], [a_spec, b_spec]
    if add is not None:
        ins.append(add)
        specs.append(o_spec)
    scratch = [] if in_place or nk == 1 else [pltpu.VMEM((tm, tn), F32)]
    res, extra = _host(kern, beside, name=name, grid=(m // tm, n // tn, nk), in_specs=specs, out_specs=out_spec,
                       out_shape=out_struct, scratch_shapes=scratch,
                       semantics=("parallel", "parallel", "arbitrary"), args=ins)
    return res if beside is None else (res, extra)


def _rowwise(fn, name, rows, bcast=(), outs=(), accs=(), tm=256, beside=None):
    rows = [r if isinstance(r, tuple) else (r, r.shape[1], 0) for r in rows]
    n_rows = rows[0][0].shape[0]
    tm = min(tm, n_rows)
    assert n_rows % tm == 0, (name, n_rows, tm)
    n_in, n_out = len(rows) + len(bcast), len(outs)
    in_specs = [pl.BlockSpec((tm, w), functools.partial(lambda i, cb: (i, cb), cb=cb)) for _, w, cb in rows]
    in_specs += [pl.BlockSpec(b.shape, lambda i: (0, 0)) for b in bcast]
    out_specs = [pl.BlockSpec((tm, w), lambda i: (i, 0)) for w, _ in outs]
    out_specs += [pl.BlockSpec(s, lambda i: (0, 0)) for s in accs]
    out_shape = [jax.ShapeDtypeStruct((n_rows, w), d) for w, d in outs] + [jax.ShapeDtypeStruct(s, F32) for s in accs]

    def kern(*refs):
        vals = fn(*[r[...] for r in refs[:n_in]])
        vals = vals if isinstance(vals, (tuple, list)) else (vals,)
        o_refs = refs[n_in:]
        for r, v in zip(o_refs[:n_out], vals[:n_out]):
            r[...] = v.astype(r.dtype)
        if accs:
            @pl.when(pl.program_id(0) == 0)
            def _():
                for r in o_refs[n_out:]:
                    r[...] = jnp.zeros_like(r)

            for r, v in zip(o_refs[n_out:], vals[n_out:]):
                r[...] += v

    res, extra = _host(kern, beside, name=name, grid=(n_rows // tm,), in_specs=in_specs, out_specs=out_specs,
                       out_shape=out_shape, scratch_shapes=[], semantics=("arbitrary",),
                       args=[r[0] for r in rows] + list(bcast))
    return res if beside is None else (res, extra)


def _rms(x, w):
    xf = x.astype(F32)
    return xf * lax.rsqrt(jnp.mean(xf * xf, axis=-1, keepdims=True) + EPS) * w


def _lane_groups(width, group):
    return [slice(g * group, (g + 1) * group) for g in range(width // group)]


def _rms_fwd(x, w, name, tm=256, beside=None):
    def fn(x, w):
        return jnp.concatenate([_rms(x[:, g], w) for g in _lane_groups(x.shape[1], w.shape[1])], axis=1)

    res = _rowwise(fn, name, [x], [w], outs=[(x.shape[1], BF16)], tm=tm, beside=beside)
    return res[0] if beside is None else (res[0][0], res[1])


def _rms_bwd(x, w, dy, name, extra=None, out_dtype=F32, tm=256, copy_bf16=False):
    def fn(x, dy, *rest):
        w = rest[-1]
        dxs, dw = [], jnp.zeros(w.shape, F32)
        for g in _lane_groups(x.shape[1], w.shape[1]):
            _, vjp = jax.vjp(_rms, x[:, g], w)
            dx_g, dw_g = vjp(dy[:, g].astype(F32))
            dxs.append(dx_g)
            dw = dw + dw_g
        dx = jnp.concatenate(dxs, axis=1)
        if extra is not None:
            dx = dx + rest[0]
        return (dx, dx, dw) if copy_bf16 else (dx, dw)

    rows = [x, dy] + ([extra] if extra is not None else [])
    outs = [(x.shape[1], out_dtype)] + ([(x.shape[1], BF16)] if copy_bf16 else [])
    return _rowwise(fn, name, rows, [w], outs=outs, accs=[w.shape], tm=tm)


def _sigmoid(x):
    return 1.0 / (1.0 + jnp.exp(-x))


def _softplus(x):
    return jnp.maximum(x, 0.0) + jnp.log(1.0 + jnp.exp(-jnp.abs(x)))


def _swiglu_fwd(gu, d_ff, name):
    def fn(g, u):
        g, u = g.astype(F32), u.astype(F32)
        return g * _sigmoid(g) * u

    return _rowwise(fn, name, [(gu, d_ff, 0), (gu, d_ff, 1)], outs=[(d_ff, BF16)], tm=128)[0]


def _swiglu_bwd(gu, dact, d_ff, name):
    def fn(g, u, da):
        g, u = g.astype(F32), u.astype(F32)
        s = _sigmoid(g)
        dg = da * u * s * (1.0 + g * (1.0 - s))
        du = da * g * s
        return jnp.concatenate([dg, du], axis=1)

    return _rowwise(fn, name, [(gu, d_ff, 0), (gu, d_ff, 1), dact], outs=[(2 * d_ff, BF16)], tm=128)[0]


def _cumsum_rows(v, reverse=False):
    n = v.shape[0]
    row = lax.broadcasted_iota(jnp.int32, v.shape, 0)
    sh = 1
    while sh < n:
        if reverse:
            v = v + jnp.where(row < n - sh, pltpu.roll(v, n - sh, 0), 0.0)
        else:
            v = v + jnp.where(row >= sh, pltpu.roll(v, sh, 0), 0.0)
        sh *= 2
    return v


def _conv_fwd(u, w, b, name):
    s, c = u.shape
    kw = w.shape[0]
    tc = _tile(c, 128)

    def kern(u_ref, w_ref, b_ref, o_ref):
        uu = u_ref[...]
        row = lax.broadcasted_iota(jnp.int32, uu.shape, 0)
        acc = jnp.zeros_like(uu) + b_ref[...]
        for k in range(kw):
            sh = kw - 1 - k
            uk = uu if sh == 0 else jnp.where(row >= sh, pltpu.roll(uu, sh, 0), 0.0)
            acc = acc + w_ref[pl.ds(k, 1), :] * uk
        o_ref[...] = acc * _sigmoid(acc)

    return pl.pallas_call(
        kern, name=name, grid=(c // tc,),
        in_specs=[pl.BlockSpec((s, tc), lambda j: (0, j)), pl.BlockSpec((kw, tc), lambda j: (0, j)),
                  pl.BlockSpec((1, tc), lambda j: (0, j))],
        out_specs=pl.BlockSpec((s, tc), lambda j: (0, j)), out_shape=jax.ShapeDtypeStruct((s, c), F32),
        compiler_params=_params(("parallel",)))(u, w, b)


def _conv_bwd(u, w, b, dact, name, beside=None):
    s, c = u.shape
    kw = w.shape[0]
    tc = _tile(c, 128)

    def kern(u_ref, w_ref, b_ref, d_ref, du_ref, dw_ref, db_ref):
        uu = u_ref[...]
        row = lax.broadcasted_iota(jnp.int32, uu.shape, 0)
        shifted = []
        acc = jnp.zeros_like(uu) + b_ref[...]
        for k in range(kw):
            sh = kw - 1 - k
            uk = uu if sh == 0 else jnp.where(row >= sh, pltpu.roll(uu, sh, 0), 0.0)
            shifted.append(uk)
            acc = acc + w_ref[pl.ds(k, 1), :] * uk
        sg = _sigmoid(acc)
        dacc = d_ref[...] * sg * (1.0 + acc * (1.0 - sg))
        db_ref[...] = jnp.sum(dacc, axis=0, keepdims=True)
        du = jnp.zeros_like(uu)
        for k in range(kw):
            sh = kw - 1 - k
            dw_ref[pl.ds(k, 1), :] = jnp.sum(dacc * shifted[k], axis=0, keepdims=True)
            dk = dacc if sh == 0 else jnp.where(row < s - sh, pltpu.roll(dacc, s - sh, 0), 0.0)
            du = du + w_ref[pl.ds(k, 1), :] * dk
        du_ref[...] = du.astype(du_ref.dtype)

    col = lambda j: (0, j)
    return _host(
        kern, beside, name=name, grid=(c // tc,),
        in_specs=[pl.BlockSpec((s, tc), col), pl.BlockSpec((kw, tc), col), pl.BlockSpec((1, tc), col),
                  pl.BlockSpec((s, tc), col)],
        out_specs=[pl.BlockSpec((s, tc), col), pl.BlockSpec((kw, tc), col), pl.BlockSpec((1, tc), col)],
        out_shape=[jax.ShapeDtypeStruct((s, c), BF16), jax.ShapeDtypeStruct((kw, c), F32),
                   jax.ShapeDtypeStruct((1, c), F32)],
        scratch_shapes=[], semantics=("parallel",), args=[u, w, b, dact])


def _dt_fwd(dtraw, bias, a_log, name):
    def fn(raw, bias, a_log):
        dt = _softplus(raw + bias)
        return dt, _cumsum_rows(dt * (-jnp.exp(a_log)))

    return _rowwise(fn, name, [dtraw], [bias, a_log], outs=[(LANES, F32), (LANES, F32)], tm=SSD_CHUNK)


def _dt_bwd(dtraw, bias, a_log, ddt, dacum, name):
    def fn(raw, ddt, dac, bias, a_log):
        z = raw + bias
        dt = _softplus(z)
        a_neg = -jnp.exp(a_log)
        da = _cumsum_rows(dac, reverse=True)
        draw = (ddt + da * a_neg) * _sigmoid(z)
        return draw, jnp.sum(draw, axis=0, keepdims=True), jnp.sum(da * dt, axis=0, keepdims=True) * a_neg

    return _rowwise(fn, name, [dtraw, ddt, dacum], [bias, a_log], outs=[(LANES, BF16)],
                    accs=[(1, LANES), (1, LANES)], tm=SSD_CHUNK)


def _ssd_pieces(xs, dt, ac, kh):
    l, gw = xs.shape
    lane_w = lax.broadcasted_iota(jnp.int32, (l, gw), 1)
    lane_k = lax.broadcasted_iota(jnp.int32, (l, LANES), 1)
    head = [(lane_w >= k * SSM_HEAD_DIM) & (lane_w < (k + 1) * SSM_HEAD_DIM) for k in range(kh)]

    def col(blk, k):
        return jnp.sum(jnp.where(lane_k == k, blk, 0.0), axis=1, keepdims=True)

    def expand(blk):
        acc = jnp.zeros((l, gw), F32)
        for k in range(kh):
            acc = jnp.where(head[k], col(blk, k), acc)
        return acc

    def collapse(wide):
        acc = jnp.zeros((l, LANES), F32)
        for k in range(kh):
            acc = jnp.where(lane_k == k, jnp.sum(jnp.where(head[k], wide, 0.0), axis=1, keepdims=True), acc)
        return acc

    return head, lane_k, col, expand, collapse


def _head_block(k):
    per_block = LANES // SSM_HEAD_DIM
    lane = lax.broadcasted_iota(jnp.int32, (1, LANES), 1)
    lo = (k % per_block) * SSM_HEAD_DIM
    return slice((k // per_block) * LANES, (k // per_block + 1) * LANES), (lane >= lo) & (lane < lo + SSM_HEAD_DIM)


def _by_block(pieces, n_blocks, like):
    zero = jnp.zeros((like.shape[0], LANES), F32)
    return jnp.concatenate([pieces.get(p, zero) for p in range(n_blocks)], axis=1)


def _ssd_specs(l, gw, n, n_xs_blocks):
    g_axis = SSM_GROUPS
    return dict(
        xs=lambda cm: pl.BlockSpec((l, gw), lambda g, c: (cm(c), g)),
        b=lambda cm: pl.BlockSpec((l, n), lambda g, c: (cm(c), n_xs_blocks + g)),
        c=lambda cm: pl.BlockSpec((l, n), lambda g, c: (cm(c), n_xs_blocks + g_axis + g)),
        col=lambda cm: pl.BlockSpec((None, l, LANES), lambda g, c: (g, cm(c), 0)),
        row=lambda cm: pl.BlockSpec((None, 8, l), lambda g, c: (g, 0, cm(c))),
        state=lambda cm: pl.BlockSpec((None, None, n, gw), lambda g, c: (cm(c), g, 0, 0)),
    )


def _ssd_fwd(xbc, dt_g, ac_g, act_g, d_inner, n, kh, name, beside=None):
    s = xbc.shape[0]
    l, g_n = SSD_CHUNK, SSM_GROUPS
    gw, nc = d_inner // g_n, s // l
    sp = _ssd_specs(l, gw, n, d_inner // n)
    fwd = lambda c: c

    def kern(xs_ref, b_ref, c_ref, dt_ref, ac_ref, act_ref, y_ref, s0_ref, st):
        @pl.when(pl.program_id(1) == 0)
        def _():
            st[...] = jnp.zeros_like(st)

        s0 = st[...]
        s0_ref[...] = s0
        xs, ac = xs_ref[...], ac_ref[...]
        head, _, col, expand, _ = _ssd_pieces(xs, dt_ref[...], ac, kh)
        ace = expand(ac)
        x = xs * expand(dt_ref[...])
        xb, bb, cb_ = x.astype(BF16), b_ref[...].astype(BF16), c_ref[...].astype(BF16)
        cb = _dot(cb_, bb, NT)
        ri = lax.broadcasted_iota(jnp.int32, (l, l), 0)
        ci = lax.broadcasted_iota(jnp.int32, (l, l), 1)
        causal = ri >= ci
        y_diag = {}
        for k in range(kh):
            seg = col(ac, k) - act_ref[pl.ds(k, 1), :]
            m = jnp.where(causal, cb * jnp.exp(jnp.where(causal, seg, 0.0)), 0.0)
            blk, mine = _head_block(k)
            yk = _dot(m.astype(BF16), xb[:, blk], NN)
            y_diag[blk.start // LANES] = jnp.where(mine, yk, y_diag.get(blk.start // LANES, 0.0))
        y_ref[...] = _dot(cb_, s0.astype(BF16), NN) * jnp.exp(ace) + _by_block(y_diag, gw // LANES, xs)
        row_w = lax.broadcasted_iota(jnp.int32, (l, gw), 0)
        alast = jnp.sum(jnp.where(row_w == l - 1, ace, 0.0), axis=0, keepdims=True)
        st[...] = s0 * jnp.exp(alast) + _dot(bb, (jnp.exp(alast - ace) * x).astype(BF16), TN)

    return _host(
        kern, beside, name=name, grid=(g_n, nc),
        in_specs=[sp['xs'](fwd), sp['b'](fwd), sp['c'](fwd), sp['col'](fwd), sp['col'](fwd), sp['row'](fwd)],
        out_specs=[pl.BlockSpec((l, gw), lambda g, c: (c, g)), sp['state'](fwd)],
        out_shape=[jax.ShapeDtypeStruct((s, d_inner), F32), jax.ShapeDtypeStruct((nc, g_n, n, gw), F32)],
        scratch_shapes=[pltpu.VMEM((n, gw), F32)], semantics=("arbitrary", "arbitrary"),
        args=[xbc, xbc, xbc, dt_g, ac_g, act_g])


def _ssd_bwd(xbc, dt_g, ac_g, act_g, s0_all, dy, dxs_skip, d_inner, n, kh, name, beside=None):
    s = xbc.shape[0]
    l, g_n = SSD_CHUNK, SSM_GROUPS
    gw, nc = d_inner // g_n, s // l
    sp = _ssd_specs(l, gw, n, d_inner // n)
    rev = lambda c: nc - 1 - c

    def kern(xs_ref, b_ref, c_ref, dt_ref, ac_ref, act_ref, s0_ref, dy_ref, skip_ref,
             dxs_ref, db_ref, dc_ref, ddt_ref, dacc_ref, dacr_ref, dst):
        @pl.when(pl.program_id(1) == 0)
        def _():
            dst[...] = jnp.zeros_like(dst)

        dsn = dst[...]
        s0 = s0_ref[...]
        xs, ac, dy = xs_ref[...], ac_ref[...], dy_ref[...]
        head, lane_k, col, expand, collapse = _ssd_pieces(xs, dt_ref[...], ac, kh)
        ace, dte = expand(ac), expand(dt_ref[...])
        x = xs * dte
        xb, bb, cb_ = x.astype(BF16), b_ref[...].astype(BF16), c_ref[...].astype(BF16)
        s0b, dsnb, dyb = s0.astype(BF16), dsn.astype(BF16), dy.astype(BF16)
        cb = _dot(cb_, bb, NT)
        ri = lax.broadcasted_iota(jnp.int32, (l, l), 0)
        ci = lax.broadcasted_iota(jnp.int32, (l, l), 1)
        causal = ri >= ci
        row_w = lax.broadcasted_iota(jnp.int32, (l, gw), 0)
        e = jnp.exp(ace)
        alast = jnp.sum(jnp.where(row_w == l - 1, ace, 0.0), axis=0, keepdims=True)
        gdec = jnp.exp(alast)
        wt = jnp.exp(alast - ace)
        cs = _dot(cb_, s0b, NN)
        dcs = (dy * e).astype(BF16)
        d_c = _dot(dcs, s0b, NT)
        ds_off = _dot(cb_, dcs, TN)
        dace = dy * cs * e
        dalast = jnp.sum(dsn * s0, axis=0, keepdims=True) * gdec
        z = wt * x
        dz = _dot(bb, dsnb, NN)
        d_b = _dot(z.astype(BF16), dsnb, NT)
        dx = dz * wt
        t = dz * z
        dalast = dalast + jnp.sum(t, axis=0, keepdims=True)
        dace = dace - t
        dcb = jnp.zeros((l, l), F32)
        dac_col = jnp.zeros((l, LANES), F32)
        dx_diag = {}
        for k in range(kh):
            seg = col(ac, k) - act_ref[pl.ds(k, 1), :]
            dk = jnp.exp(jnp.where(causal, seg, 0.0))
            mk = jnp.where(causal, cb * dk, 0.0)
            blk, mine = _head_block(k)
            dxk = _dot(mk.astype(BF16), dyb[:, blk], TN)
            dx_diag[blk.start // LANES] = jnp.where(mine, dxk, dx_diag.get(blk.start // LANES, 0.0))
            dmk = _dot(jnp.where(mine, dy[:, blk], 0.0).astype(BF16), xb[:, blk], NT)
            dcb = dcb + jnp.where(causal, dmk * dk, 0.0)
            dseg = dmk * mk
            dac_col = jnp.where(lane_k == k, jnp.sum(dseg, axis=1, keepdims=True), dac_col)
            dacr_ref[pl.ds(k, 1), :] = -jnp.sum(dseg, axis=0, keepdims=True)
        for k in range(kh, 8):
            dacr_ref[pl.ds(k, 1), :] = jnp.zeros((1, l), F32)
        dx = dx + _by_block(dx_diag, gw // LANES, xs)
        dcbb = dcb.astype(BF16)
        dc_ref[...] = d_c + _dot(dcbb, bb, NN)
        db_ref[...] = d_b + _dot(dcbb, cb_, TN)
        dace = jnp.where(row_w == l - 1, dace + dalast, dace)
        dacc_ref[...] = dac_col + collapse(dace)
        ddt_ref[...] = collapse(dx * xs)
        dxs_ref[...] = dx * dte + skip_ref[...]
        dst[...] = dsn * gdec + ds_off

    return _host(
        kern, beside, name=name, grid=(g_n, nc),
        in_specs=[sp['xs'](rev), sp['b'](rev), sp['c'](rev), sp['col'](rev), sp['col'](rev), sp['row'](rev),
                  sp['state'](rev), pl.BlockSpec((l, gw), lambda g, c: (rev(c), g)),
                  pl.BlockSpec((l, gw), lambda g, c: (rev(c), g))],
        out_specs=[pl.BlockSpec((l, gw), lambda g, c: (rev(c), g)), pl.BlockSpec((l, n), lambda g, c: (rev(c), g)),
                   pl.BlockSpec((l, n), lambda g, c: (rev(c), g)), sp['col'](rev), sp['col'](rev), sp['row'](rev)],
        out_shape=[jax.ShapeDtypeStruct((s, d_inner), F32), jax.ShapeDtypeStruct((s, g_n * n), F32),
                   jax.ShapeDtypeStruct((s, g_n * n), F32), jax.ShapeDtypeStruct((g_n, s, LANES), F32),
                   jax.ShapeDtypeStruct((g_n, s, LANES), F32), jax.ShapeDtypeStruct((g_n, 8, s), F32)],
        scratch_shapes=[pltpu.VMEM((n, gw), F32)], semantics=("arbitrary", "arbitrary"),
        args=[xbc, xbc, xbc, dt_g, ac_g, act_g, s0_all, dy, dxs_skip])


def _gate(y, xs, z, d_exp, gw):
    t = (y + xs * d_exp) * (z * _sigmoid(z))
    width = t.shape[1]
    gsz = width // SSM_GROUPS
    lane = lax.broadcasted_iota(jnp.int32, t.shape, 1)
    t2 = t * t
    scale = jnp.zeros_like(t)
    for g in range(SSM_GROUPS):
        in_g = (lane >= g * gsz) & (lane < (g + 1) * gsz)
        ms = jnp.sum(jnp.where(in_g, t2, 0.0), axis=1, keepdims=True) * (1.0 / gsz)
        scale = jnp.where(in_g, lax.rsqrt(ms + EPS), scale)
    return t * scale * gw


def _gate_fwd(y, xbc, z, d_exp, gw, d_inner, name, beside=None):
    res = _rowwise(_gate, name, [y, (xbc, d_inner, 0), z], [d_exp, gw], outs=[(d_inner, BF16)], tm=128, beside=beside)
    return res[0] if beside is None else (res[0][0], res[1])


def _gate_bwd(y, xbc, z, d_exp, gw, dyn, d_inner, name):
    def fn(y, xs, z, dyn, d_exp, gw):
        _, vjp = jax.vjp(_gate, y, xs, z, d_exp, gw)
        return vjp(dyn)

    return _rowwise(fn, name, [y, (xbc, d_inner, 0), z, dyn], [d_exp, gw],
                    outs=[(d_inner, F32), (d_inner, F32), (d_inner, BF16)], accs=[d_exp.shape, gw.shape], tm=64)


def _forget_fwd(fraw, b_f, name):
    def kern(f_ref, b_ref, o_ref):
        o_ref[...] = _cumsum_rows(-_softplus(-(f_ref[...] + b_ref[...])))

    return pl.pallas_call(kern, name=name, out_shape=jax.ShapeDtypeStruct(fraw.shape, F32),
                          compiler_params=_params())(fraw, b_f)


def _forget_bwd(fraw, b_f, dcum, name):
    def kern(f_ref, b_ref, d_ref, df_ref, db_ref):
        df = _cumsum_rows(d_ref[...], reverse=True) * _sigmoid(-(f_ref[...] + b_ref[...]))
        df_ref[...] = df.astype(df_ref.dtype)
        db_ref[...] = jnp.sum(df, axis=0, keepdims=True)

    return pl.pallas_call(
        kern, name=name,
        out_shape=[jax.ShapeDtypeStruct(fraw.shape, BF16), jax.ShapeDtypeStruct((1, fraw.shape[1]), F32)],
        compiler_params=_params())(fraw, b_f, dcum)


ATT_BLOCK = 512


def _attn_fwd(q, k, v, cq_rep, ck, name, beside=None):
    s, hd = q.shape
    h_n, d = hd // ATT_HEAD_DIM, ATT_HEAD_DIM
    tb = min(ATT_BLOCK, s)
    nb = s // tb
    scale = d ** -0.5

    def kern(q_ref, k_ref, v_ref, cq_ref, ck_ref, o_ref, lse_ref):
        i = pl.program_id(1)
        qq = q_ref[...]
        cq = jnp.max(cq_ref[...], axis=1, keepdims=True)
        rowpos = i * tb + lax.broadcasted_iota(jnp.int32, (tb, tb), 0)
        coli = lax.broadcasted_iota(jnp.int32, (tb, tb), 1)

        def step(j, carry, diagonal):
            m, l_, acc = carry
            ks = pl.ds(pl.multiple_of(j * tb, tb), tb)
            sc = _dot(qq, k_ref[ks, :], NT) * scale + (cq - ck_ref[j])
            if diagonal:
                sc = jnp.where(j * tb + coli <= rowpos, sc, NEG)
            mn = jnp.maximum(m, jnp.max(sc, axis=1, keepdims=True))
            p = jnp.exp(sc - mn)
            alpha = jnp.exp(m - mn)
            l_ = alpha * l_ + jnp.sum(p, axis=1, keepdims=True)
            acc = alpha * acc + _dot(p.astype(BF16), v_ref[ks, :], NN)
            return mn, l_, acc

        init = (jnp.full((tb, 1), NEG, F32), jnp.zeros((tb, 1), F32), jnp.zeros((tb, d), F32))
        below = lax.fori_loop(0, i, lambda j, carry: step(j, carry, False), init)
        m, l_, acc = step(i, below, True)
        o_ref[...] = (acc / l_).astype(o_ref.dtype)
        lse_ref[...] = jnp.broadcast_to(m + jnp.log(l_), (tb, d))

    return _host(
        kern, beside, name=name, grid=(h_n, nb),
        in_specs=[pl.BlockSpec((tb, d), lambda h, i: (i, h)), pl.BlockSpec((s, d), lambda h, i: (0, h)),
                  pl.BlockSpec((s, d), lambda h, i: (0, h)), pl.BlockSpec((tb, d), lambda h, i: (i, h)),
                  pl.BlockSpec((None, nb, 1, tb), lambda h, i: (h, 0, 0, 0))],
        out_specs=[pl.BlockSpec((tb, d), lambda h, i: (i, h)), pl.BlockSpec((tb, d), lambda h, i: (i, h))],
        out_shape=[jax.ShapeDtypeStruct((s, hd), BF16), jax.ShapeDtypeStruct((s, hd), F32)],
        scratch_shapes=[], semantics=("parallel", "arbitrary"), args=[q, k, v, cq_rep, ck])


def _attn_bwd(q, k, v, do, lse_rep, delta_rep, cq_rep, ck, name, beside=None):
    s, hd = q.shape
    h_n, d = hd // ATT_HEAD_DIM, ATT_HEAD_DIM
    tb = min(ATT_BLOCK, s)
    nb = s // tb
    scale = d ** -0.5

    def kern(q_ref, do_ref, k_ref, v_ref, lse_ref, dl_ref, cq_ref, ck_ref, dq_ref, dk_ref, dv_ref, dcq_ref, dck_ref):
        j = pl.program_id(1)

        @pl.when(j == 0)
        def _():
            dq_ref[...] = jnp.zeros_like(dq_ref)
            dcq_ref[...] = jnp.zeros_like(dcq_ref)

        kj, vj, ckj = k_ref[...], v_ref[...], ck_ref[...]
        colpos = j * tb + lax.broadcasted_iota(jnp.int32, (tb, tb), 1)
        rowi = lax.broadcasted_iota(jnp.int32, (tb, tb), 0)

        def step(i, carry, diagonal):
            dk, dv, dck = carry
            rs = pl.ds(pl.multiple_of(i * tb, tb), tb)
            qi, doi = q_ref[rs, :], do_ref[rs, :]
            lse = jnp.max(lse_ref[rs, :], axis=1, keepdims=True)
            dl = jnp.max(dl_ref[rs, :], axis=1, keepdims=True)
            cq = jnp.max(cq_ref[rs, :], axis=1, keepdims=True)
            sc = _dot(qi, kj, NT) * scale + ((cq - lse) - ckj)
            p = jnp.exp(jnp.where(colpos <= i * tb + rowi, sc, NEG) if diagonal else sc)
            dp = _dot(doi, vj, NT)
            ds = p * (dp - dl)
            dsb = ds.astype(BF16)
            dv = dv + _dot(p.astype(BF16), doi, TN)
            dk = dk + _dot(dsb, qi, TN)
            dq_ref[rs, :] += _dot(dsb, kj, NN) * scale
            dcq_ref[rs, :] += jnp.broadcast_to(jnp.sum(ds, axis=1, keepdims=True), (tb, d))
            return dk, dv, dck - jnp.sum(ds, axis=0, keepdims=True)

        init = (jnp.zeros((tb, d), F32), jnp.zeros((tb, d), F32), jnp.zeros((1, tb), F32))
        dk, dv, dck = lax.fori_loop(j + 1, nb, lambda i, carry: step(i, carry, False), step(j, init, True))
        dk_ref[...] = dk * scale
        dv_ref[...] = dv.astype(dv_ref.dtype)
        dck_ref[...] = dck

    whole = pl.BlockSpec((s, d), lambda h, j: (0, h))
    blk = pl.BlockSpec((tb, d), lambda h, j: (j, h))
    ckb = pl.BlockSpec((None, None, 1, tb), lambda h, j: (h, j, 0, 0))
    return _host(
        kern, beside, name=name, grid=(h_n, nb),
        in_specs=[whole, whole, blk, blk, whole, whole, whole, ckb],
        out_specs=[whole, blk, blk, whole, ckb],
        out_shape=[jax.ShapeDtypeStruct((s, hd), F32), jax.ShapeDtypeStruct((s, hd), F32),
                   jax.ShapeDtypeStruct((s, hd), BF16), jax.ShapeDtypeStruct((s, hd), F32),
                   jax.ShapeDtypeStruct((h_n, nb, 1, tb), F32)],
        scratch_shapes=[], semantics=("arbitrary", "arbitrary"), args=[q, do, k, v, lse_rep, delta_rep, cq_rep, ck])


def _adamw(w, g, m, v, name):
    cols = w.shape[1]
    tm = _tile(w.shape[0], 128) if w.shape[0] % 128 == 0 else w.shape[0]
    return _rowwise(_adamw_math, name, [w, g, m, v], outs=[(cols, F32)] * 3, tm=tm)


def _sum_arrays(arrs, out_dtype, name):
    def fn(*xs):
        acc = xs[0].astype(F32)
        for x in xs[1:]:
            acc = acc + x.astype(F32)
        return acc

    tm = PACK_ROWS if arrs[0].shape[0] % PACK_ROWS == 0 else arrs[0].shape[0]
    return _rowwise(fn, name, list(arrs), outs=[(arrs[0].shape[1], out_dtype)], tm=tm)[0]


def _half_tile(rows):
    for t in (256, 176, 128, 64, 32, 16):
        if rows % t == 0:
            return t
    return rows


def _sum_half(g, got, core, name):
    n_chip, hr, cols = got.shape
    tm = _half_tile(hr)
    nt = hr // tm

    def kern(c_ref, g_ref, a_ref, o_ref):
        o_ref[...] = (g_ref[...].astype(F32) + a_ref[...].astype(F32)).astype(o_ref.dtype)

    grid_spec = pltpu.PrefetchScalarGridSpec(
        num_scalar_prefetch=1, grid=(n_chip * nt,),
        in_specs=[pl.BlockSpec((tm, cols), lambda r, c: (((r // nt) * 2 + c[0]) * nt + r % nt, 0)),
                  pl.BlockSpec((tm, cols), lambda r, c: (r, 0))],
        out_specs=pl.BlockSpec((tm, cols), lambda r, c: (r, 0)))
    out = pl.pallas_call(
        kern, name=name, grid_spec=grid_spec, out_shape=jax.ShapeDtypeStruct((n_chip * hr, cols), BF16),
        compiler_params=_params(("arbitrary",)))(core.reshape(1), g.reshape(-1, cols), got.reshape(-1, cols))
    return out.reshape(n_chip, hr, cols)


def _sum_parts(own, parts, chip, name):
    n_parts, hr, cols = parts.shape
    tm = _half_tile(hr)

    def kern(s_ref, t_ref, p_ref, o_ref):
        acc = t_ref[...].astype(F32)
        for j in range(n_parts):
            acc = acc + p_ref[j].astype(F32)
        o_ref[...] = acc

    grid_spec = pltpu.PrefetchScalarGridSpec(
        num_scalar_prefetch=1, grid=(hr // tm,),
        in_specs=[pl.BlockSpec((None, tm, cols), lambda r, s: (s[0], r, 0)),
                  pl.BlockSpec((n_parts, tm, cols), lambda r, s: (0, r, 0))],
        out_specs=pl.BlockSpec((tm, cols), lambda r, s: (r, 0)))
    return pl.pallas_call(
        kern, name=name, grid_spec=grid_spec, out_shape=jax.ShapeDtypeStruct((hr, cols), F32),
        compiler_params=_params(("arbitrary",)))(chip.reshape(1), own, parts)


def _adamw_math(w, g, m, v):
    m = ADAM_B1 * m + (1.0 - ADAM_B1) * g
    v = ADAM_B2 * v + (1.0 - ADAM_B2) * (g * g)
    m_hat = m / (1.0 - ADAM_B1 ** ADAM_STEP)
    v_hat = v / (1.0 - ADAM_B2 ** ADAM_STEP)
    return -ADAM_LR * (m_hat / (jnp.sqrt(v_hat) + ADAM_EPS) + ADAM_WD * w), m, v


def _adamw_big(w, m, v, mine, theirs, core, layer, prev, name):
    n_layers, rows, cols = w.shape
    hr = rows // 2
    tm = next(t for t in (128, 64, 32, 16, 8) if hr % t == 0)
    nt = hr // tm

    def kern(s_ref, w_ref, m_ref, v_ref, a_ref, b_ref, *rest):
        g_ref, d_ref, nm_ref, nv_ref = rest[-4:]
        g = jnp.where(pl.program_id(0) // nt == s_ref[0], a_ref[...], b_ref[...])
        g_ref[...] = g
        d_ref[...], nm_ref[...], nv_ref[...] = _adamw_math(w_ref[...], g, m_ref[...], v_ref[...])

    lyr = pl.BlockSpec((None, tm, cols), lambda r, s: (layer, r, 0))
    half = pl.BlockSpec((tm, cols), lambda r, s: (r % nt, 0))
    passed = [] if prev is None else list(prev)
    grid_spec = pltpu.PrefetchScalarGridSpec(
        num_scalar_prefetch=1, grid=(rows // tm,),
        in_specs=[lyr, lyr, lyr, half, half] + [pl.BlockSpec(memory_space=pl.ANY)] * len(passed), out_specs=[lyr] * 4)
    return pl.pallas_call(
        kern, name=name, grid_spec=grid_spec, out_shape=[jax.ShapeDtypeStruct(w.shape, F32)] * 4,
        input_output_aliases={6 + i: i for i in range(len(passed))},
        compiler_params=_params(("arbitrary",)))(core.reshape(1), w, m, v, mine, theirs, *passed)


def _cast_into_slot(src, slot, name, beside=None):
    rows, cols = src.shape
    tm = _half_tile(rows)

    def kern(s_ref, x_ref, o_ref):
        o_ref[...] = x_ref[...].astype(o_ref.dtype)

    res, extra = _host(
        kern, beside, name=name, grid=(rows // tm,), in_specs=[pl.BlockSpec((tm, cols), lambda r, s: (r, 0))],
        out_specs=pl.BlockSpec((None, tm, cols), lambda r, s: (s[0], r, 0)),
        out_shape=jax.ShapeDtypeStruct((N_CHIPS, rows, cols), BF16), scratch_shapes=[], semantics=("arbitrary",),
        args=[src], prefetch=[slot.reshape(1)])
    return res if beside is None else (res, extra)


FLIP_C = [(0, 0, 1)]
FLIP_XY = [(1, 0, 0), (0, 1, 0), (1, 1, 0)]
FLIP_ALL = [(fx, fy, fc) for fx in (0, 1) for fy in (0, 1) for fc in (0, 1) if (fx, fy, fc) != (0, 0, 0)]


def _chip(dev):
    return 2 * dev[0] + dev[1]


def _device(dev):
    return 4 * dev[0] + 2 * dev[1] + dev[2]


def _exchange(name, srcs, dst_shapes, rels, src_view, dst_view, own_view=None, in_place=False):
    n, n_rel = len(srcs), len(rels)

    def body(*refs):
        src_refs, dst_refs = refs[:n], refs[n:2 * n]
        send_sems, recv_sems, own_sems = refs[2 * n:]
        me = (lax.axis_index("x"), lax.axis_index("y"), lax.axis_index("c"))
        peers = [tuple(1 - m if f else m for m, f in zip(me, rel)) for rel in rels]

        def copy(i, k, sender, receiver):
            return pltpu.make_async_remote_copy(
                src_ref=src_view(i, src_refs[i], sender, receiver), dst_ref=dst_view(i, dst_refs[i], sender, k),
                send_sem=send_sems.at[i * n_rel + k], recv_sem=recv_sems.at[i * n_rel + k], device_id=receiver,
                device_id_type=pl.DeviceIdType.MESH)

        sends = [copy(i, k, me, peer) for i in range(n) for k, peer in enumerate(peers)]
        for cp in sends:
            cp.start()
        mine = []
        if own_view is not None:
            for i in range(n):
                frm, to = own_view(i, src_refs[i], dst_refs[i], me)
                mine.append(pltpu.make_async_copy(frm, to, own_sems.at[i]))
                mine[-1].start()
        for i in range(n):
            for k, peer in enumerate(peers):
                copy(i, k, peer, me).wait_recv()
        for cp in sends:
            cp.wait_send()
        for cp in mine:
            cp.wait()

    any_spec = pl.BlockSpec(memory_space=pl.ANY)
    return pl.pallas_call(
        body, name=name, out_shape=[jax.ShapeDtypeStruct(s, d) for s, d in dst_shapes],
        in_specs=[any_spec] * n, out_specs=[any_spec] * n,
        input_output_aliases={i: i for i in range(n)} if in_place else {},
        scratch_shapes=[pltpu.SemaphoreType.DMA((n * n_rel,)), pltpu.SemaphoreType.DMA((n * n_rel,)),
                        pltpu.SemaphoreType.DMA((n,))])(*srcs)


def _gather_plan(bufs, rows_of=None):
    n = len(bufs)
    rows_of = rows_of or [(0, b.shape[1]) for b in bufs]
    starts = [r0 for r0, _ in rows_of]
    halves = [nr // 2 for _, nr in rows_of]

    def tools(src_refs, dst_refs, sems):
        ici_send, ici_recv, d2d_send, d2d_recv = sems
        x, y, c = lax.axis_index("x"), lax.axis_index("y"), lax.axis_index("c")
        sibling = (x, y, 1 - c)
        chips = [(1 - x, y), (x, 1 - y), (1 - x, 1 - y)]

        def rows(i, chip, core):
            return dst_refs[i].at[2 * chip[0] + chip[1], pl.ds(starts[i] + core * halves[i], halves[i]), :]

        def over_ici(i, k, src, chip_from, to):
            return pltpu.make_async_remote_copy(
                src_ref=src, dst_ref=rows(i, chip_from, c), send_sem=ici_send.at[3 * i + k],
                recv_sem=ici_recv.at[3 * i + k], device_id=to, device_id_type=pl.DeviceIdType.MESH)

        def over_d2d(i, k, core):
            return pltpu.make_async_remote_copy(
                src_ref=rows(i, chips[k], core), dst_ref=rows(i, chips[k], core), send_sem=d2d_send.at[3 * i + k],
                recv_sem=d2d_recv.at[3 * i + k], device_id=sibling, device_id_type=pl.DeviceIdType.MESH)

        def my_send(i, k):
            my_half = src_refs[i].at[2 * x + y, pl.ds(starts[i] + c * halves[i], halves[i]), :]
            return over_ici(i, k, my_half, (x, y), (*chips[k], c))

        def my_arrival(i, k):
            return over_ici(i, k, rows(i, chips[k], c), chips[k], (x, y, c))

        return c, over_d2d, my_send, my_arrival

    def start(src_refs, dst_refs, sems):
        _, _, my_send, _ = tools(src_refs, dst_refs, sems)
        for i in range(n):
            for k in range(3):
                my_send(i, k).start()

    def finish(src_refs, dst_refs, sems):
        c, over_d2d, my_send, my_arrival = tools(src_refs, dst_refs, sems)
        passed = []
        for i in range(n):
            for k in range(3):
                my_arrival(i, k).wait_recv()
                passed.append(over_d2d(i, k, c))
                passed[-1].start()
        for i in range(n):
            for k in range(3):
                over_d2d(i, k, 1 - c).wait_recv()
        for i in range(n):
            for k in range(3):
                my_send(i, k).wait_send()
        for cp in passed:
            cp.wait_send()

    return _Beside(bufs, [(b.shape, b.dtype, i) for i, b in enumerate(bufs)], [3 * n] * 4, start, finish)


def _scatter_plan(sums):
    n = len(sums)

    def copy(i, k, src_refs, dst_refs, sems, sender, receiver):
        return pltpu.make_async_remote_copy(
            src_ref=src_refs[i].at[_chip(receiver)], dst_ref=dst_refs[i].at[k], send_sem=sems[0].at[3 * i + k],
            recv_sem=sems[1].at[3 * i + k], device_id=receiver, device_id_type=pl.DeviceIdType.MESH)

    def each(fn, src_refs, dst_refs, sems, outgoing):
        me = (lax.axis_index("x"), lax.axis_index("y"), lax.axis_index("c"))
        for i in range(n):
            for k, rel in enumerate(FLIP_XY):
                peer = tuple(1 - m if f else m for m, f in zip(me, rel))
                fn(copy(i, k, src_refs, dst_refs, sems, *((me, peer) if outgoing else (peer, me))))

    def start(src_refs, dst_refs, sems):
        each(lambda cp: cp.start(), src_refs, dst_refs, sems, True)

    def finish(src_refs, dst_refs, sems):
        each(lambda cp: cp.wait_recv(), src_refs, dst_refs, sems, False)
        each(lambda cp: cp.wait_send(), src_refs, dst_refs, sems, True)

    return _Beside(sums, [((3,) + s.shape[1:], s.dtype, None) for s in sums], [3 * n] * 2, start, finish)


def _pack(parts, dtype, multiple):
    flat = jnp.concatenate([p.reshape(-1).astype(dtype) for p in parts])
    pad = (-flat.shape[0]) % multiple
    return jnp.pad(flat, (0, pad)) if pad else flat


def _unpack_shards(packs, names, shapes, axes):
    out, off = {}, 0
    for nme in names:
        sz = math.prod(shapes[nme])
        out[nme] = jnp.concatenate([packs[j, off:off + sz].reshape(shapes[nme]) for j in range(N_CHIPS)], axis=axes[nme])
        off += sz
    return out


def _shards_of(full, axis):
    sz = full.shape[axis] // N_CHIPS
    return [lax.slice_in_dim(full, j * sz, (j + 1) * sz, axis=axis) for j in range(N_CHIPS)]


def kernel(x, a_norm_w, a_in_proj, a_conv_w, a_conv_b, a_dt_bias, a_A_log, a_D, a_gnorm_w, a_out_proj, kv_norm_w, w_kvf, b_f, k_norm_w, b_norm_w, w_q, q_norm_w, w_o, ffn_norm_w, w_gate_up, w_down, loss_target, m_a_norm_w, m_a_in_proj, m_a_conv_w, m_a_conv_b, m_a_dt_bias, m_a_A_log, m_a_D, m_a_gnorm_w, m_a_out_proj, m_kv_norm_w, m_w_kvf, m_b_f, m_k_norm_w, m_b_norm_w, m_w_q, m_q_norm_w, m_w_o, m_ffn_norm_w, m_w_gate_up, m_w_down, v_a_norm_w, v_a_in_proj, v_a_conv_w, v_a_conv_b, v_a_dt_bias, v_a_A_log, v_a_D, v_a_gnorm_w, v_a_out_proj, v_kv_norm_w, v_w_kvf, v_b_f, v_k_norm_w, v_b_norm_w, v_w_q, v_q_norm_w, v_w_o, v_ffn_norm_w, v_w_gate_up, v_w_down):
    args = locals()
    w = {n: args[n] for n in WEIGHTS}
    mom = {n: args['m_' + n] for n in WEIGHTS}
    var = {n: args['v_' + n] for n in WEIGHTS}
    shapes = {n: w[n].shape for n in WEIGHTS}

    cx, cy, cc = lax.axis_index("x"), lax.axis_index("y"), lax.axis_index("c")
    my_chip = 2 * cx + cy

    xs_in = x[0]
    target = loss_target[0]
    s_len, d_model = xs_in.shape
    n_heads_ssm = a_dt_bias.shape[-1]
    d_inner = n_heads_ssm * SSM_HEAD_DIM
    d_xbc = a_conv_w.shape[-1] * N_CHIPS
    d_state = (d_xbc - d_inner) // (2 * SSM_GROUPS)
    kh = n_heads_ssm // SSM_GROUPS
    n_att = b_f.shape[0]
    d_att = n_att * ATT_HEAD_DIM
    d_ff = w_down.shape[1] * N_CHIPS

    small_names = list(SMALL_SHARDED)
    sp = _pack([w[n] for n in small_names], F32, PACK_COLS).reshape(-1, PACK_COLS)
    sp_all = _exchange("gather_small", [sp], [((N_CHIPS,) + sp.shape, F32)], FLIP_XY,
                       lambda i, r, me, peer: r, lambda i, r, sender, k: r.at[_chip(sender)],
                       lambda i, s, d, me: (s, d.at[_chip(me)]))[0]
    full = _unpack_shards(sp_all.reshape(N_CHIPS, -1), small_names, shapes, SMALL_SHARDED)
    for n in SMALL:
        if n not in full:
            full[n] = w[n]

    big2d = {'in': a_in_proj[0], 'out': a_out_proj[0], 'gu0': w_gate_up[0], 'dn0': w_down[0], 'kvf': w_kvf,
             'q': w_q[0], 'o': w_o[0], 'gu1': w_gate_up[1], 'dn1': w_down[1]}
    big_keys = list(big2d)
    norm_a, gnorm = full['a_norm_w'], full['a_gnorm_w']
    gathered = {}

    in_buf = _cast_into_slot(big2d['in'], my_chip, "cast_in")
    later = [k for k in big_keys if k != 'in']
    moved = [math.prod(big2d[k].shape) for k in later] + [s_len * d_model]
    units = big2d['in'].shape[0] // 32
    share = [m * units / sum(moved) for m in moved]
    piece = [int(sh) for sh in share]
    for i in sorted(range(len(share)), key=lambda i: share[i] - piece[i], reverse=True)[:units - sum(piece)]:
        piece[i] += 1
    slot, row0 = {}, 0
    for k, rows_k in zip(later + [None], [32 * p for p in piece]):
        plan = _gather_plan([in_buf], [(row0, rows_k)]) if rows_k else None
        row0 += rows_k
        if k is None:
            n1 = _rms_fwd(xs_in, norm_a, "norm_a", beside=plan)
        else:
            slot[k] = _cast_into_slot(big2d[k], my_chip, "cast_" + k, beside=plan)
        if plan is not None and k is None:
            n1, (in_buf,) = n1
        elif plan is not None:
            slot[k], (in_buf,) = slot[k]
    gathered['in'] = in_buf

    def gather(keys):
        return _gather_plan([slot[k] for k in keys])

    def pad_cols(a, width=LANES):
        return jnp.pad(a, ((0, 0), (0, width - a.shape[1])))

    w_in = jnp.concatenate([gathered['in'][j] for j in range(N_CHIPS)], axis=1)
    w_z, w_xbc, w_dt = w_in[:, :d_inner], w_in[:, d_inner:d_inner + d_xbc], pad_cols(w_in[:, d_inner + d_xbc:])
    conv_w, conv_b = full['a_conv_w'][0], full['a_conv_b']
    dt_bias, a_log = pad_cols(a_dt_bias), pad_cols(a_A_log)
    d_exp = jnp.repeat(a_D, SSM_HEAD_DIM, axis=1)
    kv_nw, b_nw = kv_norm_w.reshape(1, -1), b_norm_w
    k_nw, q_nw = k_norm_w.reshape(1, -1), q_norm_w
    bf_pad = pad_cols(b_f.reshape(1, -1))
    ffn_nw = [ffn_norm_w[i:i + 1] for i in range(2)]

    def to_groups(a):
        g = a[:, :n_heads_ssm].reshape(s_len, SSM_GROUPS, kh).transpose(1, 0, 2)
        return jnp.pad(g, ((0, 0), (0, 0), (0, LANES - kh)))

    def to_groups_t(a):
        g = a[:, :n_heads_ssm].reshape(s_len, SSM_GROUPS, kh).transpose(1, 2, 0)
        return jnp.pad(g, ((0, 0), (0, 8 - kh), (0, 0)))

    def from_groups(col, row=None):
        a = col[:, :, :kh].transpose(1, 0, 2).reshape(s_len, n_heads_ssm)
        if row is not None:
            a = a + row[:, :kh, :].transpose(2, 0, 1).reshape(s_len, n_heads_ssm)
        return pad_cols(a)

    z = _mm(n1, w_z, 'nn', "in_z")
    xbc_raw, (gathered['out'],) = _mm(n1, w_xbc, 'nn', "in_xbc", beside=gather(['out']))
    dtraw = _mm(n1, w_dt, 'nn', "in_dt")
    xbc = _conv_fwd(xbc_raw, conv_w, conv_b, "conv")
    dt, acum = _dt_fwd(dtraw, dt_bias, a_log, "dt")
    dt_g, ac_g, act_g = to_groups(dt), to_groups(acum), to_groups_t(acum)
    (y_ssd, states), (gathered['gu0'],) = _ssd_fwd(xbc, dt_g, ac_g, act_g, d_inner, d_state, kh, "ssd",
                                                   beside=gather(['gu0']))
    yn, (gathered['q'],) = _gate_fwd(y_ssd, xbc, z, d_exp, gnorm, d_inner, "gate", beside=gather(['q']))
    w_out = gathered['out'].reshape(-1, d_model)
    h1, (gathered['o'],) = _mm(yn, w_out, 'nn', "out_proj", add=xs_in, beside=gather(['o']))

    w_gu, w_dn = {}, {}

    def ffn_fwd(h, i, also):
        nrm = _rms_fwd(h, ffn_nw[i], f"ffn{i}_norm")
        w_gu[i] = gathered[f'gu{i}']
        gu = _mm(nrm, w_gu[i], 'nn', f"ffn{i}_up", out_dtype=BF16, shards=True,
                 beside=gather(also) if also else None)
        if also:
            gu, arrived = gu
            gathered.update(zip(also, arrived))
        act = _swiglu_fwd(gu, d_ff, f"ffn{i}_act")
        w_dn[i] = gathered[f'dn{i}'].reshape(-1, d_model)
        return nrm, gu, act, _mm(act, w_dn[i], 'nn', f"ffn{i}_down", add=h)

    n2, gu0, act0, h2 = ffn_fwd(h1, 0, ['dn0', 'kvf'])
    w_kvf_full = jnp.concatenate([gathered['kvf'][j] for j in range(N_CHIPS)], axis=1)
    w_k, w_v, w_f = w_kvf_full[:, :d_att], w_kvf_full[:, d_att:2 * d_att], pad_cols(w_kvf_full[:, 2 * d_att:])
    w_qm, w_om = gathered['q'].reshape(-1, d_att), gathered['o'].reshape(-1, d_model)
    nkv = _rms_fwd(h2, kv_nw, "kv_norm")
    k_raw = _mm(nkv, w_k, 'nn', "proj_k")
    v_att = _mm(nkv, w_v, 'nn', "proj_v", out_dtype=BF16)
    f_raw = _mm(nkv, w_f, 'nn', "proj_f")
    k_att = _rms_fwd(k_raw, k_nw, "k_norm")
    cum = _forget_fwd(f_raw, bf_pad, "forget")
    cq_rep = jnp.repeat(cum[:, :n_att], ATT_HEAD_DIM, axis=1)
    tb = min(ATT_BLOCK, s_len)
    ck = cum[:, :n_att].T.reshape(n_att, s_len // tb, 1, tb)
    n3 = _rms_fwd(h2, b_nw, "b_norm")
    q_raw = _mm(n3, w_qm, 'nn', "proj_q")
    q_att = _rms_fwd(q_raw, q_nw, "q_norm")
    (o_att, lse_rep), arrived = _attn_fwd(q_att, k_att, v_att, cq_rep, ck, "attn", beside=gather(['gu1', 'dn1']))
    gathered.update(zip(['gu1', 'dn1'], arrived))
    h3 = _mm(o_att, w_om, 'nn', "proj_o", add=h2)
    n4, gu1, act1, h4 = ffn_fwd(h3, 1, [])

    def loss_fn(h, t):
        err = h - t
        sq = jnp.sum(jnp.sum(err * err, axis=1, keepdims=True), axis=0, keepdims=True)
        return err * (1.0 / d_model), err * (1.0 / d_model), sq * (0.5 / d_model)

    dh4, dh4_b, loss_part = _rowwise(loss_fn, "loss", [h4, target], outs=[(d_model, F32), (d_model, BF16)],
                                     accs=[(1, 1)])

    n_f = n_att
    g_big, chip_sums, parts = {}, {}, {}

    def col_shards(pieces):
        fullw = jnp.concatenate(pieces, axis=1)
        return fullw.reshape(fullw.shape[0], N_CHIPS, -1).transpose(1, 0, 2)

    def row_shards(a):
        return a.reshape(N_CHIPS, -1, a.shape[1])

    def scatter(keys, tag):
        g_list = [g_big[k] for k in keys]
        hrs = [g.shape[1] // 2 for g in g_list]
        got = _exchange("reduce_d2d_" + tag, g_list,
                        [((N_CHIPS, hr, g.shape[2]), BF16) for g, hr in zip(g_list, hrs)], FLIP_C,
                        lambda i, r, me, peer: r.at[:, pl.ds((1 - me[2]) * hrs[i], hrs[i]), :],
                        lambda i, r, sender, k: r)
        for k, g, a in zip(keys, g_list, got):
            chip_sums[k] = _sum_half(g, a, cc, "reduce_sum2_" + k)
        return _scatter_plan([chip_sums[k] for k in keys])

    def ffn_bwd(dh, dh_b, h, nrm, gu, act, i, hosted):
        dact = _mm(dh_b, w_dn[i], 'nt', f"ffn{i}_down_dx", beside=scatter(hosted[:1], f"a{i}") if hosted else None)
        g_dn = _mm(act, dh_b, 'tn', f"ffn{i}_down_dw", out_dtype=BF16,
                   beside=scatter(hosted[1:], f"b{i}") if hosted else None)
        if hosted:
            (dact, first), (g_dn, second) = dact, g_dn
            parts.update(zip(hosted, first + second))
        dgu = _swiglu_bwd(gu, dact, d_ff, f"ffn{i}_act_bwd")
        dn = _mm(dgu, w_gu[i], 'nt', f"ffn{i}_up_dx", shards=True)
        g_gu = _mm(nrm, dgu, 'tn', f"ffn{i}_up_dw", out_dtype=BF16, shards=True)
        dh_new, dh_new_b, g_nw = _rms_bwd(h, ffn_nw[i], dn, f"ffn{i}_norm_bwd", extra=dh, copy_bf16=True)
        return dh_new, dh_new_b, g_dn, g_gu, g_nw

    dh3, dh3_b, g_dn1, g_gu1, g_fnw1 = ffn_bwd(dh4, dh4_b, h3, n4, gu1, act1, 1, [])
    do_att = _mm(dh3_b, w_om, 'nt', "proj_o_dx", out_dtype=BF16)
    g_wo = _mm(o_att, dh3_b, 'tn', "proj_o_dw", out_dtype=BF16)
    g_big.update(gu1=g_gu1, dn1=row_shards(g_dn1), o=row_shards(g_wo))
    late_keys = ['gu1', 'dn1', 'o']
    plan_late = scatter(late_keys, "late")

    def delta_fn(a, b):
        prod = a.astype(F32) * b.astype(F32)
        return jnp.concatenate([jnp.broadcast_to(jnp.sum(prod[:, g], axis=1, keepdims=True), (a.shape[0], ATT_HEAD_DIM))
                                for g in _lane_groups(a.shape[1], ATT_HEAD_DIM)], axis=1)

    delta_rep = _rowwise(delta_fn, "attn_delta", [do_att, o_att], outs=[(d_att, F32)])[0]
    (dq_att, dk_att, dv_att, dcq_rep, dck), arrived = _attn_bwd(q_att, k_att, v_att, do_att, lse_rep, delta_rep,
                                                                cq_rep, ck, "attn_bwd", beside=plan_late)
    parts.update(zip(late_keys, arrived))
    dq_raw, g_qnw = _rms_bwd(q_raw, q_nw, dq_att, "q_norm_bwd", out_dtype=BF16)
    dn3 = _mm(dq_raw, w_qm, 'nt', "proj_q_dx")
    g_wq = _mm(n3, dq_raw, 'tn', "proj_q_dw", out_dtype=BF16)
    dh2, g_bnw = _rms_bwd(h2, b_nw, dn3, "b_norm_bwd", extra=dh3)
    dk_raw, g_knw = _rms_bwd(k_raw, k_nw, dk_att, "k_norm_bwd", out_dtype=BF16)
    dcum = pad_cols(dcq_rep.reshape(s_len, n_att, ATT_HEAD_DIM)[:, :, 0] + dck.reshape(n_att, s_len).T)
    df_raw, g_bf = _forget_bwd(f_raw, bf_pad, dcum, "forget_bwd")
    dnkv = _mm(dk_raw, w_k, 'nt', "proj_k_dx")
    dnkv = _mm(dv_att, w_v, 'nt', "proj_v_dx", add=dnkv)
    dnkv = _mm(df_raw, w_f, 'nt', "proj_f_dx", add=dnkv)
    g_wk = _mm(nkv, dk_raw, 'tn', "proj_k_dw", out_dtype=BF16)
    g_wv = _mm(nkv, dv_att, 'tn', "proj_v_dw", out_dtype=BF16)
    g_wf = _mm(nkv, df_raw, 'tn', "proj_f_dw", out_dtype=BF16)
    dh2, dh2_b, g_kvnw = _rms_bwd(h2, kv_nw, dnkv, "kv_norm_bwd", extra=dh2, copy_bf16=True)
    g_big.update(q=row_shards(g_wq), kvf=col_shards([g_wk, g_wv, g_wf[:, :n_f]]))
    dh1, dh1_b, g_dn0, g_gu0, g_fnw0 = ffn_bwd(dh2, dh2_b, h1, n2, gu0, act0, 0, ['q', 'kvf'])
    g_big.update(gu0=g_gu0, dn0=row_shards(g_dn0))
    plan_ffn0 = scatter(['gu0', 'dn0'], "ffn0")

    dyn = _mm(dh1_b, w_out, 'nt', "out_proj_dx")
    g_wout = _mm(yn, dh1_b, 'tn', "out_proj_dw", out_dtype=BF16)
    dy_ssd, dxs_skip, dz, g_dexp, g_gnorm = _gate_bwd(y_ssd, xbc, z, d_exp, gnorm, dyn, d_inner, "gate_bwd")
    (dxs, d_b, d_c, ddt_g, dacc_g, dacr_g), arrived = _ssd_bwd(xbc, dt_g, ac_g, act_g, states, dy_ssd, dxs_skip,
                                                               d_inner, d_state, kh, "ssd_bwd", beside=plan_ffn0)
    parts.update(zip(['gu0', 'dn0'], arrived))
    g_big.update(out=row_shards(g_wout))
    dxbc_act = jnp.concatenate([dxs, d_b, d_c], axis=1)
    (du, g_convw, g_convb), (parts['out'],) = _conv_bwd(xbc_raw, conv_w, conv_b, dxbc_act, "conv_bwd",
                                                        beside=scatter(['out'], "out"))
    draw, g_dtb, g_alog = _dt_bwd(dtraw, dt_bias, a_log, from_groups(ddt_g), from_groups(dacc_g, dacr_g), "dt_bwd")
    g_wz = _mm(n1, dz, 'tn', "in_z_dw", out_dtype=BF16)
    g_wxbc = _mm(n1, du, 'tn', "in_xbc_dw", out_dtype=BF16)
    g_wdt = _mm(n1, draw, 'tn', "in_dt_dw", out_dtype=BF16)
    g_in = jnp.concatenate([g_wz, g_wxbc, g_wdt[:, :n_heads_ssm]], axis=1)
    g_in = g_in.reshape(2, d_model // 2, N_CHIPS, -1).transpose(0, 2, 1, 3)
    g_big.update(in0=g_in[0], in1=g_in[1])
    dn1, (parts['in0'],) = _mm(du, w_xbc, 'nt', "in_xbc_dx", beside=scatter(['in0'], "in0"))
    dn1, (parts['in1'],) = _mm(dz, w_z, 'nt', "in_z_dx", add=dn1, beside=scatter(['in1'], "in1"))
    dn1 = _mm(draw, w_dt, 'nt', "in_dt_dx", add=dn1)
    dx, g_norm_a = _rms_bwd(xs_in, norm_a, dn1, "norm_a_bwd", extra=dh1)

    g_small = {
        'a_norm_w': g_norm_a, 'a_conv_w': g_convw[None], 'a_conv_b': g_convb,
        'a_dt_bias': g_dtb[:, :n_heads_ssm], 'a_A_log': g_alog[:, :n_heads_ssm],
        'a_D': g_dexp.reshape(n_heads_ssm, SSM_HEAD_DIM).sum(axis=1).reshape(1, -1), 'a_gnorm_w': g_gnorm,
        'kv_norm_w': g_kvnw.reshape(-1), 'b_f': g_bf[0, :n_f], 'k_norm_w': g_knw.reshape(-1), 'b_norm_w': g_bnw,
        'q_norm_w': g_qnw, 'ffn_norm_w': jnp.concatenate([g_fnw0, g_fnw1], axis=0),
    }

    sg = _pack([g_small[n] for n in SMALL] + [loss_part], F32, 8 * PACK_COLS).reshape(-1, PACK_COLS)
    sg_all = _exchange("reduce_small", [sg], [((2 * N_CHIPS,) + sg.shape, F32)], FLIP_ALL,
                       lambda i, r, me, peer: r, lambda i, r, sender, k: r.at[_device(sender)],
                       lambda i, s, d, me: (s, d.at[_device(me)]))[0]
    sg_sum = _sum_arrays([sg_all[d] for d in range(2 * N_CHIPS)], F32, "reduce_small_sum").reshape(-1)
    red_small, off = {}, 0
    for n in SMALL:
        shp = g_small[n].shape
        red_small[n] = sg_sum[off:off + math.prod(shp)].reshape(shp)
        off += math.prod(shp)
    loss = sg_sum[off]

    red_keys = list(parts)
    half_sums = [_sum_parts(chip_sums[k], parts[k], my_chip, "reduce_sum4_" + k) for k in red_keys]
    others = _exchange("reduce_back", half_sums, [(h.shape, F32) for h in half_sums], FLIP_C,
                       lambda i, r, me, peer: r, lambda i, r, sender, k: r)
    mine_of, theirs_of = dict(zip(red_keys, half_sums)), dict(zip(red_keys, others))

    grads, delta, new_m, new_v = {}, {}, {}, {}
    layers_of = {'a_in_proj': ['in0', 'in1'], 'a_out_proj': ['out'], 'w_kvf': ['kvf'], 'w_q': ['q'], 'w_o': ['o'],
                 'w_gate_up': ['gu0', 'gu1'], 'w_down': ['dn0', 'dn1']}
    for n, keys in layers_of.items():
        three_d = (len(keys), math.prod(shapes[n][:-1]) // len(keys), shapes[n][-1])
        res = None
        for layer, k in enumerate(keys):
            res = _adamw_big(w[n].reshape(three_d), mom[n].reshape(three_d), var[n].reshape(three_d), mine_of[k],
                             theirs_of[k], cc, layer, res, "adamw_" + k)
        grads[n], delta[n], new_m[n], new_v[n] = [r.reshape(shapes[n]) for r in res]
    for n in SMALL:
        if n in SMALL_SHARDED:
            ax = SMALL_SHARDED[n]
            grads[n] = lax.dynamic_slice_in_dim(red_small[n], my_chip * shapes[n][ax], shapes[n][ax], axis=ax)
        else:
            grads[n] = red_small[n]

    packed = [_pack([src[n] for n in SMALL], F32, 8 * LANES).reshape(-1, LANES) for src in (w, grads, mom, var)]
    small_out = _adamw(*packed, "adamw_small")
    for store, flat in zip((delta, new_m, new_v), small_out):
        flat, off = flat.reshape(-1), 0
        for n in SMALL:
            sz = math.prod(shapes[n])
            store[n] = flat[off:off + sz].reshape(shapes[n])
            off += sz

    return (loss, dx[None], *[grads[n] for n in WEIGHTS], *[delta[n] for n in WEIGHTS],
            *[new_m[n] for n in WEIGHTS], *[new_v[n] for n in WEIGHTS])
```

```python
import functools
import math

import jax
import jax.numpy as jnp
from jax import lax
from jax.experimental import pallas as pl
from jax.experimental.pallas import tpu as pltpu

F32, BF16 = jnp.float32, jnp.bfloat16
EPS = 1e-6
SSM_HEAD_DIM = 64
SSM_GROUPS = 8
SSD_CHUNK = 128
ATT_HEAD_DIM = 128
LANES = 128
N_CHIPS = 4
NEG = -1e30
ADAM_LR, ADAM_B1, ADAM_B2, ADAM_EPS, ADAM_WD, ADAM_STEP = 0.001, 0.9, 0.999, 1e-08, 0.01, 10
VMEM_LIMIT_BYTES = 56 * 1024 * 1024
PACK_COLS = 1024
PACK_ROWS = 256
MM_K_TILES = (2816, 2048, 1408, 1024, 512, 256, 128)
MM_OPERAND_BYTES = 12 * 1024 * 1024

NN = ((1,), (0,))
NT = ((1,), (1,))
TN = ((0,), (0,))

WEIGHTS = ['a_norm_w', 'a_in_proj', 'a_conv_w', 'a_conv_b', 'a_dt_bias', 'a_A_log', 'a_D', 'a_gnorm_w', 'a_out_proj',
           'kv_norm_w', 'w_kvf', 'b_f', 'k_norm_w', 'b_norm_w', 'w_q', 'q_norm_w', 'w_o', 'ffn_norm_w', 'w_gate_up',
           'w_down']
BIG = {'a_in_proj': 2, 'a_out_proj': 1, 'w_kvf': 1, 'w_q': 1, 'w_o': 1, 'w_gate_up': 2, 'w_down': 1}
SMALL_SHARDED = {'a_norm_w': 1, 'a_conv_w': 2, 'a_conv_b': 1, 'a_gnorm_w': 1}
SMALL = [n for n in WEIGHTS if n not in BIG]


def _dot(a, b, dims):
    return lax.dot_general(a, b, (dims, ((), ())), preferred_element_type=F32)


def _params(sem=None):
    return pltpu.CompilerParams(dimension_semantics=sem, vmem_limit_bytes=VMEM_LIMIT_BYTES)


def _tile(dim, cap):
    for t in (1408, 1024, 512, 256, 128):
        if t <= cap and dim % t == 0:
            return t
    return dim


class _Beside:
    def __init__(self, operands, results, sem_sizes, start, finish):
        self.operands, self.results, self.sem_sizes = list(operands), list(results), list(sem_sizes)
        self.start, self.finish = start, finish


def _host(kern, beside, *, name, grid, in_specs, out_specs, out_shape, scratch_shapes, semantics, args, prefetch=()):
    single = not isinstance(out_shape, (list, tuple))
    out_specs = [out_specs] if single else list(out_specs)
    out_shape = [out_shape] if single else list(out_shape)
    n_pre = len(prefetch)

    def call(body, in_specs, out_specs, out_shape, scratch_shapes, semantics, aliases, args):
        if not n_pre:
            return pl.pallas_call(body, name=name, grid=grid, in_specs=list(in_specs), out_specs=list(out_specs),
                                  out_shape=out_shape, scratch_shapes=list(scratch_shapes),
                                  input_output_aliases=aliases, compiler_params=_params(semantics))(*args)
        spec = pltpu.PrefetchScalarGridSpec(num_scalar_prefetch=n_pre, grid=grid, in_specs=list(in_specs),
                                            out_specs=list(out_specs), scratch_shapes=list(scratch_shapes))
        return pl.pallas_call(body, name=name, grid_spec=spec, out_shape=out_shape,
                              input_output_aliases={n_pre + i: o for i, o in aliases.items()},
                              compiler_params=_params(semantics))(*prefetch, *args)

    if beside is None:
        res = call(kern, in_specs, out_specs, out_shape, scratch_shapes, semantics, {}, args)
        return (res[0] if single else res), []
    n_in, n_out, n_scr = len(in_specs), len(out_specs), len(scratch_shapes)
    nb_in, nb_out = len(beside.operands), len(beside.results)

    def body(*refs):
        pre, refs = refs[:n_pre], refs[n_pre:]
        ins, b_ins = refs[:n_in], refs[n_in:n_in + nb_in]
        outs = refs[n_in + nb_in:n_in + nb_in + n_out]
        b_outs = refs[n_in + nb_in + n_out:n_in + nb_in + n_out + nb_out]
        rest = refs[n_in + nb_in + n_out + nb_out:]
        scr, sems = rest[:n_scr], rest[n_scr:]
        ids = [pl.program_id(a) for a in range(len(grid))]
        first = functools.reduce(jnp.logical_and, [i == 0 for i in ids])
        last = functools.reduce(jnp.logical_and, [i == g - 1 for i, g in zip(ids, grid)])

        @pl.when(first)
        def _():
            beside.start(b_ins, b_outs, sems)

        kern(*pre, *ins, *outs, *scr)

        @pl.when(last)
        def _():
            beside.finish(b_ins, b_outs, sems)

    any_spec = pl.BlockSpec(memory_space=pl.ANY)
    res = call(body, list(in_specs) + [any_spec] * nb_in, out_specs + [any_spec] * nb_out,
               out_shape + [jax.ShapeDtypeStruct(s, d) for s, d, _ in beside.results],
               list(scratch_shapes) + [pltpu.SemaphoreType.DMA((k,)) for k in beside.sem_sizes],
               ("arbitrary",) * len(grid),
               {n_in + op: n_out + r for r, (_, _, op) in enumerate(beside.results) if op is not None},
               list(args) + list(beside.operands))
    mine = res[:n_out]
    return (mine[0] if single else mine), list(res[n_out:])


def _alone(beside, name):
    return _host(lambda: None, beside, name=name, grid=(1,), in_specs=[], out_specs=[], out_shape=[],
                 scratch_shapes=[], semantics=("arbitrary",), args=[])[1]


def _mm(a, b, mode, name, out_dtype=F32, add=None, shards=False, beside=None):
    if mode == 'nn':
        (m, k), n = a.shape, (b.shape[2] * N_CHIPS if shards else b.shape[1])
    elif mode == 'nt':
        (m, k), n = a.shape, (b.shape[1] if shards else b.shape[0])
    else:
        (k, m), n = a.shape, b.shape[1]
    per_chip = (k if mode == 'nt' else n) // N_CHIPS
    tm = _tile(m, 1024)
    tn = _tile(per_chip if shards and mode != 'nt' else n, 1408 if shards else 1024)
    k_dim = per_chip if shards and mode == 'nt' else k
    a_bytes, b_bytes = jnp.dtype(a.dtype).itemsize, jnp.dtype(b.dtype).itemsize
    tk = next((t for t in MM_K_TILES if k_dim % t == 0 and t * (tm * a_bytes + tn * b_bytes) <= MM_OPERAND_BYTES), k_dim)
    nk = k // tk
    in_place = nk > 1 and out_dtype == F32
    a_spec = pl.BlockSpec((tk, tm), lambda i, j, q: (q, i)) if mode == 'tn' else pl.BlockSpec((tm, tk), lambda i, j, q: (i, q))
    b_spec = pl.BlockSpec((tn, tk), lambda i, j, q: (j, q)) if mode == 'nt' else pl.BlockSpec((tk, tn), lambda i, j, q: (q, j))
    o_spec = pl.BlockSpec((tm, tn), lambda i, j, q: (i, j))
    out_struct = jax.ShapeDtypeStruct((m, n), out_dtype)
    if shards:
        per = per_chip // (tk if mode == 'nt' else tn)
        if mode == 'nn':
            b_spec = pl.BlockSpec((None, tk, tn), lambda i, j, q: (j // per, q, j % per))
        elif mode == 'nt':
            b_spec = pl.BlockSpec((None, tn, tk), lambda i, j, q: (q // per, j, q % per))
        else:
            out_struct = jax.ShapeDtypeStruct((N_CHIPS, m, per_chip), out_dtype)
    out_spec = pl.BlockSpec((None, tm, tn), lambda i, j, q: (j // per, i, j % per)) if shards and mode == 'tn' else o_spec
    dims = {'nn': NN, 'nt': NT, 'tn': TN}[mode]

    n_ins = 3 if add is not None else 2

    def kern(*refs):
        a_ref, b_ref, o_ref = refs[0], refs[1], refs[n_ins]
        acc = o_ref if in_place or nk == 1 else refs[n_ins + 1]
        q = pl.program_id(2)
        part = _dot(a_ref[...].astype(BF16), b_ref[...].astype(BF16), dims)

        def first():
            return part if add is None else part + refs[2][...]

        if nk == 1:
            o_ref[...] = first().astype(o_ref.dtype)
            return

        @pl.when(q == 0)
        def _():
            acc[...] = first()

        @pl.when(q > 0)
        def _():
            acc[...] += part

        if not in_place:
            @pl.when(q == nk - 1)
            def _():
                o_ref[...] = acc[...].astype(o_ref.dtype)

    ins, specs = [a, b], [a_spec, b_spec]
    if add is not None:
        ins.append(add)
        specs.append(o_spec)
    scratch = [] if in_place or nk == 1 else [pltpu.VMEM((tm, tn), F32)]
    res, extra = _host(kern, beside, name=name, grid=(m // tm, n // tn, nk), in_specs=specs, out_specs=out_spec,
                       out_shape=out_struct, scratch_shapes=scratch,
                       semantics=("parallel", "parallel", "arbitrary"), args=ins)
    return res if beside is None else (res, extra)


def _rowwise(fn, name, rows, bcast=(), outs=(), accs=(), tm=256, beside=None):
    rows = [r if isinstance(r, tuple) else (r, r.shape[1], 0) for r in rows]
    n_rows = rows[0][0].shape[0]
    tm = min(tm, n_rows)
    assert n_rows % tm == 0, (name, n_rows, tm)
    n_in, n_out = len(rows) + len(bcast), len(outs)
    in_specs = [pl.BlockSpec((tm, w), functools.partial(lambda i, cb: (i, cb), cb=cb)) for _, w, cb in rows]
    in_specs += [pl.BlockSpec(b.shape, lambda i: (0, 0)) for b in bcast]
    out_specs = [pl.BlockSpec((tm, w), lambda i: (i, 0)) for w, _ in outs]
    out_specs += [pl.BlockSpec(s, lambda i: (0, 0)) for s in accs]
    out_shape = [jax.ShapeDtypeStruct((n_rows, w), d) for w, d in outs] + [jax.ShapeDtypeStruct(s, F32) for s in accs]

    def kern(*refs):
        vals = fn(*[r[...] for r in refs[:n_in]])
        vals = vals if isinstance(vals, (tuple, list)) else (vals,)
        o_refs = refs[n_in:]
        for r, v in zip(o_refs[:n_out], vals[:n_out]):
            r[...] = v.astype(r.dtype)
        if accs:
            @pl.when(pl.program_id(0) == 0)
            def _():
                for r in o_refs[n_out:]:
                    r[...] = jnp.zeros_like(r)

            for r, v in zip(o_refs[n_out:], vals[n_out:]):
                r[...] += v

    res, extra = _host(kern, beside, name=name, grid=(n_rows // tm,), in_specs=in_specs, out_specs=out_specs,
                       out_shape=out_shape, scratch_shapes=[], semantics=("arbitrary",),
                       args=[r[0] for r in rows] + list(bcast))
    return res if beside is None else (res, extra)


def _rms(x, w):
    xf = x.astype(F32)
    return xf * lax.rsqrt(jnp.mean(xf * xf, axis=-1, keepdims=True) + EPS) * w


def _lane_groups(width, group):
    return [slice(g * group, (g + 1) * group) for g in range(width // group)]


def _rms_fwd(x, w, name, tm=256, beside=None):
    def fn(x, w):
        return jnp.concatenate([_rms(x[:, g], w) for g in _lane_groups(x.shape[1], w.shape[1])], axis=1)

    res = _rowwise(fn, name, [x], [w], outs=[(x.shape[1], BF16)], tm=tm, beside=beside)
    return res[0] if beside is None else (res[0][0], res[1])


def _rms_bwd(x, w, dy, name, extra=None, out_dtype=F32, tm=256, copy_bf16=False):
    def fn(x, dy, *rest):
        w = rest[-1]
        dxs, dw = [], jnp.zeros(w.shape, F32)
        for g in _lane_groups(x.shape[1], w.shape[1]):
            _, vjp = jax.vjp(_rms, x[:, g], w)
            dx_g, dw_g = vjp(dy[:, g].astype(F32))
            dxs.append(dx_g)
            dw = dw + dw_g
        dx = jnp.concatenate(dxs, axis=1)
        if extra is not None:
            dx = dx + rest[0]
        return (dx, dx, dw) if copy_bf16 else (dx, dw)

    rows = [x, dy] + ([extra] if extra is not None else [])
    outs = [(x.shape[1], out_dtype)] + ([(x.shape[1], BF16)] if copy_bf16 else [])
    return _rowwise(fn, name, rows, [w], outs=outs, accs=[w.shape], tm=tm)


def _sigmoid(x):
    return 1.0 / (1.0 + jnp.exp(-x))


def _softplus(x):
    return jnp.maximum(x, 0.0) + jnp.log(1.0 + jnp.exp(-jnp.abs(x)))


def _swiglu_fwd(gu, d_ff, name):
    def fn(g, u):
        g, u = g.astype(F32), u.astype(F32)
        return g * _sigmoid(g) * u

    return _rowwise(fn, name, [(gu, d_ff, 0), (gu, d_ff, 1)], outs=[(d_ff, BF16)], tm=128)[0]


def _swiglu_bwd(gu, dact, d_ff, name):
    def fn(g, u, da):
        g, u = g.astype(F32), u.astype(F32)
        s = _sigmoid(g)
        dg = da * u * s * (1.0 + g * (1.0 - s))
        du = da * g * s
        return jnp.concatenate([dg, du], axis=1)

    return _rowwise(fn, name, [(gu, d_ff, 0), (gu, d_ff, 1), dact], outs=[(2 * d_ff, BF16)], tm=128)[0]


def _cumsum_rows(v, reverse=False):
    n = v.shape[0]
    row = lax.broadcasted_iota(jnp.int32, v.shape, 0)
    sh = 1
    while sh < n:
        if reverse:
            v = v + jnp.where(row < n - sh, pltpu.roll(v, n - sh, 0), 0.0)
        else:
            v = v + jnp.where(row >= sh, pltpu.roll(v, sh, 0), 0.0)
        sh *= 2
    return v


def _conv_fwd(u, w, b, name):
    s, c = u.shape
    kw = w.shape[0]
    tc = _tile(c, 128)

    def kern(u_ref, w_ref, b_ref, o_ref):
        uu = u_ref[...]
        row = lax.broadcasted_iota(jnp.int32, uu.shape, 0)
        acc = jnp.zeros_like(uu) + b_ref[...]
        for k in range(kw):
            sh = kw - 1 - k
            uk = uu if sh == 0 else jnp.where(row >= sh, pltpu.roll(uu, sh, 0), 0.0)
            acc = acc + w_ref[pl.ds(k, 1), :] * uk
        o_ref[...] = acc * _sigmoid(acc)

    return pl.pallas_call(
        kern, name=name, grid=(c // tc,),
        in_specs=[pl.BlockSpec((s, tc), lambda j: (0, j)), pl.BlockSpec((kw, tc), lambda j: (0, j)),
                  pl.BlockSpec((1, tc), lambda j: (0, j))],
        out_specs=pl.BlockSpec((s, tc), lambda j: (0, j)), out_shape=jax.ShapeDtypeStruct((s, c), F32),
        compiler_params=_params(("parallel",)))(u, w, b)


def _conv_bwd(u, w, b, dact, name, beside=None):
    s, c = u.shape
    kw = w.shape[0]
    tc = _tile(c, 128)

    def kern(u_ref, w_ref, b_ref, d_ref, du_ref, dw_ref, db_ref):
        uu = u_ref[...]
        row = lax.broadcasted_iota(jnp.int32, uu.shape, 0)
        shifted = []
        acc = jnp.zeros_like(uu) + b_ref[...]
        for k in range(kw):
            sh = kw - 1 - k
            uk = uu if sh == 0 else jnp.where(row >= sh, pltpu.roll(uu, sh, 0), 0.0)
            shifted.append(uk)
            acc = acc + w_ref[pl.ds(k, 1), :] * uk
        sg = _sigmoid(acc)
        dacc = d_ref[...] * sg * (1.0 + acc * (1.0 - sg))
        db_ref[...] = jnp.sum(dacc, axis=0, keepdims=True)
        du = jnp.zeros_like(uu)
        for k in range(kw):
            sh = kw - 1 - k
            dw_ref[pl.ds(k, 1), :] = jnp.sum(dacc * shifted[k], axis=0, keepdims=True)
            dk = dacc if sh == 0 else jnp.where(row < s - sh, pltpu.roll(dacc, s - sh, 0), 0.0)
            du = du + w_ref[pl.ds(k, 1), :] * dk
        du_ref[...] = du.astype(du_ref.dtype)

    col = lambda j: (0, j)
    return _host(
        kern, beside, name=name, grid=(c // tc,),
        in_specs=[pl.BlockSpec((s, tc), col), pl.BlockSpec((kw, tc), col), pl.BlockSpec((1, tc), col),
                  pl.BlockSpec((s, tc), col)],
        out_specs=[pl.BlockSpec((s, tc), col), pl.BlockSpec((kw, tc), col), pl.BlockSpec((1, tc), col)],
        out_shape=[jax.ShapeDtypeStruct((s, c), BF16), jax.ShapeDtypeStruct((kw, c), F32),
                   jax.ShapeDtypeStruct((1, c), F32)],
        scratch_shapes=[], semantics=("parallel",), args=[u, w, b, dact])


def _dt_fwd(dtraw, bias, a_log, name):
    def fn(raw, bias, a_log):
        dt = _softplus(raw + bias)
        return dt, _cumsum_rows(dt * (-jnp.exp(a_log)))

    return _rowwise(fn, name, [dtraw], [bias, a_log], outs=[(LANES, F32), (LANES, F32)], tm=SSD_CHUNK)


def _dt_bwd(dtraw, bias, a_log, ddt, dacum, name):
    def fn(raw, ddt, dac, bias, a_log):
        z = raw + bias
        dt = _softplus(z)
        a_neg = -jnp.exp(a_log)
        da = _cumsum_rows(dac, reverse=True)
        draw = (ddt + da * a_neg) * _sigmoid(z)
        return draw, jnp.sum(draw, axis=0, keepdims=True), jnp.sum(da * dt, axis=0, keepdims=True) * a_neg

    return _rowwise(fn, name, [dtraw, ddt, dacum], [bias, a_log], outs=[(LANES, BF16)],
                    accs=[(1, LANES), (1, LANES)], tm=SSD_CHUNK)


def _ssd_pieces(l, gw):
    lane_k = lax.broadcasted_iota(jnp.int32, (l, LANES), 1)
    lane_of = lax.broadcasted_iota(jnp.int32, (gw, LANES), 0)
    head_of = lax.broadcasted_iota(jnp.int32, (gw, LANES), 1)
    in_head = ((lane_of >= head_of * SSM_HEAD_DIM) & (lane_of < (head_of + 1) * SSM_HEAD_DIM)).astype(BF16)

    def col(blk, k):
        return jnp.sum(jnp.where(lane_k == k, blk, 0.0), axis=1, keepdims=True)

    def collapse(wide):
        hi = wide.astype(BF16)
        lo = (wide - hi.astype(F32)).astype(BF16)
        return _dot(hi, in_head, NN) + _dot(lo, in_head, NN)

    return lane_k, col, collapse


def _head_block(k):
    per_block = LANES // SSM_HEAD_DIM
    lane = lax.broadcasted_iota(jnp.int32, (1, LANES), 1)
    lo = (k % per_block) * SSM_HEAD_DIM
    return slice((k // per_block) * LANES, (k // per_block + 1) * LANES), (lane >= lo) & (lane < lo + SSM_HEAD_DIM)


def _by_block(pieces, n_blocks, like):
    zero = jnp.zeros((like.shape[0], LANES), F32)
    return jnp.concatenate([pieces.get(p, zero) for p in range(n_blocks)], axis=1)


def _ssd_specs(l, gw, n, n_xs_blocks):
    g_axis = SSM_GROUPS
    return dict(
        xs=lambda cm: pl.BlockSpec((l, gw), lambda g, c: (cm(c), g)),
        b=lambda cm: pl.BlockSpec((l, n), lambda g, c: (cm(c), n_xs_blocks + g)),
        c=lambda cm: pl.BlockSpec((l, n), lambda g, c: (cm(c), n_xs_blocks + g_axis + g)),
        col=lambda cm: pl.BlockSpec((None, l, LANES), lambda g, c: (g, cm(c), 0)),
        row=lambda cm: pl.BlockSpec((None, 8, l), lambda g, c: (g, 0, cm(c))),
        state=lambda cm: pl.BlockSpec((None, None, n, gw), lambda g, c: (cm(c), g, 0, 0)),
    )


def _ssd_fwd(xbc, dt_w, ac_w, ac_g, act_g, d_inner, n, kh, name, beside=None):
    s = xbc.shape[0]
    l, g_n = SSD_CHUNK, SSM_GROUPS
    gw, nc = d_inner // g_n, s // l
    sp = _ssd_specs(l, gw, n, d_inner // n)
    fwd = lambda c: c

    def kern(xs_ref, b_ref, c_ref, dt_ref, ace_ref, ac_ref, act_ref, y_ref, s0_ref, st):
        @pl.when(pl.program_id(1) == 0)
        def _():
            st[...] = jnp.zeros_like(st)

        s0 = st[...]
        s0_ref[...] = s0
        xs, ac, ace = xs_ref[...], ac_ref[...], ace_ref[...]
        _, col, _ = _ssd_pieces(l, gw)
        x = xs * dt_ref[...]
        xb, bb, cb_ = x.astype(BF16), b_ref[...].astype(BF16), c_ref[...].astype(BF16)
        cb = _dot(cb_, bb, NT)
        ri = lax.broadcasted_iota(jnp.int32, (l, l), 0)
        ci = lax.broadcasted_iota(jnp.int32, (l, l), 1)
        causal = ri >= ci
        y_diag = {}
        for k in range(kh):
            seg = col(ac, k) - act_ref[pl.ds(k, 1), :]
            m = jnp.where(causal, cb * jnp.exp(jnp.where(causal, seg, 0.0)), 0.0)
            blk, mine = _head_block(k)
            yk = _dot(m.astype(BF16), xb[:, blk], NN)
            y_diag[blk.start // LANES] = jnp.where(mine, yk, y_diag.get(blk.start // LANES, 0.0))
        y_ref[...] = _dot(cb_, s0.astype(BF16), NN) * jnp.exp(ace) + _by_block(y_diag, gw // LANES, xs)
        row_w = lax.broadcasted_iota(jnp.int32, (l, gw), 0)
        alast = jnp.sum(jnp.where(row_w == l - 1, ace, 0.0), axis=0, keepdims=True)
        st[...] = s0 * jnp.exp(alast) + _dot(bb, (jnp.exp(alast - ace) * x).astype(BF16), TN)

    return _host(
        kern, beside, name=name, grid=(g_n, nc),
        in_specs=[sp['xs'](fwd), sp['b'](fwd), sp['c'](fwd), sp['xs'](fwd), sp['xs'](fwd), sp['col'](fwd),
                  sp['row'](fwd)],
        out_specs=[pl.BlockSpec((l, gw), lambda g, c: (c, g)), sp['state'](fwd)],
        out_shape=[jax.ShapeDtypeStruct((s, d_inner), F32), jax.ShapeDtypeStruct((nc, g_n, n, gw), F32)],
        scratch_shapes=[pltpu.VMEM((n, gw), F32)], semantics=("arbitrary", "arbitrary"),
        args=[xbc, xbc, xbc, dt_w, ac_w, ac_g, act_g])


def _ssd_bwd(xbc, dt_w, ac_w, ac_g, act_g, s0_all, dy, dxs_skip, d_inner, n, kh, name, beside=None):
    s = xbc.shape[0]
    l, g_n = SSD_CHUNK, SSM_GROUPS
    gw, nc = d_inner // g_n, s // l
    sp = _ssd_specs(l, gw, n, d_inner // n)
    rev = lambda c: nc - 1 - c

    def kern(xs_ref, b_ref, c_ref, dt_ref, ace_ref, ac_ref, act_ref, s0_ref, dy_ref, skip_ref,
             dxs_ref, db_ref, dc_ref, ddt_ref, dacc_ref, dacr_ref, dst):
        @pl.when(pl.program_id(1) == 0)
        def _():
            dst[...] = jnp.zeros_like(dst)

        dsn = dst[...]
        s0 = s0_ref[...]
        xs, ac, dy = xs_ref[...], ac_ref[...], dy_ref[...]
        lane_k, col, collapse = _ssd_pieces(l, gw)
        ace, dte = ace_ref[...], dt_ref[...]
        x = xs * dte
        xb, bb, cb_ = x.astype(BF16), b_ref[...].astype(BF16), c_ref[...].astype(BF16)
        s0b, dsnb, dyb = s0.astype(BF16), dsn.astype(BF16), dy.astype(BF16)
        cb = _dot(cb_, bb, NT)
        ri = lax.broadcasted_iota(jnp.int32, (l, l), 0)
        ci = lax.broadcasted_iota(jnp.int32, (l, l), 1)
        causal = ri >= ci
        row_w = lax.broadcasted_iota(jnp.int32, (l, gw), 0)
        e = jnp.exp(ace)
        alast = jnp.sum(jnp.where(row_w == l - 1, ace, 0.0), axis=0, keepdims=True)
        gdec = jnp.exp(alast)
        wt = jnp.exp(alast - ace)
        cs = _dot(cb_, s0b, NN)
        dcs = (dy * e).astype(BF16)
        d_c = _dot(dcs, s0b, NT)
        ds_off = _dot(cb_, dcs, TN)
        dace = dy * cs * e
        dalast = jnp.sum(dsn * s0, axis=0, keepdims=True) * gdec
        z = wt * x
        dz = _dot(bb, dsnb, NN)
        d_b = _dot(z.astype(BF16), dsnb, NT)
        dx = dz * wt
        t = dz * z
        dalast = dalast + jnp.sum(t, axis=0, keepdims=True)
        dace = dace - t
        dcb = jnp.zeros((l, l), F32)
        dac_col = jnp.zeros((l, LANES), F32)
        dx_diag = {}
        for k in range(kh):
            seg = col(ac, k) - act_ref[pl.ds(k, 1), :]
            dk = jnp.exp(jnp.where(causal, seg, 0.0))
            mk = jnp.where(causal, cb * dk, 0.0)
            blk, mine = _head_block(k)
            dxk = _dot(mk.astype(BF16), dyb[:, blk], TN)
            dx_diag[blk.start // LANES] = jnp.where(mine, dxk, dx_diag.get(blk.start // LANES, 0.0))
            dmk = _dot(jnp.where(mine, dy[:, blk], 0.0).astype(BF16), xb[:, blk], NT)
            dcb = dcb + jnp.where(causal, dmk * dk, 0.0)
            dseg = dmk * mk
            dac_col = jnp.where(lane_k == k, jnp.sum(dseg, axis=1, keepdims=True), dac_col)
            dacr_ref[pl.ds(k, 1), :] = -jnp.sum(dseg, axis=0, keepdims=True)
        for k in range(kh, 8):
            dacr_ref[pl.ds(k, 1), :] = jnp.zeros((1, l), F32)
        dx = dx + _by_block(dx_diag, gw // LANES, xs)
        dcbb = dcb.astype(BF16)
        dc_ref[...] = d_c + _dot(dcbb, bb, NN)
        db_ref[...] = d_b + _dot(dcbb, cb_, TN)
        dace = jnp.where(row_w == l - 1, dace + dalast, dace)
        dacc_ref[...] = dac_col + collapse(dace)
        ddt_ref[...] = collapse(dx * xs)
        dxs_ref[...] = dx * dte + skip_ref[...]
        dst[...] = dsn * gdec + ds_off

    return _host(
        kern, beside, name=name, grid=(g_n, nc),
        in_specs=[sp['xs'](rev), sp['b'](rev), sp['c'](rev), sp['xs'](rev), sp['xs'](rev), sp['col'](rev),
                  sp['row'](rev), sp['state'](rev), pl.BlockSpec((l, gw), lambda g, c: (rev(c), g)),
                  pl.BlockSpec((l, gw), lambda g, c: (rev(c), g))],
        out_specs=[pl.BlockSpec((l, gw), lambda g, c: (rev(c), g)), pl.BlockSpec((l, n), lambda g, c: (rev(c), g)),
                   pl.BlockSpec((l, n), lambda g, c: (rev(c), g)), sp['col'](rev), sp['col'](rev), sp['row'](rev)],
        out_shape=[jax.ShapeDtypeStruct((s, d_inner), F32), jax.ShapeDtypeStruct((s, g_n * n), F32),
                   jax.ShapeDtypeStruct((s, g_n * n), F32), jax.ShapeDtypeStruct((g_n, s, LANES), F32),
                   jax.ShapeDtypeStruct((g_n, s, LANES), F32), jax.ShapeDtypeStruct((g_n, 8, s), F32)],
        scratch_shapes=[pltpu.VMEM((n, gw), F32)], semantics=("arbitrary", "arbitrary"),
        args=[xbc, xbc, xbc, dt_w, ac_w, ac_g, act_g, s0_all, dy, dxs_skip])


def _gate(y, xs, z, d_exp, gw):
    t = (y + xs * d_exp) * (z * _sigmoid(z))
    width = t.shape[1]
    gsz = width // SSM_GROUPS
    lane = lax.broadcasted_iota(jnp.int32, t.shape, 1)
    t2 = t * t
    scale = jnp.zeros_like(t)
    for g in range(SSM_GROUPS):
        in_g = (lane >= g * gsz) & (lane < (g + 1) * gsz)
        ms = jnp.sum(jnp.where(in_g, t2, 0.0), axis=1, keepdims=True) * (1.0 / gsz)
        scale = jnp.where(in_g, lax.rsqrt(ms + EPS), scale)
    return t * scale * gw


def _gate_fwd(y, xbc, z, d_exp, gw, d_inner, name, beside=None):
    res = _rowwise(_gate, name, [y, (xbc, d_inner, 0), z], [d_exp, gw], outs=[(d_inner, BF16)], tm=128, beside=beside)
    return res[0] if beside is None else (res[0][0], res[1])


def _gate_bwd(y, xbc, z, d_exp, gw, dyn, d_inner, name):
    def fn(y, xs, z, dyn, d_exp, gw):
        _, vjp = jax.vjp(_gate, y, xs, z, d_exp, gw)
        return vjp(dyn)

    return _rowwise(fn, name, [y, (xbc, d_inner, 0), z, dyn], [d_exp, gw],
                    outs=[(d_inner, F32), (d_inner, F32), (d_inner, BF16)], accs=[d_exp.shape, gw.shape], tm=64)


def _forget_fwd(fraw, b_f, name):
    def kern(f_ref, b_ref, o_ref):
        o_ref[...] = _cumsum_rows(-_softplus(-(f_ref[...] + b_ref[...])))

    return pl.pallas_call(kern, name=name, out_shape=jax.ShapeDtypeStruct(fraw.shape, F32),
                          compiler_params=_params())(fraw, b_f)


def _forget_bwd(fraw, b_f, dcum, name):
    def kern(f_ref, b_ref, d_ref, df_ref, db_ref):
        df = _cumsum_rows(d_ref[...], reverse=True) * _sigmoid(-(f_ref[...] + b_ref[...]))
        df_ref[...] = df.astype(df_ref.dtype)
        db_ref[...] = jnp.sum(df, axis=0, keepdims=True)

    return pl.pallas_call(
        kern, name=name,
        out_shape=[jax.ShapeDtypeStruct(fraw.shape, BF16), jax.ShapeDtypeStruct((1, fraw.shape[1]), F32)],
        compiler_params=_params())(fraw, b_f, dcum)


ATT_BLOCK = 512


def _attn_fwd(q, k, v, cq_rep, ck, name, beside=None):
    s, hd = q.shape
    h_n, d = hd // ATT_HEAD_DIM, ATT_HEAD_DIM
    tb = min(ATT_BLOCK, s)
    nb = s // tb
    scale = d ** -0.5

    def kern(q_ref, k_ref, v_ref, cq_ref, ck_ref, o_ref, lse_ref):
        i = pl.program_id(1)
        qq = q_ref[...]
        cq = jnp.max(cq_ref[...], axis=1, keepdims=True)
        rowpos = i * tb + lax.broadcasted_iota(jnp.int32, (tb, tb), 0)
        coli = lax.broadcasted_iota(jnp.int32, (tb, tb), 1)

        def step(j, carry, diagonal):
            m, l_, acc = carry
            ks = pl.ds(pl.multiple_of(j * tb, tb), tb)
            sc = _dot(qq, k_ref[ks, :], NT) * scale + (cq - ck_ref[j])
            if diagonal:
                sc = jnp.where(j * tb + coli <= rowpos, sc, NEG)
            mn = jnp.maximum(m, jnp.max(sc, axis=1, keepdims=True))
            p = jnp.exp(sc - mn)
            alpha = jnp.exp(m - mn)
            l_ = alpha * l_ + jnp.sum(p, axis=1, keepdims=True)
            acc = alpha * acc + _dot(p.astype(BF16), v_ref[ks, :], NN)
            return mn, l_, acc

        init = (jnp.full((tb, 1), NEG, F32), jnp.zeros((tb, 1), F32), jnp.zeros((tb, d), F32))
        below = lax.fori_loop(0, i, lambda j, carry: step(j, carry, False), init)
        m, l_, acc = step(i, below, True)
        o_ref[...] = (acc / l_).astype(o_ref.dtype)
        lse_ref[...] = jnp.broadcast_to(m + jnp.log(l_), (tb, d))

    return _host(
        kern, beside, name=name, grid=(h_n, nb),
        in_specs=[pl.BlockSpec((tb, d), lambda h, i: (i, h)), pl.BlockSpec((s, d), lambda h, i: (0, h)),
                  pl.BlockSpec((s, d), lambda h, i: (0, h)), pl.BlockSpec((tb, d), lambda h, i: (i, h)),
                  pl.BlockSpec((None, nb, 1, tb), lambda h, i: (h, 0, 0, 0))],
        out_specs=[pl.BlockSpec((tb, d), lambda h, i: (i, h)), pl.BlockSpec((tb, d), lambda h, i: (i, h))],
        out_shape=[jax.ShapeDtypeStruct((s, hd), BF16), jax.ShapeDtypeStruct((s, hd), F32)],
        scratch_shapes=[], semantics=("parallel", "arbitrary"), args=[q, k, v, cq_rep, ck])


def _attn_bwd(q, k, v, do, lse_rep, delta_rep, cq_rep, ck, name, beside=None):
    s, hd = q.shape
    h_n, d = hd // ATT_HEAD_DIM, ATT_HEAD_DIM
    tb = min(ATT_BLOCK, s)
    nb = s // tb
    scale = d ** -0.5

    def kern(q_ref, do_ref, k_ref, v_ref, lse_ref, dl_ref, cq_ref, ck_ref, dq_ref, dk_ref, dv_ref, dcq_ref, dck_ref):
        j = pl.program_id(1)

        @pl.when(j == 0)
        def _():
            dq_ref[...] = jnp.zeros_like(dq_ref)
            dcq_ref[...] = jnp.zeros_like(dcq_ref)

        kj, vj, ckj = k_ref[...], v_ref[...], ck_ref[...]
        colpos = j * tb + lax.broadcasted_iota(jnp.int32, (tb, tb), 1)
        rowi = lax.broadcasted_iota(jnp.int32, (tb, tb), 0)

        def step(i, carry, diagonal):
            dk, dv, dck = carry
            rs = pl.ds(pl.multiple_of(i * tb, tb), tb)
            qi, doi = q_ref[rs, :], do_ref[rs, :]
            lse = jnp.max(lse_ref[rs, :], axis=1, keepdims=True)
            dl = jnp.max(dl_ref[rs, :], axis=1, keepdims=True)
            cq = jnp.max(cq_ref[rs, :], axis=1, keepdims=True)
            sc = _dot(qi, kj, NT) * scale + ((cq - lse) - ckj)
            p = jnp.exp(jnp.where(colpos <= i * tb + rowi, sc, NEG) if diagonal else sc)
            dp = _dot(doi, vj, NT)
            ds = p * (dp - dl)
            dsb = ds.astype(BF16)
            dv = dv + _dot(p.astype(BF16), doi, TN)
            dk = dk + _dot(dsb, qi, TN)
            dq_ref[rs, :] += _dot(dsb, kj, NN) * scale
            dcq_ref[rs, :] += jnp.broadcast_to(jnp.sum(ds, axis=1, keepdims=True), (tb, d))
            return dk, dv, dck - jnp.sum(ds, axis=0, keepdims=True)

        init = (jnp.zeros((tb, d), F32), jnp.zeros((tb, d), F32), jnp.zeros((1, tb), F32))
        dk, dv, dck = lax.fori_loop(j + 1, nb, lambda i, carry: step(i, carry, False), step(j, init, True))
        dk_ref[...] = dk * scale
        dv_ref[...] = dv.astype(dv_ref.dtype)
        dck_ref[...] = dck

    whole = pl.BlockSpec((s, d), lambda h, j: (0, h))
    blk = pl.BlockSpec((tb, d), lambda h, j: (j, h))
    ckb = pl.BlockSpec((None, None, 1, tb), lambda h, j: (h, j, 0, 0))
    return _host(
        kern, beside, name=name, grid=(h_n, nb),
        in_specs=[whole, whole, blk, blk, whole, whole, whole, ckb],
        out_specs=[whole, blk, blk, whole, ckb],
        out_shape=[jax.ShapeDtypeStruct((s, hd), F32), jax.ShapeDtypeStruct((s, hd), F32),
                   jax.ShapeDtypeStruct((s, hd), BF16), jax.ShapeDtypeStruct((s, hd), F32),
                   jax.ShapeDtypeStruct((h_n, nb, 1, tb), F32)],
        scratch_shapes=[], semantics=("arbitrary", "arbitrary"), args=[q, do, k, v, lse_rep, delta_rep, cq_rep, ck])


def _adamw(w, g, m, v, name):
    cols = w.shape[1]
    tm = _tile(w.shape[0], 128) if w.shape[0] % 128 == 0 else w.shape[0]
    return _rowwise(_adamw_math, name, [w, g, m, v], outs=[(cols, F32)] * 3, tm=tm)


def _sum_arrays(arrs, out_dtype, name):
    def fn(*xs):
        acc = xs[0].astype(F32)
        for x in xs[1:]:
            acc = acc + x.astype(F32)
        return acc

    tm = PACK_ROWS if arrs[0].shape[0] % PACK_ROWS == 0 else arrs[0].shape[0]
    return _rowwise(fn, name, list(arrs), outs=[(arrs[0].shape[1], out_dtype)], tm=tm)[0]


def _half_tile(rows):
    for t in (256, 176, 128, 64, 32, 16):
        if rows % t == 0:
            return t
    return rows


def _sum_half(g, got, core, name):
    n_chip, hr, cols = got.shape
    tm = _half_tile(hr)
    nt = hr // tm

    def kern(c_ref, g_ref, a_ref, o_ref):
        o_ref[...] = (g_ref[...].astype(F32) + a_ref[...].astype(F32)).astype(o_ref.dtype)

    grid_spec = pltpu.PrefetchScalarGridSpec(
        num_scalar_prefetch=1, grid=(n_chip * nt,),
        in_specs=[pl.BlockSpec((tm, cols), lambda r, c: (((r // nt) * 2 + c[0]) * nt + r % nt, 0)),
                  pl.BlockSpec((tm, cols), lambda r, c: (r, 0))],
        out_specs=pl.BlockSpec((tm, cols), lambda r, c: (r, 0)))
    out = pl.pallas_call(
        kern, name=name, grid_spec=grid_spec, out_shape=jax.ShapeDtypeStruct((n_chip * hr, cols), BF16),
        compiler_params=_params(("arbitrary",)))(core.reshape(1), g.reshape(-1, cols), got.reshape(-1, cols))
    return out.reshape(n_chip, hr, cols)


def _sum_parts(own, parts, chip, name):
    n_parts, hr, cols = parts.shape
    tm = _half_tile(hr)

    def kern(s_ref, t_ref, p_ref, o_ref):
        acc = t_ref[...].astype(F32)
        for j in range(n_parts):
            acc = acc + p_ref[j].astype(F32)
        o_ref[...] = acc

    grid_spec = pltpu.PrefetchScalarGridSpec(
        num_scalar_prefetch=1, grid=(hr // tm,),
        in_specs=[pl.BlockSpec((None, tm, cols), lambda r, s: (s[0], r, 0)),
                  pl.BlockSpec((n_parts, tm, cols), lambda r, s: (0, r, 0))],
        out_specs=pl.BlockSpec((tm, cols), lambda r, s: (r, 0)))
    return pl.pallas_call(
        kern, name=name, grid_spec=grid_spec, out_shape=jax.ShapeDtypeStruct((hr, cols), F32),
        compiler_params=_params(("arbitrary",)))(chip.reshape(1), own, parts)


def _adamw_math(w, g, m, v):
    m = ADAM_B1 * m + (1.0 - ADAM_B1) * g
    v = ADAM_B2 * v + (1.0 - ADAM_B2) * (g * g)
    m_hat = m * (1.0 / (1.0 - ADAM_B1 ** ADAM_STEP))
    v_hat = v * (1.0 / (1.0 - ADAM_B2 ** ADAM_STEP))
    return -ADAM_LR * (m_hat / (jnp.sqrt(v_hat) + ADAM_EPS) + ADAM_WD * w), m, v


def _adamw_big(w, m, v, mine, theirs, core, layer, prev, name):
    n_layers, rows, cols = w.shape
    hr = rows // 2
    tm = next(t for t in (128, 64, 32, 16, 8) if hr % t == 0)
    nt = hr // tm

    def kern(s_ref, w_ref, m_ref, v_ref, a_ref, b_ref, *rest):
        g_ref, d_ref, nm_ref, nv_ref = rest[-4:]
        g = jnp.where(pl.program_id(0) // nt == s_ref[0], a_ref[...], b_ref[...])
        g_ref[...] = g
        d_ref[...], nm_ref[...], nv_ref[...] = _adamw_math(w_ref[...], g, m_ref[...], v_ref[...])

    lyr = pl.BlockSpec((None, tm, cols), lambda r, s: (layer, r, 0))
    half = pl.BlockSpec((tm, cols), lambda r, s: (r % nt, 0))
    passed = [] if prev is None else list(prev)
    grid_spec = pltpu.PrefetchScalarGridSpec(
        num_scalar_prefetch=1, grid=(rows // tm,),
        in_specs=[lyr, lyr, lyr, half, half] + [pl.BlockSpec(memory_space=pl.ANY)] * len(passed), out_specs=[lyr] * 4)
    return pl.pallas_call(
        kern, name=name, grid_spec=grid_spec, out_shape=[jax.ShapeDtypeStruct(w.shape, F32)] * 4,
        input_output_aliases={6 + i: i for i in range(len(passed))},
        compiler_params=_params(("arbitrary",)))(core.reshape(1), w, m, v, mine, theirs, *passed)


def _cast_into_slot(src, slot, name, beside=None):
    rows, cols = src.shape
    tm = _half_tile(rows)

    def kern(s_ref, x_ref, o_ref):
        o_ref[...] = x_ref[...].astype(o_ref.dtype)

    res, extra = _host(
        kern, beside, name=name, grid=(rows // tm,), in_specs=[pl.BlockSpec((tm, cols), lambda r, s: (r, 0))],
        out_specs=pl.BlockSpec((None, tm, cols), lambda r, s: (s[0], r, 0)),
        out_shape=jax.ShapeDtypeStruct((N_CHIPS, rows, cols), BF16), scratch_shapes=[], semantics=("arbitrary",),
        args=[src], prefetch=[slot.reshape(1)])
    return res if beside is None else (res, extra)


FLIP_C = [(0, 0, 1)]
FLIP_XY = [(1, 0, 0), (0, 1, 0), (1, 1, 0)]
FLIP_ALL = [(fx, fy, fc) for fx in (0, 1) for fy in (0, 1) for fc in (0, 1) if (fx, fy, fc) != (0, 0, 0)]


def _chip(dev):
    return 2 * dev[0] + dev[1]


def _device(dev):
    return 4 * dev[0] + 2 * dev[1] + dev[2]


def _exchange(name, srcs, dst_shapes, rels, src_view, dst_view, own_view=None, in_place=False):
    n, n_rel = len(srcs), len(rels)

    def body(*refs):
        src_refs, dst_refs = refs[:n], refs[n:2 * n]
        send_sems, recv_sems, own_sems = refs[2 * n:]
        me = (lax.axis_index("x"), lax.axis_index("y"), lax.axis_index("c"))
        peers = [tuple(1 - m if f else m for m, f in zip(me, rel)) for rel in rels]

        def copy(i, k, sender, receiver):
            return pltpu.make_async_remote_copy(
                src_ref=src_view(i, src_refs[i], sender, receiver), dst_ref=dst_view(i, dst_refs[i], sender, k),
                send_sem=send_sems.at[i * n_rel + k], recv_sem=recv_sems.at[i * n_rel + k], device_id=receiver,
                device_id_type=pl.DeviceIdType.MESH)

        sends = [copy(i, k, me, peer) for i in range(n) for k, peer in enumerate(peers)]
        for cp in sends:
            cp.start()
        mine = []
        if own_view is not None:
            for i in range(n):
                frm, to = own_view(i, src_refs[i], dst_refs[i], me)
                mine.append(pltpu.make_async_copy(frm, to, own_sems.at[i]))
                mine[-1].start()
        for i in range(n):
            for k, peer in enumerate(peers):
                copy(i, k, peer, me).wait_recv()
        for cp in sends:
            cp.wait_send()
        for cp in mine:
            cp.wait()

    any_spec = pl.BlockSpec(memory_space=pl.ANY)
    return pl.pallas_call(
        body, name=name, out_shape=[jax.ShapeDtypeStruct(s, d) for s, d in dst_shapes],
        in_specs=[any_spec] * n, out_specs=[any_spec] * n,
        input_output_aliases={i: i for i in range(n)} if in_place else {},
        scratch_shapes=[pltpu.SemaphoreType.DMA((n * n_rel,)), pltpu.SemaphoreType.DMA((n * n_rel,)),
                        pltpu.SemaphoreType.DMA((n,))])(*srcs)


def _gather_plan(bufs, rows_of=None):
    n = len(bufs)
    rows_of = rows_of or [(0, b.shape[1]) for b in bufs]
    starts = [r0 for r0, _ in rows_of]
    halves = [nr // 2 for _, nr in rows_of]

    def tools(src_refs, dst_refs, sems):
        ici_send, ici_recv, d2d_send, d2d_recv = sems
        x, y, c = lax.axis_index("x"), lax.axis_index("y"), lax.axis_index("c")
        sibling = (x, y, 1 - c)
        chips = [(1 - x, y), (x, 1 - y), (1 - x, 1 - y)]

        def rows(i, chip, core):
            return dst_refs[i].at[2 * chip[0] + chip[1], pl.ds(starts[i] + core * halves[i], halves[i]), :]

        def over_ici(i, k, src, chip_from, to):
            return pltpu.make_async_remote_copy(
                src_ref=src, dst_ref=rows(i, chip_from, c), send_sem=ici_send.at[3 * i + k],
                recv_sem=ici_recv.at[3 * i + k], device_id=to, device_id_type=pl.DeviceIdType.MESH)

        def over_d2d(i, k, core):
            return pltpu.make_async_remote_copy(
                src_ref=rows(i, chips[k], core), dst_ref=rows(i, chips[k], core), send_sem=d2d_send.at[3 * i + k],
                recv_sem=d2d_recv.at[3 * i + k], device_id=sibling, device_id_type=pl.DeviceIdType.MESH)

        def my_send(i, k):
            my_half = src_refs[i].at[2 * x + y, pl.ds(starts[i] + c * halves[i], halves[i]), :]
            return over_ici(i, k, my_half, (x, y), (*chips[k], c))

        def my_arrival(i, k):
            return over_ici(i, k, rows(i, chips[k], c), chips[k], (x, y, c))

        return c, over_d2d, my_send, my_arrival

    def start(src_refs, dst_refs, sems):
        _, _, my_send, _ = tools(src_refs, dst_refs, sems)
        for i in range(n):
            for k in range(3):
                my_send(i, k).start()

    def finish(src_refs, dst_refs, sems):
        c, over_d2d, my_send, my_arrival = tools(src_refs, dst_refs, sems)
        passed = []
        for i in range(n):
            for k in range(3):
                my_arrival(i, k).wait_recv()
                passed.append(over_d2d(i, k, c))
                passed[-1].start()
        for i in range(n):
            for k in range(3):
                over_d2d(i, k, 1 - c).wait_recv()
        for i in range(n):
            for k in range(3):
                my_send(i, k).wait_send()
        for cp in passed:
            cp.wait_send()

    return _Beside(bufs, [(b.shape, b.dtype, i) for i, b in enumerate(bufs)], [3 * n] * 4, start, finish)


def _scatter_plan(sums):
    n = len(sums)

    def copy(i, k, src_refs, dst_refs, sems, sender, receiver):
        return pltpu.make_async_remote_copy(
            src_ref=src_refs[i].at[_chip(receiver)], dst_ref=dst_refs[i].at[k], send_sem=sems[0].at[3 * i + k],
            recv_sem=sems[1].at[3 * i + k], device_id=receiver, device_id_type=pl.DeviceIdType.MESH)

    def each(fn, src_refs, dst_refs, sems, outgoing):
        me = (lax.axis_index("x"), lax.axis_index("y"), lax.axis_index("c"))
        for i in range(n):
            for k, rel in enumerate(FLIP_XY):
                peer = tuple(1 - m if f else m for m, f in zip(me, rel))
                fn(copy(i, k, src_refs, dst_refs, sems, *((me, peer) if outgoing else (peer, me))))

    def start(src_refs, dst_refs, sems):
        each(lambda cp: cp.start(), src_refs, dst_refs, sems, True)

    def finish(src_refs, dst_refs, sems):
        each(lambda cp: cp.wait_recv(), src_refs, dst_refs, sems, False)
        each(lambda cp: cp.wait_send(), src_refs, dst_refs, sems, True)

    return _Beside(sums, [((3,) + s.shape[1:], s.dtype, None) for s in sums], [3 * n] * 2, start, finish)


def _pack(parts, dtype, multiple):
    flat = jnp.concatenate([p.reshape(-1).astype(dtype) for p in parts])
    pad = (-flat.shape[0]) % multiple
    return jnp.pad(flat, (0, pad)) if pad else flat


def _unpack_shards(packs, names, shapes, axes):
    out, off = {}, 0
    for nme in names:
        sz = math.prod(shapes[nme])
        out[nme] = jnp.concatenate([packs[j, off:off + sz].reshape(shapes[nme]) for j in range(N_CHIPS)], axis=axes[nme])
        off += sz
    return out


def _shards_of(full, axis):
    sz = full.shape[axis] // N_CHIPS
    return [lax.slice_in_dim(full, j * sz, (j + 1) * sz, axis=axis) for j in range(N_CHIPS)]


def kernel(x, a_norm_w, a_in_proj, a_conv_w, a_conv_b, a_dt_bias, a_A_log, a_D, a_gnorm_w, a_out_proj, kv_norm_w, w_kvf, b_f, k_norm_w, b_norm_w, w_q, q_norm_w, w_o, ffn_norm_w, w_gate_up, w_down, loss_target, m_a_norm_w, m_a_in_proj, m_a_conv_w, m_a_conv_b, m_a_dt_bias, m_a_A_log, m_a_D, m_a_gnorm_w, m_a_out_proj, m_kv_norm_w, m_w_kvf, m_b_f, m_k_norm_w, m_b_norm_w, m_w_q, m_q_norm_w, m_w_o, m_ffn_norm_w, m_w_gate_up, m_w_down, v_a_norm_w, v_a_in_proj, v_a_conv_w, v_a_conv_b, v_a_dt_bias, v_a_A_log, v_a_D, v_a_gnorm_w, v_a_out_proj, v_kv_norm_w, v_w_kvf, v_b_f, v_k_norm_w, v_b_norm_w, v_w_q, v_q_norm_w, v_w_o, v_ffn_norm_w, v_w_gate_up, v_w_down):
    args = locals()
    w = {n: args[n] for n in WEIGHTS}
    mom = {n: args['m_' + n] for n in WEIGHTS}
    var = {n: args['v_' + n] for n in WEIGHTS}
    shapes = {n: w[n].shape for n in WEIGHTS}

    cx, cy, cc = lax.axis_index("x"), lax.axis_index("y"), lax.axis_index("c")
    my_chip = 2 * cx + cy

    xs_in = x[0]
    target = loss_target[0]
    s_len, d_model = xs_in.shape
    n_heads_ssm = a_dt_bias.shape[-1]
    d_inner = n_heads_ssm * SSM_HEAD_DIM
    d_xbc = a_conv_w.shape[-1] * N_CHIPS
    d_state = (d_xbc - d_inner) // (2 * SSM_GROUPS)
    kh = n_heads_ssm // SSM_GROUPS
    n_att = b_f.shape[0]
    d_att = n_att * ATT_HEAD_DIM
    d_ff = w_down.shape[1] * N_CHIPS

    small_names = list(SMALL_SHARDED)
    sp = _pack([w[n] for n in small_names], F32, PACK_COLS).reshape(-1, PACK_COLS)
    sp_all = _exchange("gather_small", [sp], [((N_CHIPS,) + sp.shape, F32)], FLIP_XY,
                       lambda i, r, me, peer: r, lambda i, r, sender, k: r.at[_chip(sender)],
                       lambda i, s, d, me: (s, d.at[_chip(me)]))[0]
    full = _unpack_shards(sp_all.reshape(N_CHIPS, -1), small_names, shapes, SMALL_SHARDED)
    for n in SMALL:
        if n not in full:
            full[n] = w[n]

    big2d = {'in': a_in_proj[0], 'out': a_out_proj[0], 'gu0': w_gate_up[0], 'dn0': w_down[0], 'kvf': w_kvf,
             'q': w_q[0], 'o': w_o[0], 'gu1': w_gate_up[1], 'dn1': w_down[1]}
    big_keys = list(big2d)
    norm_a, gnorm = full['a_norm_w'], full['a_gnorm_w']
    gathered = {}

    in_buf = _cast_into_slot(big2d['in'], my_chip, "cast_in")
    later = [k for k in big_keys if k != 'in']
    moved = [math.prod(big2d[k].shape) for k in later] + [s_len * d_model]
    units = big2d['in'].shape[0] // 32
    share = [m * units / sum(moved) for m in moved]
    piece = [int(sh) for sh in share]
    for i in sorted(range(len(share)), key=lambda i: share[i] - piece[i], reverse=True)[:units - sum(piece)]:
        piece[i] += 1
    slot, row0 = {}, 0
    for k, rows_k in zip(later + [None], [32 * p for p in piece]):
        plan = _gather_plan([in_buf], [(row0, rows_k)]) if rows_k else None
        row0 += rows_k
        if k is None:
            n1 = _rms_fwd(xs_in, norm_a, "norm_a", beside=plan)
        else:
            slot[k] = _cast_into_slot(big2d[k], my_chip, "cast_" + k, beside=plan)
        if plan is not None and k is None:
            n1, (in_buf,) = n1
        elif plan is not None:
            slot[k], (in_buf,) = slot[k]
    gathered['in'] = in_buf

    def gather(keys):
        return _gather_plan([slot[k] for k in keys])

    def pad_cols(a, width=LANES):
        return jnp.pad(a, ((0, 0), (0, width - a.shape[1])))

    w_in = jnp.concatenate([gathered['in'][j] for j in range(N_CHIPS)], axis=1)
    w_z, w_xbc, w_dt = w_in[:, :d_inner], w_in[:, d_inner:d_inner + d_xbc], pad_cols(w_in[:, d_inner + d_xbc:])
    conv_w, conv_b = full['a_conv_w'][0], full['a_conv_b']
    dt_bias, a_log = pad_cols(a_dt_bias), pad_cols(a_A_log)
    d_exp = jnp.repeat(a_D, SSM_HEAD_DIM, axis=1)
    kv_nw, b_nw = kv_norm_w.reshape(1, -1), b_norm_w
    k_nw, q_nw = k_norm_w.reshape(1, -1), q_norm_w
    bf_pad = pad_cols(b_f.reshape(1, -1))
    ffn_nw = [ffn_norm_w[i:i + 1] for i in range(2)]

    def to_groups(a):
        g = a[:, :n_heads_ssm].reshape(s_len, SSM_GROUPS, kh).transpose(1, 0, 2)
        return jnp.pad(g, ((0, 0), (0, 0), (0, LANES - kh)))

    def to_groups_t(a):
        g = a[:, :n_heads_ssm].reshape(s_len, SSM_GROUPS, kh).transpose(1, 2, 0)
        return jnp.pad(g, ((0, 0), (0, 8 - kh), (0, 0)))

    def from_groups(col, row=None):
        a = col[:, :, :kh].transpose(1, 0, 2).reshape(s_len, n_heads_ssm)
        if row is not None:
            a = a + row[:, :kh, :].transpose(2, 0, 1).reshape(s_len, n_heads_ssm)
        return pad_cols(a)

    z = _mm(n1, w_z, 'nn', "in_z")
    xbc_raw, (gathered['out'],) = _mm(n1, w_xbc, 'nn', "in_xbc", beside=gather(['out']))
    dtraw = _mm(n1, w_dt, 'nn', "in_dt")
    xbc = _conv_fwd(xbc_raw, conv_w, conv_b, "conv")
    dt, acum = _dt_fwd(dtraw, dt_bias, a_log, "dt")
    ac_g, act_g = to_groups(acum), to_groups_t(acum)
    dt_w = jnp.repeat(dt[:, :n_heads_ssm], SSM_HEAD_DIM, axis=1)
    ac_w = jnp.repeat(acum[:, :n_heads_ssm], SSM_HEAD_DIM, axis=1)
    (y_ssd, states), (gathered['gu0'],) = _ssd_fwd(xbc, dt_w, ac_w, ac_g, act_g, d_inner, d_state, kh, "ssd",
                                                   beside=gather(['gu0']))
    yn, (gathered['q'],) = _gate_fwd(y_ssd, xbc, z, d_exp, gnorm, d_inner, "gate", beside=gather(['q']))
    w_out = gathered['out'].reshape(-1, d_model)
    h1, (gathered['o'],) = _mm(yn, w_out, 'nn', "out_proj", add=xs_in, beside=gather(['o']))

    w_gu, w_dn = {}, {}

    def ffn_fwd(h, i, also):
        nrm = _rms_fwd(h, ffn_nw[i], f"ffn{i}_norm")
        w_gu[i] = gathered[f'gu{i}']
        gu = _mm(nrm, w_gu[i], 'nn', f"ffn{i}_up", out_dtype=BF16, shards=True,
                 beside=gather(also) if also else None)
        if also:
            gu, arrived = gu
            gathered.update(zip(also, arrived))
        act = _swiglu_fwd(gu, d_ff, f"ffn{i}_act")
        w_dn[i] = gathered[f'dn{i}'].reshape(-1, d_model)
        return nrm, gu, act, _mm(act, w_dn[i], 'nn', f"ffn{i}_down", add=h)

    n2, gu0, act0, h2 = ffn_fwd(h1, 0, ['dn0', 'kvf'])
    w_kvf_full = jnp.concatenate([gathered['kvf'][j] for j in range(N_CHIPS)], axis=1)
    w_k, w_v, w_f = w_kvf_full[:, :d_att], w_kvf_full[:, d_att:2 * d_att], pad_cols(w_kvf_full[:, 2 * d_att:])
    w_qm, w_om = gathered['q'].reshape(-1, d_att), gathered['o'].reshape(-1, d_model)
    nkv = _rms_fwd(h2, kv_nw, "kv_norm")
    k_raw = _mm(nkv, w_k, 'nn', "proj_k")
    v_att = _mm(nkv, w_v, 'nn', "proj_v", out_dtype=BF16)
    f_raw = _mm(nkv, w_f, 'nn', "proj_f")
    k_att = _rms_fwd(k_raw, k_nw, "k_norm")
    cum = _forget_fwd(f_raw, bf_pad, "forget")
    cq_rep = jnp.repeat(cum[:, :n_att], ATT_HEAD_DIM, axis=1)
    tb = min(ATT_BLOCK, s_len)
    ck = cum[:, :n_att].T.reshape(n_att, s_len // tb, 1, tb)
    n3 = _rms_fwd(h2, b_nw, "b_norm")
    q_raw = _mm(n3, w_qm, 'nn', "proj_q")
    q_att = _rms_fwd(q_raw, q_nw, "q_norm")
    (o_att, lse_rep), arrived = _attn_fwd(q_att, k_att, v_att, cq_rep, ck, "attn", beside=gather(['gu1', 'dn1']))
    gathered.update(zip(['gu1', 'dn1'], arrived))
    h3 = _mm(o_att, w_om, 'nn', "proj_o", add=h2)
    n4, gu1, act1, h4 = ffn_fwd(h3, 1, [])

    def loss_fn(h, t):
        err = h - t
        sq = jnp.sum(jnp.sum(err * err, axis=1, keepdims=True), axis=0, keepdims=True)
        return err * (1.0 / d_model), err * (1.0 / d_model), sq * (0.5 / d_model)

    dh4, dh4_b, loss_part = _rowwise(loss_fn, "loss", [h4, target], outs=[(d_model, F32), (d_model, BF16)],
                                     accs=[(1, 1)])

    n_f = n_att
    g_big, chip_sums, parts = {}, {}, {}

    def col_shards(pieces):
        fullw = jnp.concatenate(pieces, axis=1)
        return fullw.reshape(fullw.shape[0], N_CHIPS, -1).transpose(1, 0, 2)

    def row_shards(a):
        return a.reshape(N_CHIPS, -1, a.shape[1])

    def scatter(keys, tag):
        g_list = [g_big[k] for k in keys]
        hrs = [g.shape[1] // 2 for g in g_list]
        got = _exchange("reduce_d2d_" + tag, g_list,
                        [((N_CHIPS, hr, g.shape[2]), BF16) for g, hr in zip(g_list, hrs)], FLIP_C,
                        lambda i, r, me, peer: r.at[:, pl.ds((1 - me[2]) * hrs[i], hrs[i]), :],
                        lambda i, r, sender, k: r)
        for k, g, a in zip(keys, g_list, got):
            chip_sums[k] = _sum_half(g, a, cc, "reduce_sum2_" + k)
        return _scatter_plan([chip_sums[k] for k in keys])

    def ffn_bwd(dh, dh_b, h, nrm, gu, act, i, hosted):
        dact = _mm(dh_b, w_dn[i], 'nt', f"ffn{i}_down_dx", beside=scatter(hosted[:1], f"a{i}") if hosted else None)
        g_dn = _mm(act, dh_b, 'tn', f"ffn{i}_down_dw", out_dtype=BF16,
                   beside=scatter(hosted[1:], f"b{i}") if hosted else None)
        if hosted:
            (dact, first), (g_dn, second) = dact, g_dn
            parts.update(zip(hosted, first + second))
        dgu = _swiglu_bwd(gu, dact, d_ff, f"ffn{i}_act_bwd")
        dn = _mm(dgu, w_gu[i], 'nt', f"ffn{i}_up_dx", shards=True)
        g_gu = _mm(nrm, dgu, 'tn', f"ffn{i}_up_dw", out_dtype=BF16, shards=True)
        dh_new, dh_new_b, g_nw = _rms_bwd(h, ffn_nw[i], dn, f"ffn{i}_norm_bwd", extra=dh, copy_bf16=True)
        return dh_new, dh_new_b, g_dn, g_gu, g_nw

    dh3, dh3_b, g_dn1, g_gu1, g_fnw1 = ffn_bwd(dh4, dh4_b, h3, n4, gu1, act1, 1, [])
    do_att = _mm(dh3_b, w_om, 'nt', "proj_o_dx", out_dtype=BF16)
    g_wo = _mm(o_att, dh3_b, 'tn', "proj_o_dw", out_dtype=BF16)
    g_big.update(gu1=g_gu1, dn1=row_shards(g_dn1), o=row_shards(g_wo))
    late_keys = ['gu1', 'dn1', 'o']
    plan_late = scatter(late_keys, "late")

    def delta_fn(a, b):
        prod = a.astype(F32) * b.astype(F32)
        return jnp.concatenate([jnp.broadcast_to(jnp.sum(prod[:, g], axis=1, keepdims=True), (a.shape[0], ATT_HEAD_DIM))
                                for g in _lane_groups(a.shape[1], ATT_HEAD_DIM)], axis=1)

    delta_rep = _rowwise(delta_fn, "attn_delta", [do_att, o_att], outs=[(d_att, F32)])[0]
    (dq_att, dk_att, dv_att, dcq_rep, dck), arrived = _attn_bwd(q_att, k_att, v_att, do_att, lse_rep, delta_rep,
                                                                cq_rep, ck, "attn_bwd", beside=plan_late)
    parts.update(zip(late_keys, arrived))
    dq_raw, g_qnw = _rms_bwd(q_raw, q_nw, dq_att, "q_norm_bwd", out_dtype=BF16)
    dn3 = _mm(dq_raw, w_qm, 'nt', "proj_q_dx")
    g_wq = _mm(n3, dq_raw, 'tn', "proj_q_dw", out_dtype=BF16)
    dh2, g_bnw = _rms_bwd(h2, b_nw, dn3, "b_norm_bwd", extra=dh3)
    dk_raw, g_knw = _rms_bwd(k_raw, k_nw, dk_att, "k_norm_bwd", out_dtype=BF16)
    dcum = pad_cols(dcq_rep.reshape(s_len, n_att, ATT_HEAD_DIM)[:, :, 0] + dck.reshape(n_att, s_len).T)
    df_raw, g_bf = _forget_bwd(f_raw, bf_pad, dcum, "forget_bwd")
    dnkv = _mm(dk_raw, w_k, 'nt', "proj_k_dx")
    dnkv = _mm(dv_att, w_v, 'nt', "proj_v_dx", add=dnkv)
    dnkv = _mm(df_raw, w_f, 'nt', "proj_f_dx", add=dnkv)
    g_wk = _mm(nkv, dk_raw, 'tn', "proj_k_dw", out_dtype=BF16)
    g_wv = _mm(nkv, dv_att, 'tn', "proj_v_dw", out_dtype=BF16)
    g_wf = _mm(nkv, df_raw, 'tn', "proj_f_dw", out_dtype=BF16)
    dh2, dh2_b, g_kvnw = _rms_bwd(h2, kv_nw, dnkv, "kv_norm_bwd", extra=dh2, copy_bf16=True)
    g_big.update(q=row_shards(g_wq), kvf=col_shards([g_wk, g_wv, g_wf[:, :n_f]]))
    dh1, dh1_b, g_dn0, g_gu0, g_fnw0 = ffn_bwd(dh2, dh2_b, h1, n2, gu0, act0, 0, ['q', 'kvf'])
    g_big.update(gu0=g_gu0, dn0=row_shards(g_dn0))
    plan_ffn0 = scatter(['gu0', 'dn0'], "ffn0")

    dyn = _mm(dh1_b, w_out, 'nt', "out_proj_dx")
    g_wout = _mm(yn, dh1_b, 'tn', "out_proj_dw", out_dtype=BF16)
    dy_ssd, dxs_skip, dz, g_dexp, g_gnorm = _gate_bwd(y_ssd, xbc, z, d_exp, gnorm, dyn, d_inner, "gate_bwd")
    (dxs, d_b, d_c, ddt_g, dacc_g, dacr_g), arrived = _ssd_bwd(xbc, dt_w, ac_w, ac_g, act_g, states, dy_ssd, dxs_skip,
                                                               d_inner, d_state, kh, "ssd_bwd", beside=plan_ffn0)
    parts.update(zip(['gu0', 'dn0'], arrived))
    g_big.update(out=row_shards(g_wout))
    dxbc_act = jnp.concatenate([dxs, d_b, d_c], axis=1)
    (du, g_convw, g_convb), (parts['out'],) = _conv_bwd(xbc_raw, conv_w, conv_b, dxbc_act, "conv_bwd",
                                                        beside=scatter(['out'], "out"))
    draw, g_dtb, g_alog = _dt_bwd(dtraw, dt_bias, a_log, from_groups(ddt_g), from_groups(dacc_g, dacr_g), "dt_bwd")
    g_wz = _mm(n1, dz, 'tn', "in_z_dw", out_dtype=BF16)
    g_wxbc = _mm(n1, du, 'tn', "in_xbc_dw", out_dtype=BF16)
    g_wdt = _mm(n1, draw, 'tn', "in_dt_dw", out_dtype=BF16)
    g_in = jnp.concatenate([g_wz, g_wxbc, g_wdt[:, :n_heads_ssm]], axis=1)
    g_in = g_in.reshape(2, d_model // 2, N_CHIPS, -1).transpose(0, 2, 1, 3)
    g_big.update(in0=g_in[0], in1=g_in[1])
    dn1, (parts['in0'],) = _mm(du, w_xbc, 'nt', "in_xbc_dx", beside=scatter(['in0'], "in0"))
    dn1, (parts['in1'],) = _mm(dz, w_z, 'nt', "in_z_dx", add=dn1, beside=scatter(['in1'], "in1"))
    dn1 = _mm(draw, w_dt, 'nt', "in_dt_dx", add=dn1)
    dx, g_norm_a = _rms_bwd(xs_in, norm_a, dn1, "norm_a_bwd", extra=dh1)

    g_small = {
        'a_norm_w': g_norm_a, 'a_conv_w': g_convw[None], 'a_conv_b': g_convb,
        'a_dt_bias': g_dtb[:, :n_heads_ssm], 'a_A_log': g_alog[:, :n_heads_ssm],
        'a_D': g_dexp.reshape(n_heads_ssm, SSM_HEAD_DIM).sum(axis=1).reshape(1, -1), 'a_gnorm_w': g_gnorm,
        'kv_norm_w': g_kvnw.reshape(-1), 'b_f': g_bf[0, :n_f], 'k_norm_w': g_knw.reshape(-1), 'b_norm_w': g_bnw,
        'q_norm_w': g_qnw, 'ffn_norm_w': jnp.concatenate([g_fnw0, g_fnw1], axis=0),
    }

    sg = _pack([g_small[n] for n in SMALL] + [loss_part], F32, 8 * PACK_COLS).reshape(-1, PACK_COLS)
    sg_all = _exchange("reduce_small", [sg], [((2 * N_CHIPS,) + sg.shape, F32)], FLIP_ALL,
                       lambda i, r, me, peer: r, lambda i, r, sender, k: r.at[_device(sender)],
                       lambda i, s, d, me: (s, d.at[_device(me)]))[0]
    sg_sum = _sum_arrays([sg_all[d] for d in range(2 * N_CHIPS)], F32, "reduce_small_sum").reshape(-1)
    red_small, off = {}, 0
    for n in SMALL:
        shp = g_small[n].shape
        red_small[n] = sg_sum[off:off + math.prod(shp)].reshape(shp)
        off += math.prod(shp)
    loss = sg_sum[off]

    red_keys = list(parts)
    half_sums = [_sum_parts(chip_sums[k], parts[k], my_chip, "reduce_sum4_" + k) for k in red_keys]
    others = _exchange("reduce_back", half_sums, [(h.shape, F32) for h in half_sums], FLIP_C,
                       lambda i, r, me, peer: r, lambda i, r, sender, k: r)
    mine_of, theirs_of = dict(zip(red_keys, half_sums)), dict(zip(red_keys, others))

    grads, delta, new_m, new_v = {}, {}, {}, {}
    layers_of = {'a_in_proj': ['in0', 'in1'], 'a_out_proj': ['out'], 'w_kvf': ['kvf'], 'w_q': ['q'], 'w_o': ['o'],
                 'w_gate_up': ['gu0', 'gu1'], 'w_down': ['dn0', 'dn1']}
    for n, keys in layers_of.items():
        three_d = (len(keys), math.prod(shapes[n][:-1]) // len(keys), shapes[n][-1])
        res = None
        for layer, k in enumerate(keys):
            res = _adamw_big(w[n].reshape(three_d), mom[n].reshape(three_d), var[n].reshape(three_d), mine_of[k],
                             theirs_of[k], cc, layer, res, "adamw_" + k)
        grads[n], delta[n], new_m[n], new_v[n] = [r.reshape(shapes[n]) for r in res]
    for n in SMALL:
        if n in SMALL_SHARDED:
            ax = SMALL_SHARDED[n]
            grads[n] = lax.dynamic_slice_in_dim(red_small[n], my_chip * shapes[n][ax], shapes[n][ax], axis=ax)
        else:
            grads[n] = red_small[n]

    packed = [_pack([src[n] for n in SMALL], F32, 8 * LANES).reshape(-1, LANES) for src in (w, grads, mom, var)]
    small_out = _adamw(*packed, "adamw_small")
    for store, flat in zip((delta, new_m, new_v), small_out):
        flat, off = flat.reshape(-1), 0
        for n in SMALL:
            sz = math.prod(shapes[n])
            store[n] = flat[off:off + sz].reshape(shapes[n])
            off += sz

    return (loss, dx[None], *[grads[n] for n in WEIGHTS], *[delta[n] for n in WEIGHTS],
            *[new_m[n] for n in WEIGHTS], *[new_v[n] for n in WEIGHTS])
```

```python
import functools
import math

import jax
import jax.numpy as jnp
from jax import lax
from jax.experimental import pallas as pl
from jax.experimental.pallas import tpu as pltpu

F32, BF16 = jnp.float32, jnp.bfloat16
EPS = 1e-6
SSM_HEAD_DIM = 64
SSM_GROUPS = 8
SSD_CHUNK = 128
ATT_HEAD_DIM = 128
LANES = 128
N_CHIPS = 4
NEG = -1e30
ADAM_LR, ADAM_B1, ADAM_B2, ADAM_EPS, ADAM_WD, ADAM_STEP = 0.001, 0.9, 0.999, 1e-08, 0.01, 10
VMEM_LIMIT_BYTES = 56 * 1024 * 1024
PACK_COLS = 1024
PACK_ROWS = 256
MM_K_TILES = (2816, 2048, 1408, 1024, 512, 256, 128)
MM_OPERAND_BYTES = 12 * 1024 * 1024

NN = ((1,), (0,))
NT = ((1,), (1,))
TN = ((0,), (0,))

WEIGHTS = ['a_norm_w', 'a_in_proj', 'a_conv_w', 'a_conv_b', 'a_dt_bias', 'a_A_log', 'a_D', 'a_gnorm_w', 'a_out_proj',
           'kv_norm_w', 'w_kvf', 'b_f', 'k_norm_w', 'b_norm_w', 'w_q', 'q_norm_w', 'w_o', 'ffn_norm_w', 'w_gate_up',
           'w_down']
BIG = {'a_in_proj': 2, 'a_out_proj': 1, 'w_kvf': 1, 'w_q': 1, 'w_o': 1, 'w_gate_up': 2, 'w_down': 1}
SMALL_SHARDED = {'a_norm_w': 1, 'a_conv_w': 2, 'a_conv_b': 1, 'a_gnorm_w': 1}
SMALL = [n for n in WEIGHTS if n not in BIG]


def _dot(a, b, dims):
    return lax.dot_general(a, b, (dims, ((), ())), preferred_element_type=F32)


def _params(sem=None):
    return pltpu.CompilerParams(dimension_semantics=sem, vmem_limit_bytes=VMEM_LIMIT_BYTES)


def _tile(dim, cap):
    for t in (1408, 1024, 512, 256, 128):
        if t <= cap and dim % t == 0:
            return t
    return dim


class _Beside:
    def __init__(self, operands, results, sem_sizes, start, finish):
        self.operands, self.results, self.sem_sizes = list(operands), list(results), list(sem_sizes)
        self.start, self.finish = start, finish


def _host(kern, beside, *, name, grid, in_specs, out_specs, out_shape, scratch_shapes, semantics, args, prefetch=()):
    single = not isinstance(out_shape, (list, tuple))
    out_specs = [out_specs] if single else list(out_specs)
    out_shape = [out_shape] if single else list(out_shape)
    n_pre = len(prefetch)

    def call(body, in_specs, out_specs, out_shape, scratch_shapes, semantics, aliases, args):
        if not n_pre:
            return pl.pallas_call(body, name=name, grid=grid, in_specs=list(in_specs), out_specs=list(out_specs),
                                  out_shape=out_shape, scratch_shapes=list(scratch_shapes),
                                  input_output_aliases=aliases, compiler_params=_params(semantics))(*args)
        spec = pltpu.PrefetchScalarGridSpec(num_scalar_prefetch=n_pre, grid=grid, in_specs=list(in_specs),
                                            out_specs=list(out_specs), scratch_shapes=list(scratch_shapes))
        return pl.pallas_call(body, name=name, grid_spec=spec, out_shape=out_shape,
                              input_output_aliases={n_pre + i: o for i, o in aliases.items()},
                              compiler_params=_params(semantics))(*prefetch, *args)

    if beside is None:
        res = call(kern, in_specs, out_specs, out_shape, scratch_shapes, semantics, {}, args)
        return (res[0] if single else res), []
    n_in, n_out, n_scr = len(in_specs), len(out_specs), len(scratch_shapes)
    nb_in, nb_out = len(beside.operands), len(beside.results)

    def body(*refs):
        pre, refs = refs[:n_pre], refs[n_pre:]
        ins, b_ins = refs[:n_in], refs[n_in:n_in + nb_in]
        outs = refs[n_in + nb_in:n_in + nb_in + n_out]
        b_outs = refs[n_in + nb_in + n_out:n_in + nb_in + n_out + nb_out]
        rest = refs[n_in + nb_in + n_out + nb_out:]
        scr, sems = rest[:n_scr], rest[n_scr:]
        ids = [pl.program_id(a) for a in range(len(grid))]
        first = functools.reduce(jnp.logical_and, [i == 0 for i in ids])
        last = functools.reduce(jnp.logical_and, [i == g - 1 for i, g in zip(ids, grid)])

        @pl.when(first)
        def _():
            beside.start(b_ins, b_outs, sems)

        kern(*pre, *ins, *outs, *scr)

        @pl.when(last)
        def _():
            beside.finish(b_ins, b_outs, sems)

    any_spec = pl.BlockSpec(memory_space=pl.ANY)
    res = call(body, list(in_specs) + [any_spec] * nb_in, out_specs + [any_spec] * nb_out,
               out_shape + [jax.ShapeDtypeStruct(s, d) for s, d, _ in beside.results],
               list(scratch_shapes) + [pltpu.SemaphoreType.DMA((k,)) for k in beside.sem_sizes],
               ("arbitrary",) * len(grid),
               {n_in + op: n_out + r for r, (_, _, op) in enumerate(beside.results) if op is not None},
               list(args) + list(beside.operands))
    mine = res[:n_out]
    return (mine[0] if single else mine), list(res[n_out:])


def _alone(beside, name):
    return _host(lambda: None, beside, name=name, grid=(1,), in_specs=[], out_specs=[], out_shape=[],
                 scratch_shapes=[], semantics=("arbitrary",), args=[])[1]


def _mm(a, b, mode, name, out_dtype=F32, add=None, shards=False, beside=None):
    if mode == 'nn':
        (m, k), n = a.shape, (b.shape[2] * N_CHIPS if shards else b.shape[1])
    elif mode == 'nt':
        (m, k), n = a.shape, (b.shape[1] if shards else b.shape[0])
    else:
        (k, m), n = a.shape, b.shape[1]
    per_chip = (k if mode == 'nt' else n) // N_CHIPS
    tm = _tile(m, 1024)
    tn = _tile(per_chip if shards and mode != 'nt' else n, 1408 if shards else 1024)
    k_dim = per_chip if shards and mode == 'nt' else k
    a_bytes, b_bytes = jnp.dtype(a.dtype).itemsize, jnp.dtype(b.dtype).itemsize
    tk = next((t for t in MM_K_TILES if k_dim % t == 0 and t * (tm * a_bytes + tn * b_bytes) <= MM_OPERAND_BYTES), k_dim)
    nk = k // tk
    in_place = nk > 1 and out_dtype == F32
    a_spec = pl.BlockSpec((tk, tm), lambda i, j, q: (q, i)) if mode == 'tn' else pl.BlockSpec((tm, tk), lambda i, j, q: (i, q))
    b_spec = pl.BlockSpec((tn, tk), lambda i, j, q: (j, q)) if mode == 'nt' else pl.BlockSpec((tk, tn), lambda i, j, q: (q, j))
    o_spec = pl.BlockSpec((tm, tn), lambda i, j, q: (i, j))
    out_struct = jax.ShapeDtypeStruct((m, n), out_dtype)
    if shards:
        per = per_chip // (tk if mode == 'nt' else tn)
        if mode == 'nn':
            b_spec = pl.BlockSpec((None, tk, tn), lambda i, j, q: (j // per, q, j % per))
        elif mode == 'nt':
            b_spec = pl.BlockSpec((None, tn, tk), lambda i, j, q: (q // per, j, q % per))
        else:
            out_struct = jax.ShapeDtypeStruct((N_CHIPS, m, per_chip), out_dtype)
    out_spec = pl.BlockSpec((None, tm, tn), lambda i, j, q: (j // per, i, j % per)) if shards and mode == 'tn' else o_spec
    dims = {'nn': NN, 'nt': NT, 'tn': TN}[mode]

    n_ins = 3 if add is not None else 2

    def kern(*refs):
        a_ref, b_ref, o_ref = refs[0], refs[1], refs[n_ins]
        acc = o_ref if in_place or nk == 1 else refs[n_ins + 1]
        q = pl.program_id(2)
        part = _dot(a_ref[...].astype(BF16), b_ref[...].astype(BF16), dims)

        def first():
            return part if add is None else part + refs[2][...]

        if nk == 1:
            o_ref[...] = first().astype(o_ref.dtype)
            return

        @pl.when(q == 0)
        def _():
            acc[...] = first()

        @pl.when(q > 0)
        def _():
            acc[...] += part

        if not in_place:
            @pl.when(q == nk - 1)
            def _():
                o_ref[...] = acc[...].astype(o_ref.dtype)

    ins, specs = [a, b], [a_spec, b_spec]
    if add is not None:
        ins.append(add)
        specs.append(o_spec)
    scratch = [] if in_place or nk == 1 else [pltpu.VMEM((tm, tn), F32)]
    res, extra = _host(kern, beside, name=name, grid=(m // tm, n // tn, nk), in_specs=specs, out_specs=out_spec,
                       out_shape=out_struct, scratch_shapes=scratch,
                       semantics=("parallel", "parallel", "arbitrary"), args=ins)
    return res if beside is None else (res, extra)


def _rowwise(fn, name, rows, bcast=(), outs=(), accs=(), tm=256, beside=None):
    rows = [r if isinstance(r, tuple) else (r, r.shape[1], 0) for r in rows]
    n_rows = rows[0][0].shape[0]
    tm = min(tm, n_rows)
    assert n_rows % tm == 0, (name, n_rows, tm)
    n_in, n_out = len(rows) + len(bcast), len(outs)
    in_specs = [pl.BlockSpec((tm, w), functools.partial(lambda i, cb: (i, cb), cb=cb)) for _, w, cb in rows]
    in_specs += [pl.BlockSpec(b.shape, lambda i: (0, 0)) for b in bcast]
    out_specs = [pl.BlockSpec((tm, w), lambda i: (i, 0)) for w, _ in outs]
    out_specs += [pl.BlockSpec(s, lambda i: (0, 0)) for s in accs]
    out_shape = [jax.ShapeDtypeStruct((n_rows, w), d) for w, d in outs] + [jax.ShapeDtypeStruct(s, F32) for s in accs]

    def kern(*refs):
        vals = fn(*[r[...] for r in refs[:n_in]])
        vals = vals if isinstance(vals, (tuple, list)) else (vals,)
        o_refs = refs[n_in:]
        for r, v in zip(o_refs[:n_out], vals[:n_out]):
            r[...] = v.astype(r.dtype)
        if accs:
            @pl.when(pl.program_id(0) == 0)
            def _():
                for r in o_refs[n_out:]:
                    r[...] = jnp.zeros_like(r)

            for r, v in zip(o_refs[n_out:], vals[n_out:]):
                r[...] += v

    res, extra = _host(kern, beside, name=name, grid=(n_rows // tm,), in_specs=in_specs, out_specs=out_specs,
                       out_shape=out_shape, scratch_shapes=[], semantics=("arbitrary",),
                       args=[r[0] for r in rows] + list(bcast))
    return res if beside is None else (res, extra)


def _rms(x, w):
    xf = x.astype(F32)
    return xf * lax.rsqrt(jnp.mean(xf * xf, axis=-1, keepdims=True) + EPS) * w


def _lane_groups(width, group):
    return [slice(g * group, (g + 1) * group) for g in range(width // group)]


def _rms_fwd(x, w, name, tm=256, beside=None):
    def fn(x, w):
        return jnp.concatenate([_rms(x[:, g], w) for g in _lane_groups(x.shape[1], w.shape[1])], axis=1)

    res = _rowwise(fn, name, [x], [w], outs=[(x.shape[1], BF16)], tm=tm, beside=beside)
    return res[0] if beside is None else (res[0][0], res[1])


def _rms_bwd(x, w, dy, name, extra=None, out_dtype=F32, tm=256, copy_bf16=False):
    def fn(x, dy, *rest):
        w = rest[-1]
        dxs, dw = [], jnp.zeros(w.shape, F32)
        for g in _lane_groups(x.shape[1], w.shape[1]):
            _, vjp = jax.vjp(_rms, x[:, g], w)
            dx_g, dw_g = vjp(dy[:, g].astype(F32))
            dxs.append(dx_g)
            dw = dw + dw_g
        dx = jnp.concatenate(dxs, axis=1)
        if extra is not None:
            dx = dx + rest[0]
        return (dx, dx, dw) if copy_bf16 else (dx, dw)

    rows = [x, dy] + ([extra] if extra is not None else [])
    outs = [(x.shape[1], out_dtype)] + ([(x.shape[1], BF16)] if copy_bf16 else [])
    return _rowwise(fn, name, rows, [w], outs=outs, accs=[w.shape], tm=tm)


def _sigmoid(x):
    return 1.0 / (1.0 + jnp.exp(-x))


def _softplus(x):
    return jnp.maximum(x, 0.0) + jnp.log(1.0 + jnp.exp(-jnp.abs(x)))


def _swiglu_fwd(gu, d_ff, name):
    def fn(g, u):
        g, u = g.astype(F32), u.astype(F32)
        return g * _sigmoid(g) * u

    return _rowwise(fn, name, [(gu, d_ff, 0), (gu, d_ff, 1)], outs=[(d_ff, BF16)], tm=128)[0]


def _swiglu_bwd(gu, dact, d_ff, name):
    def fn(g, u, da):
        g, u = g.astype(F32), u.astype(F32)
        s = _sigmoid(g)
        dg = da * u * s * (1.0 + g * (1.0 - s))
        du = da * g * s
        return jnp.concatenate([dg, du], axis=1)

    return _rowwise(fn, name, [(gu, d_ff, 0), (gu, d_ff, 1), dact], outs=[(2 * d_ff, BF16)], tm=128)[0]


def _cumsum_rows(v, reverse=False):
    n = v.shape[0]
    row = lax.broadcasted_iota(jnp.int32, v.shape, 0)
    sh = 1
    while sh < n:
        if reverse:
            v = v + jnp.where(row < n - sh, pltpu.roll(v, n - sh, 0), 0.0)
        else:
            v = v + jnp.where(row >= sh, pltpu.roll(v, sh, 0), 0.0)
        sh *= 2
    return v


def _conv_fwd(u, w, b, name):
    s, c = u.shape
    kw = w.shape[0]
    tc = _tile(c, 128)

    def kern(u_ref, w_ref, b_ref, o_ref):
        uu = u_ref[...]
        row = lax.broadcasted_iota(jnp.int32, uu.shape, 0)
        acc = jnp.zeros_like(uu) + b_ref[...]
        for k in range(kw):
            sh = kw - 1 - k
            uk = uu if sh == 0 else jnp.where(row >= sh, pltpu.roll(uu, sh, 0), 0.0)
            acc = acc + w_ref[pl.ds(k, 1), :] * uk
        o_ref[...] = acc * _sigmoid(acc)

    return pl.pallas_call(
        kern, name=name, grid=(c // tc,),
        in_specs=[pl.BlockSpec((s, tc), lambda j: (0, j)), pl.BlockSpec((kw, tc), lambda j: (0, j)),
                  pl.BlockSpec((1, tc), lambda j: (0, j))],
        out_specs=pl.BlockSpec((s, tc), lambda j: (0, j)), out_shape=jax.ShapeDtypeStruct((s, c), F32),
        compiler_params=_params(("parallel",)))(u, w, b)


def _conv_bwd(u, w, b, dact, name, beside=None):
    s, c = u.shape
    kw = w.shape[0]
    tc = _tile(c, 128)

    def kern(u_ref, w_ref, b_ref, d_ref, du_ref, dw_ref, db_ref):
        uu = u_ref[...]
        row = lax.broadcasted_iota(jnp.int32, uu.shape, 0)
        shifted = []
        acc = jnp.zeros_like(uu) + b_ref[...]
        for k in range(kw):
            sh = kw - 1 - k
            uk = uu if sh == 0 else jnp.where(row >= sh, pltpu.roll(uu, sh, 0), 0.0)
            shifted.append(uk)
            acc = acc + w_ref[pl.ds(k, 1), :] * uk
        sg = _sigmoid(acc)
        dacc = d_ref[...] * sg * (1.0 + acc * (1.0 - sg))
        db_ref[...] = jnp.sum(dacc, axis=0, keepdims=True)
        du = jnp.zeros_like(uu)
        for k in range(kw):
            sh = kw - 1 - k
            dw_ref[pl.ds(k, 1), :] = jnp.sum(dacc * shifted[k], axis=0, keepdims=True)
            dk = dacc if sh == 0 else jnp.where(row < s - sh, pltpu.roll(dacc, s - sh, 0), 0.0)
            du = du + w_ref[pl.ds(k, 1), :] * dk
        du_ref[...] = du.astype(du_ref.dtype)

    col = lambda j: (0, j)
    return _host(
        kern, beside, name=name, grid=(c // tc,),
        in_specs=[pl.BlockSpec((s, tc), col), pl.BlockSpec((kw, tc), col), pl.BlockSpec((1, tc), col),
                  pl.BlockSpec((s, tc), col)],
        out_specs=[pl.BlockSpec((s, tc), col), pl.BlockSpec((kw, tc), col), pl.BlockSpec((1, tc), col)],
        out_shape=[jax.ShapeDtypeStruct((s, c), BF16), jax.ShapeDtypeStruct((kw, c), F32),
                   jax.ShapeDtypeStruct((1, c), F32)],
        scratch_shapes=[], semantics=("parallel",), args=[u, w, b, dact])


def _dt_fwd(dtraw, bias, a_log, name):
    def fn(raw, bias, a_log):
        dt = _softplus(raw + bias)
        return dt, _cumsum_rows(dt * (-jnp.exp(a_log)))

    return _rowwise(fn, name, [dtraw], [bias, a_log], outs=[(LANES, F32), (LANES, F32)], tm=SSD_CHUNK)


def _dt_bwd(dtraw, bias, a_log, ddt, dacum, name):
    def fn(raw, ddt, dac, bias, a_log):
        z = raw + bias
        dt = _softplus(z)
        a_neg = -jnp.exp(a_log)
        da = _cumsum_rows(dac, reverse=True)
        draw = (ddt + da * a_neg) * _sigmoid(z)
        return draw, jnp.sum(draw, axis=0, keepdims=True), jnp.sum(da * dt, axis=0, keepdims=True) * a_neg

    return _rowwise(fn, name, [dtraw, ddt, dacum], [bias, a_log], outs=[(LANES, BF16)],
                    accs=[(1, LANES), (1, LANES)], tm=SSD_CHUNK)


def _ssd_pieces(l, gw):
    lane_k = lax.broadcasted_iota(jnp.int32, (l, LANES), 1)
    lane_of = lax.broadcasted_iota(jnp.int32, (gw, LANES), 0)
    head_of = lax.broadcasted_iota(jnp.int32, (gw, LANES), 1)
    in_head = ((lane_of >= head_of * SSM_HEAD_DIM) & (lane_of < (head_of + 1) * SSM_HEAD_DIM)).astype(BF16)

    lane_w = lax.broadcasted_iota(jnp.int32, (l, gw), 1)

    def col(blk, k):
        return jnp.sum(jnp.where(lane_k == k, blk, 0.0), axis=1, keepdims=True)

    def expand(blk):
        acc = jnp.zeros((l, gw), F32)
        for k in range(gw // SSM_HEAD_DIM):
            acc = jnp.where((lane_w >= k * SSM_HEAD_DIM) & (lane_w < (k + 1) * SSM_HEAD_DIM), col(blk, k), acc)
        return acc

    def collapse(wide):
        hi = wide.astype(BF16)
        lo = (wide - hi.astype(F32)).astype(BF16)
        return _dot(hi, in_head, NN) + _dot(lo, in_head, NN)

    return lane_k, col, expand, collapse


def _head_block(k):
    per_block = LANES // SSM_HEAD_DIM
    lane = lax.broadcasted_iota(jnp.int32, (1, LANES), 1)
    lo = (k % per_block) * SSM_HEAD_DIM
    return slice((k // per_block) * LANES, (k // per_block + 1) * LANES), (lane >= lo) & (lane < lo + SSM_HEAD_DIM)


def _by_block(pieces, n_blocks, like):
    zero = jnp.zeros((like.shape[0], LANES), F32)
    return jnp.concatenate([pieces.get(p, zero) for p in range(n_blocks)], axis=1)


def _ssd_specs(l, gw, n, n_xs_blocks):
    g_axis = SSM_GROUPS
    return dict(
        xs=lambda cm: pl.BlockSpec((l, gw), lambda g, c: (cm(c), g)),
        b=lambda cm: pl.BlockSpec((l, n), lambda g, c: (cm(c), n_xs_blocks + g)),
        c=lambda cm: pl.BlockSpec((l, n), lambda g, c: (cm(c), n_xs_blocks + g_axis + g)),
        col=lambda cm: pl.BlockSpec((None, l, LANES), lambda g, c: (g, cm(c), 0)),
        row=lambda cm: pl.BlockSpec((None, 8, l), lambda g, c: (g, 0, cm(c))),
        state=lambda cm: pl.BlockSpec((None, None, n, gw), lambda g, c: (cm(c), g, 0, 0)),
    )


def _ssd_fwd(xbc, dt_g, ac_g, act_g, d_inner, n, kh, name, beside=None):
    s = xbc.shape[0]
    l, g_n = SSD_CHUNK, SSM_GROUPS
    gw, nc = d_inner // g_n, s // l
    sp = _ssd_specs(l, gw, n, d_inner // n)
    fwd = lambda c: c

    def kern(xs_ref, b_ref, c_ref, dt_ref, ac_ref, act_ref, y_ref, s0_ref, st):
        @pl.when(pl.program_id(1) == 0)
        def _():
            st[...] = jnp.zeros_like(st)

        s0 = st[...]
        s0_ref[...] = s0
        xs, ac = xs_ref[...], ac_ref[...]
        _, col, expand, _ = _ssd_pieces(l, gw)
        ace = expand(ac)
        x = xs * expand(dt_ref[...])
        xb, bb, cb_ = x.astype(BF16), b_ref[...].astype(BF16), c_ref[...].astype(BF16)
        cb = _dot(cb_, bb, NT)
        ri = lax.broadcasted_iota(jnp.int32, (l, l), 0)
        ci = lax.broadcasted_iota(jnp.int32, (l, l), 1)
        causal = ri >= ci
        y_diag = {}
        for k in range(kh):
            seg = col(ac, k) - act_ref[pl.ds(k, 1), :]
            m = jnp.where(causal, cb * jnp.exp(jnp.where(causal, seg, 0.0)), 0.0)
            blk, mine = _head_block(k)
            yk = _dot(m.astype(BF16), xb[:, blk], NN)
            y_diag[blk.start // LANES] = jnp.where(mine, yk, y_diag.get(blk.start // LANES, 0.0))
        y_ref[...] = _dot(cb_, s0.astype(BF16), NN) * jnp.exp(ace) + _by_block(y_diag, gw // LANES, xs)
        row_w = lax.broadcasted_iota(jnp.int32, (l, gw), 0)
        alast = jnp.sum(jnp.where(row_w == l - 1, ace, 0.0), axis=0, keepdims=True)
        st[...] = s0 * jnp.exp(alast) + _dot(bb, (jnp.exp(alast - ace) * x).astype(BF16), TN)

    return _host(
        kern, beside, name=name, grid=(g_n, nc),
        in_specs=[sp['xs'](fwd), sp['b'](fwd), sp['c'](fwd), sp['col'](fwd), sp['col'](fwd), sp['row'](fwd)],
        out_specs=[pl.BlockSpec((l, gw), lambda g, c: (c, g)), sp['state'](fwd)],
        out_shape=[jax.ShapeDtypeStruct((s, d_inner), F32), jax.ShapeDtypeStruct((nc, g_n, n, gw), F32)],
        scratch_shapes=[pltpu.VMEM((n, gw), F32)], semantics=("arbitrary", "arbitrary"),
        args=[xbc, xbc, xbc, dt_g, ac_g, act_g])


def _ssd_bwd(xbc, dt_g, ac_g, act_g, s0_all, dy, dxs_skip, d_inner, n, kh, name, beside=None):
    s = xbc.shape[0]
    l, g_n = SSD_CHUNK, SSM_GROUPS
    gw, nc = d_inner // g_n, s // l
    sp = _ssd_specs(l, gw, n, d_inner // n)
    rev = lambda c: nc - 1 - c

    def kern(xs_ref, b_ref, c_ref, dt_ref, ac_ref, act_ref, s0_ref, dy_ref, skip_ref,
             dxs_ref, db_ref, dc_ref, ddt_ref, dacc_ref, dacr_ref, dst):
        @pl.when(pl.program_id(1) == 0)
        def _():
            dst[...] = jnp.zeros_like(dst)

        dsn = dst[...]
        s0 = s0_ref[...]
        xs, ac, dy = xs_ref[...], ac_ref[...], dy_ref[...]
        lane_k, col, expand, collapse = _ssd_pieces(l, gw)
        ace, dte = expand(ac), expand(dt_ref[...])
        x = xs * dte
        xb, bb, cb_ = x.astype(BF16), b_ref[...].astype(BF16), c_ref[...].astype(BF16)
        s0b, dsnb, dyb = s0.astype(BF16), dsn.astype(BF16), dy.astype(BF16)
        cb = _dot(cb_, bb, NT)
        ri = lax.broadcasted_iota(jnp.int32, (l, l), 0)
        ci = lax.broadcasted_iota(jnp.int32, (l, l), 1)
        causal = ri >= ci
        row_w = lax.broadcasted_iota(jnp.int32, (l, gw), 0)
        e = jnp.exp(ace)
        alast = jnp.sum(jnp.where(row_w == l - 1, ace, 0.0), axis=0, keepdims=True)
        gdec = jnp.exp(alast)
        wt = jnp.exp(alast - ace)
        cs = _dot(cb_, s0b, NN)
        dcs = (dy * e).astype(BF16)
        d_c = _dot(dcs, s0b, NT)
        ds_off = _dot(cb_, dcs, TN)
        dace = dy * cs * e
        dalast = jnp.sum(dsn * s0, axis=0, keepdims=True) * gdec
        z = wt * x
        dz = _dot(bb, dsnb, NN)
        d_b = _dot(z.astype(BF16), dsnb, NT)
        dx = dz * wt
        t = dz * z
        dalast = dalast + jnp.sum(t, axis=0, keepdims=True)
        dace = dace - t
        dcb = jnp.zeros((l, l), F32)
        dac_col = jnp.zeros((l, LANES), F32)
        dx_diag = {}
        for k in range(kh):
            seg = col(ac, k) - act_ref[pl.ds(k, 1), :]
            dk = jnp.exp(jnp.where(causal, seg, 0.0))
            mk = jnp.where(causal, cb * dk, 0.0)
            blk, mine = _head_block(k)
            dxk = _dot(mk.astype(BF16), dyb[:, blk], TN)
            dx_diag[blk.start // LANES] = jnp.where(mine, dxk, dx_diag.get(blk.start // LANES, 0.0))
            dmk = _dot(jnp.where(mine, dy[:, blk], 0.0).astype(BF16), xb[:, blk], NT)
            dcb = dcb + jnp.where(causal, dmk * dk, 0.0)
            dseg = dmk * mk
            dac_col = jnp.where(lane_k == k, jnp.sum(dseg, axis=1, keepdims=True), dac_col)
            dacr_ref[pl.ds(k, 1), :] = -jnp.sum(dseg, axis=0, keepdims=True)
        for k in range(kh, 8):
            dacr_ref[pl.ds(k, 1), :] = jnp.zeros((1, l), F32)
        dx = dx + _by_block(dx_diag, gw // LANES, xs)
        dcbb = dcb.astype(BF16)
        dc_ref[...] = d_c + _dot(dcbb, bb, NN)
        db_ref[...] = d_b + _dot(dcbb, cb_, TN)
        dace = jnp.where(row_w == l - 1, dace + dalast, dace)
        dacc_ref[...] = dac_col + collapse(dace)
        ddt_ref[...] = collapse(dx * xs)
        dxs_ref[...] = dx * dte + skip_ref[...]
        dst[...] = dsn * gdec + ds_off

    return _host(
        kern, beside, name=name, grid=(g_n, nc),
        in_specs=[sp['xs'](rev), sp['b'](rev), sp['c'](rev), sp['col'](rev), sp['col'](rev), sp['row'](rev),
                  sp['state'](rev), pl.BlockSpec((l, gw), lambda g, c: (rev(c), g)),
                  pl.BlockSpec((l, gw), lambda g, c: (rev(c), g))],
        out_specs=[pl.BlockSpec((l, gw), lambda g, c: (rev(c), g)), pl.BlockSpec((l, n), lambda g, c: (rev(c), g)),
                   pl.BlockSpec((l, n), lambda g, c: (rev(c), g)), sp['col'](rev), sp['col'](rev), sp['row'](rev)],
        out_shape=[jax.ShapeDtypeStruct((s, d_inner), F32), jax.ShapeDtypeStruct((s, g_n * n), F32),
                   jax.ShapeDtypeStruct((s, g_n * n), F32), jax.ShapeDtypeStruct((g_n, s, LANES), F32),
                   jax.ShapeDtypeStruct((g_n, s, LANES), F32), jax.ShapeDtypeStruct((g_n, 8, s), F32)],
        scratch_shapes=[pltpu.VMEM((n, gw), F32)], semantics=("arbitrary", "arbitrary"),
        args=[xbc, xbc, xbc, dt_g, ac_g, act_g, s0_all, dy, dxs_skip])


def _gate(y, xs, z, d_exp, gw):
    t = (y + xs * d_exp) * (z * _sigmoid(z))
    width = t.shape[1]
    gsz = width // SSM_GROUPS
    lane = lax.broadcasted_iota(jnp.int32, t.shape, 1)
    t2 = t * t
    scale = jnp.zeros_like(t)
    for g in range(SSM_GROUPS):
        in_g = (lane >= g * gsz) & (lane < (g + 1) * gsz)
        ms = jnp.sum(jnp.where(in_g, t2, 0.0), axis=1, keepdims=True) * (1.0 / gsz)
        scale = jnp.where(in_g, lax.rsqrt(ms + EPS), scale)
    return t * scale * gw


def _gate_fwd(y, xbc, z, d_exp, gw, d_inner, name, beside=None):
    res = _rowwise(_gate, name, [y, (xbc, d_inner, 0), z], [d_exp, gw], outs=[(d_inner, BF16)], tm=128, beside=beside)
    return res[0] if beside is None else (res[0][0], res[1])


def _gate_bwd(y, xbc, z, d_exp, gw, dyn, d_inner, name):
    def fn(y, xs, z, dyn, d_exp, gw):
        _, vjp = jax.vjp(_gate, y, xs, z, d_exp, gw)
        return vjp(dyn)

    return _rowwise(fn, name, [y, (xbc, d_inner, 0), z, dyn], [d_exp, gw],
                    outs=[(d_inner, F32), (d_inner, F32), (d_inner, BF16)], accs=[d_exp.shape, gw.shape], tm=64)


def _forget_fwd(fraw, b_f, name):
    def kern(f_ref, b_ref, o_ref):
        o_ref[...] = _cumsum_rows(-_softplus(-(f_ref[...] + b_ref[...])))

    return pl.pallas_call(kern, name=name, out_shape=jax.ShapeDtypeStruct(fraw.shape, F32),
                          compiler_params=_params())(fraw, b_f)


def _forget_bwd(fraw, b_f, dcum, name):
    def kern(f_ref, b_ref, d_ref, df_ref, db_ref):
        df = _cumsum_rows(d_ref[...], reverse=True) * _sigmoid(-(f_ref[...] + b_ref[...]))
        df_ref[...] = df.astype(df_ref.dtype)
        db_ref[...] = jnp.sum(df, axis=0, keepdims=True)

    return pl.pallas_call(
        kern, name=name,
        out_shape=[jax.ShapeDtypeStruct(fraw.shape, BF16), jax.ShapeDtypeStruct((1, fraw.shape[1]), F32)],
        compiler_params=_params())(fraw, b_f, dcum)


ATT_BLOCK = 512


def _attn_fwd(q, k, v, name, beside=None):
    s, hd = v.shape
    h_n, d = hd // ATT_HEAD_DIM, ATT_HEAD_DIM
    tb = min(ATT_BLOCK, s)
    nb = s // tb

    def kern(q_ref, k_ref, v_ref, o_ref, lse_ref):
        i = pl.program_id(1)
        qq = q_ref[...]
        rowpos = i * tb + lax.broadcasted_iota(jnp.int32, (tb, tb), 0)
        coli = lax.broadcasted_iota(jnp.int32, (tb, tb), 1)

        def step(j, carry, diagonal):
            m, l_, acc = carry
            ks = pl.ds(pl.multiple_of(j * tb, tb), tb)
            sc = _dot(qq, k_ref[ks, :], NT)
            if diagonal:
                sc = jnp.where(j * tb + coli <= rowpos, sc, NEG)
            mn = jnp.maximum(m, jnp.max(sc, axis=1, keepdims=True))
            p = jnp.exp(sc - mn)
            alpha = jnp.exp(m - mn)
            l_ = alpha * l_ + jnp.sum(p, axis=1, keepdims=True)
            acc = alpha * acc + _dot(p.astype(BF16), v_ref[ks, :], NN)
            return mn, l_, acc

        init = (jnp.full((tb, 1), NEG, F32), jnp.zeros((tb, 1), F32), jnp.zeros((tb, d), F32))
        below = lax.fori_loop(0, i, lambda j, carry: step(j, carry, False), init)
        m, l_, acc = step(i, below, True)
        o_ref[...] = (acc / l_).astype(o_ref.dtype)
        lse_ref[...] = jnp.broadcast_to(m + jnp.log(l_), (tb, d))

    return _host(
        kern, beside, name=name, grid=(h_n, nb),
        in_specs=[pl.BlockSpec((tb, 2 * d), lambda h, i: (i, h)), pl.BlockSpec((s, 2 * d), lambda h, i: (0, h)),
                  pl.BlockSpec((s, d), lambda h, i: (0, h))],
        out_specs=[pl.BlockSpec((tb, d), lambda h, i: (i, h)), pl.BlockSpec((tb, d), lambda h, i: (i, h))],
        out_shape=[jax.ShapeDtypeStruct((s, hd), BF16), jax.ShapeDtypeStruct((s, hd), F32)],
        scratch_shapes=[], semantics=("parallel", "arbitrary"), args=[q, k, v])


def _attn_bwd(q, k, v, do, delta_rep, name, beside=None):
    s, hd = v.shape
    h_n, d = hd // ATT_HEAD_DIM, ATT_HEAD_DIM
    tb = min(ATT_BLOCK, s)
    nb = s // tb
    scale = d ** -0.5

    def kern(q_ref, do_ref, k_ref, v_ref, dl_ref, dq_ref, dk_ref, dv_ref, dcq_ref, dck_ref):
        j = pl.program_id(1)

        @pl.when(j == 0)
        def _():
            dq_ref[...] = jnp.zeros_like(dq_ref)
            dcq_ref[...] = jnp.zeros_like(dcq_ref)

        kj, vj = k_ref[...], v_ref[...]
        colpos = j * tb + lax.broadcasted_iota(jnp.int32, (tb, tb), 1)
        rowi = lax.broadcasted_iota(jnp.int32, (tb, tb), 0)

        def step(i, carry, diagonal):
            dk, dv, dck = carry
            rs = pl.ds(pl.multiple_of(i * tb, tb), tb)
            qi, doi = q_ref[rs, :], do_ref[rs, :]
            dl = jnp.max(dl_ref[rs, :], axis=1, keepdims=True)
            sc = _dot(qi, kj, NT)
            p = jnp.exp(jnp.where(colpos <= i * tb + rowi, sc, NEG) if diagonal else sc)
            dp = _dot(doi, vj, NT)
            ds = p * (dp - dl)
            dsb = ds.astype(BF16)
            dv = dv + _dot(p.astype(BF16), doi, TN)
            dk = dk + _dot(dsb, qi[:, :d], TN)
            dq_ref[rs, :] += _dot(dsb, kj[:, :d], NN) * scale
            dcq_ref[rs, :] += jnp.broadcast_to(jnp.sum(ds, axis=1, keepdims=True), (tb, d))
            return dk, dv, dck - jnp.sum(ds, axis=0, keepdims=True)

        init = (jnp.zeros((tb, d), F32), jnp.zeros((tb, d), F32), jnp.zeros((1, tb), F32))
        dk, dv, dck = lax.fori_loop(j + 1, nb, lambda i, carry: step(i, carry, False), step(j, init, True))
        dk_ref[...] = dk
        dv_ref[...] = dv.astype(dv_ref.dtype)
        dck_ref[...] = dck

    whole = pl.BlockSpec((s, d), lambda h, j: (0, h))
    blk = pl.BlockSpec((tb, d), lambda h, j: (j, h))
    ckb = pl.BlockSpec((None, None, 1, tb), lambda h, j: (h, j, 0, 0))
    return _host(
        kern, beside, name=name, grid=(h_n, nb),
        in_specs=[pl.BlockSpec((s, 2 * d), lambda h, j: (0, h)), whole, pl.BlockSpec((tb, 2 * d), lambda h, j: (j, h)),
                  blk, whole],
        out_specs=[whole, blk, blk, whole, ckb],
        out_shape=[jax.ShapeDtypeStruct((s, hd), F32), jax.ShapeDtypeStruct((s, hd), F32),
                   jax.ShapeDtypeStruct((s, hd), BF16), jax.ShapeDtypeStruct((s, hd), F32),
                   jax.ShapeDtypeStruct((h_n, nb, 1, tb), F32)],
        scratch_shapes=[], semantics=("arbitrary", "arbitrary"), args=[q, do, k, v, delta_rep])


def _adamw(w, g, m, v, name):
    cols = w.shape[1]
    tm = _tile(w.shape[0], 128) if w.shape[0] % 128 == 0 else w.shape[0]
    return _rowwise(_adamw_math, name, [w, g, m, v], outs=[(cols, F32)] * 3, tm=tm)


def _sum_arrays(arrs, out_dtype, name):
    def fn(*xs):
        acc = xs[0].astype(F32)
        for x in xs[1:]:
            acc = acc + x.astype(F32)
        return acc

    tm = PACK_ROWS if arrs[0].shape[0] % PACK_ROWS == 0 else arrs[0].shape[0]
    return _rowwise(fn, name, list(arrs), outs=[(arrs[0].shape[1], out_dtype)], tm=tm)[0]


def _half_tile(rows):
    for t in (256, 176, 128, 64, 32, 16):
        if rows % t == 0:
            return t
    return rows


def _sum_half(g, got, core, name):
    n_chip, hr, cols = got.shape
    tm = _half_tile(hr)
    nt = hr // tm

    def kern(c_ref, g_ref, a_ref, o_ref):
        o_ref[...] = (g_ref[...].astype(F32) + a_ref[...].astype(F32)).astype(o_ref.dtype)

    grid_spec = pltpu.PrefetchScalarGridSpec(
        num_scalar_prefetch=1, grid=(n_chip * nt,),
        in_specs=[pl.BlockSpec((tm, cols), lambda r, c: (((r // nt) * 2 + c[0]) * nt + r % nt, 0)),
                  pl.BlockSpec((tm, cols), lambda r, c: (r, 0))],
        out_specs=pl.BlockSpec((tm, cols), lambda r, c: (r, 0)))
    out = pl.pallas_call(
        kern, name=name, grid_spec=grid_spec, out_shape=jax.ShapeDtypeStruct((n_chip * hr, cols), BF16),
        compiler_params=_params(("arbitrary",)))(core.reshape(1), g.reshape(-1, cols), got.reshape(-1, cols))
    return out.reshape(n_chip, hr, cols)


def _sum_parts(own, parts, chip, name):
    n_parts, hr, cols = parts.shape
    tm = _half_tile(hr)

    def kern(s_ref, t_ref, p_ref, o_ref):
        acc = t_ref[...].astype(F32)
        for j in range(n_parts):
            acc = acc + p_ref[j].astype(F32)
        o_ref[...] = acc

    grid_spec = pltpu.PrefetchScalarGridSpec(
        num_scalar_prefetch=1, grid=(hr // tm,),
        in_specs=[pl.BlockSpec((None, tm, cols), lambda r, s: (s[0], r, 0)),
                  pl.BlockSpec((n_parts, tm, cols), lambda r, s: (0, r, 0))],
        out_specs=pl.BlockSpec((tm, cols), lambda r, s: (r, 0)))
    return pl.pallas_call(
        kern, name=name, grid_spec=grid_spec, out_shape=jax.ShapeDtypeStruct((hr, cols), F32),
        compiler_params=_params(("arbitrary",)))(chip.reshape(1), own, parts)


def _adamw_math(w, g, m, v):
    m = ADAM_B1 * m + (1.0 - ADAM_B1) * g
    v = ADAM_B2 * v + (1.0 - ADAM_B2) * (g * g)
    m_hat = m * (1.0 / (1.0 - ADAM_B1 ** ADAM_STEP))
    v_hat = v * (1.0 / (1.0 - ADAM_B2 ** ADAM_STEP))
    return -ADAM_LR * (m_hat / (jnp.sqrt(v_hat) + ADAM_EPS) + ADAM_WD * w), m, v


def _adamw_big(w, m, v, mine, theirs, core, layer, prev, name):
    n_layers, rows, cols = w.shape
    hr = rows // 2
    tm = next(t for t in (128, 64, 32, 16, 8) if hr % t == 0)
    nt = hr // tm

    def kern(s_ref, w_ref, m_ref, v_ref, a_ref, b_ref, *rest):
        g_ref, d_ref, nm_ref, nv_ref = rest[-4:]
        g = jnp.where(pl.program_id(0) // nt == s_ref[0], a_ref[...], b_ref[...])
        g_ref[...] = g
        d_ref[...], nm_ref[...], nv_ref[...] = _adamw_math(w_ref[...], g, m_ref[...], v_ref[...])

    lyr = pl.BlockSpec((None, tm, cols), lambda r, s: (layer, r, 0))
    half = pl.BlockSpec((tm, cols), lambda r, s: (r % nt, 0))
    passed = [] if prev is None else list(prev)
    grid_spec = pltpu.PrefetchScalarGridSpec(
        num_scalar_prefetch=1, grid=(rows // tm,),
        in_specs=[lyr, lyr, lyr, half, half] + [pl.BlockSpec(memory_space=pl.ANY)] * len(passed), out_specs=[lyr] * 4)
    return pl.pallas_call(
        kern, name=name, grid_spec=grid_spec, out_shape=[jax.ShapeDtypeStruct(w.shape, F32)] * 4,
        input_output_aliases={6 + i: i for i in range(len(passed))},
        compiler_params=_params(("arbitrary",)))(core.reshape(1), w, m, v, mine, theirs, *passed)


def _cast_into_slot(src, slot, name, beside=None):
    rows, cols = src.shape
    tm = _half_tile(rows)

    def kern(s_ref, x_ref, o_ref):
        o_ref[...] = x_ref[...].astype(o_ref.dtype)

    res, extra = _host(
        kern, beside, name=name, grid=(rows // tm,), in_specs=[pl.BlockSpec((tm, cols), lambda r, s: (r, 0))],
        out_specs=pl.BlockSpec((None, tm, cols), lambda r, s: (s[0], r, 0)),
        out_shape=jax.ShapeDtypeStruct((N_CHIPS, rows, cols), BF16), scratch_shapes=[], semantics=("arbitrary",),
        args=[src], prefetch=[slot.reshape(1)])
    return res if beside is None else (res, extra)


FLIP_C = [(0, 0, 1)]
FLIP_XY = [(1, 0, 0), (0, 1, 0), (1, 1, 0)]
FLIP_ALL = [(fx, fy, fc) for fx in (0, 1) for fy in (0, 1) for fc in (0, 1) if (fx, fy, fc) != (0, 0, 0)]


def _chip(dev):
    return 2 * dev[0] + dev[1]


def _device(dev):
    return 4 * dev[0] + 2 * dev[1] + dev[2]


def _exchange(name, srcs, dst_shapes, rels, src_view, dst_view, own_view=None, in_place=False):
    n, n_rel = len(srcs), len(rels)

    def body(*refs):
        src_refs, dst_refs = refs[:n], refs[n:2 * n]
        send_sems, recv_sems, own_sems = refs[2 * n:]
        me = (lax.axis_index("x"), lax.axis_index("y"), lax.axis_index("c"))
        peers = [tuple(1 - m if f else m for m, f in zip(me, rel)) for rel in rels]

        def copy(i, k, sender, receiver):
            return pltpu.make_async_remote_copy(
                src_ref=src_view(i, src_refs[i], sender, receiver), dst_ref=dst_view(i, dst_refs[i], sender, k),
                send_sem=send_sems.at[i * n_rel + k], recv_sem=recv_sems.at[i * n_rel + k], device_id=receiver,
                device_id_type=pl.DeviceIdType.MESH)

        sends = [copy(i, k, me, peer) for i in range(n) for k, peer in enumerate(peers)]
        for cp in sends:
            cp.start()
        mine = []
        if own_view is not None:
            for i in range(n):
                frm, to = own_view(i, src_refs[i], dst_refs[i], me)
                mine.append(pltpu.make_async_copy(frm, to, own_sems.at[i]))
                mine[-1].start()
        for i in range(n):
            for k, peer in enumerate(peers):
                copy(i, k, peer, me).wait_recv()
        for cp in sends:
            cp.wait_send()
        for cp in mine:
            cp.wait()

    any_spec = pl.BlockSpec(memory_space=pl.ANY)
    return pl.pallas_call(
        body, name=name, out_shape=[jax.ShapeDtypeStruct(s, d) for s, d in dst_shapes],
        in_specs=[any_spec] * n, out_specs=[any_spec] * n,
        input_output_aliases={i: i for i in range(n)} if in_place else {},
        scratch_shapes=[pltpu.SemaphoreType.DMA((n * n_rel,)), pltpu.SemaphoreType.DMA((n * n_rel,)),
                        pltpu.SemaphoreType.DMA((n,))])(*srcs)


def _gather_plan(bufs, rows_of=None):
    n = len(bufs)
    rows_of = rows_of or [(0, b.shape[1]) for b in bufs]
    starts = [r0 for r0, _ in rows_of]
    halves = [nr // 2 for _, nr in rows_of]

    def tools(src_refs, dst_refs, sems):
        ici_send, ici_recv, d2d_send, d2d_recv = sems
        x, y, c = lax.axis_index("x"), lax.axis_index("y"), lax.axis_index("c")
        sibling = (x, y, 1 - c)
        chips = [(1 - x, y), (x, 1 - y), (1 - x, 1 - y)]

        def rows(i, chip, core):
            return dst_refs[i].at[2 * chip[0] + chip[1], pl.ds(starts[i] + core * halves[i], halves[i]), :]

        def over_ici(i, k, src, chip_from, to):
            return pltpu.make_async_remote_copy(
                src_ref=src, dst_ref=rows(i, chip_from, c), send_sem=ici_send.at[3 * i + k],
                recv_sem=ici_recv.at[3 * i + k], device_id=to, device_id_type=pl.DeviceIdType.MESH)

        def over_d2d(i, k, core):
            return pltpu.make_async_remote_copy(
                src_ref=rows(i, chips[k], core), dst_ref=rows(i, chips[k], core), send_sem=d2d_send.at[3 * i + k],
                recv_sem=d2d_recv.at[3 * i + k], device_id=sibling, device_id_type=pl.DeviceIdType.MESH)

        def my_send(i, k):
            my_half = src_refs[i].at[2 * x + y, pl.ds(starts[i] + c * halves[i], halves[i]), :]
            return over_ici(i, k, my_half, (x, y), (*chips[k], c))

        def my_arrival(i, k):
            return over_ici(i, k, rows(i, chips[k], c), chips[k], (x, y, c))

        return c, over_d2d, my_send, my_arrival

    def start(src_refs, dst_refs, sems):
        _, _, my_send, _ = tools(src_refs, dst_refs, sems)
        for i in range(n):
            for k in range(3):
                my_send(i, k).start()

    def finish(src_refs, dst_refs, sems):
        c, over_d2d, my_send, my_arrival = tools(src_refs, dst_refs, sems)
        passed = []
        for i in range(n):
            for k in range(3):
                my_arrival(i, k).wait_recv()
                passed.append(over_d2d(i, k, c))
                passed[-1].start()
        for i in range(n):
            for k in range(3):
                over_d2d(i, k, 1 - c).wait_recv()
        for i in range(n):
            for k in range(3):
                my_send(i, k).wait_send()
        for cp in passed:
            cp.wait_send()

    return _Beside(bufs, [(b.shape, b.dtype, i) for i, b in enumerate(bufs)], [3 * n] * 4, start, finish)


def _scatter_plan(sums):
    n = len(sums)

    def copy(i, k, src_refs, dst_refs, sems, sender, receiver):
        return pltpu.make_async_remote_copy(
            src_ref=src_refs[i].at[_chip(receiver)], dst_ref=dst_refs[i].at[k], send_sem=sems[0].at[3 * i + k],
            recv_sem=sems[1].at[3 * i + k], device_id=receiver, device_id_type=pl.DeviceIdType.MESH)

    def each(fn, src_refs, dst_refs, sems, outgoing):
        me = (lax.axis_index("x"), lax.axis_index("y"), lax.axis_index("c"))
        for i in range(n):
            for k, rel in enumerate(FLIP_XY):
                peer = tuple(1 - m if f else m for m, f in zip(me, rel))
                fn(copy(i, k, src_refs, dst_refs, sems, *((me, peer) if outgoing else (peer, me))))

    def start(src_refs, dst_refs, sems):
        each(lambda cp: cp.start(), src_refs, dst_refs, sems, True)

    def finish(src_refs, dst_refs, sems):
        each(lambda cp: cp.wait_recv(), src_refs, dst_refs, sems, False)
        each(lambda cp: cp.wait_send(), src_refs, dst_refs, sems, True)

    return _Beside(sums, [((3,) + s.shape[1:], s.dtype, None) for s in sums], [3 * n] * 2, start, finish)


def _pack(parts, dtype, multiple):
    flat = jnp.concatenate([p.reshape(-1).astype(dtype) for p in parts])
    pad = (-flat.shape[0]) % multiple
    return jnp.pad(flat, (0, pad)) if pad else flat


def _unpack_shards(packs, names, shapes, axes):
    out, off = {}, 0
    for nme in names:
        sz = math.prod(shapes[nme])
        out[nme] = jnp.concatenate([packs[j, off:off + sz].reshape(shapes[nme]) for j in range(N_CHIPS)], axis=axes[nme])
        off += sz
    return out


def _shards_of(full, axis):
    sz = full.shape[axis] // N_CHIPS
    return [lax.slice_in_dim(full, j * sz, (j + 1) * sz, axis=axis) for j in range(N_CHIPS)]


def kernel(x, a_norm_w, a_in_proj, a_conv_w, a_conv_b, a_dt_bias, a_A_log, a_D, a_gnorm_w, a_out_proj, kv_norm_w, w_kvf, b_f, k_norm_w, b_norm_w, w_q, q_norm_w, w_o, ffn_norm_w, w_gate_up, w_down, loss_target, m_a_norm_w, m_a_in_proj, m_a_conv_w, m_a_conv_b, m_a_dt_bias, m_a_A_log, m_a_D, m_a_gnorm_w, m_a_out_proj, m_kv_norm_w, m_w_kvf, m_b_f, m_k_norm_w, m_b_norm_w, m_w_q, m_q_norm_w, m_w_o, m_ffn_norm_w, m_w_gate_up, m_w_down, v_a_norm_w, v_a_in_proj, v_a_conv_w, v_a_conv_b, v_a_dt_bias, v_a_A_log, v_a_D, v_a_gnorm_w, v_a_out_proj, v_kv_norm_w, v_w_kvf, v_b_f, v_k_norm_w, v_b_norm_w, v_w_q, v_q_norm_w, v_w_o, v_ffn_norm_w, v_w_gate_up, v_w_down):
    args = locals()
    w = {n: args[n] for n in WEIGHTS}
    mom = {n: args['m_' + n] for n in WEIGHTS}
    var = {n: args['v_' + n] for n in WEIGHTS}
    shapes = {n: w[n].shape for n in WEIGHTS}

    cx, cy, cc = lax.axis_index("x"), lax.axis_index("y"), lax.axis_index("c")
    my_chip = 2 * cx + cy

    xs_in = x[0]
    target = loss_target[0]
    s_len, d_model = xs_in.shape
    n_heads_ssm = a_dt_bias.shape[-1]
    d_inner = n_heads_ssm * SSM_HEAD_DIM
    d_xbc = a_conv_w.shape[-1] * N_CHIPS
    d_state = (d_xbc - d_inner) // (2 * SSM_GROUPS)
    kh = n_heads_ssm // SSM_GROUPS
    n_att = b_f.shape[0]
    d_att = n_att * ATT_HEAD_DIM
    d_ff = w_down.shape[1] * N_CHIPS

    small_names = list(SMALL_SHARDED)
    sp = _pack([w[n] for n in small_names], F32, PACK_COLS).reshape(-1, PACK_COLS)
    sp_all = _exchange("gather_small", [sp], [((N_CHIPS,) + sp.shape, F32)], FLIP_XY,
                       lambda i, r, me, peer: r, lambda i, r, sender, k: r.at[_chip(sender)],
                       lambda i, s, d, me: (s, d.at[_chip(me)]))[0]
    full = _unpack_shards(sp_all.reshape(N_CHIPS, -1), small_names, shapes, SMALL_SHARDED)
    for n in SMALL:
        if n not in full:
            full[n] = w[n]

    big2d = {'in': a_in_proj[0], 'out': a_out_proj[0], 'gu0': w_gate_up[0], 'dn0': w_down[0], 'kvf': w_kvf,
             'q': w_q[0], 'o': w_o[0], 'gu1': w_gate_up[1], 'dn1': w_down[1]}
    big_keys = list(big2d)
    norm_a, gnorm = full['a_norm_w'], full['a_gnorm_w']
    gathered = {}

    in_buf = _cast_into_slot(big2d['in'], my_chip, "cast_in")
    later = [k for k in big_keys if k != 'in']
    moved = [math.prod(big2d[k].shape) for k in later] + [s_len * d_model]
    units = big2d['in'].shape[0] // 32
    share = [m * units / sum(moved) for m in moved]
    piece = [int(sh) for sh in share]
    for i in sorted(range(len(share)), key=lambda i: share[i] - piece[i], reverse=True)[:units - sum(piece)]:
        piece[i] += 1
    slot, row0 = {}, 0
    for k, rows_k in zip(later + [None], [32 * p for p in piece]):
        plan = _gather_plan([in_buf], [(row0, rows_k)]) if rows_k else None
        row0 += rows_k
        if k is None:
            n1 = _rms_fwd(xs_in, norm_a, "norm_a", beside=plan)
        else:
            slot[k] = _cast_into_slot(big2d[k], my_chip, "cast_" + k, beside=plan)
        if plan is not None and k is None:
            n1, (in_buf,) = n1
        elif plan is not None:
            slot[k], (in_buf,) = slot[k]
    gathered['in'] = in_buf

    def gather(keys):
        return _gather_plan([slot[k] for k in keys])

    def pad_cols(a, width=LANES):
        return jnp.pad(a, ((0, 0), (0, width - a.shape[1])))

    w_in = jnp.concatenate([gathered['in'][j] for j in range(N_CHIPS)], axis=1)
    w_z, w_xbc, w_dt = w_in[:, :d_inner], w_in[:, d_inner:d_inner + d_xbc], pad_cols(w_in[:, d_inner + d_xbc:])
    conv_w, conv_b = full['a_conv_w'][0], full['a_conv_b']
    dt_bias, a_log = pad_cols(a_dt_bias), pad_cols(a_A_log)
    d_exp = jnp.repeat(a_D, SSM_HEAD_DIM, axis=1)
    kv_nw, b_nw = kv_norm_w.reshape(1, -1), b_norm_w
    k_nw, q_nw = k_norm_w.reshape(1, -1), q_norm_w
    bf_pad = pad_cols(b_f.reshape(1, -1))
    ffn_nw = [ffn_norm_w[i:i + 1] for i in range(2)]

    def to_groups(a):
        g = a[:, :n_heads_ssm].reshape(s_len, SSM_GROUPS, kh).transpose(1, 0, 2)
        return jnp.pad(g, ((0, 0), (0, 0), (0, LANES - kh)))

    def to_groups_t(a):
        g = a[:, :n_heads_ssm].reshape(s_len, SSM_GROUPS, kh).transpose(1, 2, 0)
        return jnp.pad(g, ((0, 0), (0, 8 - kh), (0, 0)))

    def from_groups(col, row=None):
        a = col[:, :, :kh].transpose(1, 0, 2).reshape(s_len, n_heads_ssm)
        if row is not None:
            a = a + row[:, :kh, :].transpose(2, 0, 1).reshape(s_len, n_heads_ssm)
        return pad_cols(a)

    z = _mm(n1, w_z, 'nn', "in_z")
    xbc_raw, (gathered['out'],) = _mm(n1, w_xbc, 'nn', "in_xbc", beside=gather(['out']))
    dtraw = _mm(n1, w_dt, 'nn', "in_dt")
    xbc = _conv_fwd(xbc_raw, conv_w, conv_b, "conv")
    dt, acum = _dt_fwd(dtraw, dt_bias, a_log, "dt")
    dt_g, ac_g, act_g = to_groups(dt), to_groups(acum), to_groups_t(acum)
    (y_ssd, states), (gathered['gu0'],) = _ssd_fwd(xbc, dt_g, ac_g, act_g, d_inner, d_state, kh, "ssd",
                                                   beside=gather(['gu0']))
    yn, (gathered['q'],) = _gate_fwd(y_ssd, xbc, z, d_exp, gnorm, d_inner, "gate", beside=gather(['q']))
    w_out = gathered['out'].reshape(-1, d_model)
    h1, (gathered['o'],) = _mm(yn, w_out, 'nn', "out_proj", add=xs_in, beside=gather(['o']))

    w_gu, w_dn = {}, {}

    def ffn_fwd(h, i, also):
        nrm = _rms_fwd(h, ffn_nw[i], f"ffn{i}_norm")
        w_gu[i] = gathered[f'gu{i}']
        gu = _mm(nrm, w_gu[i], 'nn', f"ffn{i}_up", out_dtype=BF16, shards=True,
                 beside=gather(also) if also else None)
        if also:
            gu, arrived = gu
            gathered.update(zip(also, arrived))
        act = _swiglu_fwd(gu, d_ff, f"ffn{i}_act")
        w_dn[i] = gathered[f'dn{i}'].reshape(-1, d_model)
        return nrm, gu, act, _mm(act, w_dn[i], 'nn', f"ffn{i}_down", add=h)

    n2, gu0, act0, h2 = ffn_fwd(h1, 0, ['dn0', 'kvf'])
    w_kvf_full = jnp.concatenate([gathered['kvf'][j] for j in range(N_CHIPS)], axis=1)
    w_k, w_v, w_f = w_kvf_full[:, :d_att], w_kvf_full[:, d_att:2 * d_att], pad_cols(w_kvf_full[:, 2 * d_att:])
    w_qm, w_om = gathered['q'].reshape(-1, d_att), gathered['o'].reshape(-1, d_model)
    nkv = _rms_fwd(h2, kv_nw, "kv_norm")
    k_raw = _mm(nkv, w_k, 'nn', "proj_k")
    v_att = _mm(nkv, w_v, 'nn', "proj_v", out_dtype=BF16)
    f_raw = _mm(nkv, w_f, 'nn', "proj_f")
    k_att = _rms_fwd(k_raw, k_nw, "k_norm")
    cum = _forget_fwd(f_raw, bf_pad, "forget")
    cum_h = cum[:, :n_att]

    def three_terms(c):
        as_bf16 = lambda t: lax.reduce_precision(t, exponent_bits=8, mantissa_bits=7)
        c1 = as_bf16(c)
        c2 = as_bf16(c - c1)
        return [t.astype(BF16) for t in (c1, c2, as_bf16(c - c1 - c2))]

    def with_terms(main, cols):
        extra = jnp.pad(jnp.stack(cols, axis=2), ((0, 0), (0, 0), (0, ATT_HEAD_DIM - len(cols))))
        return jnp.concatenate([main.reshape(s_len, n_att, ATT_HEAD_DIM), extra], axis=2).reshape(s_len, -1)

    ones_h = jnp.ones(cum_h.shape, BF16)
    k_aug = with_terms(k_att, [ones_h] * 3 + three_terms(-cum_h))
    n3 = _rms_fwd(h2, b_nw, "b_norm")
    q_raw = _mm(n3, w_qm, 'nn', "proj_q")
    q_att = _rms_fwd(q_raw, q_nw * ATT_HEAD_DIM ** -0.5, "q_norm")
    q_aug = with_terms(q_att, three_terms(cum_h) + [ones_h] * 3)
    (o_att, lse_rep), arrived = _attn_fwd(q_aug, k_aug, v_att, "attn", beside=gather(['gu1', 'dn1']))
    gathered.update(zip(['gu1', 'dn1'], arrived))
    h3 = _mm(o_att, w_om, 'nn', "proj_o", add=h2)
    n4, gu1, act1, h4 = ffn_fwd(h3, 1, [])

    def loss_fn(h, t):
        err = h - t
        sq = jnp.sum(jnp.sum(err * err, axis=1, keepdims=True), axis=0, keepdims=True)
        return err * (1.0 / d_model), err * (1.0 / d_model), sq * (0.5 / d_model)

    dh4, dh4_b, loss_part = _rowwise(loss_fn, "loss", [h4, target], outs=[(d_model, F32), (d_model, BF16)],
                                     accs=[(1, 1)])

    n_f = n_att
    g_big, chip_sums, parts = {}, {}, {}

    def col_shards(pieces):
        fullw = jnp.concatenate(pieces, axis=1)
        return fullw.reshape(fullw.shape[0], N_CHIPS, -1).transpose(1, 0, 2)

    def row_shards(a):
        return a.reshape(N_CHIPS, -1, a.shape[1])

    def scatter(keys, tag):
        g_list = [g_big[k] for k in keys]
        hrs = [g.shape[1] // 2 for g in g_list]
        got = _exchange("reduce_d2d_" + tag, g_list,
                        [((N_CHIPS, hr, g.shape[2]), BF16) for g, hr in zip(g_list, hrs)], FLIP_C,
                        lambda i, r, me, peer: r.at[:, pl.ds((1 - me[2]) * hrs[i], hrs[i]), :],
                        lambda i, r, sender, k: r)
        for k, g, a in zip(keys, g_list, got):
            chip_sums[k] = _sum_half(g, a, cc, "reduce_sum2_" + k)
        return _scatter_plan([chip_sums[k] for k in keys])

    def ffn_bwd(dh, dh_b, h, nrm, gu, act, i, hosted):
        dact = _mm(dh_b, w_dn[i], 'nt', f"ffn{i}_down_dx", beside=scatter(hosted[:1], f"a{i}") if hosted else None)
        g_dn = _mm(act, dh_b, 'tn', f"ffn{i}_down_dw", out_dtype=BF16,
                   beside=scatter(hosted[1:], f"b{i}") if hosted else None)
        if hosted:
            (dact, first), (g_dn, second) = dact, g_dn
            parts.update(zip(hosted, first + second))
        dgu = _swiglu_bwd(gu, dact, d_ff, f"ffn{i}_act_bwd")
        dn = _mm(dgu, w_gu[i], 'nt', f"ffn{i}_up_dx", shards=True)
        g_gu = _mm(nrm, dgu, 'tn', f"ffn{i}_up_dw", out_dtype=BF16, shards=True)
        dh_new, dh_new_b, g_nw = _rms_bwd(h, ffn_nw[i], dn, f"ffn{i}_norm_bwd", extra=dh, copy_bf16=True)
        return dh_new, dh_new_b, g_dn, g_gu, g_nw

    dh3, dh3_b, g_dn1, g_gu1, g_fnw1 = ffn_bwd(dh4, dh4_b, h3, n4, gu1, act1, 1, [])
    do_att = _mm(dh3_b, w_om, 'nt', "proj_o_dx", out_dtype=BF16)
    g_wo = _mm(o_att, dh3_b, 'tn', "proj_o_dw", out_dtype=BF16)
    g_big.update(gu1=g_gu1, dn1=row_shards(g_dn1), o=row_shards(g_wo))
    late_keys = ['gu1', 'dn1', 'o']
    plan_late = scatter(late_keys, "late")

    def delta_fn(a, b):
        prod = a.astype(F32) * b.astype(F32)
        return jnp.concatenate([jnp.broadcast_to(jnp.sum(prod[:, g], axis=1, keepdims=True), (a.shape[0], ATT_HEAD_DIM))
                                for g in _lane_groups(a.shape[1], ATT_HEAD_DIM)], axis=1)

    delta_rep = _rowwise(delta_fn, "attn_delta", [do_att, o_att], outs=[(d_att, F32)])[0]
    lse_h = lse_rep.reshape(s_len, n_att, ATT_HEAD_DIM)[:, :, 0]
    q_aug_b = with_terms(q_att, three_terms(cum_h - lse_h) + [ones_h] * 3)
    (dq_att, dk_att, dv_att, dcq_rep, dck), arrived = _attn_bwd(q_aug_b, k_aug, v_att, do_att, delta_rep,
                                                                "attn_bwd", beside=plan_late)
    parts.update(zip(late_keys, arrived))
    dq_raw, g_qnw = _rms_bwd(q_raw, q_nw, dq_att, "q_norm_bwd", out_dtype=BF16)
    dn3 = _mm(dq_raw, w_qm, 'nt', "proj_q_dx")
    g_wq = _mm(n3, dq_raw, 'tn', "proj_q_dw", out_dtype=BF16)
    dh2, g_bnw = _rms_bwd(h2, b_nw, dn3, "b_norm_bwd", extra=dh3)
    dk_raw, g_knw = _rms_bwd(k_raw, k_nw, dk_att, "k_norm_bwd", out_dtype=BF16)
    dcum = pad_cols(dcq_rep.reshape(s_len, n_att, ATT_HEAD_DIM)[:, :, 0] + dck.reshape(n_att, s_len).T)
    df_raw, g_bf = _forget_bwd(f_raw, bf_pad, dcum, "forget_bwd")
    dnkv = _mm(dk_raw, w_k, 'nt', "proj_k_dx")
    dnkv = _mm(dv_att, w_v, 'nt', "proj_v_dx", add=dnkv)
    dnkv = _mm(df_raw, w_f, 'nt', "proj_f_dx", add=dnkv)
    g_wk = _mm(nkv, dk_raw, 'tn', "proj_k_dw", out_dtype=BF16)
    g_wv = _mm(nkv, dv_att, 'tn', "proj_v_dw", out_dtype=BF16)
    g_wf = _mm(nkv, df_raw, 'tn', "proj_f_dw", out_dtype=BF16)
    dh2, dh2_b, g_kvnw = _rms_bwd(h2, kv_nw, dnkv, "kv_norm_bwd", extra=dh2, copy_bf16=True)
    g_big.update(q=row_shards(g_wq), kvf=col_shards([g_wk, g_wv, g_wf[:, :n_f]]))
    dh1, dh1_b, g_dn0, g_gu0, g_fnw0 = ffn_bwd(dh2, dh2_b, h1, n2, gu0, act0, 0, ['q', 'kvf'])
    g_big.update(gu0=g_gu0, dn0=row_shards(g_dn0))
    plan_ffn0 = scatter(['gu0', 'dn0'], "ffn0")

    dyn = _mm(dh1_b, w_out, 'nt', "out_proj_dx")
    g_wout = _mm(yn, dh1_b, 'tn', "out_proj_dw", out_dtype=BF16)
    dy_ssd, dxs_skip, dz, g_dexp, g_gnorm = _gate_bwd(y_ssd, xbc, z, d_exp, gnorm, dyn, d_inner, "gate_bwd")
    (dxs, d_b, d_c, ddt_g, dacc_g, dacr_g), arrived = _ssd_bwd(xbc, dt_g, ac_g, act_g, states, dy_ssd, dxs_skip,
                                                               d_inner, d_state, kh, "ssd_bwd", beside=plan_ffn0)
    parts.update(zip(['gu0', 'dn0'], arrived))
    g_big.update(out=row_shards(g_wout))
    dxbc_act = jnp.concatenate([dxs, d_b, d_c], axis=1)
    (du, g_convw, g_convb), (parts['out'],) = _conv_bwd(xbc_raw, conv_w, conv_b, dxbc_act, "conv_bwd",
                                                        beside=scatter(['out'], "out"))
    draw, g_dtb, g_alog = _dt_bwd(dtraw, dt_bias, a_log, from_groups(ddt_g), from_groups(dacc_g, dacr_g), "dt_bwd")
    g_wz = _mm(n1, dz, 'tn', "in_z_dw", out_dtype=BF16)
    g_wxbc = _mm(n1, du, 'tn', "in_xbc_dw", out_dtype=BF16)
    g_wdt = _mm(n1, draw, 'tn', "in_dt_dw", out_dtype=BF16)
    g_in = jnp.concatenate([g_wz, g_wxbc, g_wdt[:, :n_heads_ssm]], axis=1)
    g_in = g_in.reshape(2, d_model // 2, N_CHIPS, -1).transpose(0, 2, 1, 3)
    g_big.update(in0=g_in[0], in1=g_in[1])
    dn1, (parts['in0'],) = _mm(du, w_xbc, 'nt', "in_xbc_dx", beside=scatter(['in0'], "in0"))
    dn1, (parts['in1'],) = _mm(dz, w_z, 'nt', "in_z_dx", add=dn1, beside=scatter(['in1'], "in1"))
    dn1 = _mm(draw, w_dt, 'nt', "in_dt_dx", add=dn1)
    dx, g_norm_a = _rms_bwd(xs_in, norm_a, dn1, "norm_a_bwd", extra=dh1)

    g_small = {
        'a_norm_w': g_norm_a, 'a_conv_w': g_convw[None], 'a_conv_b': g_convb,
        'a_dt_bias': g_dtb[:, :n_heads_ssm], 'a_A_log': g_alog[:, :n_heads_ssm],
        'a_D': g_dexp.reshape(n_heads_ssm, SSM_HEAD_DIM).sum(axis=1).reshape(1, -1), 'a_gnorm_w': g_gnorm,
        'kv_norm_w': g_kvnw.reshape(-1), 'b_f': g_bf[0, :n_f], 'k_norm_w': g_knw.reshape(-1), 'b_norm_w': g_bnw,
        'q_norm_w': g_qnw, 'ffn_norm_w': jnp.concatenate([g_fnw0, g_fnw1], axis=0),
    }

    sg = _pack([g_small[n] for n in SMALL] + [loss_part], F32, 8 * PACK_COLS).reshape(-1, PACK_COLS)
    sg_all = _exchange("reduce_small", [sg], [((2 * N_CHIPS,) + sg.shape, F32)], FLIP_ALL,
                       lambda i, r, me, peer: r, lambda i, r, sender, k: r.at[_device(sender)],
                       lambda i, s, d, me: (s, d.at[_device(me)]))[0]
    sg_sum = _sum_arrays([sg_all[d] for d in range(2 * N_CHIPS)], F32, "reduce_small_sum").reshape(-1)
    red_small, off = {}, 0
    for n in SMALL:
        shp = g_small[n].shape
        red_small[n] = sg_sum[off:off + math.prod(shp)].reshape(shp)
        off += math.prod(shp)
    loss = sg_sum[off]

    red_keys = list(parts)
    half_sums = [_sum_parts(chip_sums[k], parts[k], my_chip, "reduce_sum4_" + k) for k in red_keys]
    others = _exchange("reduce_back", half_sums, [(h.shape, F32) for h in half_sums], FLIP_C,
                       lambda i, r, me, peer: r, lambda i, r, sender, k: r)
    mine_of, theirs_of = dict(zip(red_keys, half_sums)), dict(zip(red_keys, others))

    grads, delta, new_m, new_v = {}, {}, {}, {}
    layers_of = {'a_in_proj': ['in0', 'in1'], 'a_out_proj': ['out'], 'w_kvf': ['kvf'], 'w_q': ['q'], 'w_o': ['o'],
                 'w_gate_up': ['gu0', 'gu1'], 'w_down': ['dn0', 'dn1']}
    for n, keys in layers_of.items():
        three_d = (len(keys), math.prod(shapes[n][:-1]) // len(keys), shapes[n][-1])
        res = None
        for layer, k in enumerate(keys):
            res = _adamw_big(w[n].reshape(three_d), mom[n].reshape(three_d), var[n].reshape(three_d), mine_of[k],
                             theirs_of[k], cc, layer, res, "adamw_" + k)
        grads[n], delta[n], new_m[n], new_v[n] = [r.reshape(shapes[n]) for r in res]
    for n in SMALL:
        if n in SMALL_SHARDED:
            ax = SMALL_SHARDED[n]
            grads[n] = lax.dynamic_slice_in_dim(red_small[n], my_chip * shapes[n][ax], shapes[n][ax], axis=ax)
        else:
            grads[n] = red_small[n]

    packed = [_pack([src[n] for n in SMALL], F32, 8 * LANES).reshape(-1, LANES) for src in (w, grads, mom, var)]
    small_out = _adamw(*packed, "adamw_small")
    for store, flat in zip((delta, new_m, new_v), small_out):
        flat, off = flat.reshape(-1), 0
        for n in SMALL:
            sz = math.prod(shapes[n])
            store[n] = flat[off:off + sz].reshape(shapes[n])
            off += sz

    return (loss, dx[None], *[grads[n] for n in WEIGHTS], *[delta[n] for n in WEIGHTS],
            *[new_m[n] for n in WEIGHTS], *[new_v[n] for n in WEIGHTS])
```

```python
import functools
import math

import jax
import jax.numpy as jnp
from jax import lax
from jax.experimental import pallas as pl
from jax.experimental.pallas import tpu as pltpu

F32, BF16 = jnp.float32, jnp.bfloat16
EPS = 1e-6
SSM_HEAD_DIM = 64
SSM_GROUPS = 8
SSD_CHUNK = 128
ATT_HEAD_DIM = 128
LANES = 128
N_CHIPS = 4
NEG = -1e30
ADAM_LR, ADAM_B1, ADAM_B2, ADAM_EPS, ADAM_WD, ADAM_STEP = 0.001, 0.9, 0.999, 1e-08, 0.01, 10
VMEM_LIMIT_BYTES = 56 * 1024 * 1024
PACK_COLS = 1024
PACK_ROWS = 256
MM_K_TILES = (2816, 2048, 1408, 1024, 512, 256, 128)
MM_OPERAND_BYTES = 12 * 1024 * 1024

NN = ((1,), (0,))
NT = ((1,), (1,))
TN = ((0,), (0,))

WEIGHTS = ['a_norm_w', 'a_in_proj', 'a_conv_w', 'a_conv_b', 'a_dt_bias', 'a_A_log', 'a_D', 'a_gnorm_w', 'a_out_proj',
           'kv_norm_w', 'w_kvf', 'b_f', 'k_norm_w', 'b_norm_w', 'w_q', 'q_norm_w', 'w_o', 'ffn_norm_w', 'w_gate_up',
           'w_down']
BIG = {'a_in_proj': 2, 'a_out_proj': 1, 'w_kvf': 1, 'w_q': 1, 'w_o': 1, 'w_gate_up': 2, 'w_down': 1}
SMALL_SHARDED = {'a_norm_w': 1, 'a_conv_w': 2, 'a_conv_b': 1, 'a_gnorm_w': 1}
SMALL = [n for n in WEIGHTS if n not in BIG]


def _dot(a, b, dims):
    return lax.dot_general(a, b, (dims, ((), ())), preferred_element_type=F32)


def _params(sem=None):
    return pltpu.CompilerParams(dimension_semantics=sem, vmem_limit_bytes=VMEM_LIMIT_BYTES)


def _tile(dim, cap):
    for t in (1408, 1024, 512, 256, 128):
        if t <= cap and dim % t == 0:
            return t
    return dim


class _Beside:
    def __init__(self, operands, results, sem_sizes, start, finish):
        self.operands, self.results, self.sem_sizes = list(operands), list(results), list(sem_sizes)
        self.start, self.finish = start, finish


def _host(kern, beside, *, name, grid, in_specs, out_specs, out_shape, scratch_shapes, semantics, args, prefetch=()):
    single = not isinstance(out_shape, (list, tuple))
    out_specs = [out_specs] if single else list(out_specs)
    out_shape = [out_shape] if single else list(out_shape)
    n_pre = len(prefetch)

    def call(body, in_specs, out_specs, out_shape, scratch_shapes, semantics, aliases, args):
        if not n_pre:
            return pl.pallas_call(body, name=name, grid=grid, in_specs=list(in_specs), out_specs=list(out_specs),
                                  out_shape=out_shape, scratch_shapes=list(scratch_shapes),
                                  input_output_aliases=aliases, compiler_params=_params(semantics))(*args)
        spec = pltpu.PrefetchScalarGridSpec(num_scalar_prefetch=n_pre, grid=grid, in_specs=list(in_specs),
                                            out_specs=list(out_specs), scratch_shapes=list(scratch_shapes))
        return pl.pallas_call(body, name=name, grid_spec=spec, out_shape=out_shape,
                              input_output_aliases={n_pre + i: o for i, o in aliases.items()},
                              compiler_params=_params(semantics))(*prefetch, *args)

    if beside is None:
        res = call(kern, in_specs, out_specs, out_shape, scratch_shapes, semantics, {}, args)
        return (res[0] if single else res), []
    n_in, n_out, n_scr = len(in_specs), len(out_specs), len(scratch_shapes)
    nb_in, nb_out = len(beside.operands), len(beside.results)

    def body(*refs):
        pre, refs = refs[:n_pre], refs[n_pre:]
        ins, b_ins = refs[:n_in], refs[n_in:n_in + nb_in]
        outs = refs[n_in + nb_in:n_in + nb_in + n_out]
        b_outs = refs[n_in + nb_in + n_out:n_in + nb_in + n_out + nb_out]
        rest = refs[n_in + nb_in + n_out + nb_out:]
        scr, sems = rest[:n_scr], rest[n_scr:]
        ids = [pl.program_id(a) for a in range(len(grid))]
        first = functools.reduce(jnp.logical_and, [i == 0 for i in ids])
        last = functools.reduce(jnp.logical_and, [i == g - 1 for i, g in zip(ids, grid)])

        @pl.when(first)
        def _():
            beside.start(b_ins, b_outs, sems)

        kern(*pre, *ins, *outs, *scr)

        @pl.when(last)
        def _():
            beside.finish(b_ins, b_outs, sems)

    any_spec = pl.BlockSpec(memory_space=pl.ANY)
    res = call(body, list(in_specs) + [any_spec] * nb_in, out_specs + [any_spec] * nb_out,
               out_shape + [jax.ShapeDtypeStruct(s, d) for s, d, _ in beside.results],
               list(scratch_shapes) + [pltpu.SemaphoreType.DMA((k,)) for k in beside.sem_sizes],
               ("arbitrary",) * len(grid),
               {n_in + op: n_out + r for r, (_, _, op) in enumerate(beside.results) if op is not None},
               list(args) + list(beside.operands))
    mine = res[:n_out]
    return (mine[0] if single else mine), list(res[n_out:])


def _alone(beside, name):
    return _host(lambda: None, beside, name=name, grid=(1,), in_specs=[], out_specs=[], out_shape=[],
                 scratch_shapes=[], semantics=("arbitrary",), args=[])[1]


def _mm(a, b, mode, name, out_dtype=F32, add=None, shards=False, beside=None):
    if mode == 'nn':
        (m, k), n = a.shape, (b.shape[2] * N_CHIPS if shards else b.shape[1])
    elif mode == 'nt':
        (m, k), n = a.shape, (b.shape[1] if shards else b.shape[0])
    else:
        (k, m), n = a.shape, b.shape[1]
    per_chip = (k if mode == 'nt' else n) // N_CHIPS
    tm = _tile(m, 1024)
    tn = _tile(per_chip if shards and mode != 'nt' else n, 1408 if shards else 1024)
    k_dim = per_chip if shards and mode == 'nt' else k
    a_bytes, b_bytes = jnp.dtype(a.dtype).itemsize, jnp.dtype(b.dtype).itemsize
    tk = next((t for t in MM_K_TILES if k_dim % t == 0 and t * (tm * a_bytes + tn * b_bytes) <= MM_OPERAND_BYTES), k_dim)
    nk = k // tk
    in_place = nk > 1 and out_dtype == F32
    a_spec = pl.BlockSpec((tk, tm), lambda i, j, q: (q, i)) if mode == 'tn' else pl.BlockSpec((tm, tk), lambda i, j, q: (i, q))
    b_spec = pl.BlockSpec((tn, tk), lambda i, j, q: (j, q)) if mode == 'nt' else pl.BlockSpec((tk, tn), lambda i, j, q: (q, j))
    o_spec = pl.BlockSpec((tm, tn), lambda i, j, q: (i, j))
    out_struct = jax.ShapeDtypeStruct((m, n), out_dtype)
    if shards:
        per = per_chip // (tk if mode == 'nt' else tn)
        if mode == 'nn':
            b_spec = pl.BlockSpec((None, tk, tn), lambda i, j, q: (j // per, q, j % per))
        elif mode == 'nt':
            b_spec = pl.BlockSpec((None, tn, tk), lambda i, j, q: (q // per, j, q % per))
        else:
            out_struct = jax.ShapeDtypeStruct((N_CHIPS, m, per_chip), out_dtype)
    out_spec = pl.BlockSpec((None, tm, tn), lambda i, j, q: (j // per, i, j % per)) if shards and mode == 'tn' else o_spec
    dims = {'nn': NN, 'nt': NT, 'tn': TN}[mode]

    n_ins = 3 if add is not None else 2

    def kern(*refs):
        a_ref, b_ref, o_ref = refs[0], refs[1], refs[n_ins]
        acc = o_ref if in_place or nk == 1 else refs[n_ins + 1]
        q = pl.program_id(2)
        part = _dot(a_ref[...].astype(BF16), b_ref[...].astype(BF16), dims)

        def first():
            return part if add is None else part + refs[2][...]

        if nk == 1:
            o_ref[...] = first().astype(o_ref.dtype)
            return

        @pl.when(q == 0)
        def _():
            acc[...] = first()

        @pl.when(q > 0)
        def _():
            acc[...] += part

        if not in_place:
            @pl.when(q == nk - 1)
            def _():
                o_ref[...] = acc[...].astype(o_ref.dtype)

    ins, specs = [a, b], [a_spec, b_spec]
    if add is not None:
        ins.append(add)
        specs.append(o_spec)
    scratch = [] if in_place or nk == 1 else [pltpu.VMEM((tm, tn), F32)]
    res, extra = _host(kern, beside, name=name, grid=(m // tm, n // tn, nk), in_specs=specs, out_specs=out_spec,
                       out_shape=out_struct, scratch_shapes=scratch,
                       semantics=("parallel", "parallel", "arbitrary"), args=ins)
    return res if beside is None else (res, extra)


def _rowwise(fn, name, rows, bcast=(), outs=(), accs=(), tm=256, beside=None):
    rows = [r if isinstance(r, tuple) else (r, r.shape[1], 0) for r in rows]
    n_rows = rows[0][0].shape[0]
    tm = min(tm, n_rows)
    assert n_rows % tm == 0, (name, n_rows, tm)
    n_in, n_out = len(rows) + len(bcast), len(outs)
    in_specs = [pl.BlockSpec((tm, w), functools.partial(lambda i, cb: (i, cb), cb=cb)) for _, w, cb in rows]
    in_specs += [pl.BlockSpec(b.shape, lambda i: (0, 0)) for b in bcast]
    out_specs = [pl.BlockSpec((tm, w), lambda i: (i, 0)) for w, _ in outs]
    out_specs += [pl.BlockSpec(s, lambda i: (0, 0)) for s in accs]
    out_shape = [jax.ShapeDtypeStruct((n_rows, w), d) for w, d in outs] + [jax.ShapeDtypeStruct(s, F32) for s in accs]

    def kern(*refs):
        vals = fn(*[r[...] for r in refs[:n_in]])
        vals = vals if isinstance(vals, (tuple, list)) else (vals,)
        o_refs = refs[n_in:]
        for r, v in zip(o_refs[:n_out], vals[:n_out]):
            r[...] = v.astype(r.dtype)
        if accs:
            @pl.when(pl.program_id(0) == 0)
            def _():
                for r in o_refs[n_out:]:
                    r[...] = jnp.zeros_like(r)

            for r, v in zip(o_refs[n_out:], vals[n_out:]):
                r[...] += v

    res, extra = _host(kern, beside, name=name, grid=(n_rows // tm,), in_specs=in_specs, out_specs=out_specs,
                       out_shape=out_shape, scratch_shapes=[], semantics=("arbitrary",),
                       args=[r[0] for r in rows] + list(bcast))
    return res if beside is None else (res, extra)


def _rms(x, w):
    xf = x.astype(F32)
    return xf * lax.rsqrt(jnp.mean(xf * xf, axis=-1, keepdims=True) + EPS) * w


def _lane_groups(width, group):
    return [slice(g * group, (g + 1) * group) for g in range(width // group)]


def _rms_fwd(x, w, name, tm=256, beside=None):
    def fn(x, w):
        return jnp.concatenate([_rms(x[:, g], w) for g in _lane_groups(x.shape[1], w.shape[1])], axis=1)

    res = _rowwise(fn, name, [x], [w], outs=[(x.shape[1], BF16)], tm=tm, beside=beside)
    return res[0] if beside is None else (res[0][0], res[1])


def _rms_bwd(x, w, dy, name, extra=None, out_dtype=F32, tm=256, copy_bf16=False, beside=None):
    def fn(x, dy, *rest):
        w = rest[-1]
        dxs, dw = [], jnp.zeros(w.shape, F32)
        for g in _lane_groups(x.shape[1], w.shape[1]):
            _, vjp = jax.vjp(_rms, x[:, g], w)
            dx_g, dw_g = vjp(dy[:, g].astype(F32))
            dxs.append(dx_g)
            dw = dw + dw_g
        dx = jnp.concatenate(dxs, axis=1)
        if extra is not None:
            dx = dx + rest[0]
        return (dx, dx, dw) if copy_bf16 else (dx, dw)

    rows = [x, dy] + ([extra] if extra is not None else [])
    outs = [(x.shape[1], out_dtype)] + ([(x.shape[1], BF16)] if copy_bf16 else [])
    return _rowwise(fn, name, rows, [w], outs=outs, accs=[w.shape], tm=tm, beside=beside)


def _sigmoid(x):
    return 1.0 / (1.0 + jnp.exp(-x))


def _softplus(x):
    return jnp.maximum(x, 0.0) + jnp.log(1.0 + jnp.exp(-jnp.abs(x)))


def _swiglu_fwd(gu, d_ff, name):
    def fn(g, u):
        g, u = g.astype(F32), u.astype(F32)
        return g * _sigmoid(g) * u

    return _rowwise(fn, name, [(gu, d_ff, 0), (gu, d_ff, 1)], outs=[(d_ff, BF16)], tm=128)[0]


def _swiglu_bwd(gu, dact, d_ff, name):
    def fn(g, u, da):
        g, u = g.astype(F32), u.astype(F32)
        s = _sigmoid(g)
        dg = da * u * s * (1.0 + g * (1.0 - s))
        du = da * g * s
        return jnp.concatenate([dg, du], axis=1)

    return _rowwise(fn, name, [(gu, d_ff, 0), (gu, d_ff, 1), dact], outs=[(2 * d_ff, BF16)], tm=128)[0]


def _cumsum_rows(v, reverse=False):
    n = v.shape[0]
    row = lax.broadcasted_iota(jnp.int32, v.shape, 0)
    sh = 1
    while sh < n:
        if reverse:
            v = v + jnp.where(row < n - sh, pltpu.roll(v, n - sh, 0), 0.0)
        else:
            v = v + jnp.where(row >= sh, pltpu.roll(v, sh, 0), 0.0)
        sh *= 2
    return v


def _conv_fwd(u, w, b, name):
    s, c = u.shape
    kw = w.shape[0]
    tc = _tile(c, 128)

    def kern(u_ref, w_ref, b_ref, o_ref):
        uu = u_ref[...]
        row = lax.broadcasted_iota(jnp.int32, uu.shape, 0)
        acc = jnp.zeros_like(uu) + b_ref[...]
        for k in range(kw):
            sh = kw - 1 - k
            uk = uu if sh == 0 else jnp.where(row >= sh, pltpu.roll(uu, sh, 0), 0.0)
            acc = acc + w_ref[pl.ds(k, 1), :] * uk
        o_ref[...] = acc * _sigmoid(acc)

    return pl.pallas_call(
        kern, name=name, grid=(c // tc,),
        in_specs=[pl.BlockSpec((s, tc), lambda j: (0, j)), pl.BlockSpec((kw, tc), lambda j: (0, j)),
                  pl.BlockSpec((1, tc), lambda j: (0, j))],
        out_specs=pl.BlockSpec((s, tc), lambda j: (0, j)), out_shape=jax.ShapeDtypeStruct((s, c), F32),
        compiler_params=_params(("parallel",)))(u, w, b)


def _conv_bwd(u, w, b, dact, name, beside=None):
    s, c = u.shape
    kw = w.shape[0]
    tc = _tile(c, 128)

    def kern(u_ref, w_ref, b_ref, d_ref, du_ref, dw_ref, db_ref):
        uu = u_ref[...]
        row = lax.broadcasted_iota(jnp.int32, uu.shape, 0)
        shifted = []
        acc = jnp.zeros_like(uu) + b_ref[...]
        for k in range(kw):
            sh = kw - 1 - k
            uk = uu if sh == 0 else jnp.where(row >= sh, pltpu.roll(uu, sh, 0), 0.0)
            shifted.append(uk)
            acc = acc + w_ref[pl.ds(k, 1), :] * uk
        sg = _sigmoid(acc)
        dacc = d_ref[...] * sg * (1.0 + acc * (1.0 - sg))
        db_ref[...] = jnp.sum(dacc, axis=0, keepdims=True)
        du = jnp.zeros_like(uu)
        for k in range(kw):
            sh = kw - 1 - k
            dw_ref[pl.ds(k, 1), :] = jnp.sum(dacc * shifted[k], axis=0, keepdims=True)
            dk = dacc if sh == 0 else jnp.where(row < s - sh, pltpu.roll(dacc, s - sh, 0), 0.0)
            du = du + w_ref[pl.ds(k, 1), :] * dk
        du_ref[...] = du.astype(du_ref.dtype)

    col = lambda j: (0, j)
    return _host(
        kern, beside, name=name, grid=(c // tc,),
        in_specs=[pl.BlockSpec((s, tc), col), pl.BlockSpec((kw, tc), col), pl.BlockSpec((1, tc), col),
                  pl.BlockSpec((s, tc), col)],
        out_specs=[pl.BlockSpec((s, tc), col), pl.BlockSpec((kw, tc), col), pl.BlockSpec((1, tc), col)],
        out_shape=[jax.ShapeDtypeStruct((s, c), BF16), jax.ShapeDtypeStruct((kw, c), F32),
                   jax.ShapeDtypeStruct((1, c), F32)],
        scratch_shapes=[], semantics=("parallel",), args=[u, w, b, dact])


def _dt_fwd(dtraw, bias, a_log, name):
    def fn(raw, bias, a_log):
        dt = _softplus(raw + bias)
        return dt, _cumsum_rows(dt * (-jnp.exp(a_log)))

    return _rowwise(fn, name, [dtraw], [bias, a_log], outs=[(LANES, F32), (LANES, F32)], tm=SSD_CHUNK)


def _dt_bwd(dtraw, bias, a_log, ddt, dacum, name):
    def fn(raw, ddt, dac, bias, a_log):
        z = raw + bias
        dt = _softplus(z)
        a_neg = -jnp.exp(a_log)
        da = _cumsum_rows(dac, reverse=True)
        draw = (ddt + da * a_neg) * _sigmoid(z)
        return draw, jnp.sum(draw, axis=0, keepdims=True), jnp.sum(da * dt, axis=0, keepdims=True) * a_neg

    return _rowwise(fn, name, [dtraw, ddt, dacum], [bias, a_log], outs=[(LANES, BF16)],
                    accs=[(1, LANES), (1, LANES)], tm=SSD_CHUNK)


def _ssd_pieces(l, gw):
    lane_k = lax.broadcasted_iota(jnp.int32, (l, LANES), 1)
    lane_of = lax.broadcasted_iota(jnp.int32, (gw, LANES), 0)
    head_of = lax.broadcasted_iota(jnp.int32, (gw, LANES), 1)
    in_head = ((lane_of >= head_of * SSM_HEAD_DIM) & (lane_of < (head_of + 1) * SSM_HEAD_DIM)).astype(BF16)

    lane_w = lax.broadcasted_iota(jnp.int32, (l, gw), 1)

    def col(blk, k):
        return jnp.sum(jnp.where(lane_k == k, blk, 0.0), axis=1, keepdims=True)

    def expand(blk):
        acc = jnp.zeros((l, gw), F32)
        for k in range(gw // SSM_HEAD_DIM):
            acc = jnp.where((lane_w >= k * SSM_HEAD_DIM) & (lane_w < (k + 1) * SSM_HEAD_DIM), col(blk, k), acc)
        return acc

    def collapse(wide):
        hi = wide.astype(BF16)
        lo = (wide - hi.astype(F32)).astype(BF16)
        return _dot(hi, in_head, NN) + _dot(lo, in_head, NN)

    return lane_k, col, expand, collapse


def _head_block(k):
    per_block = LANES // SSM_HEAD_DIM
    lane = lax.broadcasted_iota(jnp.int32, (1, LANES), 1)
    lo = (k % per_block) * SSM_HEAD_DIM
    return slice((k // per_block) * LANES, (k // per_block + 1) * LANES), (lane >= lo) & (lane < lo + SSM_HEAD_DIM)


def _by_block(pieces, n_blocks, like):
    zero = jnp.zeros((like.shape[0], LANES), F32)
    return jnp.concatenate([pieces.get(p, zero) for p in range(n_blocks)], axis=1)


def _ssd_specs(l, gw, n, n_xs_blocks):
    g_axis = SSM_GROUPS
    return dict(
        xs=lambda cm: pl.BlockSpec((l, gw), lambda g, c: (cm(c), g)),
        b=lambda cm: pl.BlockSpec((l, n), lambda g, c: (cm(c), n_xs_blocks + g)),
        c=lambda cm: pl.BlockSpec((l, n), lambda g, c: (cm(c), n_xs_blocks + g_axis + g)),
        col=lambda cm: pl.BlockSpec((None, l, LANES), lambda g, c: (g, cm(c), 0)),
        row=lambda cm: pl.BlockSpec((None, 8, l), lambda g, c: (g, 0, cm(c))),
        state=lambda cm: pl.BlockSpec((None, None, n, gw), lambda g, c: (cm(c), g, 0, 0)),
    )


def _ssd_fwd(xbc, dt_g, ac_g, act_g, d_inner, n, kh, name, beside=None):
    s = xbc.shape[0]
    l, g_n = SSD_CHUNK, SSM_GROUPS
    gw, nc = d_inner // g_n, s // l
    sp = _ssd_specs(l, gw, n, d_inner // n)
    fwd = lambda c: c

    def kern(xs_ref, b_ref, c_ref, dt_ref, ac_ref, act_ref, y_ref, s0_ref, st):
        @pl.when(pl.program_id(1) == 0)
        def _():
            st[...] = jnp.zeros_like(st)

        s0 = st[...]
        s0_ref[...] = s0
        xs, ac = xs_ref[...], ac_ref[...]
        _, col, expand, _ = _ssd_pieces(l, gw)
        ace = expand(ac)
        x = xs * expand(dt_ref[...])
        xb, bb, cb_ = x.astype(BF16), b_ref[...].astype(BF16), c_ref[...].astype(BF16)
        cb = _dot(cb_, bb, NT)
        ri = lax.broadcasted_iota(jnp.int32, (l, l), 0)
        ci = lax.broadcasted_iota(jnp.int32, (l, l), 1)
        causal = ri >= ci
        y_diag = {}
        for k in range(kh):
            seg = col(ac, k) - act_ref[pl.ds(k, 1), :]
            m = jnp.where(causal, cb * jnp.exp(jnp.where(causal, seg, 0.0)), 0.0)
            blk, mine = _head_block(k)
            yk = _dot(m.astype(BF16), xb[:, blk], NN)
            y_diag[blk.start // LANES] = jnp.where(mine, yk, y_diag.get(blk.start // LANES, 0.0))
        y_ref[...] = _dot(cb_, s0.astype(BF16), NN) * jnp.exp(ace) + _by_block(y_diag, gw // LANES, xs)
        row_w = lax.broadcasted_iota(jnp.int32, (l, gw), 0)
        alast = jnp.sum(jnp.where(row_w == l - 1, ace, 0.0), axis=0, keepdims=True)
        st[...] = s0 * jnp.exp(alast) + _dot(bb, (jnp.exp(alast - ace) * x).astype(BF16), TN)

    return _host(
        kern, beside, name=name, grid=(g_n, nc),
        in_specs=[sp['xs'](fwd), sp['b'](fwd), sp['c'](fwd), sp['col'](fwd), sp['col'](fwd), sp['row'](fwd)],
        out_specs=[pl.BlockSpec((l, gw), lambda g, c: (c, g)), sp['state'](fwd)],
        out_shape=[jax.ShapeDtypeStruct((s, d_inner), F32), jax.ShapeDtypeStruct((nc, g_n, n, gw), F32)],
        scratch_shapes=[pltpu.VMEM((n, gw), F32)], semantics=("arbitrary", "arbitrary"),
        args=[xbc, xbc, xbc, dt_g, ac_g, act_g])


def _ssd_bwd(xbc, dt_g, ac_g, act_g, s0_all, dy, dxs_skip, d_inner, n, kh, name, beside=None):
    s = xbc.shape[0]
    l, g_n = SSD_CHUNK, SSM_GROUPS
    gw, nc = d_inner // g_n, s // l
    sp = _ssd_specs(l, gw, n, d_inner // n)
    rev = lambda c: nc - 1 - c

    def kern(xs_ref, b_ref, c_ref, dt_ref, ac_ref, act_ref, s0_ref, dy_ref, skip_ref,
             dxs_ref, db_ref, dc_ref, ddt_ref, dacc_ref, dacr_ref, dst):
        @pl.when(pl.program_id(1) == 0)
        def _():
            dst[...] = jnp.zeros_like(dst)

        dsn = dst[...]
        s0 = s0_ref[...]
        xs, ac, dy = xs_ref[...], ac_ref[...], dy_ref[...]
        lane_k, col, expand, collapse = _ssd_pieces(l, gw)
        ace, dte = expand(ac), expand(dt_ref[...])
        x = xs * dte
        xb, bb, cb_ = x.astype(BF16), b_ref[...].astype(BF16), c_ref[...].astype(BF16)
        s0b, dsnb, dyb = s0.astype(BF16), dsn.astype(BF16), dy.astype(BF16)
        cb = _dot(cb_, bb, NT)
        ri = lax.broadcasted_iota(jnp.int32, (l, l), 0)
        ci = lax.broadcasted_iota(jnp.int32, (l, l), 1)
        causal = ri >= ci
        row_w = lax.broadcasted_iota(jnp.int32, (l, gw), 0)
        e = jnp.exp(ace)
        alast = jnp.sum(jnp.where(row_w == l - 1, ace, 0.0), axis=0, keepdims=True)
        gdec = jnp.exp(alast)
        wt = jnp.exp(alast - ace)
        cs = _dot(cb_, s0b, NN)
        dcs = (dy * e).astype(BF16)
        d_c = _dot(dcs, s0b, NT)
        ds_off = _dot(cb_, dcs, TN)
        dace = dy * cs * e
        dalast = jnp.sum(dsn * s0, axis=0, keepdims=True) * gdec
        z = wt * x
        dz = _dot(bb, dsnb, NN)
        d_b = _dot(z.astype(BF16), dsnb, NT)
        dx = dz * wt
        t = dz * z
        dalast = dalast + jnp.sum(t, axis=0, keepdims=True)
        dace = dace - t
        dcb = jnp.zeros((l, l), F32)
        dac_col = jnp.zeros((l, LANES), F32)
        dx_diag = {}
        for k in range(kh):
            seg = col(ac, k) - act_ref[pl.ds(k, 1), :]
            dk = jnp.exp(jnp.where(causal, seg, 0.0))
            mk = jnp.where(causal, cb * dk, 0.0)
            blk, mine = _head_block(k)
            dxk = _dot(mk.astype(BF16), dyb[:, blk], TN)
            dx_diag[blk.start // LANES] = jnp.where(mine, dxk, dx_diag.get(blk.start // LANES, 0.0))
            dmk = _dot(jnp.where(mine, dy[:, blk], 0.0).astype(BF16), xb[:, blk], NT)
            dcb = dcb + jnp.where(causal, dmk * dk, 0.0)
            dseg = dmk * mk
            dac_col = jnp.where(lane_k == k, jnp.sum(dseg, axis=1, keepdims=True), dac_col)
            dacr_ref[pl.ds(k, 1), :] = -jnp.sum(dseg, axis=0, keepdims=True)
        for k in range(kh, 8):
            dacr_ref[pl.ds(k, 1), :] = jnp.zeros((1, l), F32)
        dx = dx + _by_block(dx_diag, gw // LANES, xs)
        dcbb = dcb.astype(BF16)
        dc_ref[...] = d_c + _dot(dcbb, bb, NN)
        db_ref[...] = d_b + _dot(dcbb, cb_, TN)
        dace = jnp.where(row_w == l - 1, dace + dalast, dace)
        dacc_ref[...] = dac_col + collapse(dace)
        ddt_ref[...] = collapse(dx * xs)
        dxs_ref[...] = dx * dte + skip_ref[...]
        dst[...] = dsn * gdec + ds_off

    return _host(
        kern, beside, name=name, grid=(g_n, nc),
        in_specs=[sp['xs'](rev), sp['b'](rev), sp['c'](rev), sp['col'](rev), sp['col'](rev), sp['row'](rev),
                  sp['state'](rev), pl.BlockSpec((l, gw), lambda g, c: (rev(c), g)),
                  pl.BlockSpec((l, gw), lambda g, c: (rev(c), g))],
        out_specs=[pl.BlockSpec((l, gw), lambda g, c: (rev(c), g)), pl.BlockSpec((l, n), lambda g, c: (rev(c), g)),
                   pl.BlockSpec((l, n), lambda g, c: (rev(c), g)), sp['col'](rev), sp['col'](rev), sp['row'](rev)],
        out_shape=[jax.ShapeDtypeStruct((s, d_inner), F32), jax.ShapeDtypeStruct((s, g_n * n), F32),
                   jax.ShapeDtypeStruct((s, g_n * n), F32), jax.ShapeDtypeStruct((g_n, s, LANES), F32),
                   jax.ShapeDtypeStruct((g_n, s, LANES), F32), jax.ShapeDtypeStruct((g_n, 8, s), F32)],
        scratch_shapes=[pltpu.VMEM((n, gw), F32)], semantics=("arbitrary", "arbitrary"),
        args=[xbc, xbc, xbc, dt_g, ac_g, act_g, s0_all, dy, dxs_skip])


def _gate(y, xs, z, d_exp, gw):
    t = (y + xs * d_exp) * (z * _sigmoid(z))
    width = t.shape[1]
    gsz = width // SSM_GROUPS
    lane = lax.broadcasted_iota(jnp.int32, t.shape, 1)
    t2 = t * t
    scale = jnp.zeros_like(t)
    for g in range(SSM_GROUPS):
        in_g = (lane >= g * gsz) & (lane < (g + 1) * gsz)
        ms = jnp.sum(jnp.where(in_g, t2, 0.0), axis=1, keepdims=True) * (1.0 / gsz)
        scale = jnp.where(in_g, lax.rsqrt(ms + EPS), scale)
    return t * scale * gw


def _gate_fwd(y, xbc, z, d_exp, gw, d_inner, name, beside=None):
    res = _rowwise(_gate, name, [y, (xbc, d_inner, 0), z], [d_exp, gw], outs=[(d_inner, BF16)], tm=128, beside=beside)
    return res[0] if beside is None else (res[0][0], res[1])


def _gate_bwd(y, xbc, z, d_exp, gw, dyn, d_inner, name):
    def fn(y, xs, z, dyn, d_exp, gw):
        _, vjp = jax.vjp(_gate, y, xs, z, d_exp, gw)
        return vjp(dyn)

    return _rowwise(fn, name, [y, (xbc, d_inner, 0), z, dyn], [d_exp, gw],
                    outs=[(d_inner, F32), (d_inner, F32), (d_inner, BF16)], accs=[d_exp.shape, gw.shape], tm=64)


def _forget_fwd(fraw, b_f, name):
    def kern(f_ref, b_ref, o_ref):
        o_ref[...] = _cumsum_rows(-_softplus(-(f_ref[...] + b_ref[...])))

    return pl.pallas_call(kern, name=name, out_shape=jax.ShapeDtypeStruct(fraw.shape, F32),
                          compiler_params=_params())(fraw, b_f)


def _forget_bwd(fraw, b_f, dcum, name):
    def kern(f_ref, b_ref, d_ref, df_ref, db_ref):
        df = _cumsum_rows(d_ref[...], reverse=True) * _sigmoid(-(f_ref[...] + b_ref[...]))
        df_ref[...] = df.astype(df_ref.dtype)
        db_ref[...] = jnp.sum(df, axis=0, keepdims=True)

    return pl.pallas_call(
        kern, name=name,
        out_shape=[jax.ShapeDtypeStruct(fraw.shape, BF16), jax.ShapeDtypeStruct((1, fraw.shape[1]), F32)],
        compiler_params=_params())(fraw, b_f, dcum)


ATT_BLOCK = 512


def _attn_fwd(q, k, v, cq_rep, ck, name, beside=None):
    s, hd = q.shape
    h_n, d = hd // ATT_HEAD_DIM, ATT_HEAD_DIM
    tb = min(ATT_BLOCK, s)
    nb = s // tb
    scale = d ** -0.5

    def kern(q_ref, k_ref, v_ref, cq_ref, ck_ref, o_ref, lse_ref):
        i = pl.program_id(1)
        qq = q_ref[...]
        cq = jnp.max(cq_ref[...], axis=1, keepdims=True)
        rowpos = i * tb + lax.broadcasted_iota(jnp.int32, (tb, tb), 0)
        coli = lax.broadcasted_iota(jnp.int32, (tb, tb), 1)

        def step(j, carry, diagonal):
            m, l_, acc = carry
            ks = pl.ds(pl.multiple_of(j * tb, tb), tb)
            sc = _dot(qq, k_ref[ks, :], NT) * scale + (cq - ck_ref[j])
            if diagonal:
                sc = jnp.where(j * tb + coli <= rowpos, sc, NEG)
            mn = jnp.maximum(m, jnp.max(sc, axis=1, keepdims=True))
            p = jnp.exp(sc - mn)
            alpha = jnp.exp(m - mn)
            l_ = alpha * l_ + jnp.sum(p, axis=1, keepdims=True)
            acc = alpha * acc + _dot(p.astype(BF16), v_ref[ks, :], NN)
            return mn, l_, acc

        init = (jnp.full((tb, 1), NEG, F32), jnp.zeros((tb, 1), F32), jnp.zeros((tb, d), F32))
        below = lax.fori_loop(0, i, lambda j, carry: step(j, carry, False), init)
        m, l_, acc = step(i, below, True)
        o_ref[...] = (acc / l_).astype(o_ref.dtype)
        lse_ref[...] = jnp.broadcast_to(m + jnp.log(l_), (tb, d))

    return _host(
        kern, beside, name=name, grid=(h_n, nb),
        in_specs=[pl.BlockSpec((tb, d), lambda h, i: (i, h)), pl.BlockSpec((s, d), lambda h, i: (0, h)),
                  pl.BlockSpec((s, d), lambda h, i: (0, h)), pl.BlockSpec((tb, d), lambda h, i: (i, h)),
                  pl.BlockSpec((None, nb, 1, tb), lambda h, i: (h, 0, 0, 0))],
        out_specs=[pl.BlockSpec((tb, d), lambda h, i: (i, h)), pl.BlockSpec((tb, d), lambda h, i: (i, h))],
        out_shape=[jax.ShapeDtypeStruct((s, hd), BF16), jax.ShapeDtypeStruct((s, hd), F32)],
        scratch_shapes=[], semantics=("parallel", "arbitrary"), args=[q, k, v, cq_rep, ck])


def _attn_bwd(q, k, v, do, lse_rep, delta_rep, cq_rep, ck, name, beside=None):
    s, hd = q.shape
    h_n, d = hd // ATT_HEAD_DIM, ATT_HEAD_DIM
    tb = min(ATT_BLOCK, s)
    nb = s // tb
    scale = d ** -0.5

    def kern(q_ref, do_ref, k_ref, v_ref, lse_ref, dl_ref, cq_ref, ck_ref, dq_ref, dk_ref, dv_ref, dcq_ref, dck_ref):
        j = pl.program_id(1)

        @pl.when(j == 0)
        def _():
            dq_ref[...] = jnp.zeros_like(dq_ref)
            dcq_ref[...] = jnp.zeros_like(dcq_ref)

        kj, vj, ckj = k_ref[...], v_ref[...], ck_ref[...]
        colpos = j * tb + lax.broadcasted_iota(jnp.int32, (tb, tb), 1)
        rowi = lax.broadcasted_iota(jnp.int32, (tb, tb), 0)

        def step(i, carry, diagonal):
            dk, dv, dck = carry
            rs = pl.ds(pl.multiple_of(i * tb, tb), tb)
            qi, doi = q_ref[rs, :], do_ref[rs, :]
            lse = jnp.max(lse_ref[rs, :], axis=1, keepdims=True)
            dl = jnp.max(dl_ref[rs, :], axis=1, keepdims=True)
            cq = jnp.max(cq_ref[rs, :], axis=1, keepdims=True)
            sc = _dot(qi, kj, NT) * scale + ((cq - lse) - ckj)
            p = jnp.exp(jnp.where(colpos <= i * tb + rowi, sc, NEG) if diagonal else sc)
            dp = _dot(doi, vj, NT)
            ds = p * (dp - dl)
            dsb = ds.astype(BF16)
            dv = dv + _dot(p.astype(BF16), doi, TN)
            dk = dk + _dot(dsb, qi, TN)
            dq_ref[rs, :] += _dot(dsb, kj, NN) * scale
            dcq_ref[rs, :] += jnp.broadcast_to(jnp.sum(ds, axis=1, keepdims=True), (tb, d))
            return dk, dv, dck - jnp.sum(ds, axis=0, keepdims=True)

        init = (jnp.zeros((tb, d), F32), jnp.zeros((tb, d), F32), jnp.zeros((1, tb), F32))
        dk, dv, dck = lax.fori_loop(j + 1, nb, lambda i, carry: step(i, carry, False), step(j, init, True))
        dk_ref[...] = dk * scale
        dv_ref[...] = dv.astype(dv_ref.dtype)
        dck_ref[...] = dck

    whole = pl.BlockSpec((s, d), lambda h, j: (0, h))
    blk = pl.BlockSpec((tb, d), lambda h, j: (j, h))
    ckb = pl.BlockSpec((None, None, 1, tb), lambda h, j: (h, j, 0, 0))
    return _host(
        kern, beside, name=name, grid=(h_n, nb),
        in_specs=[whole, whole, blk, blk, whole, whole, whole, ckb],
        out_specs=[whole, blk, blk, whole, ckb],
        out_shape=[jax.ShapeDtypeStruct((s, hd), F32), jax.ShapeDtypeStruct((s, hd), F32),
                   jax.ShapeDtypeStruct((s, hd), BF16), jax.ShapeDtypeStruct((s, hd), F32),
                   jax.ShapeDtypeStruct((h_n, nb, 1, tb), F32)],
        scratch_shapes=[], semantics=("arbitrary", "arbitrary"), args=[q, do, k, v, lse_rep, delta_rep, cq_rep, ck])


def _adamw(w, g, m, v, name):
    cols = w.shape[1]
    tm = _tile(w.shape[0], 128) if w.shape[0] % 128 == 0 else w.shape[0]
    return _rowwise(_adamw_math, name, [w, g, m, v], outs=[(cols, F32)] * 3, tm=tm)


def _sum_arrays(arrs, out_dtype, name):
    def fn(*xs):
        acc = xs[0].astype(F32)
        for x in xs[1:]:
            acc = acc + x.astype(F32)
        return acc

    tm = PACK_ROWS if arrs[0].shape[0] % PACK_ROWS == 0 else arrs[0].shape[0]
    return _rowwise(fn, name, list(arrs), outs=[(arrs[0].shape[1], out_dtype)], tm=tm)[0]


def _half_tile(rows):
    for t in (256, 176, 128, 64, 32, 16):
        if rows % t == 0:
            return t
    return rows


def _sum_half(g, got, core, name):
    n_chip, hr, cols = got.shape
    tm = _half_tile(hr)
    nt = hr // tm

    def kern(c_ref, g_ref, a_ref, o_ref):
        o_ref[...] = (g_ref[...].astype(F32) + a_ref[...].astype(F32)).astype(o_ref.dtype)

    grid_spec = pltpu.PrefetchScalarGridSpec(
        num_scalar_prefetch=1, grid=(n_chip * nt,),
        in_specs=[pl.BlockSpec((tm, cols), lambda r, c: (((r // nt) * 2 + c[0]) * nt + r % nt, 0)),
                  pl.BlockSpec((tm, cols), lambda r, c: (r, 0))],
        out_specs=pl.BlockSpec((tm, cols), lambda r, c: (r, 0)))
    out = pl.pallas_call(
        kern, name=name, grid_spec=grid_spec, out_shape=jax.ShapeDtypeStruct((n_chip * hr, cols), BF16),
        compiler_params=_params(("arbitrary",)))(core.reshape(1), g.reshape(-1, cols), got.reshape(-1, cols))
    return out.reshape(n_chip, hr, cols)


def _sum_parts(own, parts, chip, name):
    n_parts, hr, cols = parts.shape
    tm = _half_tile(hr)

    def kern(s_ref, t_ref, p_ref, o_ref):
        acc = t_ref[...].astype(F32)
        for j in range(n_parts):
            acc = acc + p_ref[j].astype(F32)
        o_ref[...] = acc

    grid_spec = pltpu.PrefetchScalarGridSpec(
        num_scalar_prefetch=1, grid=(hr // tm,),
        in_specs=[pl.BlockSpec((None, tm, cols), lambda r, s: (s[0], r, 0)),
                  pl.BlockSpec((n_parts, tm, cols), lambda r, s: (0, r, 0))],
        out_specs=pl.BlockSpec((tm, cols), lambda r, s: (r, 0)))
    return pl.pallas_call(
        kern, name=name, grid_spec=grid_spec, out_shape=jax.ShapeDtypeStruct((hr, cols), F32),
        compiler_params=_params(("arbitrary",)))(chip.reshape(1), own, parts)


def _adamw_math(w, g, m, v):
    m = ADAM_B1 * m + (1.0 - ADAM_B1) * g
    v = ADAM_B2 * v + (1.0 - ADAM_B2) * (g * g)
    m_hat = m * (1.0 / (1.0 - ADAM_B1 ** ADAM_STEP))
    v_hat = v * (1.0 / (1.0 - ADAM_B2 ** ADAM_STEP))
    return -ADAM_LR * (m_hat / (jnp.sqrt(v_hat) + ADAM_EPS) + ADAM_WD * w), m, v


def _adamw_big(w, m, v, mine, theirs, core, layer, prev, name):
    n_layers, rows, cols = w.shape
    hr = rows // 2
    tm = next(t for t in (128, 64, 32, 16, 8) if hr % t == 0)
    nt = hr // tm

    def kern(s_ref, w_ref, m_ref, v_ref, a_ref, b_ref, *rest):
        g_ref, d_ref, nm_ref, nv_ref = rest[-4:]
        g = jnp.where(pl.program_id(0) // nt == s_ref[0], a_ref[...], b_ref[...])
        g_ref[...] = g
        d_ref[...], nm_ref[...], nv_ref[...] = _adamw_math(w_ref[...], g, m_ref[...], v_ref[...])

    lyr = pl.BlockSpec((None, tm, cols), lambda r, s: (layer, r, 0))
    half = pl.BlockSpec((tm, cols), lambda r, s: (r % nt, 0))
    passed = [] if prev is None else list(prev)
    grid_spec = pltpu.PrefetchScalarGridSpec(
        num_scalar_prefetch=1, grid=(rows // tm,),
        in_specs=[lyr, lyr, lyr, half, half] + [pl.BlockSpec(memory_space=pl.ANY)] * len(passed), out_specs=[lyr] * 4)
    return pl.pallas_call(
        kern, name=name, grid_spec=grid_spec, out_shape=[jax.ShapeDtypeStruct(w.shape, F32)] * 4,
        input_output_aliases={6 + i: i for i in range(len(passed))},
        compiler_params=_params(("arbitrary",)))(core.reshape(1), w, m, v, mine, theirs, *passed)


def _cast_into_slot(src, slot, name, beside=None):
    rows, cols = src.shape
    tm = _half_tile(rows)

    def kern(s_ref, x_ref, o_ref):
        o_ref[...] = x_ref[...].astype(o_ref.dtype)

    res, extra = _host(
        kern, beside, name=name, grid=(rows // tm,), in_specs=[pl.BlockSpec((tm, cols), lambda r, s: (r, 0))],
        out_specs=pl.BlockSpec((None, tm, cols), lambda r, s: (s[0], r, 0)),
        out_shape=jax.ShapeDtypeStruct((N_CHIPS, rows, cols), BF16), scratch_shapes=[], semantics=("arbitrary",),
        args=[src], prefetch=[slot.reshape(1)])
    return res if beside is None else (res, extra)


FLIP_C = [(0, 0, 1)]
FLIP_XY = [(1, 0, 0), (0, 1, 0), (1, 1, 0)]
FLIP_ALL = [(fx, fy, fc) for fx in (0, 1) for fy in (0, 1) for fc in (0, 1) if (fx, fy, fc) != (0, 0, 0)]


def _chip(dev):
    return 2 * dev[0] + dev[1]


def _device(dev):
    return 4 * dev[0] + 2 * dev[1] + dev[2]


def _exchange(name, srcs, dst_shapes, rels, src_view, dst_view, own_view=None, in_place=False):
    n, n_rel = len(srcs), len(rels)

    def body(*refs):
        src_refs, dst_refs = refs[:n], refs[n:2 * n]
        send_sems, recv_sems, own_sems = refs[2 * n:]
        me = (lax.axis_index("x"), lax.axis_index("y"), lax.axis_index("c"))
        peers = [tuple(1 - m if f else m for m, f in zip(me, rel)) for rel in rels]

        def copy(i, k, sender, receiver):
            return pltpu.make_async_remote_copy(
                src_ref=src_view(i, src_refs[i], sender, receiver), dst_ref=dst_view(i, dst_refs[i], sender, k),
                send_sem=send_sems.at[i * n_rel + k], recv_sem=recv_sems.at[i * n_rel + k], device_id=receiver,
                device_id_type=pl.DeviceIdType.MESH)

        sends = [copy(i, k, me, peer) for i in range(n) for k, peer in enumerate(peers)]
        for cp in sends:
            cp.start()
        mine = []
        if own_view is not None:
            for i in range(n):
                frm, to = own_view(i, src_refs[i], dst_refs[i], me)
                mine.append(pltpu.make_async_copy(frm, to, own_sems.at[i]))
                mine[-1].start()
        for i in range(n):
            for k, peer in enumerate(peers):
                copy(i, k, peer, me).wait_recv()
        for cp in sends:
            cp.wait_send()
        for cp in mine:
            cp.wait()

    any_spec = pl.BlockSpec(memory_space=pl.ANY)
    return pl.pallas_call(
        body, name=name, out_shape=[jax.ShapeDtypeStruct(s, d) for s, d in dst_shapes],
        in_specs=[any_spec] * n, out_specs=[any_spec] * n,
        input_output_aliases={i: i for i in range(n)} if in_place else {},
        scratch_shapes=[pltpu.SemaphoreType.DMA((n * n_rel,)), pltpu.SemaphoreType.DMA((n * n_rel,)),
                        pltpu.SemaphoreType.DMA((n,))])(*srcs)


def _gather_plan(bufs, rows_of=None):
    n = len(bufs)
    rows_of = rows_of or [(0, b.shape[1]) for b in bufs]
    starts = [r0 for r0, _ in rows_of]
    halves = [nr // 2 for _, nr in rows_of]

    def tools(src_refs, dst_refs, sems):
        ici_send, ici_recv, d2d_send, d2d_recv = sems
        x, y, c = lax.axis_index("x"), lax.axis_index("y"), lax.axis_index("c")
        sibling = (x, y, 1 - c)
        chips = [(1 - x, y), (x, 1 - y), (1 - x, 1 - y)]

        def rows(i, chip, core):
            return dst_refs[i].at[2 * chip[0] + chip[1], pl.ds(starts[i] + core * halves[i], halves[i]), :]

        def over_ici(i, k, src, chip_from, to):
            return pltpu.make_async_remote_copy(
                src_ref=src, dst_ref=rows(i, chip_from, c), send_sem=ici_send.at[3 * i + k],
                recv_sem=ici_recv.at[3 * i + k], device_id=to, device_id_type=pl.DeviceIdType.MESH)

        def over_d2d(i, k, core):
            return pltpu.make_async_remote_copy(
                src_ref=rows(i, chips[k], core), dst_ref=rows(i, chips[k], core), send_sem=d2d_send.at[3 * i + k],
                recv_sem=d2d_recv.at[3 * i + k], device_id=sibling, device_id_type=pl.DeviceIdType.MESH)

        def my_send(i, k):
            my_half = src_refs[i].at[2 * x + y, pl.ds(starts[i] + c * halves[i], halves[i]), :]
            return over_ici(i, k, my_half, (x, y), (*chips[k], c))

        def my_arrival(i, k):
            return over_ici(i, k, rows(i, chips[k], c), chips[k], (x, y, c))

        return c, over_d2d, my_send, my_arrival

    def start(src_refs, dst_refs, sems):
        _, _, my_send, _ = tools(src_refs, dst_refs, sems)
        for i in range(n):
            for k in range(3):
                my_send(i, k).start()

    def finish(src_refs, dst_refs, sems):
        c, over_d2d, my_send, my_arrival = tools(src_refs, dst_refs, sems)
        passed = []
        for i in range(n):
            for k in range(3):
                my_arrival(i, k).wait_recv()
                passed.append(over_d2d(i, k, c))
                passed[-1].start()
        for i in range(n):
            for k in range(3):
                over_d2d(i, k, 1 - c).wait_recv()
        for i in range(n):
            for k in range(3):
                my_send(i, k).wait_send()
        for cp in passed:
            cp.wait_send()

    return _Beside(bufs, [(b.shape, b.dtype, i) for i, b in enumerate(bufs)], [3 * n] * 4, start, finish)


def _swap_plan(grads):
    n = len(grads)
    hrs = [g.shape[1] // 2 for g in grads]

    def copy(i, src_refs, dst_refs, sems, sender_core):
        x, y, c = lax.axis_index("x"), lax.axis_index("y"), lax.axis_index("c")
        return pltpu.make_async_remote_copy(
            src_ref=src_refs[i].at[:, pl.ds((1 - sender_core) * hrs[i], hrs[i]), :], dst_ref=dst_refs[i],
            send_sem=sems[0].at[i], recv_sem=sems[1].at[i], device_id=(x, y, 1 - c),
            device_id_type=pl.DeviceIdType.MESH)

    def start(src_refs, dst_refs, sems):
        c = lax.axis_index("c")
        for i in range(n):
            copy(i, src_refs, dst_refs, sems, c).start()

    def finish(src_refs, dst_refs, sems):
        c = lax.axis_index("c")
        for i in range(n):
            copy(i, src_refs, dst_refs, sems, 1 - c).wait_recv()
        for i in range(n):
            copy(i, src_refs, dst_refs, sems, c).wait_send()

    return _Beside(grads, [((N_CHIPS, hr, g.shape[2]), g.dtype, None) for g, hr in zip(grads, hrs)], [n] * 2,
                   start, finish)


def _scatter_plan(sums):
    n = len(sums)

    def copy(i, k, src_refs, dst_refs, sems, sender, receiver):
        return pltpu.make_async_remote_copy(
            src_ref=src_refs[i].at[_chip(receiver)], dst_ref=dst_refs[i].at[k], send_sem=sems[0].at[3 * i + k],
            recv_sem=sems[1].at[3 * i + k], device_id=receiver, device_id_type=pl.DeviceIdType.MESH)

    def each(fn, src_refs, dst_refs, sems, outgoing):
        me = (lax.axis_index("x"), lax.axis_index("y"), lax.axis_index("c"))
        for i in range(n):
            for k, rel in enumerate(FLIP_XY):
                peer = tuple(1 - m if f else m for m, f in zip(me, rel))
                fn(copy(i, k, src_refs, dst_refs, sems, *((me, peer) if outgoing else (peer, me))))

    def start(src_refs, dst_refs, sems):
        each(lambda cp: cp.start(), src_refs, dst_refs, sems, True)

    def finish(src_refs, dst_refs, sems):
        each(lambda cp: cp.wait_recv(), src_refs, dst_refs, sems, False)
        each(lambda cp: cp.wait_send(), src_refs, dst_refs, sems, True)

    return _Beside(sums, [((3,) + s.shape[1:], s.dtype, None) for s in sums], [3 * n] * 2, start, finish)


def _pack(parts, dtype, multiple):
    flat = jnp.concatenate([p.reshape(-1).astype(dtype) for p in parts])
    pad = (-flat.shape[0]) % multiple
    return jnp.pad(flat, (0, pad)) if pad else flat


def _unpack_shards(packs, names, shapes, axes):
    out, off = {}, 0
    for nme in names:
        sz = math.prod(shapes[nme])
        out[nme] = jnp.concatenate([packs[j, off:off + sz].reshape(shapes[nme]) for j in range(N_CHIPS)], axis=axes[nme])
        off += sz
    return out


def _shards_of(full, axis):
    sz = full.shape[axis] // N_CHIPS
    return [lax.slice_in_dim(full, j * sz, (j + 1) * sz, axis=axis) for j in range(N_CHIPS)]


def kernel(x, a_norm_w, a_in_proj, a_conv_w, a_conv_b, a_dt_bias, a_A_log, a_D, a_gnorm_w, a_out_proj, kv_norm_w, w_kvf, b_f, k_norm_w, b_norm_w, w_q, q_norm_w, w_o, ffn_norm_w, w_gate_up, w_down, loss_target, m_a_norm_w, m_a_in_proj, m_a_conv_w, m_a_conv_b, m_a_dt_bias, m_a_A_log, m_a_D, m_a_gnorm_w, m_a_out_proj, m_kv_norm_w, m_w_kvf, m_b_f, m_k_norm_w, m_b_norm_w, m_w_q, m_q_norm_w, m_w_o, m_ffn_norm_w, m_w_gate_up, m_w_down, v_a_norm_w, v_a_in_proj, v_a_conv_w, v_a_conv_b, v_a_dt_bias, v_a_A_log, v_a_D, v_a_gnorm_w, v_a_out_proj, v_kv_norm_w, v_w_kvf, v_b_f, v_k_norm_w, v_b_norm_w, v_w_q, v_q_norm_w, v_w_o, v_ffn_norm_w, v_w_gate_up, v_w_down):
    args = locals()
    w = {n: args[n] for n in WEIGHTS}
    mom = {n: args['m_' + n] for n in WEIGHTS}
    var = {n: args['v_' + n] for n in WEIGHTS}
    shapes = {n: w[n].shape for n in WEIGHTS}

    cx, cy, cc = lax.axis_index("x"), lax.axis_index("y"), lax.axis_index("c")
    my_chip = 2 * cx + cy

    xs_in = x[0]
    target = loss_target[0]
    s_len, d_model = xs_in.shape
    n_heads_ssm = a_dt_bias.shape[-1]
    d_inner = n_heads_ssm * SSM_HEAD_DIM
    d_xbc = a_conv_w.shape[-1] * N_CHIPS
    d_state = (d_xbc - d_inner) // (2 * SSM_GROUPS)
    kh = n_heads_ssm // SSM_GROUPS
    n_att = b_f.shape[0]
    d_att = n_att * ATT_HEAD_DIM
    d_ff = w_down.shape[1] * N_CHIPS

    small_names = list(SMALL_SHARDED)
    sp = _pack([w[n] for n in small_names], F32, PACK_COLS).reshape(-1, PACK_COLS)
    sp_all = _exchange("gather_small", [sp], [((N_CHIPS,) + sp.shape, F32)], FLIP_XY,
                       lambda i, r, me, peer: r, lambda i, r, sender, k: r.at[_chip(sender)],
                       lambda i, s, d, me: (s, d.at[_chip(me)]))[0]
    full = _unpack_shards(sp_all.reshape(N_CHIPS, -1), small_names, shapes, SMALL_SHARDED)
    for n in SMALL:
        if n not in full:
            full[n] = w[n]

    big2d = {'in': a_in_proj[0], 'out': a_out_proj[0], 'gu0': w_gate_up[0], 'dn0': w_down[0], 'kvf': w_kvf,
             'q': w_q[0], 'o': w_o[0], 'gu1': w_gate_up[1], 'dn1': w_down[1]}
    big_keys = list(big2d)
    norm_a, gnorm = full['a_norm_w'], full['a_gnorm_w']
    gathered = {}

    in_buf = _cast_into_slot(big2d['in'], my_chip, "cast_in")
    later = [k for k in big_keys if k != 'in']
    moved = [math.prod(big2d[k].shape) for k in later] + [s_len * d_model]
    units = big2d['in'].shape[0] // 32
    share = [m * units / sum(moved) for m in moved]
    piece = [int(sh) for sh in share]
    for i in sorted(range(len(share)), key=lambda i: share[i] - piece[i], reverse=True)[:units - sum(piece)]:
        piece[i] += 1
    slot, row0 = {}, 0
    for k, rows_k in zip(later + [None], [32 * p for p in piece]):
        plan = _gather_plan([in_buf], [(row0, rows_k)]) if rows_k else None
        row0 += rows_k
        if k is None:
            n1 = _rms_fwd(xs_in, norm_a, "norm_a", beside=plan)
        else:
            slot[k] = _cast_into_slot(big2d[k], my_chip, "cast_" + k, beside=plan)
        if plan is not None and k is None:
            n1, (in_buf,) = n1
        elif plan is not None:
            slot[k], (in_buf,) = slot[k]
    gathered['in'] = in_buf

    def gather(keys):
        return _gather_plan([slot[k] for k in keys])

    def pad_cols(a, width=LANES):
        return jnp.pad(a, ((0, 0), (0, width - a.shape[1])))

    w_in = jnp.concatenate([gathered['in'][j] for j in range(N_CHIPS)], axis=1)
    w_z, w_xbc, w_dt = w_in[:, :d_inner], w_in[:, d_inner:d_inner + d_xbc], pad_cols(w_in[:, d_inner + d_xbc:])
    conv_w, conv_b = full['a_conv_w'][0], full['a_conv_b']
    dt_bias, a_log = pad_cols(a_dt_bias), pad_cols(a_A_log)
    d_exp = jnp.repeat(a_D, SSM_HEAD_DIM, axis=1)
    kv_nw, b_nw = kv_norm_w.reshape(1, -1), b_norm_w
    k_nw, q_nw = k_norm_w.reshape(1, -1), q_norm_w
    bf_pad = pad_cols(b_f.reshape(1, -1))
    ffn_nw = [ffn_norm_w[i:i + 1] for i in range(2)]

    def to_groups(a):
        g = a[:, :n_heads_ssm].reshape(s_len, SSM_GROUPS, kh).transpose(1, 0, 2)
        return jnp.pad(g, ((0, 0), (0, 0), (0, LANES - kh)))

    def to_groups_t(a):
        g = a[:, :n_heads_ssm].reshape(s_len, SSM_GROUPS, kh).transpose(1, 2, 0)
        return jnp.pad(g, ((0, 0), (0, 8 - kh), (0, 0)))

    def from_groups(col, row=None):
        a = col[:, :, :kh].transpose(1, 0, 2).reshape(s_len, n_heads_ssm)
        if row is not None:
            a = a + row[:, :kh, :].transpose(2, 0, 1).reshape(s_len, n_heads_ssm)
        return pad_cols(a)

    z = _mm(n1, w_z, 'nn', "in_z")
    xbc_raw, (gathered['out'],) = _mm(n1, w_xbc, 'nn', "in_xbc", beside=gather(['out']))
    dtraw = _mm(n1, w_dt, 'nn', "in_dt")
    xbc = _conv_fwd(xbc_raw, conv_w, conv_b, "conv")
    dt, acum = _dt_fwd(dtraw, dt_bias, a_log, "dt")
    dt_g, ac_g, act_g = to_groups(dt), to_groups(acum), to_groups_t(acum)
    (y_ssd, states), (gathered['gu0'],) = _ssd_fwd(xbc, dt_g, ac_g, act_g, d_inner, d_state, kh, "ssd",
                                                   beside=gather(['gu0']))
    yn, (gathered['q'],) = _gate_fwd(y_ssd, xbc, z, d_exp, gnorm, d_inner, "gate", beside=gather(['q']))
    w_out = gathered['out'].reshape(-1, d_model)
    h1, (gathered['o'],) = _mm(yn, w_out, 'nn', "out_proj", add=xs_in, beside=gather(['o']))

    w_gu, w_dn = {}, {}

    def ffn_fwd(h, i, also):
        nrm = _rms_fwd(h, ffn_nw[i], f"ffn{i}_norm")
        w_gu[i] = gathered[f'gu{i}']
        gu = _mm(nrm, w_gu[i], 'nn', f"ffn{i}_up", out_dtype=BF16, shards=True,
                 beside=gather(also) if also else None)
        if also:
            gu, arrived = gu
            gathered.update(zip(also, arrived))
        act = _swiglu_fwd(gu, d_ff, f"ffn{i}_act")
        w_dn[i] = gathered[f'dn{i}'].reshape(-1, d_model)
        return nrm, gu, act, _mm(act, w_dn[i], 'nn', f"ffn{i}_down", add=h)

    n2, gu0, act0, h2 = ffn_fwd(h1, 0, ['dn0', 'kvf'])
    w_kvf_full = jnp.concatenate([gathered['kvf'][j] for j in range(N_CHIPS)], axis=1)
    w_k, w_v, w_f = w_kvf_full[:, :d_att], w_kvf_full[:, d_att:2 * d_att], pad_cols(w_kvf_full[:, 2 * d_att:])
    w_qm, w_om = gathered['q'].reshape(-1, d_att), gathered['o'].reshape(-1, d_model)
    nkv = _rms_fwd(h2, kv_nw, "kv_norm")
    k_raw = _mm(nkv, w_k, 'nn', "proj_k")
    v_att = _mm(nkv, w_v, 'nn', "proj_v", out_dtype=BF16)
    f_raw = _mm(nkv, w_f, 'nn', "proj_f")
    k_att = _rms_fwd(k_raw, k_nw, "k_norm")
    cum = _forget_fwd(f_raw, bf_pad, "forget")
    cq_rep = jnp.repeat(cum[:, :n_att], ATT_HEAD_DIM, axis=1)
    tb = min(ATT_BLOCK, s_len)
    ck = cum[:, :n_att].T.reshape(n_att, s_len // tb, 1, tb)
    n3 = _rms_fwd(h2, b_nw, "b_norm")
    q_raw = _mm(n3, w_qm, 'nn', "proj_q")
    q_att = _rms_fwd(q_raw, q_nw, "q_norm")
    (o_att, lse_rep), arrived = _attn_fwd(q_att, k_att, v_att, cq_rep, ck, "attn", beside=gather(['gu1', 'dn1']))
    gathered.update(zip(['gu1', 'dn1'], arrived))
    h3 = _mm(o_att, w_om, 'nn', "proj_o", add=h2)
    n4, gu1, act1, h4 = ffn_fwd(h3, 1, [])

    def loss_fn(h, t):
        err = h - t
        sq = jnp.sum(jnp.sum(err * err, axis=1, keepdims=True), axis=0, keepdims=True)
        return err * (1.0 / d_model), err * (1.0 / d_model), sq * (0.5 / d_model)

    dh4, dh4_b, loss_part = _rowwise(loss_fn, "loss", [h4, target], outs=[(d_model, F32), (d_model, BF16)],
                                     accs=[(1, 1)])

    n_f = n_att
    g_big, chip_sums, parts = {}, {}, {}

    def col_shards(pieces):
        fullw = jnp.concatenate(pieces, axis=1)
        return fullw.reshape(fullw.shape[0], N_CHIPS, -1).transpose(1, 0, 2)

    def row_shards(a):
        return a.reshape(N_CHIPS, -1, a.shape[1])

    def add_sibling(keys, got):
        for k, a in zip(keys, got):
            chip_sums[k] = _sum_half(g_big[k], a, cc, "reduce_sum2_" + k)

    def scatter(keys, tag):
        todo = [k for k in keys if k not in chip_sums]
        if todo:
            g_list = [g_big[k] for k in todo]
            hrs = [g.shape[1] // 2 for g in g_list]
            add_sibling(todo, _exchange("reduce_d2d_" + tag, g_list,
                                        [((N_CHIPS, hr, g.shape[2]), BF16) for g, hr in zip(g_list, hrs)], FLIP_C,
                                        lambda i, r, me, peer: r.at[:, pl.ds((1 - me[2]) * hrs[i], hrs[i]), :],
                                        lambda i, r, sender, k: r))
        return _scatter_plan([chip_sums[k] for k in keys])

    def ffn_bwd(dh, dh_b, h, nrm, gu, act, i, hosted):
        dact = _mm(dh_b, w_dn[i], 'nt', f"ffn{i}_down_dx", beside=scatter(hosted[:1], f"a{i}") if hosted else None)
        g_dn = _mm(act, dh_b, 'tn', f"ffn{i}_down_dw", out_dtype=BF16,
                   beside=scatter(hosted[1:], f"b{i}") if hosted else None)
        if hosted:
            (dact, first), (g_dn, second) = dact, g_dn
            parts.update(zip(hosted, first + second))
        dgu = _swiglu_bwd(gu, dact, d_ff, f"ffn{i}_act_bwd")
        dn = _mm(dgu, w_gu[i], 'nt', f"ffn{i}_up_dx", shards=True)
        g_gu = _mm(nrm, dgu, 'tn', f"ffn{i}_up_dw", out_dtype=BF16, shards=True)
        mine = [f"gu{i}", f"dn{i}"]
        g_big.update(zip(mine, [g_gu, row_shards(g_dn)]))
        (dh_new, dh_new_b, g_nw), got = _rms_bwd(h, ffn_nw[i], dn, f"ffn{i}_norm_bwd", extra=dh, copy_bf16=True,
                                                 beside=_swap_plan([g_big[k] for k in mine]))
        add_sibling(mine, got)
        return dh_new, dh_new_b, g_nw

    dh3, dh3_b, g_fnw1 = ffn_bwd(dh4, dh4_b, h3, n4, gu1, act1, 1, [])
    do_att = _mm(dh3_b, w_om, 'nt', "proj_o_dx", out_dtype=BF16)
    g_wo = _mm(o_att, dh3_b, 'tn', "proj_o_dw", out_dtype=BF16)
    g_big.update(o=row_shards(g_wo))
    late_keys = ['gu1', 'dn1', 'o']
    plan_late = scatter(late_keys, "late")

    def delta_fn(a, b):
        prod = a.astype(F32) * b.astype(F32)
        return jnp.concatenate([jnp.broadcast_to(jnp.sum(prod[:, g], axis=1, keepdims=True), (a.shape[0], ATT_HEAD_DIM))
                                for g in _lane_groups(a.shape[1], ATT_HEAD_DIM)], axis=1)

    delta_rep = _rowwise(delta_fn, "attn_delta", [do_att, o_att], outs=[(d_att, F32)])[0]
    (dq_att, dk_att, dv_att, dcq_rep, dck), arrived = _attn_bwd(q_att, k_att, v_att, do_att, lse_rep, delta_rep,
                                                                cq_rep, ck, "attn_bwd", beside=plan_late)
    parts.update(zip(late_keys, arrived))
    dq_raw, g_qnw = _rms_bwd(q_raw, q_nw, dq_att, "q_norm_bwd", out_dtype=BF16)
    dn3 = _mm(dq_raw, w_qm, 'nt', "proj_q_dx")
    g_wq = _mm(n3, dq_raw, 'tn', "proj_q_dw", out_dtype=BF16)
    dh2, g_bnw = _rms_bwd(h2, b_nw, dn3, "b_norm_bwd", extra=dh3)
    dk_raw, g_knw = _rms_bwd(k_raw, k_nw, dk_att, "k_norm_bwd", out_dtype=BF16)
    dcum = pad_cols(dcq_rep.reshape(s_len, n_att, ATT_HEAD_DIM)[:, :, 0] + dck.reshape(n_att, s_len).T)
    df_raw, g_bf = _forget_bwd(f_raw, bf_pad, dcum, "forget_bwd")
    dnkv = _mm(dk_raw, w_k, 'nt', "proj_k_dx")
    dnkv = _mm(dv_att, w_v, 'nt', "proj_v_dx", add=dnkv)
    dnkv = _mm(df_raw, w_f, 'nt', "proj_f_dx", add=dnkv)
    g_wk = _mm(nkv, dk_raw, 'tn', "proj_k_dw", out_dtype=BF16)
    g_wv = _mm(nkv, dv_att, 'tn', "proj_v_dw", out_dtype=BF16)
    g_wf = _mm(nkv, df_raw, 'tn', "proj_f_dw", out_dtype=BF16)
    g_big.update(q=row_shards(g_wq), kvf=col_shards([g_wk, g_wv, g_wf[:, :n_f]]))
    (dh2, dh2_b, g_kvnw), got = _rms_bwd(h2, kv_nw, dnkv, "kv_norm_bwd", extra=dh2, copy_bf16=True,
                                         beside=_swap_plan([g_big['q'], g_big['kvf']]))
    add_sibling(['q', 'kvf'], got)
    dh1, dh1_b, g_fnw0 = ffn_bwd(dh2, dh2_b, h1, n2, gu0, act0, 0, ['q', 'kvf'])
    plan_ffn0 = scatter(['gu0', 'dn0'], "ffn0")

    dyn = _mm(dh1_b, w_out, 'nt', "out_proj_dx")
    g_wout = _mm(yn, dh1_b, 'tn', "out_proj_dw", out_dtype=BF16)
    dy_ssd, dxs_skip, dz, g_dexp, g_gnorm = _gate_bwd(y_ssd, xbc, z, d_exp, gnorm, dyn, d_inner, "gate_bwd")
    (dxs, d_b, d_c, ddt_g, dacc_g, dacr_g), arrived = _ssd_bwd(xbc, dt_g, ac_g, act_g, states, dy_ssd, dxs_skip,
                                                               d_inner, d_state, kh, "ssd_bwd", beside=plan_ffn0)
    parts.update(zip(['gu0', 'dn0'], arrived))
    g_big.update(out=row_shards(g_wout))
    dxbc_act = jnp.concatenate([dxs, d_b, d_c], axis=1)
    (du, g_convw, g_convb), (parts['out'],) = _conv_bwd(xbc_raw, conv_w, conv_b, dxbc_act, "conv_bwd",
                                                        beside=scatter(['out'], "out"))
    draw, g_dtb, g_alog = _dt_bwd(dtraw, dt_bias, a_log, from_groups(ddt_g), from_groups(dacc_g, dacr_g), "dt_bwd")
    g_wz = _mm(n1, dz, 'tn', "in_z_dw", out_dtype=BF16)
    g_wxbc = _mm(n1, du, 'tn', "in_xbc_dw", out_dtype=BF16)
    g_wdt = _mm(n1, draw, 'tn', "in_dt_dw", out_dtype=BF16)
    g_in = jnp.concatenate([g_wz, g_wxbc, g_wdt[:, :n_heads_ssm]], axis=1)
    g_in = g_in.reshape(2, d_model // 2, N_CHIPS, -1).transpose(0, 2, 1, 3)
    g_big.update(in0=g_in[0], in1=g_in[1])
    dn1, (parts['in0'],) = _mm(du, w_xbc, 'nt', "in_xbc_dx", beside=scatter(['in0'], "in0"))
    dn1, (parts['in1'],) = _mm(dz, w_z, 'nt', "in_z_dx", add=dn1, beside=scatter(['in1'], "in1"))
    dn1 = _mm(draw, w_dt, 'nt', "in_dt_dx", add=dn1)
    dx, g_norm_a = _rms_bwd(xs_in, norm_a, dn1, "norm_a_bwd", extra=dh1)

    g_small = {
        'a_norm_w': g_norm_a, 'a_conv_w': g_convw[None], 'a_conv_b': g_convb,
        'a_dt_bias': g_dtb[:, :n_heads_ssm], 'a_A_log': g_alog[:, :n_heads_ssm],
        'a_D': g_dexp.reshape(n_heads_ssm, SSM_HEAD_DIM).sum(axis=1).reshape(1, -1), 'a_gnorm_w': g_gnorm,
        'kv_norm_w': g_kvnw.reshape(-1), 'b_f': g_bf[0, :n_f], 'k_norm_w': g_knw.reshape(-1), 'b_norm_w': g_bnw,
        'q_norm_w': g_qnw, 'ffn_norm_w': jnp.concatenate([g_fnw0, g_fnw1], axis=0),
    }

    sg = _pack([g_small[n] for n in SMALL] + [loss_part], F32, 8 * PACK_COLS).reshape(-1, PACK_COLS)
    sg_all = _exchange("reduce_small", [sg], [((2 * N_CHIPS,) + sg.shape, F32)], FLIP_ALL,
                       lambda i, r, me, peer: r, lambda i, r, sender, k: r.at[_device(sender)],
                       lambda i, s, d, me: (s, d.at[_device(me)]))[0]
    sg_sum = _sum_arrays([sg_all[d] for d in range(2 * N_CHIPS)], F32, "reduce_small_sum").reshape(-1)
    red_small, off = {}, 0
    for n in SMALL:
        shp = g_small[n].shape
        red_small[n] = sg_sum[off:off + math.prod(shp)].reshape(shp)
        off += math.prod(shp)
    loss = sg_sum[off]

    red_keys = list(parts)
    half_sums = [_sum_parts(chip_sums[k], parts[k], my_chip, "reduce_sum4_" + k) for k in red_keys]
    others = _exchange("reduce_back", half_sums, [(h.shape, F32) for h in half_sums], FLIP_C,
                       lambda i, r, me, peer: r, lambda i, r, sender, k: r)
    mine_of, theirs_of = dict(zip(red_keys, half_sums)), dict(zip(red_keys, others))

    grads, delta, new_m, new_v = {}, {}, {}, {}
    layers_of = {'a_in_proj': ['in0', 'in1'], 'a_out_proj': ['out'], 'w_kvf': ['kvf'], 'w_q': ['q'], 'w_o': ['o'],
                 'w_gate_up': ['gu0', 'gu1'], 'w_down': ['dn0', 'dn1']}
    for n, keys in layers_of.items():
        three_d = (len(keys), math.prod(shapes[n][:-1]) // len(keys), shapes[n][-1])
        res = None
        for layer, k in enumerate(keys):
            res = _adamw_big(w[n].reshape(three_d), mom[n].reshape(three_d), var[n].reshape(three_d), mine_of[k],
                             theirs_of[k], cc, layer, res, "adamw_" + k)
        grads[n], delta[n], new_m[n], new_v[n] = [r.reshape(shapes[n]) for r in res]
    for n in SMALL:
        if n in SMALL_SHARDED:
            ax = SMALL_SHARDED[n]
            grads[n] = lax.dynamic_slice_in_dim(red_small[n], my_chip * shapes[n][ax], shapes[n][ax], axis=ax)
        else:
            grads[n] = red_small[n]

    packed = [_pack([src[n] for n in SMALL], F32, 8 * LANES).reshape(-1, LANES) for src in (w, grads, mom, var)]
    small_out = _adamw(*packed, "adamw_small")
    for store, flat in zip((delta, new_m, new_v), small_out):
        flat, off = flat.reshape(-1), 0
        for n in SMALL:
            sz = math.prod(shapes[n])
            store[n] = flat[off:off + sz].reshape(shapes[n])
            off += sz

    return (loss, dx[None], *[grads[n] for n in WEIGHTS], *[delta[n] for n in WEIGHTS],
            *[new_m[n] for n in WEIGHTS], *[new_v[n] for n in WEIGHTS])
```

```python
import functools
import math

import jax
import jax.numpy as jnp
from jax import lax
from jax.experimental import pallas as pl
from jax.experimental.pallas import tpu as pltpu

F32, BF16 = jnp.float32, jnp.bfloat16
EPS = 1e-6
SSM_HEAD_DIM = 64
SSM_GROUPS = 8
SSD_CHUNK = 128
ATT_HEAD_DIM = 128
LANES = 128
N_CHIPS = 4
NEG = -1e30
ADAM_LR, ADAM_B1, ADAM_B2, ADAM_EPS, ADAM_WD, ADAM_STEP = 0.001, 0.9, 0.999, 1e-08, 0.01, 10
VMEM_LIMIT_BYTES = 56 * 1024 * 1024
PACK_COLS = 1024
PACK_ROWS = 256
MM_K_TILES = (2816, 2048, 1408, 1024, 512, 256, 128)
MM_OPERAND_BYTES = 12 * 1024 * 1024

NN = ((1,), (0,))
NT = ((1,), (1,))
TN = ((0,), (0,))

WEIGHTS = ['a_norm_w', 'a_in_proj', 'a_conv_w', 'a_conv_b', 'a_dt_bias', 'a_A_log', 'a_D', 'a_gnorm_w', 'a_out_proj',
           'kv_norm_w', 'w_kvf', 'b_f', 'k_norm_w', 'b_norm_w', 'w_q', 'q_norm_w', 'w_o', 'ffn_norm_w', 'w_gate_up',
           'w_down']
BIG = {'a_in_proj': 2, 'a_out_proj': 1, 'w_kvf': 1, 'w_q': 1, 'w_o': 1, 'w_gate_up': 2, 'w_down': 1}
SMALL_SHARDED = {'a_norm_w': 1, 'a_conv_w': 2, 'a_conv_b': 1, 'a_gnorm_w': 1}
SMALL = [n for n in WEIGHTS if n not in BIG]


def _dot(a, b, dims):
    return lax.dot_general(a, b, (dims, ((), ())), preferred_element_type=F32)


def _params(sem=None):
    return pltpu.CompilerParams(dimension_semantics=sem, vmem_limit_bytes=VMEM_LIMIT_BYTES)


def _tile(dim, cap):
    for t in (1408, 1024, 512, 256, 128):
        if t <= cap and dim % t == 0:
            return t
    return dim


class _Beside:
    def __init__(self, operands, results, sem_sizes, start, finish):
        self.operands, self.results, self.sem_sizes = list(operands), list(results), list(sem_sizes)
        self.start, self.finish = start, finish


def _host(kern, beside, *, name, grid, in_specs, out_specs, out_shape, scratch_shapes, semantics, args, prefetch=()):
    single = not isinstance(out_shape, (list, tuple))
    out_specs = [out_specs] if single else list(out_specs)
    out_shape = [out_shape] if single else list(out_shape)
    n_pre = len(prefetch)

    def call(body, in_specs, out_specs, out_shape, scratch_shapes, semantics, aliases, args):
        if not n_pre:
            return pl.pallas_call(body, name=name, grid=grid, in_specs=list(in_specs), out_specs=list(out_specs),
                                  out_shape=out_shape, scratch_shapes=list(scratch_shapes),
                                  input_output_aliases=aliases, compiler_params=_params(semantics))(*args)
        spec = pltpu.PrefetchScalarGridSpec(num_scalar_prefetch=n_pre, grid=grid, in_specs=list(in_specs),
                                            out_specs=list(out_specs), scratch_shapes=list(scratch_shapes))
        return pl.pallas_call(body, name=name, grid_spec=spec, out_shape=out_shape,
                              input_output_aliases={n_pre + i: o for i, o in aliases.items()},
                              compiler_params=_params(semantics))(*prefetch, *args)

    if beside is None:
        res = call(kern, in_specs, out_specs, out_shape, scratch_shapes, semantics, {}, args)
        return (res[0] if single else res), []
    n_in, n_out, n_scr = len(in_specs), len(out_specs), len(scratch_shapes)
    nb_in, nb_out = len(beside.operands), len(beside.results)

    def body(*refs):
        pre, refs = refs[:n_pre], refs[n_pre:]
        ins, b_ins = refs[:n_in], refs[n_in:n_in + nb_in]
        outs = refs[n_in + nb_in:n_in + nb_in + n_out]
        b_outs = refs[n_in + nb_in + n_out:n_in + nb_in + n_out + nb_out]
        rest = refs[n_in + nb_in + n_out + nb_out:]
        scr, sems = rest[:n_scr], rest[n_scr:]
        ids = [pl.program_id(a) for a in range(len(grid))]
        first = functools.reduce(jnp.logical_and, [i == 0 for i in ids])
        last = functools.reduce(jnp.logical_and, [i == g - 1 for i, g in zip(ids, grid)])

        @pl.when(first)
        def _():
            beside.start(b_ins, b_outs, sems)

        kern(*pre, *ins, *outs, *scr)

        @pl.when(last)
        def _():
            beside.finish(b_ins, b_outs, sems)

    any_spec = pl.BlockSpec(memory_space=pl.ANY)
    res = call(body, list(in_specs) + [any_spec] * nb_in, out_specs + [any_spec] * nb_out,
               out_shape + [jax.ShapeDtypeStruct(s, d) for s, d, _ in beside.results],
               list(scratch_shapes) + [pltpu.SemaphoreType.DMA((k,)) for k in beside.sem_sizes],
               ("arbitrary",) * len(grid),
               {n_in + op: n_out + r for r, (_, _, op) in enumerate(beside.results) if op is not None},
               list(args) + list(beside.operands))
    mine = res[:n_out]
    return (mine[0] if single else mine), list(res[n_out:])


def _alone(beside, name):
    return _host(lambda: None, beside, name=name, grid=(1,), in_specs=[], out_specs=[], out_shape=[],
                 scratch_shapes=[], semantics=("arbitrary",), args=[])[1]


def _mm(a, b, mode, name, out_dtype=F32, add=None, shards=False, beside=None):
    if mode == 'nn':
        (m, k), n = a.shape, (b.shape[2] * N_CHIPS if shards else b.shape[1])
    elif mode == 'nt':
        (m, k), n = a.shape, (b.shape[1] if shards else b.shape[0])
    else:
        (k, m), n = a.shape, b.shape[1]
    per_chip = (k if mode == 'nt' else n) // N_CHIPS
    tm = _tile(m, 1024)
    tn = _tile(per_chip if shards and mode != 'nt' else n, 1408 if shards else 1024)
    k_dim = per_chip if shards and mode == 'nt' else k
    a_bytes, b_bytes = jnp.dtype(a.dtype).itemsize, jnp.dtype(b.dtype).itemsize
    tk = next((t for t in MM_K_TILES if k_dim % t == 0 and t * (tm * a_bytes + tn * b_bytes) <= MM_OPERAND_BYTES), k_dim)
    nk = k // tk
    in_place = nk > 1 and out_dtype == F32
    a_spec = pl.BlockSpec((tk, tm), lambda i, j, q: (q, i)) if mode == 'tn' else pl.BlockSpec((tm, tk), lambda i, j, q: (i, q))
    b_spec = pl.BlockSpec((tn, tk), lambda i, j, q: (j, q)) if mode == 'nt' else pl.BlockSpec((tk, tn), lambda i, j, q: (q, j))
    o_spec = pl.BlockSpec((tm, tn), lambda i, j, q: (i, j))
    out_struct = jax.ShapeDtypeStruct((m, n), out_dtype)
    if shards:
        per = per_chip // (tk if mode == 'nt' else tn)
        if mode == 'nn':
            b_spec = pl.BlockSpec((None, tk, tn), lambda i, j, q: (j // per, q, j % per))
        elif mode == 'nt':
            b_spec = pl.BlockSpec((None, tn, tk), lambda i, j, q: (q // per, j, q % per))
        else:
            out_struct = jax.ShapeDtypeStruct((N_CHIPS, m, per_chip), out_dtype)
    out_spec = pl.BlockSpec((None, tm, tn), lambda i, j, q: (j // per, i, j % per)) if shards and mode == 'tn' else o_spec
    dims = {'nn': NN, 'nt': NT, 'tn': TN}[mode]

    n_ins = 3 if add is not None else 2

    def kern(*refs):
        a_ref, b_ref, o_ref = refs[0], refs[1], refs[n_ins]
        acc = o_ref if in_place or nk == 1 else refs[n_ins + 1]
        q = pl.program_id(2)
        part = _dot(a_ref[...].astype(BF16), b_ref[...].astype(BF16), dims)

        def first():
            return part if add is None else part + refs[2][...]

        if nk == 1:
            o_ref[...] = first().astype(o_ref.dtype)
            return

        @pl.when(q == 0)
        def _():
            acc[...] = first()

        @pl.when(q > 0)
        def _():
            acc[...] += part

        if not in_place:
            @pl.when(q == nk - 1)
            def _():
                o_ref[...] = acc[...].astype(o_ref.dtype)

    ins, specs = [a, b], [a_spec, b_spec]
    if add is not None:
        ins.append(add)
        specs.append(o_spec)
    scratch = [] if in_place or nk == 1 else [pltpu.VMEM((tm, tn), F32)]
    res, extra = _host(kern, beside, name=name, grid=(m // tm, n // tn, nk), in_specs=specs, out_specs=out_spec,
                       out_shape=out_struct, scratch_shapes=scratch,
                       semantics=("parallel", "parallel", "arbitrary"), args=ins)
    return res if beside is None else (res, extra)


def _rowwise(fn, name, rows, bcast=(), outs=(), accs=(), tm=256, beside=None):
    rows = [r if isinstance(r, tuple) else (r, r.shape[1], 0) for r in rows]
    n_rows = rows[0][0].shape[0]
    tm = min(tm, n_rows)
    assert n_rows % tm == 0, (name, n_rows, tm)
    n_in, n_out = len(rows) + len(bcast), len(outs)
    in_specs = [pl.BlockSpec((tm, w), functools.partial(lambda i, cb: (i, cb), cb=cb)) for _, w, cb in rows]
    in_specs += [pl.BlockSpec(b.shape, lambda i: (0, 0)) for b in bcast]
    out_specs = [pl.BlockSpec((tm, w), lambda i: (i, 0)) for w, _ in outs]
    out_specs += [pl.BlockSpec(s, lambda i: (0, 0)) for s in accs]
    out_shape = [jax.ShapeDtypeStruct((n_rows, w), d) for w, d in outs] + [jax.ShapeDtypeStruct(s, F32) for s in accs]

    def kern(*refs):
        vals = fn(*[r[...] for r in refs[:n_in]])
        vals = vals if isinstance(vals, (tuple, list)) else (vals,)
        o_refs = refs[n_in:]
        for r, v in zip(o_refs[:n_out], vals[:n_out]):
            r[...] = v.astype(r.dtype)
        if accs:
            @pl.when(pl.program_id(0) == 0)
            def _():
                for r in o_refs[n_out:]:
                    r[...] = jnp.zeros_like(r)

            for r, v in zip(o_refs[n_out:], vals[n_out:]):
                r[...] += v

    res, extra = _host(kern, beside, name=name, grid=(n_rows // tm,), in_specs=in_specs, out_specs=out_specs,
                       out_shape=out_shape, scratch_shapes=[], semantics=("arbitrary",),
                       args=[r[0] for r in rows] + list(bcast))
    return res if beside is None else (res, extra)


def _rms(x, w):
    xf = x.astype(F32)
    return xf * lax.rsqrt(jnp.mean(xf * xf, axis=-1, keepdims=True) + EPS) * w


def _lane_groups(width, group):
    return [slice(g * group, (g + 1) * group) for g in range(width // group)]


def _rms_fwd(x, w, name, tm=256, beside=None):
    def fn(x, w):
        return jnp.concatenate([_rms(x[:, g], w) for g in _lane_groups(x.shape[1], w.shape[1])], axis=1)

    res = _rowwise(fn, name, [x], [w], outs=[(x.shape[1], BF16)], tm=tm, beside=beside)
    return res[0] if beside is None else (res[0][0], res[1])


def _rms_bwd(x, w, dy, name, extra=None, out_dtype=F32, tm=256, copy_bf16=False, beside=None):
    def fn(x, dy, *rest):
        w = rest[-1]
        dxs, dw = [], jnp.zeros(w.shape, F32)
        for g in _lane_groups(x.shape[1], w.shape[1]):
            _, vjp = jax.vjp(_rms, x[:, g], w)
            dx_g, dw_g = vjp(dy[:, g].astype(F32))
            dxs.append(dx_g)
            dw = dw + dw_g
        dx = jnp.concatenate(dxs, axis=1)
        if extra is not None:
            dx = dx + rest[0]
        return (dx, dx, dw) if copy_bf16 else (dx, dw)

    rows = [x, dy] + ([extra] if extra is not None else [])
    outs = [(x.shape[1], out_dtype)] + ([(x.shape[1], BF16)] if copy_bf16 else [])
    return _rowwise(fn, name, rows, [w], outs=outs, accs=[w.shape], tm=tm, beside=beside)


def _sigmoid(x):
    return 1.0 / (1.0 + jnp.exp(-x))


def _softplus(x):
    return jnp.maximum(x, 0.0) + jnp.log(1.0 + jnp.exp(-jnp.abs(x)))


def _swiglu_fwd(gu, d_ff, name):
    def fn(g, u):
        g, u = g.astype(F32), u.astype(F32)
        return g * _sigmoid(g) * u

    return _rowwise(fn, name, [(gu, d_ff, 0), (gu, d_ff, 1)], outs=[(d_ff, BF16)], tm=128)[0]


def _swiglu_bwd(gu, dact, d_ff, name):
    def fn(g, u, da):
        g, u = g.astype(F32), u.astype(F32)
        s = _sigmoid(g)
        dg = da * u * s * (1.0 + g * (1.0 - s))
        du = da * g * s
        return jnp.concatenate([dg, du], axis=1)

    return _rowwise(fn, name, [(gu, d_ff, 0), (gu, d_ff, 1), dact], outs=[(2 * d_ff, BF16)], tm=128)[0]


def _cumsum_rows(v, reverse=False):
    n = v.shape[0]
    row = lax.broadcasted_iota(jnp.int32, v.shape, 0)
    sh = 1
    while sh < n:
        if reverse:
            v = v + jnp.where(row < n - sh, pltpu.roll(v, n - sh, 0), 0.0)
        else:
            v = v + jnp.where(row >= sh, pltpu.roll(v, sh, 0), 0.0)
        sh *= 2
    return v


def _conv_fwd(u, w, b, name):
    s, c = u.shape
    kw = w.shape[0]
    tc = _tile(c, 128)

    def kern(u_ref, w_ref, b_ref, o_ref):
        uu = u_ref[...]
        row = lax.broadcasted_iota(jnp.int32, uu.shape, 0)
        acc = jnp.zeros_like(uu) + b_ref[...]
        for k in range(kw):
            sh = kw - 1 - k
            uk = uu if sh == 0 else jnp.where(row >= sh, pltpu.roll(uu, sh, 0), 0.0)
            acc = acc + w_ref[pl.ds(k, 1), :] * uk
        o_ref[...] = acc * _sigmoid(acc)

    return pl.pallas_call(
        kern, name=name, grid=(c // tc,),
        in_specs=[pl.BlockSpec((s, tc), lambda j: (0, j)), pl.BlockSpec((kw, tc), lambda j: (0, j)),
                  pl.BlockSpec((1, tc), lambda j: (0, j))],
        out_specs=pl.BlockSpec((s, tc), lambda j: (0, j)), out_shape=jax.ShapeDtypeStruct((s, c), F32),
        compiler_params=_params(("parallel",)))(u, w, b)


def _conv_bwd(u, w, b, dacts, name, beside=None):
    s, c = u.shape
    kw = w.shape[0]
    tc = _tile(c, 128)
    first = [sum(d.shape[1] for d in dacts[:i]) // tc for i in range(len(dacts))]
    count = [d.shape[1] // tc for d in dacts]

    def kern(u_ref, w_ref, b_ref, *rest):
        d_refs, (du_ref, dw_ref, db_ref) = rest[:len(dacts)], rest[len(dacts):]
        j = pl.program_id(0)
        d_out = d_refs[0][...]
        for i in range(1, len(dacts)):
            d_out = jnp.where(j >= first[i], d_refs[i][...], d_out)
        uu = u_ref[...]
        row = lax.broadcasted_iota(jnp.int32, uu.shape, 0)
        shifted = []
        acc = jnp.zeros_like(uu) + b_ref[...]
        for k in range(kw):
            sh = kw - 1 - k
            uk = uu if sh == 0 else jnp.where(row >= sh, pltpu.roll(uu, sh, 0), 0.0)
            shifted.append(uk)
            acc = acc + w_ref[pl.ds(k, 1), :] * uk
        sg = _sigmoid(acc)
        dacc = d_out * sg * (1.0 + acc * (1.0 - sg))
        db_ref[...] = jnp.sum(dacc, axis=0, keepdims=True)
        du = jnp.zeros_like(uu)
        for k in range(kw):
            sh = kw - 1 - k
            dw_ref[pl.ds(k, 1), :] = jnp.sum(dacc * shifted[k], axis=0, keepdims=True)
            dk = dacc if sh == 0 else jnp.where(row < s - sh, pltpu.roll(dacc, s - sh, 0), 0.0)
            du = du + w_ref[pl.ds(k, 1), :] * dk
        du_ref[...] = du.astype(du_ref.dtype)

    col = lambda j: (0, j)
    d_specs = [pl.BlockSpec((s, tc), functools.partial(lambda j, f, n: (0, jnp.clip(j - f, 0, n - 1)), f=f, n=n))
               for f, n in zip(first, count)]
    return _host(
        kern, beside, name=name, grid=(c // tc,),
        in_specs=[pl.BlockSpec((s, tc), col), pl.BlockSpec((kw, tc), col), pl.BlockSpec((1, tc), col)] + d_specs,
        out_specs=[pl.BlockSpec((s, tc), col), pl.BlockSpec((kw, tc), col), pl.BlockSpec((1, tc), col)],
        out_shape=[jax.ShapeDtypeStruct((s, c), BF16), jax.ShapeDtypeStruct((kw, c), F32),
                   jax.ShapeDtypeStruct((1, c), F32)],
        scratch_shapes=[], semantics=("parallel",), args=[u, w, b] + list(dacts))


def _dt_fwd(dtraw, bias, a_log, name):
    def fn(raw, bias, a_log):
        dt = _softplus(raw + bias)
        return dt, _cumsum_rows(dt * (-jnp.exp(a_log)))

    return _rowwise(fn, name, [dtraw], [bias, a_log], outs=[(LANES, F32), (LANES, F32)], tm=SSD_CHUNK)


def _dt_bwd(dtraw, bias, a_log, ddt, dacum, name):
    def fn(raw, ddt, dac, bias, a_log):
        z = raw + bias
        dt = _softplus(z)
        a_neg = -jnp.exp(a_log)
        da = _cumsum_rows(dac, reverse=True)
        draw = (ddt + da * a_neg) * _sigmoid(z)
        return draw, jnp.sum(draw, axis=0, keepdims=True), jnp.sum(da * dt, axis=0, keepdims=True) * a_neg

    return _rowwise(fn, name, [dtraw, ddt, dacum], [bias, a_log], outs=[(LANES, BF16)],
                    accs=[(1, LANES), (1, LANES)], tm=SSD_CHUNK)


def _ssd_pieces(l, gw):
    lane_k = lax.broadcasted_iota(jnp.int32, (l, LANES), 1)
    lane_of = lax.broadcasted_iota(jnp.int32, (gw, LANES), 0)
    head_of = lax.broadcasted_iota(jnp.int32, (gw, LANES), 1)
    in_head = ((lane_of >= head_of * SSM_HEAD_DIM) & (lane_of < (head_of + 1) * SSM_HEAD_DIM)).astype(BF16)

    lane_w = lax.broadcasted_iota(jnp.int32, (l, gw), 1)

    def col(blk, k):
        return jnp.sum(jnp.where(lane_k == k, blk, 0.0), axis=1, keepdims=True)

    def expand(blk):
        acc = jnp.zeros((l, gw), F32)
        for k in range(gw // SSM_HEAD_DIM):
            acc = jnp.where((lane_w >= k * SSM_HEAD_DIM) & (lane_w < (k + 1) * SSM_HEAD_DIM), col(blk, k), acc)
        return acc

    def collapse(wide):
        hi = wide.astype(BF16)
        lo = (wide - hi.astype(F32)).astype(BF16)
        return _dot(hi, in_head, NN) + _dot(lo, in_head, NN)

    return lane_k, col, expand, collapse


def _head_block(k):
    per_block = LANES // SSM_HEAD_DIM
    lane = lax.broadcasted_iota(jnp.int32, (1, LANES), 1)
    lo = (k % per_block) * SSM_HEAD_DIM
    return slice((k // per_block) * LANES, (k // per_block + 1) * LANES), (lane >= lo) & (lane < lo + SSM_HEAD_DIM)


def _by_block(pieces, n_blocks, like):
    zero = jnp.zeros((like.shape[0], LANES), F32)
    return jnp.concatenate([pieces.get(p, zero) for p in range(n_blocks)], axis=1)


def _ssd_specs(l, gw, n, n_xs_blocks):
    g_axis = SSM_GROUPS
    return dict(
        xs=lambda cm: pl.BlockSpec((l, gw), lambda g, c: (cm(c), g)),
        b=lambda cm: pl.BlockSpec((l, n), lambda g, c: (cm(c), n_xs_blocks + g)),
        c=lambda cm: pl.BlockSpec((l, n), lambda g, c: (cm(c), n_xs_blocks + g_axis + g)),
        col=lambda cm: pl.BlockSpec((None, l, LANES), lambda g, c: (g, cm(c), 0)),
        row=lambda cm: pl.BlockSpec((None, 8, l), lambda g, c: (g, 0, cm(c))),
        state=lambda cm: pl.BlockSpec((None, None, n, gw), lambda g, c: (cm(c), g, 0, 0)),
    )


def _ssd_fwd(xbc, dt_g, ac_g, act_g, d_inner, n, kh, name, beside=None):
    s = xbc.shape[0]
    l, g_n = SSD_CHUNK, SSM_GROUPS
    gw, nc = d_inner // g_n, s // l
    sp = _ssd_specs(l, gw, n, d_inner // n)
    fwd = lambda c: c

    def kern(xs_ref, b_ref, c_ref, dt_ref, ac_ref, act_ref, y_ref, s0_ref, st):
        @pl.when(pl.program_id(1) == 0)
        def _():
            st[...] = jnp.zeros_like(st)

        s0 = st[...]
        s0_ref[...] = s0
        xs, ac = xs_ref[...], ac_ref[...]
        _, col, expand, _ = _ssd_pieces(l, gw)
        ace = expand(ac)
        x = xs * expand(dt_ref[...])
        xb, bb, cb_ = x.astype(BF16), b_ref[...].astype(BF16), c_ref[...].astype(BF16)
        cb = _dot(cb_, bb, NT)
        ri = lax.broadcasted_iota(jnp.int32, (l, l), 0)
        ci = lax.broadcasted_iota(jnp.int32, (l, l), 1)
        causal = ri >= ci
        y_diag = {}
        for k in range(kh):
            seg = col(ac, k) - act_ref[pl.ds(k, 1), :]
            m = jnp.where(causal, cb * jnp.exp(jnp.where(causal, seg, 0.0)), 0.0)
            blk, mine = _head_block(k)
            yk = _dot(m.astype(BF16), xb[:, blk], NN)
            y_diag[blk.start // LANES] = jnp.where(mine, yk, y_diag.get(blk.start // LANES, 0.0))
        y_ref[...] = _dot(cb_, s0.astype(BF16), NN) * jnp.exp(ace) + _by_block(y_diag, gw // LANES, xs)
        row_w = lax.broadcasted_iota(jnp.int32, (l, gw), 0)
        alast = jnp.sum(jnp.where(row_w == l - 1, ace, 0.0), axis=0, keepdims=True)
        st[...] = s0 * jnp.exp(alast) + _dot(bb, (jnp.exp(alast - ace) * x).astype(BF16), TN)

    return _host(
        kern, beside, name=name, grid=(g_n, nc),
        in_specs=[sp['xs'](fwd), sp['b'](fwd), sp['c'](fwd), sp['col'](fwd), sp['col'](fwd), sp['row'](fwd)],
        out_specs=[pl.BlockSpec((l, gw), lambda g, c: (c, g)), sp['state'](fwd)],
        out_shape=[jax.ShapeDtypeStruct((s, d_inner), F32), jax.ShapeDtypeStruct((nc, g_n, n, gw), F32)],
        scratch_shapes=[pltpu.VMEM((n, gw), F32)], semantics=("arbitrary", "arbitrary"),
        args=[xbc, xbc, xbc, dt_g, ac_g, act_g])


def _ssd_bwd(xbc, dt_g, ac_g, act_g, s0_all, dy, dxs_skip, d_inner, n, kh, name, beside=None):
    s = xbc.shape[0]
    l, g_n = SSD_CHUNK, SSM_GROUPS
    gw, nc = d_inner // g_n, s // l
    sp = _ssd_specs(l, gw, n, d_inner // n)
    rev = lambda c: nc - 1 - c

    def kern(xs_ref, b_ref, c_ref, dt_ref, ac_ref, act_ref, s0_ref, dy_ref, skip_ref,
             dxs_ref, db_ref, dc_ref, ddt_ref, dacc_ref, dacr_ref, dst):
        @pl.when(pl.program_id(1) == 0)
        def _():
            dst[...] = jnp.zeros_like(dst)

        dsn = dst[...]
        s0 = s0_ref[...]
        xs, ac, dy = xs_ref[...], ac_ref[...], dy_ref[...]
        lane_k, col, expand, collapse = _ssd_pieces(l, gw)
        ace, dte = expand(ac), expand(dt_ref[...])
        x = xs * dte
        xb, bb, cb_ = x.astype(BF16), b_ref[...].astype(BF16), c_ref[...].astype(BF16)
        s0b, dsnb, dyb = s0.astype(BF16), dsn.astype(BF16), dy.astype(BF16)
        cb = _dot(cb_, bb, NT)
        ri = lax.broadcasted_iota(jnp.int32, (l, l), 0)
        ci = lax.broadcasted_iota(jnp.int32, (l, l), 1)
        causal = ri >= ci
        row_w = lax.broadcasted_iota(jnp.int32, (l, gw), 0)
        e = jnp.exp(ace)
        alast = jnp.sum(jnp.where(row_w == l - 1, ace, 0.0), axis=0, keepdims=True)
        gdec = jnp.exp(alast)
        wt = jnp.exp(alast - ace)
        cs = _dot(cb_, s0b, NN)
        dcs = (dy * e).astype(BF16)
        d_c = _dot(dcs, s0b, NT)
        ds_off = _dot(cb_, dcs, TN)
        dace = dy * cs * e
        dalast = jnp.sum(dsn * s0, axis=0, keepdims=True) * gdec
        z = wt * x
        dz = _dot(bb, dsnb, NN)
        d_b = _dot(z.astype(BF16), dsnb, NT)
        dx = dz * wt
        t = dz * z
        dalast = dalast + jnp.sum(t, axis=0, keepdims=True)
        dace = dace - t
        dcb = jnp.zeros((l, l), F32)
        dac_col = jnp.zeros((l, LANES), F32)
        dx_diag = {}
        for k in range(kh):
            seg = col(ac, k) - act_ref[pl.ds(k, 1), :]
            dk = jnp.exp(jnp.where(causal, seg, 0.0))
            mk = jnp.where(causal, cb * dk, 0.0)
            blk, mine = _head_block(k)
            dxk = _dot(mk.astype(BF16), dyb[:, blk], TN)
            dx_diag[blk.start // LANES] = jnp.where(mine, dxk, dx_diag.get(blk.start // LANES, 0.0))
            dmk = _dot(jnp.where(mine, dy[:, blk], 0.0).astype(BF16), xb[:, blk], NT)
            dcb = dcb + jnp.where(causal, dmk * dk, 0.0)
            dseg = dmk * mk
            dac_col = jnp.where(lane_k == k, jnp.sum(dseg, axis=1, keepdims=True), dac_col)
            dacr_ref[pl.ds(k, 1), :] = -jnp.sum(dseg, axis=0, keepdims=True)
        for k in range(kh, 8):
            dacr_ref[pl.ds(k, 1), :] = jnp.zeros((1, l), F32)
        dx = dx + _by_block(dx_diag, gw // LANES, xs)
        dcbb = dcb.astype(BF16)
        dc_ref[...] = d_c + _dot(dcbb, bb, NN)
        db_ref[...] = d_b + _dot(dcbb, cb_, TN)
        dace = jnp.where(row_w == l - 1, dace + dalast, dace)
        dacc_ref[...] = dac_col + collapse(dace)
        ddt_ref[...] = collapse(dx * xs)
        dxs_ref[...] = dx * dte + skip_ref[...]
        dst[...] = dsn * gdec + ds_off

    return _host(
        kern, beside, name=name, grid=(g_n, nc),
        in_specs=[sp['xs'](rev), sp['b'](rev), sp['c'](rev), sp['col'](rev), sp['col'](rev), sp['row'](rev),
                  sp['state'](rev), pl.BlockSpec((l, gw), lambda g, c: (rev(c), g)),
                  pl.BlockSpec((l, gw), lambda g, c: (rev(c), g))],
        out_specs=[pl.BlockSpec((l, gw), lambda g, c: (rev(c), g)), pl.BlockSpec((l, n), lambda g, c: (rev(c), g)),
                   pl.BlockSpec((l, n), lambda g, c: (rev(c), g)), sp['col'](rev), sp['col'](rev), sp['row'](rev)],
        out_shape=[jax.ShapeDtypeStruct((s, d_inner), F32), jax.ShapeDtypeStruct((s, g_n * n), F32),
                   jax.ShapeDtypeStruct((s, g_n * n), F32), jax.ShapeDtypeStruct((g_n, s, LANES), F32),
                   jax.ShapeDtypeStruct((g_n, s, LANES), F32), jax.ShapeDtypeStruct((g_n, 8, s), F32)],
        scratch_shapes=[pltpu.VMEM((n, gw), F32)], semantics=("arbitrary", "arbitrary"),
        args=[xbc, xbc, xbc, dt_g, ac_g, act_g, s0_all, dy, dxs_skip])


def _gate(y, xs, z, d_exp, gw):
    t = (y + xs * d_exp) * (z * _sigmoid(z))
    width = t.shape[1]
    gsz = width // SSM_GROUPS
    lane = lax.broadcasted_iota(jnp.int32, t.shape, 1)
    t2 = t * t
    scale = jnp.zeros_like(t)
    for g in range(SSM_GROUPS):
        in_g = (lane >= g * gsz) & (lane < (g + 1) * gsz)
        ms = jnp.sum(jnp.where(in_g, t2, 0.0), axis=1, keepdims=True) * (1.0 / gsz)
        scale = jnp.where(in_g, lax.rsqrt(ms + EPS), scale)
    return t * scale * gw


def _gate_fwd(y, xbc, z, d_exp, gw, d_inner, name, beside=None):
    res = _rowwise(_gate, name, [y, (xbc, d_inner, 0), z], [d_exp, gw], outs=[(d_inner, BF16)], tm=128, beside=beside)
    return res[0] if beside is None else (res[0][0], res[1])


def _gate_bwd(y, xbc, z, d_exp, gw, dyn, d_inner, name):
    def fn(y, xs, z, dyn, d_exp, gw):
        _, vjp = jax.vjp(_gate, y, xs, z, d_exp, gw)
        return vjp(dyn)

    return _rowwise(fn, name, [y, (xbc, d_inner, 0), z, dyn], [d_exp, gw],
                    outs=[(d_inner, F32), (d_inner, F32), (d_inner, BF16)], accs=[d_exp.shape, gw.shape], tm=64)


def _forget_fwd(fraw, b_f, name):
    def kern(f_ref, b_ref, o_ref):
        o_ref[...] = _cumsum_rows(-_softplus(-(f_ref[...] + b_ref[...])))

    return pl.pallas_call(kern, name=name, out_shape=jax.ShapeDtypeStruct(fraw.shape, F32),
                          compiler_params=_params())(fraw, b_f)


def _forget_bwd(fraw, b_f, dcum, name):
    def kern(f_ref, b_ref, d_ref, df_ref, db_ref):
        df = _cumsum_rows(d_ref[...], reverse=True) * _sigmoid(-(f_ref[...] + b_ref[...]))
        df_ref[...] = df.astype(df_ref.dtype)
        db_ref[...] = jnp.sum(df, axis=0, keepdims=True)

    return pl.pallas_call(
        kern, name=name,
        out_shape=[jax.ShapeDtypeStruct(fraw.shape, BF16), jax.ShapeDtypeStruct((1, fraw.shape[1]), F32)],
        compiler_params=_params())(fraw, b_f, dcum)


ATT_BLOCK = 512


def _attn_fwd(q, k, v, cq_rep, ck, name, beside=None):
    s, hd = q.shape
    h_n, d = hd // ATT_HEAD_DIM, ATT_HEAD_DIM
    tb = min(ATT_BLOCK, s)
    nb = s // tb
    scale = d ** -0.5

    def kern(q_ref, k_ref, v_ref, cq_ref, ck_ref, o_ref, lse_ref):
        i = pl.program_id(1)
        qq = q_ref[...]
        cq = jnp.max(cq_ref[...], axis=1, keepdims=True)
        rowpos = i * tb + lax.broadcasted_iota(jnp.int32, (tb, tb), 0)
        coli = lax.broadcasted_iota(jnp.int32, (tb, tb), 1)

        def step(j, carry, diagonal):
            m, l_, acc = carry
            ks = pl.ds(pl.multiple_of(j * tb, tb), tb)
            sc = _dot(qq, k_ref[ks, :], NT) * scale + (cq - ck_ref[j])
            if diagonal:
                sc = jnp.where(j * tb + coli <= rowpos, sc, NEG)
            mn = jnp.maximum(m, jnp.max(sc, axis=1, keepdims=True))
            p = jnp.exp(sc - mn)
            alpha = jnp.exp(m - mn)
            l_ = alpha * l_ + jnp.sum(p, axis=1, keepdims=True)
            acc = alpha * acc + _dot(p.astype(BF16), v_ref[ks, :], NN)
            return mn, l_, acc

        init = (jnp.full((tb, 1), NEG, F32), jnp.zeros((tb, 1), F32), jnp.zeros((tb, d), F32))
        below = lax.fori_loop(0, i, lambda j, carry: step(j, carry, False), init)
        m, l_, acc = step(i, below, True)
        o_ref[...] = (acc / l_).astype(o_ref.dtype)
        lse_ref[...] = jnp.broadcast_to(m + jnp.log(l_), (tb, d))

    return _host(
        kern, beside, name=name, grid=(h_n, nb),
        in_specs=[pl.BlockSpec((tb, d), lambda h, i: (i, h)), pl.BlockSpec((s, d), lambda h, i: (0, h)),
                  pl.BlockSpec((s, d), lambda h, i: (0, h)), pl.BlockSpec((tb, d), lambda h, i: (i, h)),
                  pl.BlockSpec((None, nb, 1, tb), lambda h, i: (h, 0, 0, 0))],
        out_specs=[pl.BlockSpec((tb, d), lambda h, i: (i, h)), pl.BlockSpec((tb, d), lambda h, i: (i, h))],
        out_shape=[jax.ShapeDtypeStruct((s, hd), BF16), jax.ShapeDtypeStruct((s, hd), F32)],
        scratch_shapes=[], semantics=("parallel", "arbitrary"), args=[q, k, v, cq_rep, ck])


def _attn_bwd(q, k, v, do, lse_rep, delta_rep, cq_rep, ck, name, beside=None):
    s, hd = q.shape
    h_n, d = hd // ATT_HEAD_DIM, ATT_HEAD_DIM
    tb = min(ATT_BLOCK, s)
    nb = s // tb
    scale = d ** -0.5

    def kern(q_ref, do_ref, k_ref, v_ref, lse_ref, dl_ref, cq_ref, ck_ref, dq_ref, dk_ref, dv_ref, dcq_ref, dck_ref):
        j = pl.program_id(1)

        @pl.when(j == 0)
        def _():
            dq_ref[...] = jnp.zeros_like(dq_ref)
            dcq_ref[...] = jnp.zeros_like(dcq_ref)

        kj, vj, ckj = k_ref[...], v_ref[...], ck_ref[...]
        colpos = j * tb + lax.broadcasted_iota(jnp.int32, (tb, tb), 1)
        rowi = lax.broadcasted_iota(jnp.int32, (tb, tb), 0)

        def step(i, carry, diagonal):
            dk, dv, dck = carry
            rs = pl.ds(pl.multiple_of(i * tb, tb), tb)
            qi, doi = q_ref[rs, :], do_ref[rs, :]
            lse = jnp.max(lse_ref[rs, :], axis=1, keepdims=True)
            dl = jnp.max(dl_ref[rs, :], axis=1, keepdims=True)
            cq = jnp.max(cq_ref[rs, :], axis=1, keepdims=True)
            sc = _dot(qi, kj, NT) * scale + ((cq - lse) - ckj)
            p = jnp.exp(jnp.where(colpos <= i * tb + rowi, sc, NEG) if diagonal else sc)
            dp = _dot(doi, vj, NT)
            ds = p * (dp - dl)
            dsb = ds.astype(BF16)
            dv = dv + _dot(p.astype(BF16), doi, TN)
            dk = dk + _dot(dsb, qi, TN)
            dq_ref[rs, :] += _dot(dsb, kj, NN) * scale
            dcq_ref[rs, :] += jnp.broadcast_to(jnp.sum(ds, axis=1, keepdims=True), (tb, d))
            return dk, dv, dck - jnp.sum(ds, axis=0, keepdims=True)

        init = (jnp.zeros((tb, d), F32), jnp.zeros((tb, d), F32), jnp.zeros((1, tb), F32))
        dk, dv, dck = lax.fori_loop(j + 1, nb, lambda i, carry: step(i, carry, False), step(j, init, True))
        dk_ref[...] = dk * scale
        dv_ref[...] = dv.astype(dv_ref.dtype)
        dck_ref[...] = dck

    whole = pl.BlockSpec((s, d), lambda h, j: (0, h))
    blk = pl.BlockSpec((tb, d), lambda h, j: (j, h))
    ckb = pl.BlockSpec((None, None, 1, tb), lambda h, j: (h, j, 0, 0))
    return _host(
        kern, beside, name=name, grid=(h_n, nb),
        in_specs=[whole, whole, blk, blk, whole, whole, whole, ckb],
        out_specs=[whole, blk, blk, whole, ckb],
        out_shape=[jax.ShapeDtypeStruct((s, hd), F32), jax.ShapeDtypeStruct((s, hd), F32),
                   jax.ShapeDtypeStruct((s, hd), BF16), jax.ShapeDtypeStruct((s, hd), F32),
                   jax.ShapeDtypeStruct((h_n, nb, 1, tb), F32)],
        scratch_shapes=[], semantics=("arbitrary", "arbitrary"), args=[q, do, k, v, lse_rep, delta_rep, cq_rep, ck])


def _adamw(w, g, m, v, name):
    cols = w.shape[1]
    tm = _tile(w.shape[0], 128) if w.shape[0] % 128 == 0 else w.shape[0]
    return _rowwise(_adamw_math, name, [w, g, m, v], outs=[(cols, F32)] * 3, tm=tm)


def _sum_arrays(arrs, out_dtype, name):
    def fn(*xs):
        acc = xs[0].astype(F32)
        for x in xs[1:]:
            acc = acc + x.astype(F32)
        return acc

    tm = PACK_ROWS if arrs[0].shape[0] % PACK_ROWS == 0 else arrs[0].shape[0]
    return _rowwise(fn, name, list(arrs), outs=[(arrs[0].shape[1], out_dtype)], tm=tm)[0]


def _half_tile(rows):
    for t in (256, 176, 128, 64, 32, 16):
        if rows % t == 0:
            return t
    return rows


def _sum_half(g, got, core, name):
    n_chip, hr, cols = got.shape
    tm = _half_tile(hr)
    nt = hr // tm

    def kern(c_ref, g_ref, a_ref, o_ref):
        o_ref[...] = (g_ref[...].astype(F32) + a_ref[...].astype(F32)).astype(o_ref.dtype)

    grid_spec = pltpu.PrefetchScalarGridSpec(
        num_scalar_prefetch=1, grid=(n_chip * nt,),
        in_specs=[pl.BlockSpec((tm, cols), lambda r, c: (((r // nt) * 2 + c[0]) * nt + r % nt, 0)),
                  pl.BlockSpec((tm, cols), lambda r, c: (r, 0))],
        out_specs=pl.BlockSpec((tm, cols), lambda r, c: (r, 0)))
    out = pl.pallas_call(
        kern, name=name, grid_spec=grid_spec, out_shape=jax.ShapeDtypeStruct((n_chip * hr, cols), BF16),
        compiler_params=_params(("arbitrary",)))(core.reshape(1), g.reshape(-1, cols), got.reshape(-1, cols))
    return out.reshape(n_chip, hr, cols)


def _sum_parts(own, parts, chip, name):
    n_parts, hr, cols = parts.shape
    tm = _half_tile(hr)

    def kern(s_ref, t_ref, p_ref, o_ref):
        acc = t_ref[...].astype(F32)
        for j in range(n_parts):
            acc = acc + p_ref[j].astype(F32)
        o_ref[...] = acc

    grid_spec = pltpu.PrefetchScalarGridSpec(
        num_scalar_prefetch=1, grid=(hr // tm,),
        in_specs=[pl.BlockSpec((None, tm, cols), lambda r, s: (s[0], r, 0)),
                  pl.BlockSpec((n_parts, tm, cols), lambda r, s: (0, r, 0))],
        out_specs=pl.BlockSpec((tm, cols), lambda r, s: (r, 0)))
    return pl.pallas_call(
        kern, name=name, grid_spec=grid_spec, out_shape=jax.ShapeDtypeStruct((hr, cols), F32),
        compiler_params=_params(("arbitrary",)))(chip.reshape(1), own, parts)


def _adamw_math(w, g, m, v):
    m = ADAM_B1 * m + (1.0 - ADAM_B1) * g
    v = ADAM_B2 * v + (1.0 - ADAM_B2) * (g * g)
    m_hat = m * (1.0 / (1.0 - ADAM_B1 ** ADAM_STEP))
    v_hat = v * (1.0 / (1.0 - ADAM_B2 ** ADAM_STEP))
    return -ADAM_LR * (m_hat / (jnp.sqrt(v_hat) + ADAM_EPS) + ADAM_WD * w), m, v


def _adamw_big(w, m, v, mine, theirs, core, layer, prev, name):
    n_layers, rows, cols = w.shape
    hr = rows // 2
    tm = next(t for t in (128, 64, 32, 16, 8) if hr % t == 0)
    nt = hr // tm

    def kern(s_ref, w_ref, m_ref, v_ref, a_ref, b_ref, *rest):
        g_ref, d_ref, nm_ref, nv_ref = rest[-4:]
        g = jnp.where(pl.program_id(0) // nt == s_ref[0], a_ref[...], b_ref[...])
        g_ref[...] = g
        d_ref[...], nm_ref[...], nv_ref[...] = _adamw_math(w_ref[...], g, m_ref[...], v_ref[...])

    lyr = pl.BlockSpec((None, tm, cols), lambda r, s: (layer, r, 0))
    half = pl.BlockSpec((tm, cols), lambda r, s: (r % nt, 0))
    passed = [] if prev is None else list(prev)
    grid_spec = pltpu.PrefetchScalarGridSpec(
        num_scalar_prefetch=1, grid=(rows // tm,),
        in_specs=[lyr, lyr, lyr, half, half] + [pl.BlockSpec(memory_space=pl.ANY)] * len(passed), out_specs=[lyr] * 4)
    return pl.pallas_call(
        kern, name=name, grid_spec=grid_spec, out_shape=[jax.ShapeDtypeStruct(w.shape, F32)] * 4,
        input_output_aliases={6 + i: i for i in range(len(passed))},
        compiler_params=_params(("arbitrary",)))(core.reshape(1), w, m, v, mine, theirs, *passed)


def _cast_into_slot(src, slot, name, beside=None):
    rows, cols = src.shape
    tm = _half_tile(rows)

    def kern(s_ref, x_ref, o_ref):
        o_ref[...] = x_ref[...].astype(o_ref.dtype)

    res, extra = _host(
        kern, beside, name=name, grid=(rows // tm,), in_specs=[pl.BlockSpec((tm, cols), lambda r, s: (r, 0))],
        out_specs=pl.BlockSpec((None, tm, cols), lambda r, s: (s[0], r, 0)),
        out_shape=jax.ShapeDtypeStruct((N_CHIPS, rows, cols), BF16), scratch_shapes=[], semantics=("arbitrary",),
        args=[src], prefetch=[slot.reshape(1)])
    return res if beside is None else (res, extra)


FLIP_C = [(0, 0, 1)]
FLIP_XY = [(1, 0, 0), (0, 1, 0), (1, 1, 0)]
FLIP_ALL = [(fx, fy, fc) for fx in (0, 1) for fy in (0, 1) for fc in (0, 1) if (fx, fy, fc) != (0, 0, 0)]


def _chip(dev):
    return 2 * dev[0] + dev[1]


def _device(dev):
    return 4 * dev[0] + 2 * dev[1] + dev[2]


def _exchange(name, srcs, dst_shapes, rels, src_view, dst_view, own_view=None, in_place=False):
    n, n_rel = len(srcs), len(rels)

    def body(*refs):
        src_refs, dst_refs = refs[:n], refs[n:2 * n]
        send_sems, recv_sems, own_sems = refs[2 * n:]
        me = (lax.axis_index("x"), lax.axis_index("y"), lax.axis_index("c"))
        peers = [tuple(1 - m if f else m for m, f in zip(me, rel)) for rel in rels]

        def copy(i, k, sender, receiver):
            return pltpu.make_async_remote_copy(
                src_ref=src_view(i, src_refs[i], sender, receiver), dst_ref=dst_view(i, dst_refs[i], sender, k),
                send_sem=send_sems.at[i * n_rel + k], recv_sem=recv_sems.at[i * n_rel + k], device_id=receiver,
                device_id_type=pl.DeviceIdType.MESH)

        sends = [copy(i, k, me, peer) for i in range(n) for k, peer in enumerate(peers)]
        for cp in sends:
            cp.start()
        mine = []
        if own_view is not None:
            for i in range(n):
                frm, to = own_view(i, src_refs[i], dst_refs[i], me)
                mine.append(pltpu.make_async_copy(frm, to, own_sems.at[i]))
                mine[-1].start()
        for i in range(n):
            for k, peer in enumerate(peers):
                copy(i, k, peer, me).wait_recv()
        for cp in sends:
            cp.wait_send()
        for cp in mine:
            cp.wait()

    any_spec = pl.BlockSpec(memory_space=pl.ANY)
    return pl.pallas_call(
        body, name=name, out_shape=[jax.ShapeDtypeStruct(s, d) for s, d in dst_shapes],
        in_specs=[any_spec] * n, out_specs=[any_spec] * n,
        input_output_aliases={i: i for i in range(n)} if in_place else {},
        scratch_shapes=[pltpu.SemaphoreType.DMA((n * n_rel,)), pltpu.SemaphoreType.DMA((n * n_rel,)),
                        pltpu.SemaphoreType.DMA((n,))])(*srcs)


def _gather_plan(bufs, rows_of=None):
    n = len(bufs)
    rows_of = rows_of or [(0, b.shape[1]) for b in bufs]
    starts = [r0 for r0, _ in rows_of]
    halves = [nr // 2 for _, nr in rows_of]

    def tools(src_refs, dst_refs, sems):
        ici_send, ici_recv, d2d_send, d2d_recv = sems
        x, y, c = lax.axis_index("x"), lax.axis_index("y"), lax.axis_index("c")
        sibling = (x, y, 1 - c)
        chips = [(1 - x, y), (x, 1 - y), (1 - x, 1 - y)]

        def rows(i, chip, core):
            return dst_refs[i].at[2 * chip[0] + chip[1], pl.ds(starts[i] + core * halves[i], halves[i]), :]

        def over_ici(i, k, src, chip_from, to):
            return pltpu.make_async_remote_copy(
                src_ref=src, dst_ref=rows(i, chip_from, c), send_sem=ici_send.at[3 * i + k],
                recv_sem=ici_recv.at[3 * i + k], device_id=to, device_id_type=pl.DeviceIdType.MESH)

        def over_d2d(i, k, core):
            return pltpu.make_async_remote_copy(
                src_ref=rows(i, chips[k], core), dst_ref=rows(i, chips[k], core), send_sem=d2d_send.at[3 * i + k],
                recv_sem=d2d_recv.at[3 * i + k], device_id=sibling, device_id_type=pl.DeviceIdType.MESH)

        def my_send(i, k):
            my_half = src_refs[i].at[2 * x + y, pl.ds(starts[i] + c * halves[i], halves[i]), :]
            return over_ici(i, k, my_half, (x, y), (*chips[k], c))

        def my_arrival(i, k):
            return over_ici(i, k, rows(i, chips[k], c), chips[k], (x, y, c))

        return c, over_d2d, my_send, my_arrival

    def start(src_refs, dst_refs, sems):
        _, _, my_send, _ = tools(src_refs, dst_refs, sems)
        for i in range(n):
            for k in range(3):
                my_send(i, k).start()

    def finish(src_refs, dst_refs, sems):
        c, over_d2d, my_send, my_arrival = tools(src_refs, dst_refs, sems)
        passed = []
        for i in range(n):
            for k in range(3):
                my_arrival(i, k).wait_recv()
                passed.append(over_d2d(i, k, c))
                passed[-1].start()
        for i in range(n):
            for k in range(3):
                over_d2d(i, k, 1 - c).wait_recv()
        for i in range(n):
            for k in range(3):
                my_send(i, k).wait_send()
        for cp in passed:
            cp.wait_send()

    return _Beside(bufs, [(b.shape, b.dtype, i) for i, b in enumerate(bufs)], [3 * n] * 4, start, finish)


def _swap_plan(grads):
    n = len(grads)
    hrs = [g.shape[1] // 2 for g in grads]

    def copy(i, src_refs, dst_refs, sems, sender_core):
        x, y, c = lax.axis_index("x"), lax.axis_index("y"), lax.axis_index("c")
        return pltpu.make_async_remote_copy(
            src_ref=src_refs[i].at[:, pl.ds((1 - sender_core) * hrs[i], hrs[i]), :], dst_ref=dst_refs[i],
            send_sem=sems[0].at[i], recv_sem=sems[1].at[i], device_id=(x, y, 1 - c),
            device_id_type=pl.DeviceIdType.MESH)

    def start(src_refs, dst_refs, sems):
        c = lax.axis_index("c")
        for i in range(n):
            copy(i, src_refs, dst_refs, sems, c).start()

    def finish(src_refs, dst_refs, sems):
        c = lax.axis_index("c")
        for i in range(n):
            copy(i, src_refs, dst_refs, sems, 1 - c).wait_recv()
        for i in range(n):
            copy(i, src_refs, dst_refs, sems, c).wait_send()

    return _Beside(grads, [((N_CHIPS, hr, g.shape[2]), g.dtype, None) for g, hr in zip(grads, hrs)], [n] * 2,
                   start, finish)


def _scatter_plan(sums):
    n = len(sums)

    def copy(i, k, src_refs, dst_refs, sems, sender, receiver):
        return pltpu.make_async_remote_copy(
            src_ref=src_refs[i].at[_chip(receiver)], dst_ref=dst_refs[i].at[k], send_sem=sems[0].at[3 * i + k],
            recv_sem=sems[1].at[3 * i + k], device_id=receiver, device_id_type=pl.DeviceIdType.MESH)

    def each(fn, src_refs, dst_refs, sems, outgoing):
        me = (lax.axis_index("x"), lax.axis_index("y"), lax.axis_index("c"))
        for i in range(n):
            for k, rel in enumerate(FLIP_XY):
                peer = tuple(1 - m if f else m for m, f in zip(me, rel))
                fn(copy(i, k, src_refs, dst_refs, sems, *((me, peer) if outgoing else (peer, me))))

    def start(src_refs, dst_refs, sems):
        each(lambda cp: cp.start(), src_refs, dst_refs, sems, True)

    def finish(src_refs, dst_refs, sems):
        each(lambda cp: cp.wait_recv(), src_refs, dst_refs, sems, False)
        each(lambda cp: cp.wait_send(), src_refs, dst_refs, sems, True)

    return _Beside(sums, [((3,) + s.shape[1:], s.dtype, None) for s in sums], [3 * n] * 2, start, finish)


def _pack(parts, dtype, multiple):
    flat = jnp.concatenate([p.reshape(-1).astype(dtype) for p in parts])
    pad = (-flat.shape[0]) % multiple
    return jnp.pad(flat, (0, pad)) if pad else flat


def _unpack_shards(packs, names, shapes, axes):
    out, off = {}, 0
    for nme in names:
        sz = math.prod(shapes[nme])
        out[nme] = jnp.concatenate([packs[j, off:off + sz].reshape(shapes[nme]) for j in range(N_CHIPS)], axis=axes[nme])
        off += sz
    return out


def _shards_of(full, axis):
    sz = full.shape[axis] // N_CHIPS
    return [lax.slice_in_dim(full, j * sz, (j + 1) * sz, axis=axis) for j in range(N_CHIPS)]


def kernel(x, a_norm_w, a_in_proj, a_conv_w, a_conv_b, a_dt_bias, a_A_log, a_D, a_gnorm_w, a_out_proj, kv_norm_w, w_kvf, b_f, k_norm_w, b_norm_w, w_q, q_norm_w, w_o, ffn_norm_w, w_gate_up, w_down, loss_target, m_a_norm_w, m_a_in_proj, m_a_conv_w, m_a_conv_b, m_a_dt_bias, m_a_A_log, m_a_D, m_a_gnorm_w, m_a_out_proj, m_kv_norm_w, m_w_kvf, m_b_f, m_k_norm_w, m_b_norm_w, m_w_q, m_q_norm_w, m_w_o, m_ffn_norm_w, m_w_gate_up, m_w_down, v_a_norm_w, v_a_in_proj, v_a_conv_w, v_a_conv_b, v_a_dt_bias, v_a_A_log, v_a_D, v_a_gnorm_w, v_a_out_proj, v_kv_norm_w, v_w_kvf, v_b_f, v_k_norm_w, v_b_norm_w, v_w_q, v_q_norm_w, v_w_o, v_ffn_norm_w, v_w_gate_up, v_w_down):
    args = locals()
    w = {n: args[n] for n in WEIGHTS}
    mom = {n: args['m_' + n] for n in WEIGHTS}
    var = {n: args['v_' + n] for n in WEIGHTS}
    shapes = {n: w[n].shape for n in WEIGHTS}

    cx, cy, cc = lax.axis_index("x"), lax.axis_index("y"), lax.axis_index("c")
    my_chip = 2 * cx + cy

    xs_in = x[0]
    target = loss_target[0]
    s_len, d_model = xs_in.shape
    n_heads_ssm = a_dt_bias.shape[-1]
    d_inner = n_heads_ssm * SSM_HEAD_DIM
    d_xbc = a_conv_w.shape[-1] * N_CHIPS
    d_state = (d_xbc - d_inner) // (2 * SSM_GROUPS)
    kh = n_heads_ssm // SSM_GROUPS
    n_att = b_f.shape[0]
    d_att = n_att * ATT_HEAD_DIM
    d_ff = w_down.shape[1] * N_CHIPS

    small_names = list(SMALL_SHARDED)
    sp = _pack([w[n] for n in small_names], F32, PACK_COLS).reshape(-1, PACK_COLS)
    sp_all = _exchange("gather_small", [sp], [((N_CHIPS,) + sp.shape, F32)], FLIP_XY,
                       lambda i, r, me, peer: r, lambda i, r, sender, k: r.at[_chip(sender)],
                       lambda i, s, d, me: (s, d.at[_chip(me)]))[0]
    full = _unpack_shards(sp_all.reshape(N_CHIPS, -1), small_names, shapes, SMALL_SHARDED)
    for n in SMALL:
        if n not in full:
            full[n] = w[n]

    big2d = {'in': a_in_proj[0], 'out': a_out_proj[0], 'gu0': w_gate_up[0], 'dn0': w_down[0], 'kvf': w_kvf,
             'q': w_q[0], 'o': w_o[0], 'gu1': w_gate_up[1], 'dn1': w_down[1]}
    big_keys = list(big2d)
    norm_a, gnorm = full['a_norm_w'], full['a_gnorm_w']
    gathered = {}

    in_buf = _cast_into_slot(big2d['in'], my_chip, "cast_in")
    later = [k for k in big_keys if k != 'in']
    moved = [math.prod(big2d[k].shape) for k in later] + [s_len * d_model]
    units = big2d['in'].shape[0] // 32
    share = [m * units / sum(moved) for m in moved]
    piece = [int(sh) for sh in share]
    for i in sorted(range(len(share)), key=lambda i: share[i] - piece[i], reverse=True)[:units - sum(piece)]:
        piece[i] += 1
    slot, row0 = {}, 0
    for k, rows_k in zip(later + [None], [32 * p for p in piece]):
        plan = _gather_plan([in_buf], [(row0, rows_k)]) if rows_k else None
        row0 += rows_k
        if k is None:
            n1 = _rms_fwd(xs_in, norm_a, "norm_a", beside=plan)
        else:
            slot[k] = _cast_into_slot(big2d[k], my_chip, "cast_" + k, beside=plan)
        if plan is not None and k is None:
            n1, (in_buf,) = n1
        elif plan is not None:
            slot[k], (in_buf,) = slot[k]
    gathered['in'] = in_buf

    def gather(keys):
        return _gather_plan([slot[k] for k in keys])

    def pad_cols(a, width=LANES):
        return jnp.pad(a, ((0, 0), (0, width - a.shape[1])))

    w_in = jnp.concatenate([gathered['in'][j] for j in range(N_CHIPS)], axis=1)
    w_z, w_xbc, w_dt = w_in[:, :d_inner], w_in[:, d_inner:d_inner + d_xbc], pad_cols(w_in[:, d_inner + d_xbc:])
    conv_w, conv_b = full['a_conv_w'][0], full['a_conv_b']
    dt_bias, a_log = pad_cols(a_dt_bias), pad_cols(a_A_log)
    d_exp = jnp.repeat(a_D, SSM_HEAD_DIM, axis=1)
    kv_nw, b_nw = kv_norm_w.reshape(1, -1), b_norm_w
    k_nw, q_nw = k_norm_w.reshape(1, -1), q_norm_w
    bf_pad = pad_cols(b_f.reshape(1, -1))
    ffn_nw = [ffn_norm_w[i:i + 1] for i in range(2)]

    def to_groups(a):
        g = a[:, :n_heads_ssm].reshape(s_len, SSM_GROUPS, kh).transpose(1, 0, 2)
        return jnp.pad(g, ((0, 0), (0, 0), (0, LANES - kh)))

    def to_groups_t(a):
        g = a[:, :n_heads_ssm].reshape(s_len, SSM_GROUPS, kh).transpose(1, 2, 0)
        return jnp.pad(g, ((0, 0), (0, 8 - kh), (0, 0)))

    def from_groups(col, row=None):
        a = col[:, :, :kh].transpose(1, 0, 2).reshape(s_len, n_heads_ssm)
        if row is not None:
            a = a + row[:, :kh, :].transpose(2, 0, 1).reshape(s_len, n_heads_ssm)
        return pad_cols(a)

    z = _mm(n1, w_z, 'nn', "in_z")
    xbc_raw, (gathered['out'],) = _mm(n1, w_xbc, 'nn', "in_xbc", beside=gather(['out']))
    dtraw = _mm(n1, w_dt, 'nn', "in_dt")
    xbc = _conv_fwd(xbc_raw, conv_w, conv_b, "conv")
    dt, acum = _dt_fwd(dtraw, dt_bias, a_log, "dt")
    dt_g, ac_g, act_g = to_groups(dt), to_groups(acum), to_groups_t(acum)
    (y_ssd, states), (gathered['gu0'],) = _ssd_fwd(xbc, dt_g, ac_g, act_g, d_inner, d_state, kh, "ssd",
                                                   beside=gather(['gu0']))
    yn, (gathered['q'],) = _gate_fwd(y_ssd, xbc, z, d_exp, gnorm, d_inner, "gate", beside=gather(['q']))
    w_out = gathered['out'].reshape(-1, d_model)
    h1, (gathered['o'],) = _mm(yn, w_out, 'nn', "out_proj", add=xs_in, beside=gather(['o']))

    w_gu, w_dn = {}, {}

    def ffn_fwd(h, i, also):
        nrm = _rms_fwd(h, ffn_nw[i], f"ffn{i}_norm")
        w_gu[i] = gathered[f'gu{i}']
        gu = _mm(nrm, w_gu[i], 'nn', f"ffn{i}_up", out_dtype=BF16, shards=True,
                 beside=gather(also) if also else None)
        if also:
            gu, arrived = gu
            gathered.update(zip(also, arrived))
        act = _swiglu_fwd(gu, d_ff, f"ffn{i}_act")
        w_dn[i] = gathered[f'dn{i}'].reshape(-1, d_model)
        return nrm, gu, act, _mm(act, w_dn[i], 'nn', f"ffn{i}_down", add=h)

    n2, gu0, act0, h2 = ffn_fwd(h1, 0, ['dn0', 'kvf'])
    w_kvf_full = jnp.concatenate([gathered['kvf'][j] for j in range(N_CHIPS)], axis=1)
    w_k, w_v, w_f = w_kvf_full[:, :d_att], w_kvf_full[:, d_att:2 * d_att], pad_cols(w_kvf_full[:, 2 * d_att:])
    w_qm, w_om = gathered['q'].reshape(-1, d_att), gathered['o'].reshape(-1, d_model)
    nkv = _rms_fwd(h2, kv_nw, "kv_norm")
    k_raw = _mm(nkv, w_k, 'nn', "proj_k")
    v_att = _mm(nkv, w_v, 'nn', "proj_v", out_dtype=BF16)
    f_raw = _mm(nkv, w_f, 'nn', "proj_f")
    k_att = _rms_fwd(k_raw, k_nw, "k_norm")
    cum = _forget_fwd(f_raw, bf_pad, "forget")
    cq_rep = jnp.repeat(cum[:, :n_att], ATT_HEAD_DIM, axis=1)
    tb = min(ATT_BLOCK, s_len)
    ck = cum[:, :n_att].T.reshape(n_att, s_len // tb, 1, tb)
    n3 = _rms_fwd(h2, b_nw, "b_norm")
    q_raw = _mm(n3, w_qm, 'nn', "proj_q")
    q_att = _rms_fwd(q_raw, q_nw, "q_norm")
    (o_att, lse_rep), arrived = _attn_fwd(q_att, k_att, v_att, cq_rep, ck, "attn", beside=gather(['gu1', 'dn1']))
    gathered.update(zip(['gu1', 'dn1'], arrived))
    h3 = _mm(o_att, w_om, 'nn', "proj_o", add=h2)
    n4, gu1, act1, h4 = ffn_fwd(h3, 1, [])

    def loss_fn(h, t):
        err = h - t
        sq = jnp.sum(jnp.sum(err * err, axis=1, keepdims=True), axis=0, keepdims=True)
        return err * (1.0 / d_model), err * (1.0 / d_model), sq * (0.5 / d_model)

    dh4, dh4_b, loss_part = _rowwise(loss_fn, "loss", [h4, target], outs=[(d_model, F32), (d_model, BF16)],
                                     accs=[(1, 1)])

    n_f = n_att
    g_big, chip_sums, parts = {}, {}, {}

    def col_shards(pieces):
        fullw = jnp.concatenate(pieces, axis=1)
        return fullw.reshape(fullw.shape[0], N_CHIPS, -1).transpose(1, 0, 2)

    def row_shards(a):
        return a.reshape(N_CHIPS, -1, a.shape[1])

    def add_sibling(keys, got):
        for k, a in zip(keys, got):
            chip_sums[k] = _sum_half(g_big[k], a, cc, "reduce_sum2_" + k)

    def scatter(keys, tag):
        todo = [k for k in keys if k not in chip_sums]
        if todo:
            g_list = [g_big[k] for k in todo]
            hrs = [g.shape[1] // 2 for g in g_list]
            add_sibling(todo, _exchange("reduce_d2d_" + tag, g_list,
                                        [((N_CHIPS, hr, g.shape[2]), BF16) for g, hr in zip(g_list, hrs)], FLIP_C,
                                        lambda i, r, me, peer: r.at[:, pl.ds((1 - me[2]) * hrs[i], hrs[i]), :],
                                        lambda i, r, sender, k: r))
        return _scatter_plan([chip_sums[k] for k in keys])

    def ffn_bwd(dh, dh_b, h, nrm, gu, act, i, hosted):
        dact = _mm(dh_b, w_dn[i], 'nt', f"ffn{i}_down_dx", beside=scatter(hosted[:1], f"a{i}") if hosted else None)
        g_dn = _mm(act, dh_b, 'tn', f"ffn{i}_down_dw", out_dtype=BF16,
                   beside=scatter(hosted[1:], f"b{i}") if hosted else None)
        if hosted:
            (dact, first), (g_dn, second) = dact, g_dn
            parts.update(zip(hosted, first + second))
        dgu = _swiglu_bwd(gu, dact, d_ff, f"ffn{i}_act_bwd")
        dn = _mm(dgu, w_gu[i], 'nt', f"ffn{i}_up_dx", shards=True)
        g_gu = _mm(nrm, dgu, 'tn', f"ffn{i}_up_dw", out_dtype=BF16, shards=True)
        mine = [f"gu{i}", f"dn{i}"]
        g_big.update(zip(mine, [g_gu, row_shards(g_dn)]))
        (dh_new, dh_new_b, g_nw), got = _rms_bwd(h, ffn_nw[i], dn, f"ffn{i}_norm_bwd", extra=dh, copy_bf16=True,
                                                 beside=_swap_plan([g_big[k] for k in mine]))
        add_sibling(mine, got)
        return dh_new, dh_new_b, g_nw

    dh3, dh3_b, g_fnw1 = ffn_bwd(dh4, dh4_b, h3, n4, gu1, act1, 1, [])
    do_att = _mm(dh3_b, w_om, 'nt', "proj_o_dx", out_dtype=BF16)
    g_wo = _mm(o_att, dh3_b, 'tn', "proj_o_dw", out_dtype=BF16)
    g_big.update(o=row_shards(g_wo))
    late_keys = ['gu1', 'dn1', 'o']
    plan_late = scatter(late_keys, "late")

    def delta_fn(a, b):
        prod = a.astype(F32) * b.astype(F32)
        return jnp.concatenate([jnp.broadcast_to(jnp.sum(prod[:, g], axis=1, keepdims=True), (a.shape[0], ATT_HEAD_DIM))
                                for g in _lane_groups(a.shape[1], ATT_HEAD_DIM)], axis=1)

    delta_rep = _rowwise(delta_fn, "attn_delta", [do_att, o_att], outs=[(d_att, F32)])[0]
    (dq_att, dk_att, dv_att, dcq_rep, dck), arrived = _attn_bwd(q_att, k_att, v_att, do_att, lse_rep, delta_rep,
                                                                cq_rep, ck, "attn_bwd", beside=plan_late)
    parts.update(zip(late_keys, arrived))
    dq_raw, g_qnw = _rms_bwd(q_raw, q_nw, dq_att, "q_norm_bwd", out_dtype=BF16)
    dn3 = _mm(dq_raw, w_qm, 'nt', "proj_q_dx")
    g_wq = _mm(n3, dq_raw, 'tn', "proj_q_dw", out_dtype=BF16)
    dh2, g_bnw = _rms_bwd(h2, b_nw, dn3, "b_norm_bwd", extra=dh3)
    dk_raw, g_knw = _rms_bwd(k_raw, k_nw, dk_att, "k_norm_bwd", out_dtype=BF16)
    dcum = pad_cols(dcq_rep.reshape(s_len, n_att, ATT_HEAD_DIM)[:, :, 0] + dck.reshape(n_att, s_len).T)
    df_raw, g_bf = _forget_bwd(f_raw, bf_pad, dcum, "forget_bwd")
    dnkv = _mm(dk_raw, w_k, 'nt', "proj_k_dx")
    dnkv = _mm(dv_att, w_v, 'nt', "proj_v_dx", add=dnkv)
    dnkv = _mm(df_raw, w_f, 'nt', "proj_f_dx", add=dnkv)
    g_wk = _mm(nkv, dk_raw, 'tn', "proj_k_dw", out_dtype=BF16)
    g_wv = _mm(nkv, dv_att, 'tn', "proj_v_dw", out_dtype=BF16)
    g_wf = _mm(nkv, df_raw, 'tn', "proj_f_dw", out_dtype=BF16)
    g_big.update(q=row_shards(g_wq), kvf=col_shards([g_wk, g_wv, g_wf[:, :n_f]]))
    (dh2, dh2_b, g_kvnw), got = _rms_bwd(h2, kv_nw, dnkv, "kv_norm_bwd", extra=dh2, copy_bf16=True,
                                         beside=_swap_plan([g_big['q'], g_big['kvf']]))
    add_sibling(['q', 'kvf'], got)
    dh1, dh1_b, g_fnw0 = ffn_bwd(dh2, dh2_b, h1, n2, gu0, act0, 0, ['q', 'kvf'])
    plan_ffn0 = scatter(['gu0', 'dn0'], "ffn0")

    dyn = _mm(dh1_b, w_out, 'nt', "out_proj_dx")
    g_wout = _mm(yn, dh1_b, 'tn', "out_proj_dw", out_dtype=BF16)
    dy_ssd, dxs_skip, dz, g_dexp, g_gnorm = _gate_bwd(y_ssd, xbc, z, d_exp, gnorm, dyn, d_inner, "gate_bwd")
    (dxs, d_b, d_c, ddt_g, dacc_g, dacr_g), arrived = _ssd_bwd(xbc, dt_g, ac_g, act_g, states, dy_ssd, dxs_skip,
                                                               d_inner, d_state, kh, "ssd_bwd", beside=plan_ffn0)
    parts.update(zip(['gu0', 'dn0'], arrived))
    g_big.update(out=row_shards(g_wout))
    (du, g_convw, g_convb), (parts['out'],) = _conv_bwd(xbc_raw, conv_w, conv_b, [dxs, d_b, d_c], "conv_bwd",
                                                        beside=scatter(['out'], "out"))
    draw, g_dtb, g_alog = _dt_bwd(dtraw, dt_bias, a_log, from_groups(ddt_g), from_groups(dacc_g, dacr_g), "dt_bwd")
    g_wz = _mm(n1, dz, 'tn', "in_z_dw", out_dtype=BF16)
    g_wxbc = _mm(n1, du, 'tn', "in_xbc_dw", out_dtype=BF16)
    g_wdt = _mm(n1, draw, 'tn', "in_dt_dw", out_dtype=BF16)
    g_in = jnp.concatenate([g_wz, g_wxbc, g_wdt[:, :n_heads_ssm]], axis=1)
    g_in = g_in.reshape(2, d_model // 2, N_CHIPS, -1).transpose(0, 2, 1, 3)
    g_big.update(in0=g_in[0], in1=g_in[1])
    dn1, (parts['in0'],) = _mm(du, w_xbc, 'nt', "in_xbc_dx", beside=scatter(['in0'], "in0"))
    dn1, (parts['in1'],) = _mm(dz, w_z, 'nt', "in_z_dx", add=dn1, beside=scatter(['in1'], "in1"))
    dn1 = _mm(draw, w_dt, 'nt', "in_dt_dx", add=dn1)
    dx, g_norm_a = _rms_bwd(xs_in, norm_a, dn1, "norm_a_bwd", extra=dh1)

    g_small = {
        'a_norm_w': g_norm_a, 'a_conv_w': g_convw[None], 'a_conv_b': g_convb,
        'a_dt_bias': g_dtb[:, :n_heads_ssm], 'a_A_log': g_alog[:, :n_heads_ssm],
        'a_D': g_dexp.reshape(n_heads_ssm, SSM_HEAD_DIM).sum(axis=1).reshape(1, -1), 'a_gnorm_w': g_gnorm,
        'kv_norm_w': g_kvnw.reshape(-1), 'b_f': g_bf[0, :n_f], 'k_norm_w': g_knw.reshape(-1), 'b_norm_w': g_bnw,
        'q_norm_w': g_qnw, 'ffn_norm_w': jnp.concatenate([g_fnw0, g_fnw1], axis=0),
    }

    sg = _pack([g_small[n] for n in SMALL] + [loss_part], F32, 8 * PACK_COLS).reshape(-1, PACK_COLS)
    sg_all = _exchange("reduce_small", [sg], [((2 * N_CHIPS,) + sg.shape, F32)], FLIP_ALL,
                       lambda i, r, me, peer: r, lambda i, r, sender, k: r.at[_device(sender)],
                       lambda i, s, d, me: (s, d.at[_device(me)]))[0]
    sg_sum = _sum_arrays([sg_all[d] for d in range(2 * N_CHIPS)], F32, "reduce_small_sum").reshape(-1)
    red_small, off = {}, 0
    for n in SMALL:
        shp = g_small[n].shape
        red_small[n] = sg_sum[off:off + math.prod(shp)].reshape(shp)
        off += math.prod(shp)
    loss = sg_sum[off]

    red_keys = list(parts)
    half_sums = [_sum_parts(chip_sums[k], parts[k], my_chip, "reduce_sum4_" + k) for k in red_keys]
    others = _exchange("reduce_back", half_sums, [(h.shape, F32) for h in half_sums], FLIP_C,
                       lambda i, r, me, peer: r, lambda i, r, sender, k: r)
    mine_of, theirs_of = dict(zip(red_keys, half_sums)), dict(zip(red_keys, others))

    grads, delta, new_m, new_v = {}, {}, {}, {}
    layers_of = {'a_in_proj': ['in0', 'in1'], 'a_out_proj': ['out'], 'w_kvf': ['kvf'], 'w_q': ['q'], 'w_o': ['o'],
                 'w_gate_up': ['gu0', 'gu1'], 'w_down': ['dn0', 'dn1']}
    for n, keys in layers_of.items():
        three_d = (len(keys), math.prod(shapes[n][:-1]) // len(keys), shapes[n][-1])
        res = None
        for layer, k in enumerate(keys):
            res = _adamw_big(w[n].reshape(three_d), mom[n].reshape(three_d), var[n].reshape(three_d), mine_of[k],
                             theirs_of[k], cc, layer, res, "adamw_" + k)
        grads[n], delta[n], new_m[n], new_v[n] = [r.reshape(shapes[n]) for r in res]
    for n in SMALL:
        if n in SMALL_SHARDED:
            ax = SMALL_SHARDED[n]
            grads[n] = lax.dynamic_slice_in_dim(red_small[n], my_chip * shapes[n][ax], shapes[n][ax], axis=ax)
        else:
            grads[n] = red_small[n]

    packed = [_pack([src[n] for n in SMALL], F32, 8 * LANES).reshape(-1, LANES) for src in (w, grads, mom, var)]
    small_out = _adamw(*packed, "adamw_small")
    for store, flat in zip((delta, new_m, new_v), small_out):
        flat, off = flat.reshape(-1), 0
        for n in SMALL:
            sz = math.prod(shapes[n])
            store[n] = flat[off:off + sz].reshape(shapes[n])
            off += sz

    return (loss, dx[None], *[grads[n] for n in WEIGHTS], *[delta[n] for n in WEIGHTS],
            *[new_m[n] for n in WEIGHTS], *[new_v[n] for n in WEIGHTS])
```

```python
import functools
import math

import jax
import jax.numpy as jnp
from jax import lax
from jax.experimental import pallas as pl
from jax.experimental.pallas import tpu as pltpu

F32, BF16 = jnp.float32, jnp.bfloat16
EPS = 1e-6
SSM_HEAD_DIM = 64
SSM_GROUPS = 8
SSD_CHUNK = 128
ATT_HEAD_DIM = 128
LANES = 128
N_CHIPS = 4
NEG = -1e30
ADAM_LR, ADAM_B1, ADAM_B2, ADAM_EPS, ADAM_WD, ADAM_STEP = 0.001, 0.9, 0.999, 1e-08, 0.01, 10
VMEM_LIMIT_BYTES = 56 * 1024 * 1024
PACK_COLS = 1024
PACK_ROWS = 256
MM_K_TILES = (2816, 2048, 1408, 1024, 512, 256, 128)
MM_OPERAND_BYTES = 12 * 1024 * 1024

NN = ((1,), (0,))
NT = ((1,), (1,))
TN = ((0,), (0,))

WEIGHTS = ['a_norm_w', 'a_in_proj', 'a_conv_w', 'a_conv_b', 'a_dt_bias', 'a_A_log', 'a_D', 'a_gnorm_w', 'a_out_proj',
           'kv_norm_w', 'w_kvf', 'b_f', 'k_norm_w', 'b_norm_w', 'w_q', 'q_norm_w', 'w_o', 'ffn_norm_w', 'w_gate_up',
           'w_down']
BIG = {'a_in_proj': 2, 'a_out_proj': 1, 'w_kvf': 1, 'w_q': 1, 'w_o': 1, 'w_gate_up': 2, 'w_down': 1}
SMALL_SHARDED = {'a_norm_w': 1, 'a_conv_w': 2, 'a_conv_b': 1, 'a_gnorm_w': 1}
SMALL = [n for n in WEIGHTS if n not in BIG]


def _dot(a, b, dims):
    return lax.dot_general(a, b, (dims, ((), ())), preferred_element_type=F32)


def _params(sem=None):
    return pltpu.CompilerParams(dimension_semantics=sem, vmem_limit_bytes=VMEM_LIMIT_BYTES)


def _tile(dim, cap):
    for t in (1408, 1024, 512, 256, 128):
        if t <= cap and dim % t == 0:
            return t
    return dim


class _Beside:
    def __init__(self, operands, results, sem_sizes, start, finish):
        self.operands, self.results, self.sem_sizes = list(operands), list(results), list(sem_sizes)
        self.start, self.finish = start, finish


def _host(kern, beside, *, name, grid, in_specs, out_specs, out_shape, scratch_shapes, semantics, args, prefetch=()):
    single = not isinstance(out_shape, (list, tuple))
    out_specs = [out_specs] if single else list(out_specs)
    out_shape = [out_shape] if single else list(out_shape)
    n_pre = len(prefetch)

    def call(body, in_specs, out_specs, out_shape, scratch_shapes, semantics, aliases, args):
        if not n_pre:
            return pl.pallas_call(body, name=name, grid=grid, in_specs=list(in_specs), out_specs=list(out_specs),
                                  out_shape=out_shape, scratch_shapes=list(scratch_shapes),
                                  input_output_aliases=aliases, compiler_params=_params(semantics))(*args)
        spec = pltpu.PrefetchScalarGridSpec(num_scalar_prefetch=n_pre, grid=grid, in_specs=list(in_specs),
                                            out_specs=list(out_specs), scratch_shapes=list(scratch_shapes))
        return pl.pallas_call(body, name=name, grid_spec=spec, out_shape=out_shape,
                              input_output_aliases={n_pre + i: o for i, o in aliases.items()},
                              compiler_params=_params(semantics))(*prefetch, *args)

    if beside is None:
        res = call(kern, in_specs, out_specs, out_shape, scratch_shapes, semantics, {}, args)
        return (res[0] if single else res), []
    n_in, n_out, n_scr = len(in_specs), len(out_specs), len(scratch_shapes)
    nb_in, nb_out = len(beside.operands), len(beside.results)

    def body(*refs):
        pre, refs = refs[:n_pre], refs[n_pre:]
        ins, b_ins = refs[:n_in], refs[n_in:n_in + nb_in]
        outs = refs[n_in + nb_in:n_in + nb_in + n_out]
        b_outs = refs[n_in + nb_in + n_out:n_in + nb_in + n_out + nb_out]
        rest = refs[n_in + nb_in + n_out + nb_out:]
        scr, sems = rest[:n_scr], rest[n_scr:]
        ids = [pl.program_id(a) for a in range(len(grid))]
        first = functools.reduce(jnp.logical_and, [i == 0 for i in ids])
        last = functools.reduce(jnp.logical_and, [i == g - 1 for i, g in zip(ids, grid)])

        @pl.when(first)
        def _():
            beside.start(b_ins, b_outs, sems)

        kern(*pre, *ins, *outs, *scr)

        @pl.when(last)
        def _():
            beside.finish(b_ins, b_outs, sems)

    any_spec = pl.BlockSpec(memory_space=pl.ANY)
    res = call(body, list(in_specs) + [any_spec] * nb_in, out_specs + [any_spec] * nb_out,
               out_shape + [jax.ShapeDtypeStruct(s, d) for s, d, _ in beside.results],
               list(scratch_shapes) + [pltpu.SemaphoreType.DMA((k,)) for k in beside.sem_sizes],
               ("arbitrary",) * len(grid),
               {n_in + op: n_out + r for r, (_, _, op) in enumerate(beside.results) if op is not None},
               list(args) + list(beside.operands))
    mine = res[:n_out]
    return (mine[0] if single else mine), list(res[n_out:])


def _alone(beside, name):
    return _host(lambda: None, beside, name=name, grid=(1,), in_specs=[], out_specs=[], out_shape=[],
                 scratch_shapes=[], semantics=("arbitrary",), args=[])[1]


def _mm(a, b, mode, name, out_dtype=F32, add=None, shards=False, beside=None):
    if mode == 'nn':
        (m, k), n = a.shape, (b.shape[2] * N_CHIPS if shards else b.shape[1])
    elif mode == 'nt':
        (m, k), n = a.shape, (b.shape[1] if shards else b.shape[0])
    else:
        (k, m), n = a.shape, b.shape[1]
    per_chip = (k if mode == 'nt' else n) // N_CHIPS
    tm = _tile(m, 1024)
    tn = _tile(per_chip if shards and mode != 'nt' else n, 1408 if shards else 1024)
    k_dim = per_chip if shards and mode == 'nt' else k
    a_bytes, b_bytes = jnp.dtype(a.dtype).itemsize, jnp.dtype(b.dtype).itemsize
    tk = next((t for t in MM_K_TILES if k_dim % t == 0 and t * (tm * a_bytes + tn * b_bytes) <= MM_OPERAND_BYTES), k_dim)
    nk = k // tk
    in_place = nk > 1 and out_dtype == F32
    a_spec = pl.BlockSpec((tk, tm), lambda i, j, q: (q, i)) if mode == 'tn' else pl.BlockSpec((tm, tk), lambda i, j, q: (i, q))
    b_spec = pl.BlockSpec((tn, tk), lambda i, j, q: (j, q)) if mode == 'nt' else pl.BlockSpec((tk, tn), lambda i, j, q: (q, j))
    o_spec = pl.BlockSpec((tm, tn), lambda i, j, q: (i, j))
    out_struct = jax.ShapeDtypeStruct((m, n), out_dtype)
    if shards:
        per = per_chip // (tk if mode == 'nt' else tn)
        if mode == 'nn':
            b_spec = pl.BlockSpec((None, tk, tn), lambda i, j, q: (j // per, q, j % per))
        elif mode == 'nt':
            b_spec = pl.BlockSpec((None, tn, tk), lambda i, j, q: (q // per, j, q % per))
        else:
            out_struct = jax.ShapeDtypeStruct((N_CHIPS, m, per_chip), out_dtype)
    out_spec = pl.BlockSpec((None, tm, tn), lambda i, j, q: (j // per, i, j % per)) if shards and mode == 'tn' else o_spec
    dims = {'nn': NN, 'nt': NT, 'tn': TN}[mode]

    n_ins = 3 if add is not None else 2

    def kern(*refs):
        a_ref, b_ref, o_ref = refs[0], refs[1], refs[n_ins]
        acc = o_ref if in_place or nk == 1 else refs[n_ins + 1]
        q = pl.program_id(2)
        part = _dot(a_ref[...].astype(BF16), b_ref[...].astype(BF16), dims)

        def first():
            return part if add is None else part + refs[2][...]

        if nk == 1:
            o_ref[...] = first().astype(o_ref.dtype)
            return

        @pl.when(q == 0)
        def _():
            acc[...] = first()

        @pl.when(q > 0)
        def _():
            acc[...] += part

        if not in_place:
            @pl.when(q == nk - 1)
            def _():
                o_ref[...] = acc[...].astype(o_ref.dtype)

    ins, specs = [a, b], [a_spec, b_spec]
    if add is not None:
        ins.append(add)
        specs.append(o_spec)
    scratch = [] if in_place or nk == 1 else [pltpu.VMEM((tm, tn), F32)]
    res, extra = _host(kern, beside, name=name, grid=(m // tm, n // tn, nk), in_specs=specs, out_specs=out_spec,
                       out_shape=out_struct, scratch_shapes=scratch,
                       semantics=("parallel", "parallel", "arbitrary"), args=ins)
    return res if beside is None else (res, extra)


def _rowwise(fn, name, rows, bcast=(), outs=(), accs=(), tm=256, beside=None):
    rows = [r if isinstance(r, tuple) else (r, r.shape[1], 0) for r in rows]
    n_rows = rows[0][0].shape[0]
    tm = min(tm, n_rows)
    assert n_rows % tm == 0, (name, n_rows, tm)
    n_in, n_out = len(rows) + len(bcast), len(outs)
    in_specs = [pl.BlockSpec((tm, w), functools.partial(lambda i, cb: (i, cb), cb=cb)) for _, w, cb in rows]
    in_specs += [pl.BlockSpec(b.shape, lambda i: (0, 0)) for b in bcast]
    out_specs = [pl.BlockSpec((tm, w), lambda i: (i, 0)) for w, _ in outs]
    out_specs += [pl.BlockSpec(s, lambda i: (0, 0)) for s in accs]
    out_shape = [jax.ShapeDtypeStruct((n_rows, w), d) for w, d in outs] + [jax.ShapeDtypeStruct(s, F32) for s in accs]

    def kern(*refs):
        vals = fn(*[r[...] for r in refs[:n_in]])
        vals = vals if isinstance(vals, (tuple, list)) else (vals,)
        o_refs = refs[n_in:]
        for r, v in zip(o_refs[:n_out], vals[:n_out]):
            r[...] = v.astype(r.dtype)
        if accs:
            @pl.when(pl.program_id(0) == 0)
            def _():
                for r in o_refs[n_out:]:
                    r[...] = jnp.zeros_like(r)

            for r, v in zip(o_refs[n_out:], vals[n_out:]):
                r[...] += v

    res, extra = _host(kern, beside, name=name, grid=(n_rows // tm,), in_specs=in_specs, out_specs=out_specs,
                       out_shape=out_shape, scratch_shapes=[], semantics=("arbitrary",),
                       args=[r[0] for r in rows] + list(bcast))
    return res if beside is None else (res, extra)


def _rms(x, w):
    xf = x.astype(F32)
    return xf * lax.rsqrt(jnp.mean(xf * xf, axis=-1, keepdims=True) + EPS) * w


def _lane_groups(width, group):
    return [slice(g * group, (g + 1) * group) for g in range(width // group)]


def _rms_fwd(x, w, name, tm=256, beside=None):
    def fn(x, w):
        return jnp.concatenate([_rms(x[:, g], w) for g in _lane_groups(x.shape[1], w.shape[1])], axis=1)

    res = _rowwise(fn, name, [x], [w], outs=[(x.shape[1], BF16)], tm=tm, beside=beside)
    return res[0] if beside is None else (res[0][0], res[1])


def _rms_bwd(x, w, dy, name, extra=None, out_dtype=F32, tm=256, copy_bf16=False, beside=None):
    def fn(x, dy, *rest):
        w = rest[-1]
        dxs, dw = [], jnp.zeros(w.shape, F32)
        for g in _lane_groups(x.shape[1], w.shape[1]):
            _, vjp = jax.vjp(_rms, x[:, g], w)
            dx_g, dw_g = vjp(dy[:, g].astype(F32))
            dxs.append(dx_g)
            dw = dw + dw_g
        dx = jnp.concatenate(dxs, axis=1)
        if extra is not None:
            dx = dx + rest[0]
        return (dx, dx, dw) if copy_bf16 else (dx, dw)

    rows = [x, dy] + ([extra] if extra is not None else [])
    outs = [(x.shape[1], out_dtype)] + ([(x.shape[1], BF16)] if copy_bf16 else [])
    return _rowwise(fn, name, rows, [w], outs=outs, accs=[w.shape], tm=tm, beside=beside)


def _sigmoid(x):
    return 1.0 / (1.0 + jnp.exp(-x))


def _softplus(x):
    return jnp.maximum(x, 0.0) + jnp.log(1.0 + jnp.exp(-jnp.abs(x)))


def _swiglu_fwd(gu, d_ff, name):
    def fn(g, u):
        g, u = g.astype(F32), u.astype(F32)
        return g * _sigmoid(g) * u

    return _rowwise(fn, name, [(gu, d_ff, 0), (gu, d_ff, 1)], outs=[(d_ff, BF16)], tm=128)[0]


def _swiglu_bwd(gu, dact, d_ff, name):
    def fn(g, u, da):
        g, u = g.astype(F32), u.astype(F32)
        s = _sigmoid(g)
        dg = da * u * s * (1.0 + g * (1.0 - s))
        du = da * g * s
        return jnp.concatenate([dg, du], axis=1)

    return _rowwise(fn, name, [(gu, d_ff, 0), (gu, d_ff, 1), dact], outs=[(2 * d_ff, BF16)], tm=128)[0]


def _cumsum_rows(v, reverse=False):
    n = v.shape[0]
    row = lax.broadcasted_iota(jnp.int32, v.shape, 0)
    sh = 1
    while sh < n:
        if reverse:
            v = v + jnp.where(row < n - sh, pltpu.roll(v, n - sh, 0), 0.0)
        else:
            v = v + jnp.where(row >= sh, pltpu.roll(v, sh, 0), 0.0)
        sh *= 2
    return v


def _conv_fwd(u, w, b, name):
    s, c = u.shape
    kw = w.shape[0]
    tc = _tile(c, 128)

    def kern(u_ref, w_ref, b_ref, o_ref):
        uu = u_ref[...]
        row = lax.broadcasted_iota(jnp.int32, uu.shape, 0)
        acc = jnp.zeros_like(uu) + b_ref[...]
        for k in range(kw):
            sh = kw - 1 - k
            uk = uu if sh == 0 else jnp.where(row >= sh, pltpu.roll(uu, sh, 0), 0.0)
            acc = acc + w_ref[pl.ds(k, 1), :] * uk
        o_ref[...] = acc * _sigmoid(acc)

    return pl.pallas_call(
        kern, name=name, grid=(c // tc,),
        in_specs=[pl.BlockSpec((s, tc), lambda j: (0, j)), pl.BlockSpec((kw, tc), lambda j: (0, j)),
                  pl.BlockSpec((1, tc), lambda j: (0, j))],
        out_specs=pl.BlockSpec((s, tc), lambda j: (0, j)), out_shape=jax.ShapeDtypeStruct((s, c), F32),
        compiler_params=_params(("parallel",)))(u, w, b)


def _conv_bwd(u, w, b, dacts, name, beside=None):
    s, c = u.shape
    kw = w.shape[0]
    tc = _tile(c, 128)
    first = [sum(d.shape[1] for d in dacts[:i]) // tc for i in range(len(dacts))]
    count = [d.shape[1] // tc for d in dacts]

    def kern(u_ref, w_ref, b_ref, *rest):
        d_refs, (du_ref, dw_ref, db_ref) = rest[:len(dacts)], rest[len(dacts):]
        j = pl.program_id(0)
        d_out = d_refs[0][...]
        for i in range(1, len(dacts)):
            d_out = jnp.where(j >= first[i], d_refs[i][...], d_out)
        uu = u_ref[...]
        row = lax.broadcasted_iota(jnp.int32, uu.shape, 0)
        shifted = []
        acc = jnp.zeros_like(uu) + b_ref[...]
        for k in range(kw):
            sh = kw - 1 - k
            uk = uu if sh == 0 else jnp.where(row >= sh, pltpu.roll(uu, sh, 0), 0.0)
            shifted.append(uk)
            acc = acc + w_ref[pl.ds(k, 1), :] * uk
        sg = _sigmoid(acc)
        dacc = d_out * sg * (1.0 + acc * (1.0 - sg))
        db_ref[...] = jnp.sum(dacc, axis=0, keepdims=True)
        du = jnp.zeros_like(uu)
        for k in range(kw):
            sh = kw - 1 - k
            dw_ref[pl.ds(k, 1), :] = jnp.sum(dacc * shifted[k], axis=0, keepdims=True)
            dk = dacc if sh == 0 else jnp.where(row < s - sh, pltpu.roll(dacc, s - sh, 0), 0.0)
            du = du + w_ref[pl.ds(k, 1), :] * dk
        du_ref[...] = du.astype(du_ref.dtype)

    col = lambda j: (0, j)
    d_specs = [pl.BlockSpec((s, tc), functools.partial(lambda j, f, n: (0, jnp.clip(j - f, 0, n - 1)), f=f, n=n))
               for f, n in zip(first, count)]
    return _host(
        kern, beside, name=name, grid=(c // tc,),
        in_specs=[pl.BlockSpec((s, tc), col), pl.BlockSpec((kw, tc), col), pl.BlockSpec((1, tc), col)] + d_specs,
        out_specs=[pl.BlockSpec((s, tc), col), pl.BlockSpec((kw, tc), col), pl.BlockSpec((1, tc), col)],
        out_shape=[jax.ShapeDtypeStruct((s, c), BF16), jax.ShapeDtypeStruct((kw, c), F32),
                   jax.ShapeDtypeStruct((1, c), F32)],
        scratch_shapes=[], semantics=("parallel",), args=[u, w, b] + list(dacts))


def _dt_fwd(dtraw, bias, a_log, name):
    def fn(raw, bias, a_log):
        dt = _softplus(raw + bias)
        return dt, _cumsum_rows(dt * (-jnp.exp(a_log)))

    return _rowwise(fn, name, [dtraw], [bias, a_log], outs=[(LANES, F32), (LANES, F32)], tm=SSD_CHUNK)


def _dt_bwd(dtraw, bias, a_log, ddt, dacum, name):
    def fn(raw, ddt, dac, bias, a_log):
        z = raw + bias
        dt = _softplus(z)
        a_neg = -jnp.exp(a_log)
        da = _cumsum_rows(dac, reverse=True)
        draw = (ddt + da * a_neg) * _sigmoid(z)
        return draw, jnp.sum(draw, axis=0, keepdims=True), jnp.sum(da * dt, axis=0, keepdims=True) * a_neg

    return _rowwise(fn, name, [dtraw, ddt, dacum], [bias, a_log], outs=[(LANES, BF16)],
                    accs=[(1, LANES), (1, LANES)], tm=SSD_CHUNK)


def _ssd_pieces(l, gw):
    lane_k = lax.broadcasted_iota(jnp.int32, (l, LANES), 1)
    lane_of = lax.broadcasted_iota(jnp.int32, (gw, LANES), 0)
    head_of = lax.broadcasted_iota(jnp.int32, (gw, LANES), 1)
    in_head = ((lane_of >= head_of * SSM_HEAD_DIM) & (lane_of < (head_of + 1) * SSM_HEAD_DIM)).astype(BF16)

    lane_w = lax.broadcasted_iota(jnp.int32, (l, gw), 1)

    def col(blk, k):
        return jnp.sum(jnp.where(lane_k == k, blk, 0.0), axis=1, keepdims=True)

    def expand(blk):
        acc = jnp.zeros((l, gw), F32)
        for k in range(gw // SSM_HEAD_DIM):
            acc = jnp.where((lane_w >= k * SSM_HEAD_DIM) & (lane_w < (k + 1) * SSM_HEAD_DIM), col(blk, k), acc)
        return acc

    def collapse(wide):
        hi = wide.astype(BF16)
        lo = (wide - hi.astype(F32)).astype(BF16)
        return _dot(hi, in_head, NN) + _dot(lo, in_head, NN)

    return lane_k, col, expand, collapse


def _head_block(k):
    per_block = LANES // SSM_HEAD_DIM
    lane = lax.broadcasted_iota(jnp.int32, (1, LANES), 1)
    lo = (k % per_block) * SSM_HEAD_DIM
    return slice((k // per_block) * LANES, (k // per_block + 1) * LANES), (lane >= lo) & (lane < lo + SSM_HEAD_DIM)


def _by_block(pieces, n_blocks, like):
    zero = jnp.zeros((like.shape[0], LANES), F32)
    return jnp.concatenate([pieces.get(p, zero) for p in range(n_blocks)], axis=1)


def _ssd_specs(l, gw, n, n_xs_blocks):
    g_axis = SSM_GROUPS
    return dict(
        xs=lambda cm: pl.BlockSpec((l, gw), lambda g, c: (cm(c), g)),
        b=lambda cm: pl.BlockSpec((l, n), lambda g, c: (cm(c), n_xs_blocks + g)),
        c=lambda cm: pl.BlockSpec((l, n), lambda g, c: (cm(c), n_xs_blocks + g_axis + g)),
        col=lambda cm: pl.BlockSpec((None, l, LANES), lambda g, c: (g, cm(c), 0)),
        row=lambda cm: pl.BlockSpec((None, 8, l), lambda g, c: (g, 0, cm(c))),
        state=lambda cm: pl.BlockSpec((None, None, n, gw), lambda g, c: (cm(c), g, 0, 0)),
    )


def _ssd_fwd(xbc, dt_g, ac_g, act_g, d_inner, n, kh, name, beside=None):
    s = xbc.shape[0]
    l, g_n = SSD_CHUNK, SSM_GROUPS
    gw, nc = d_inner // g_n, s // l
    sp = _ssd_specs(l, gw, n, d_inner // n)
    fwd = lambda c: c

    def kern(xs_ref, b_ref, c_ref, dt_ref, ac_ref, act_ref, y_ref, s0_ref, st):
        @pl.when(pl.program_id(1) == 0)
        def _():
            st[...] = jnp.zeros_like(st)

        s0 = st[...]
        s0_ref[...] = s0
        xs, ac = xs_ref[...], ac_ref[...]
        _, col, expand, _ = _ssd_pieces(l, gw)
        ace = expand(ac)
        x = xs * expand(dt_ref[...])
        xb, bb, cb_ = x.astype(BF16), b_ref[...].astype(BF16), c_ref[...].astype(BF16)
        cb = _dot(cb_, bb, NT)
        ri = lax.broadcasted_iota(jnp.int32, (l, l), 0)
        ci = lax.broadcasted_iota(jnp.int32, (l, l), 1)
        causal = ri >= ci
        y_diag = {}
        for k in range(kh):
            seg = col(ac, k) - act_ref[pl.ds(k, 1), :]
            m = jnp.where(causal, cb * jnp.exp(jnp.where(causal, seg, 0.0)), 0.0)
            blk, mine = _head_block(k)
            yk = _dot(m.astype(BF16), xb[:, blk], NN)
            y_diag[blk.start // LANES] = jnp.where(mine, yk, y_diag.get(blk.start // LANES, 0.0))
        y_ref[...] = _dot(cb_, s0.astype(BF16), NN) * jnp.exp(ace) + _by_block(y_diag, gw // LANES, xs)
        row_w = lax.broadcasted_iota(jnp.int32, (l, gw), 0)
        alast = jnp.sum(jnp.where(row_w == l - 1, ace, 0.0), axis=0, keepdims=True)
        st[...] = s0 * jnp.exp(alast) + _dot(bb, (jnp.exp(alast - ace) * x).astype(BF16), TN)

    return _host(
        kern, beside, name=name, grid=(g_n, nc),
        in_specs=[sp['xs'](fwd), sp['b'](fwd), sp['c'](fwd), sp['col'](fwd), sp['col'](fwd), sp['row'](fwd)],
        out_specs=[pl.BlockSpec((l, gw), lambda g, c: (c, g)), sp['state'](fwd)],
        out_shape=[jax.ShapeDtypeStruct((s, d_inner), F32), jax.ShapeDtypeStruct((nc, g_n, n, gw), F32)],
        scratch_shapes=[pltpu.VMEM((n, gw), F32)], semantics=("arbitrary", "arbitrary"),
        args=[xbc, xbc, xbc, dt_g, ac_g, act_g])


def _ssd_bwd(xbc, dt_g, ac_g, act_g, s0_all, dy, dxs_skip, d_inner, n, kh, name, beside=None):
    s = xbc.shape[0]
    l, g_n = SSD_CHUNK, SSM_GROUPS
    gw, nc = d_inner // g_n, s // l
    sp = _ssd_specs(l, gw, n, d_inner // n)
    rev = lambda c: nc - 1 - c

    def kern(xs_ref, b_ref, c_ref, dt_ref, ac_ref, act_ref, s0_ref, dy_ref, skip_ref,
             dxs_ref, db_ref, dc_ref, ddt_ref, dacc_ref, dacr_ref, dst):
        @pl.when(pl.program_id(1) == 0)
        def _():
            dst[...] = jnp.zeros_like(dst)

        dsn = dst[...]
        s0 = s0_ref[...]
        xs, ac, dy = xs_ref[...], ac_ref[...], dy_ref[...]
        lane_k, col, expand, collapse = _ssd_pieces(l, gw)
        ace, dte = expand(ac), expand(dt_ref[...])
        x = xs * dte
        xb, bb, cb_ = x.astype(BF16), b_ref[...].astype(BF16), c_ref[...].astype(BF16)
        s0b, dsnb, dyb = s0.astype(BF16), dsn.astype(BF16), dy.astype(BF16)
        cb = _dot(cb_, bb, NT)
        ri = lax.broadcasted_iota(jnp.int32, (l, l), 0)
        ci = lax.broadcasted_iota(jnp.int32, (l, l), 1)
        causal = ri >= ci
        row_w = lax.broadcasted_iota(jnp.int32, (l, gw), 0)
        e = jnp.exp(ace)
        alast = jnp.sum(jnp.where(row_w == l - 1, ace, 0.0), axis=0, keepdims=True)
        gdec = jnp.exp(alast)
        wt = jnp.exp(alast - ace)
        cs = _dot(cb_, s0b, NN)
        dcs = (dy * e).astype(BF16)
        d_c = _dot(dcs, s0b, NT)
        ds_off = _dot(cb_, dcs, TN)
        dace = dy * cs * e
        dalast = jnp.sum(dsn * s0, axis=0, keepdims=True) * gdec
        z = wt * x
        dz = _dot(bb, dsnb, NN)
        d_b = _dot(z.astype(BF16), dsnb, NT)
        dx = dz * wt
        t = dz * z
        dalast = dalast + jnp.sum(t, axis=0, keepdims=True)
        dace = dace - t
        dcb = jnp.zeros((l, l), F32)
        dac_col = jnp.zeros((l, LANES), F32)
        dx_diag = {}
        for k in range(kh):
            seg = col(ac, k) - act_ref[pl.ds(k, 1), :]
            dk = jnp.exp(jnp.where(causal, seg, 0.0))
            mk = jnp.where(causal, cb * dk, 0.0)
            blk, mine = _head_block(k)
            dxk = _dot(mk.astype(BF16), dyb[:, blk], TN)
            dx_diag[blk.start // LANES] = jnp.where(mine, dxk, dx_diag.get(blk.start // LANES, 0.0))
            dmk = _dot(jnp.where(mine, dy[:, blk], 0.0).astype(BF16), xb[:, blk], NT)
            dcb = dcb + jnp.where(causal, dmk * dk, 0.0)
            dseg = dmk * mk
            dac_col = jnp.where(lane_k == k, jnp.sum(dseg, axis=1, keepdims=True), dac_col)
            dacr_ref[pl.ds(k, 1), :] = -jnp.sum(dseg, axis=0, keepdims=True)
        for k in range(kh, 8):
            dacr_ref[pl.ds(k, 1), :] = jnp.zeros((1, l), F32)
        dx = dx + _by_block(dx_diag, gw // LANES, xs)
        dcbb = dcb.astype(BF16)
        dc_ref[...] = d_c + _dot(dcbb, bb, NN)
        db_ref[...] = d_b + _dot(dcbb, cb_, TN)
        dace = jnp.where(row_w == l - 1, dace + dalast, dace)
        dacc_ref[...] = dac_col + collapse(dace)
        ddt_ref[...] = collapse(dx * xs)
        dxs_ref[...] = dx * dte + skip_ref[...]
        dst[...] = dsn * gdec + ds_off

    return _host(
        kern, beside, name=name, grid=(g_n, nc),
        in_specs=[sp['xs'](rev), sp['b'](rev), sp['c'](rev), sp['col'](rev), sp['col'](rev), sp['row'](rev),
                  sp['state'](rev), pl.BlockSpec((l, gw), lambda g, c: (rev(c), g)),
                  pl.BlockSpec((l, gw), lambda g, c: (rev(c), g))],
        out_specs=[pl.BlockSpec((l, gw), lambda g, c: (rev(c), g)), pl.BlockSpec((l, n), lambda g, c: (rev(c), g)),
                   pl.BlockSpec((l, n), lambda g, c: (rev(c), g)), sp['col'](rev), sp['col'](rev), sp['row'](rev)],
        out_shape=[jax.ShapeDtypeStruct((s, d_inner), F32), jax.ShapeDtypeStruct((s, g_n * n), F32),
                   jax.ShapeDtypeStruct((s, g_n * n), F32), jax.ShapeDtypeStruct((g_n, s, LANES), F32),
                   jax.ShapeDtypeStruct((g_n, s, LANES), F32), jax.ShapeDtypeStruct((g_n, 8, s), F32)],
        scratch_shapes=[pltpu.VMEM((n, gw), F32)], semantics=("arbitrary", "arbitrary"),
        args=[xbc, xbc, xbc, dt_g, ac_g, act_g, s0_all, dy, dxs_skip])


def _gate(y, xs, z, d_exp, gw):
    t = (y + xs * d_exp) * (z * _sigmoid(z))
    width = t.shape[1]
    gsz = width // SSM_GROUPS
    lane = lax.broadcasted_iota(jnp.int32, t.shape, 1)
    t2 = t * t
    scale = jnp.zeros_like(t)
    for g in range(SSM_GROUPS):
        in_g = (lane >= g * gsz) & (lane < (g + 1) * gsz)
        ms = jnp.sum(jnp.where(in_g, t2, 0.0), axis=1, keepdims=True) * (1.0 / gsz)
        scale = jnp.where(in_g, lax.rsqrt(ms + EPS), scale)
    return t * scale * gw


def _gate_fwd(y, xbc, z, d_exp, gw, d_inner, name, beside=None):
    res = _rowwise(_gate, name, [y, (xbc, d_inner, 0), z], [d_exp, gw], outs=[(d_inner, BF16)], tm=128, beside=beside)
    return res[0] if beside is None else (res[0][0], res[1])


def _gate_bwd(y, xbc, z, d_exp, gw, dyn, d_inner, name):
    def fn(y, xs, z, dyn, d_exp, gw):
        _, vjp = jax.vjp(_gate, y, xs, z, d_exp, gw)
        return vjp(dyn)

    return _rowwise(fn, name, [y, (xbc, d_inner, 0), z, dyn], [d_exp, gw],
                    outs=[(d_inner, F32), (d_inner, F32), (d_inner, BF16)], accs=[d_exp.shape, gw.shape], tm=64)


def _forget_fwd(fraw, b_f, name):
    def kern(f_ref, b_ref, o_ref):
        o_ref[...] = _cumsum_rows(-_softplus(-(f_ref[...] + b_ref[...])))

    return pl.pallas_call(kern, name=name, out_shape=jax.ShapeDtypeStruct(fraw.shape, F32),
                          compiler_params=_params())(fraw, b_f)


def _forget_bwd(fraw, b_f, dcum, name):
    def kern(f_ref, b_ref, d_ref, df_ref, db_ref):
        df = _cumsum_rows(d_ref[...], reverse=True) * _sigmoid(-(f_ref[...] + b_ref[...]))
        df_ref[...] = df.astype(df_ref.dtype)
        db_ref[...] = jnp.sum(df, axis=0, keepdims=True)

    return pl.pallas_call(
        kern, name=name,
        out_shape=[jax.ShapeDtypeStruct(fraw.shape, BF16), jax.ShapeDtypeStruct((1, fraw.shape[1]), F32)],
        compiler_params=_params())(fraw, b_f, dcum)


ATT_BLOCK = 512


def _attn_fwd(q, k, v, cq_rep, ck, name, beside=None):
    s, hd = q.shape
    h_n, d = hd // ATT_HEAD_DIM, ATT_HEAD_DIM
    tb = min(ATT_BLOCK, s)
    tk = tb // 2
    nb = s // tb
    scale = d ** -0.5

    def kern(q_ref, k_ref, v_ref, cq_ref, ck_ref, o_ref, lse_ref):
        i = pl.program_id(1)
        qq = q_ref[...]
        cq = jnp.max(cq_ref[...], axis=1, keepdims=True)
        rowpos = i * tb + lax.broadcasted_iota(jnp.int32, (tb, tk), 0)
        coli = lax.broadcasted_iota(jnp.int32, (tb, tk), 1)

        def step(j, carry, diagonal):
            m, l_, acc = carry
            ks = pl.ds(pl.multiple_of(j * tk, tk), tk)
            sc = _dot(qq, k_ref[ks, :], NT) * scale + (cq - ck_ref[j])
            if diagonal:
                sc = jnp.where(j * tk + coli <= rowpos, sc, NEG)
            mn = jnp.maximum(m, jnp.max(sc, axis=1, keepdims=True))
            p = jnp.exp(sc - mn)
            alpha = jnp.exp(m - mn)
            l_ = alpha * l_ + jnp.sum(p, axis=1, keepdims=True)
            acc = alpha * acc + _dot(p.astype(BF16), v_ref[ks, :], NN)
            return mn, l_, acc

        init = (jnp.full((tb, 1), NEG, F32), jnp.zeros((tb, 1), F32), jnp.zeros((tb, d), F32))
        below = lax.fori_loop(0, 2 * i, lambda j, carry: step(j, carry, False), init)
        m, l_, acc = step(2 * i + 1, step(2 * i, below, True), True)
        o_ref[...] = (acc / l_).astype(o_ref.dtype)
        lse_ref[...] = jnp.broadcast_to(m + jnp.log(l_), (tb, d))

    return _host(
        kern, beside, name=name, grid=(h_n, nb),
        in_specs=[pl.BlockSpec((tb, d), lambda h, i: (i, h)), pl.BlockSpec((s, d), lambda h, i: (0, h)),
                  pl.BlockSpec((s, d), lambda h, i: (0, h)), pl.BlockSpec((tb, d), lambda h, i: (i, h)),
                  pl.BlockSpec((None, 2 * nb, 1, tk), lambda h, i: (h, 0, 0, 0))],
        out_specs=[pl.BlockSpec((tb, d), lambda h, i: (i, h)), pl.BlockSpec((tb, d), lambda h, i: (i, h))],
        out_shape=[jax.ShapeDtypeStruct((s, hd), BF16), jax.ShapeDtypeStruct((s, hd), F32)],
        scratch_shapes=[], semantics=("parallel", "arbitrary"), args=[q, k, v, cq_rep, ck])


def _attn_bwd(q, k, v, do, lse_rep, delta_rep, cq_rep, ck, name, beside=None):
    s, hd = q.shape
    h_n, d = hd // ATT_HEAD_DIM, ATT_HEAD_DIM
    tb = min(ATT_BLOCK, s)
    nb = s // tb
    scale = d ** -0.5

    def kern(q_ref, do_ref, k_ref, v_ref, lse_ref, dl_ref, cq_ref, ck_ref, dq_ref, dk_ref, dv_ref, dcq_ref, dck_ref):
        j = pl.program_id(1)

        @pl.when(j == 0)
        def _():
            dq_ref[...] = jnp.zeros_like(dq_ref)
            dcq_ref[...] = jnp.zeros_like(dcq_ref)

        kj, vj, ckj = k_ref[...], v_ref[...], ck_ref[...]
        colpos = j * tb + lax.broadcasted_iota(jnp.int32, (tb, tb), 1)
        rowi = lax.broadcasted_iota(jnp.int32, (tb, tb), 0)

        def step(i, carry, diagonal):
            dk, dv, dck = carry
            rs = pl.ds(pl.multiple_of(i * tb, tb), tb)
            qi, doi = q_ref[rs, :], do_ref[rs, :]
            lse = jnp.max(lse_ref[rs, :], axis=1, keepdims=True)
            dl = jnp.max(dl_ref[rs, :], axis=1, keepdims=True)
            cq = jnp.max(cq_ref[rs, :], axis=1, keepdims=True)
            sc = _dot(qi, kj, NT) * scale + ((cq - lse) - ckj)
            p = jnp.exp(jnp.where(colpos <= i * tb + rowi, sc, NEG) if diagonal else sc)
            dp = _dot(doi, vj, NT)
            ds = p * (dp - dl)
            dsb = ds.astype(BF16)
            dv = dv + _dot(p.astype(BF16), doi, TN)
            dk = dk + _dot(dsb, qi, TN)
            dq_ref[rs, :] += _dot(dsb, kj, NN) * scale
            dcq_ref[rs, :] += jnp.broadcast_to(jnp.sum(ds, axis=1, keepdims=True), (tb, d))
            return dk, dv, dck - jnp.sum(ds, axis=0, keepdims=True)

        init = (jnp.zeros((tb, d), F32), jnp.zeros((tb, d), F32), jnp.zeros((1, tb), F32))
        dk, dv, dck = lax.fori_loop(j + 1, nb, lambda i, carry: step(i, carry, False), step(j, init, True))
        dk_ref[...] = dk * scale
        dv_ref[...] = dv.astype(dv_ref.dtype)
        dck_ref[...] = dck

    whole = pl.BlockSpec((s, d), lambda h, j: (0, h))
    blk = pl.BlockSpec((tb, d), lambda h, j: (j, h))
    ckb = pl.BlockSpec((None, None, 1, tb), lambda h, j: (h, j, 0, 0))
    return _host(
        kern, beside, name=name, grid=(h_n, nb),
        in_specs=[whole, whole, blk, blk, whole, whole, whole, ckb],
        out_specs=[whole, blk, blk, whole, ckb],
        out_shape=[jax.ShapeDtypeStruct((s, hd), F32), jax.ShapeDtypeStruct((s, hd), F32),
                   jax.ShapeDtypeStruct((s, hd), BF16), jax.ShapeDtypeStruct((s, hd), F32),
                   jax.ShapeDtypeStruct((h_n, nb, 1, tb), F32)],
        scratch_shapes=[], semantics=("arbitrary", "arbitrary"), args=[q, do, k, v, lse_rep, delta_rep, cq_rep, ck])


def _adamw(w, g, m, v, name):
    cols = w.shape[1]
    tm = _tile(w.shape[0], 128) if w.shape[0] % 128 == 0 else w.shape[0]
    return _rowwise(_adamw_math, name, [w, g, m, v], outs=[(cols, F32)] * 3, tm=tm)


def _sum_arrays(arrs, out_dtype, name):
    def fn(*xs):
        acc = xs[0].astype(F32)
        for x in xs[1:]:
            acc = acc + x.astype(F32)
        return acc

    tm = PACK_ROWS if arrs[0].shape[0] % PACK_ROWS == 0 else arrs[0].shape[0]
    return _rowwise(fn, name, list(arrs), outs=[(arrs[0].shape[1], out_dtype)], tm=tm)[0]


def _half_tile(rows):
    for t in (256, 176, 128, 64, 32, 16):
        if rows % t == 0:
            return t
    return rows


def _sum_half(g, got, core, name):
    n_chip, hr, cols = got.shape
    tm = _half_tile(hr)
    nt = hr // tm

    def kern(c_ref, g_ref, a_ref, o_ref):
        o_ref[...] = (g_ref[...].astype(F32) + a_ref[...].astype(F32)).astype(o_ref.dtype)

    grid_spec = pltpu.PrefetchScalarGridSpec(
        num_scalar_prefetch=1, grid=(n_chip * nt,),
        in_specs=[pl.BlockSpec((tm, cols), lambda r, c: (((r // nt) * 2 + c[0]) * nt + r % nt, 0)),
                  pl.BlockSpec((tm, cols), lambda r, c: (r, 0))],
        out_specs=pl.BlockSpec((tm, cols), lambda r, c: (r, 0)))
    out = pl.pallas_call(
        kern, name=name, grid_spec=grid_spec, out_shape=jax.ShapeDtypeStruct((n_chip * hr, cols), BF16),
        compiler_params=_params(("arbitrary",)))(core.reshape(1), g.reshape(-1, cols), got.reshape(-1, cols))
    return out.reshape(n_chip, hr, cols)


def _sum_parts(own, parts, chip, name):
    n_parts, hr, cols = parts.shape
    tm = _half_tile(hr)

    def kern(s_ref, t_ref, p_ref, o_ref):
        acc = t_ref[...].astype(F32)
        for j in range(n_parts):
            acc = acc + p_ref[j].astype(F32)
        o_ref[...] = acc

    grid_spec = pltpu.PrefetchScalarGridSpec(
        num_scalar_prefetch=1, grid=(hr // tm,),
        in_specs=[pl.BlockSpec((None, tm, cols), lambda r, s: (s[0], r, 0)),
                  pl.BlockSpec((n_parts, tm, cols), lambda r, s: (0, r, 0))],
        out_specs=pl.BlockSpec((tm, cols), lambda r, s: (r, 0)))
    return pl.pallas_call(
        kern, name=name, grid_spec=grid_spec, out_shape=jax.ShapeDtypeStruct((hr, cols), F32),
        compiler_params=_params(("arbitrary",)))(chip.reshape(1), own, parts)


def _adamw_math(w, g, m, v):
    m = ADAM_B1 * m + (1.0 - ADAM_B1) * g
    v = ADAM_B2 * v + (1.0 - ADAM_B2) * (g * g)
    m_hat = m * (1.0 / (1.0 - ADAM_B1 ** ADAM_STEP))
    v_hat = v * (1.0 / (1.0 - ADAM_B2 ** ADAM_STEP))
    return -ADAM_LR * (m_hat / (jnp.sqrt(v_hat) + ADAM_EPS) + ADAM_WD * w), m, v


def _adamw_big(w, m, v, mine, theirs, core, layer, prev, name):
    n_layers, rows, cols = w.shape
    hr = rows // 2
    tm = next(t for t in (128, 64, 32, 16, 8) if hr % t == 0)
    nt = hr // tm

    def kern(s_ref, w_ref, m_ref, v_ref, a_ref, b_ref, *rest):
        g_ref, d_ref, nm_ref, nv_ref = rest[-4:]
        g = jnp.where(pl.program_id(0) // nt == s_ref[0], a_ref[...], b_ref[...])
        g_ref[...] = g
        d_ref[...], nm_ref[...], nv_ref[...] = _adamw_math(w_ref[...], g, m_ref[...], v_ref[...])

    lyr = pl.BlockSpec((None, tm, cols), lambda r, s: (layer, r, 0))
    half = pl.BlockSpec((tm, cols), lambda r, s: (r % nt, 0))
    passed = [] if prev is None else list(prev)
    grid_spec = pltpu.PrefetchScalarGridSpec(
        num_scalar_prefetch=1, grid=(rows // tm,),
        in_specs=[lyr, lyr, lyr, half, half] + [pl.BlockSpec(memory_space=pl.ANY)] * len(passed), out_specs=[lyr] * 4)
    return pl.pallas_call(
        kern, name=name, grid_spec=grid_spec, out_shape=[jax.ShapeDtypeStruct(w.shape, F32)] * 4,
        input_output_aliases={6 + i: i for i in range(len(passed))},
        compiler_params=_params(("arbitrary",)))(core.reshape(1), w, m, v, mine, theirs, *passed)


def _cast_into_slot(src, slot, name, beside=None):
    rows, cols = src.shape
    tm = _half_tile(rows)

    def kern(s_ref, x_ref, o_ref):
        o_ref[...] = x_ref[...].astype(o_ref.dtype)

    res, extra = _host(
        kern, beside, name=name, grid=(rows // tm,), in_specs=[pl.BlockSpec((tm, cols), lambda r, s: (r, 0))],
        out_specs=pl.BlockSpec((None, tm, cols), lambda r, s: (s[0], r, 0)),
        out_shape=jax.ShapeDtypeStruct((N_CHIPS, rows, cols), BF16), scratch_shapes=[], semantics=("arbitrary",),
        args=[src], prefetch=[slot.reshape(1)])
    return res if beside is None else (res, extra)


FLIP_C = [(0, 0, 1)]
FLIP_XY = [(1, 0, 0), (0, 1, 0), (1, 1, 0)]
FLIP_ALL = [(fx, fy, fc) for fx in (0, 1) for fy in (0, 1) for fc in (0, 1) if (fx, fy, fc) != (0, 0, 0)]


def _chip(dev):
    return 2 * dev[0] + dev[1]


def _device(dev):
    return 4 * dev[0] + 2 * dev[1] + dev[2]


def _exchange(name, srcs, dst_shapes, rels, src_view, dst_view, own_view=None, in_place=False):
    n, n_rel = len(srcs), len(rels)

    def body(*refs):
        src_refs, dst_refs = refs[:n], refs[n:2 * n]
        send_sems, recv_sems, own_sems = refs[2 * n:]
        me = (lax.axis_index("x"), lax.axis_index("y"), lax.axis_index("c"))
        peers = [tuple(1 - m if f else m for m, f in zip(me, rel)) for rel in rels]

        def copy(i, k, sender, receiver):
            return pltpu.make_async_remote_copy(
                src_ref=src_view(i, src_refs[i], sender, receiver), dst_ref=dst_view(i, dst_refs[i], sender, k),
                send_sem=send_sems.at[i * n_rel + k], recv_sem=recv_sems.at[i * n_rel + k], device_id=receiver,
                device_id_type=pl.DeviceIdType.MESH)

        sends = [copy(i, k, me, peer) for i in range(n) for k, peer in enumerate(peers)]
        for cp in sends:
            cp.start()
        mine = []
        if own_view is not None:
            for i in range(n):
                frm, to = own_view(i, src_refs[i], dst_refs[i], me)
                mine.append(pltpu.make_async_copy(frm, to, own_sems.at[i]))
                mine[-1].start()
        for i in range(n):
            for k, peer in enumerate(peers):
                copy(i, k, peer, me).wait_recv()
        for cp in sends:
            cp.wait_send()
        for cp in mine:
            cp.wait()

    any_spec = pl.BlockSpec(memory_space=pl.ANY)
    return pl.pallas_call(
        body, name=name, out_shape=[jax.ShapeDtypeStruct(s, d) for s, d in dst_shapes],
        in_specs=[any_spec] * n, out_specs=[any_spec] * n,
        input_output_aliases={i: i for i in range(n)} if in_place else {},
        scratch_shapes=[pltpu.SemaphoreType.DMA((n * n_rel,)), pltpu.SemaphoreType.DMA((n * n_rel,)),
                        pltpu.SemaphoreType.DMA((n,))])(*srcs)


def _gather_plan(bufs, rows_of=None):
    n = len(bufs)
    rows_of = rows_of or [(0, b.shape[1]) for b in bufs]
    starts = [r0 for r0, _ in rows_of]
    halves = [nr // 2 for _, nr in rows_of]

    def tools(src_refs, dst_refs, sems):
        ici_send, ici_recv, d2d_send, d2d_recv = sems
        x, y, c = lax.axis_index("x"), lax.axis_index("y"), lax.axis_index("c")
        sibling = (x, y, 1 - c)
        chips = [(1 - x, y), (x, 1 - y), (1 - x, 1 - y)]

        def rows(i, chip, core):
            return dst_refs[i].at[2 * chip[0] + chip[1], pl.ds(starts[i] + core * halves[i], halves[i]), :]

        def over_ici(i, k, src, chip_from, to):
            return pltpu.make_async_remote_copy(
                src_ref=src, dst_ref=rows(i, chip_from, c), send_sem=ici_send.at[3 * i + k],
                recv_sem=ici_recv.at[3 * i + k], device_id=to, device_id_type=pl.DeviceIdType.MESH)

        def over_d2d(i, k, core):
            return pltpu.make_async_remote_copy(
                src_ref=rows(i, chips[k], core), dst_ref=rows(i, chips[k], core), send_sem=d2d_send.at[3 * i + k],
                recv_sem=d2d_recv.at[3 * i + k], device_id=sibling, device_id_type=pl.DeviceIdType.MESH)

        def my_send(i, k):
            my_half = src_refs[i].at[2 * x + y, pl.ds(starts[i] + c * halves[i], halves[i]), :]
            return over_ici(i, k, my_half, (x, y), (*chips[k], c))

        def my_arrival(i, k):
            return over_ici(i, k, rows(i, chips[k], c), chips[k], (x, y, c))

        return c, over_d2d, my_send, my_arrival

    def start(src_refs, dst_refs, sems):
        _, _, my_send, _ = tools(src_refs, dst_refs, sems)
        for i in range(n):
            for k in range(3):
                my_send(i, k).start()

    def finish(src_refs, dst_refs, sems):
        c, over_d2d, my_send, my_arrival = tools(src_refs, dst_refs, sems)
        passed = []
        for i in range(n):
            for k in range(3):
                my_arrival(i, k).wait_recv()
                passed.append(over_d2d(i, k, c))
                passed[-1].start()
        for i in range(n):
            for k in range(3):
                over_d2d(i, k, 1 - c).wait_recv()
        for i in range(n):
            for k in range(3):
                my_send(i, k).wait_send()
        for cp in passed:
            cp.wait_send()

    return _Beside(bufs, [(b.shape, b.dtype, i) for i, b in enumerate(bufs)], [3 * n] * 4, start, finish)


def _swap_plan(grads):
    n = len(grads)
    hrs = [g.shape[1] // 2 for g in grads]

    def copy(i, src_refs, dst_refs, sems, sender_core):
        x, y, c = lax.axis_index("x"), lax.axis_index("y"), lax.axis_index("c")
        return pltpu.make_async_remote_copy(
            src_ref=src_refs[i].at[:, pl.ds((1 - sender_core) * hrs[i], hrs[i]), :], dst_ref=dst_refs[i],
            send_sem=sems[0].at[i], recv_sem=sems[1].at[i], device_id=(x, y, 1 - c),
            device_id_type=pl.DeviceIdType.MESH)

    def start(src_refs, dst_refs, sems):
        c = lax.axis_index("c")
        for i in range(n):
            copy(i, src_refs, dst_refs, sems, c).start()

    def finish(src_refs, dst_refs, sems):
        c = lax.axis_index("c")
        for i in range(n):
            copy(i, src_refs, dst_refs, sems, 1 - c).wait_recv()
        for i in range(n):
            copy(i, src_refs, dst_refs, sems, c).wait_send()

    return _Beside(grads, [((N_CHIPS, hr, g.shape[2]), g.dtype, None) for g, hr in zip(grads, hrs)], [n] * 2,
                   start, finish)


def _scatter_plan(sums):
    n = len(sums)

    def copy(i, k, src_refs, dst_refs, sems, sender, receiver):
        return pltpu.make_async_remote_copy(
            src_ref=src_refs[i].at[_chip(receiver)], dst_ref=dst_refs[i].at[k], send_sem=sems[0].at[3 * i + k],
            recv_sem=sems[1].at[3 * i + k], device_id=receiver, device_id_type=pl.DeviceIdType.MESH)

    def each(fn, src_refs, dst_refs, sems, outgoing):
        me = (lax.axis_index("x"), lax.axis_index("y"), lax.axis_index("c"))
        for i in range(n):
            for k, rel in enumerate(FLIP_XY):
                peer = tuple(1 - m if f else m for m, f in zip(me, rel))
                fn(copy(i, k, src_refs, dst_refs, sems, *((me, peer) if outgoing else (peer, me))))

    def start(src_refs, dst_refs, sems):
        each(lambda cp: cp.start(), src_refs, dst_refs, sems, True)

    def finish(src_refs, dst_refs, sems):
        each(lambda cp: cp.wait_recv(), src_refs, dst_refs, sems, False)
        each(lambda cp: cp.wait_send(), src_refs, dst_refs, sems, True)

    return _Beside(sums, [((3,) + s.shape[1:], s.dtype, None) for s in sums], [3 * n] * 2, start, finish)


def _pack(parts, dtype, multiple):
    flat = jnp.concatenate([p.reshape(-1).astype(dtype) for p in parts])
    pad = (-flat.shape[0]) % multiple
    return jnp.pad(flat, (0, pad)) if pad else flat


def _unpack_shards(packs, names, shapes, axes):
    out, off = {}, 0
    for nme in names:
        sz = math.prod(shapes[nme])
        out[nme] = jnp.concatenate([packs[j, off:off + sz].reshape(shapes[nme]) for j in range(N_CHIPS)], axis=axes[nme])
        off += sz
    return out


def _shards_of(full, axis):
    sz = full.shape[axis] // N_CHIPS
    return [lax.slice_in_dim(full, j * sz, (j + 1) * sz, axis=axis) for j in range(N_CHIPS)]


def kernel(x, a_norm_w, a_in_proj, a_conv_w, a_conv_b, a_dt_bias, a_A_log, a_D, a_gnorm_w, a_out_proj, kv_norm_w, w_kvf, b_f, k_norm_w, b_norm_w, w_q, q_norm_w, w_o, ffn_norm_w, w_gate_up, w_down, loss_target, m_a_norm_w, m_a_in_proj, m_a_conv_w, m_a_conv_b, m_a_dt_bias, m_a_A_log, m_a_D, m_a_gnorm_w, m_a_out_proj, m_kv_norm_w, m_w_kvf, m_b_f, m_k_norm_w, m_b_norm_w, m_w_q, m_q_norm_w, m_w_o, m_ffn_norm_w, m_w_gate_up, m_w_down, v_a_norm_w, v_a_in_proj, v_a_conv_w, v_a_conv_b, v_a_dt_bias, v_a_A_log, v_a_D, v_a_gnorm_w, v_a_out_proj, v_kv_norm_w, v_w_kvf, v_b_f, v_k_norm_w, v_b_norm_w, v_w_q, v_q_norm_w, v_w_o, v_ffn_norm_w, v_w_gate_up, v_w_down):
    args = locals()
    w = {n: args[n] for n in WEIGHTS}
    mom = {n: args['m_' + n] for n in WEIGHTS}
    var = {n: args['v_' + n] for n in WEIGHTS}
    shapes = {n: w[n].shape for n in WEIGHTS}

    cx, cy, cc = lax.axis_index("x"), lax.axis_index("y"), lax.axis_index("c")
    my_chip = 2 * cx + cy

    xs_in = x[0]
    target = loss_target[0]
    s_len, d_model = xs_in.shape
    n_heads_ssm = a_dt_bias.shape[-1]
    d_inner = n_heads_ssm * SSM_HEAD_DIM
    d_xbc = a_conv_w.shape[-1] * N_CHIPS
    d_state = (d_xbc - d_inner) // (2 * SSM_GROUPS)
    kh = n_heads_ssm // SSM_GROUPS
    n_att = b_f.shape[0]
    d_att = n_att * ATT_HEAD_DIM
    d_ff = w_down.shape[1] * N_CHIPS

    small_names = list(SMALL_SHARDED)
    sp = _pack([w[n] for n in small_names], F32, PACK_COLS).reshape(-1, PACK_COLS)
    sp_all = _exchange("gather_small", [sp], [((N_CHIPS,) + sp.shape, F32)], FLIP_XY,
                       lambda i, r, me, peer: r, lambda i, r, sender, k: r.at[_chip(sender)],
                       lambda i, s, d, me: (s, d.at[_chip(me)]))[0]
    full = _unpack_shards(sp_all.reshape(N_CHIPS, -1), small_names, shapes, SMALL_SHARDED)
    for n in SMALL:
        if n not in full:
            full[n] = w[n]

    big2d = {'in': a_in_proj[0], 'out': a_out_proj[0], 'gu0': w_gate_up[0], 'dn0': w_down[0], 'kvf': w_kvf,
             'q': w_q[0], 'o': w_o[0], 'gu1': w_gate_up[1], 'dn1': w_down[1]}
    big_keys = list(big2d)
    norm_a, gnorm = full['a_norm_w'], full['a_gnorm_w']
    gathered = {}

    in_buf = _cast_into_slot(big2d['in'], my_chip, "cast_in")
    later = [k for k in big_keys if k != 'in']
    moved = [math.prod(big2d[k].shape) for k in later] + [s_len * d_model]
    units = big2d['in'].shape[0] // 32
    share = [m * units / sum(moved) for m in moved]
    piece = [int(sh) for sh in share]
    for i in sorted(range(len(share)), key=lambda i: share[i] - piece[i], reverse=True)[:units - sum(piece)]:
        piece[i] += 1
    slot, row0 = {}, 0
    for k, rows_k in zip(later + [None], [32 * p for p in piece]):
        plan = _gather_plan([in_buf], [(row0, rows_k)]) if rows_k else None
        row0 += rows_k
        if k is None:
            n1 = _rms_fwd(xs_in, norm_a, "norm_a", beside=plan)
        else:
            slot[k] = _cast_into_slot(big2d[k], my_chip, "cast_" + k, beside=plan)
        if plan is not None and k is None:
            n1, (in_buf,) = n1
        elif plan is not None:
            slot[k], (in_buf,) = slot[k]
    gathered['in'] = in_buf

    def gather(keys):
        return _gather_plan([slot[k] for k in keys])

    def pad_cols(a, width=LANES):
        return jnp.pad(a, ((0, 0), (0, width - a.shape[1])))

    w_in = jnp.concatenate([gathered['in'][j] for j in range(N_CHIPS)], axis=1)
    w_z, w_xbc, w_dt = w_in[:, :d_inner], w_in[:, d_inner:d_inner + d_xbc], pad_cols(w_in[:, d_inner + d_xbc:])
    conv_w, conv_b = full['a_conv_w'][0], full['a_conv_b']
    dt_bias, a_log = pad_cols(a_dt_bias), pad_cols(a_A_log)
    d_exp = jnp.repeat(a_D, SSM_HEAD_DIM, axis=1)
    kv_nw, b_nw = kv_norm_w.reshape(1, -1), b_norm_w
    k_nw, q_nw = k_norm_w.reshape(1, -1), q_norm_w
    bf_pad = pad_cols(b_f.reshape(1, -1))
    ffn_nw = [ffn_norm_w[i:i + 1] for i in range(2)]

    def to_groups(a):
        g = a[:, :n_heads_ssm].reshape(s_len, SSM_GROUPS, kh).transpose(1, 0, 2)
        return jnp.pad(g, ((0, 0), (0, 0), (0, LANES - kh)))

    def to_groups_t(a):
        g = a[:, :n_heads_ssm].reshape(s_len, SSM_GROUPS, kh).transpose(1, 2, 0)
        return jnp.pad(g, ((0, 0), (0, 8 - kh), (0, 0)))

    def from_groups(col, row=None):
        a = col[:, :, :kh].transpose(1, 0, 2).reshape(s_len, n_heads_ssm)
        if row is not None:
            a = a + row[:, :kh, :].transpose(2, 0, 1).reshape(s_len, n_heads_ssm)
        return pad_cols(a)

    z = _mm(n1, w_z, 'nn', "in_z")
    xbc_raw, (gathered['out'],) = _mm(n1, w_xbc, 'nn', "in_xbc", beside=gather(['out']))
    dtraw = _mm(n1, w_dt, 'nn', "in_dt")
    xbc = _conv_fwd(xbc_raw, conv_w, conv_b, "conv")
    dt, acum = _dt_fwd(dtraw, dt_bias, a_log, "dt")
    dt_g, ac_g, act_g = to_groups(dt), to_groups(acum), to_groups_t(acum)
    (y_ssd, states), (gathered['gu0'],) = _ssd_fwd(xbc, dt_g, ac_g, act_g, d_inner, d_state, kh, "ssd",
                                                   beside=gather(['gu0']))
    yn, (gathered['q'],) = _gate_fwd(y_ssd, xbc, z, d_exp, gnorm, d_inner, "gate", beside=gather(['q']))
    w_out = gathered['out'].reshape(-1, d_model)
    h1, (gathered['o'],) = _mm(yn, w_out, 'nn', "out_proj", add=xs_in, beside=gather(['o']))

    w_gu, w_dn = {}, {}

    def ffn_fwd(h, i, also):
        nrm = _rms_fwd(h, ffn_nw[i], f"ffn{i}_norm")
        w_gu[i] = gathered[f'gu{i}']
        gu = _mm(nrm, w_gu[i], 'nn', f"ffn{i}_up", out_dtype=BF16, shards=True,
                 beside=gather(also) if also else None)
        if also:
            gu, arrived = gu
            gathered.update(zip(also, arrived))
        act = _swiglu_fwd(gu, d_ff, f"ffn{i}_act")
        w_dn[i] = gathered[f'dn{i}'].reshape(-1, d_model)
        return nrm, gu, act, _mm(act, w_dn[i], 'nn', f"ffn{i}_down", add=h)

    n2, gu0, act0, h2 = ffn_fwd(h1, 0, ['dn0', 'kvf'])
    w_kvf_full = jnp.concatenate([gathered['kvf'][j] for j in range(N_CHIPS)], axis=1)
    w_k, w_v, w_f = w_kvf_full[:, :d_att], w_kvf_full[:, d_att:2 * d_att], pad_cols(w_kvf_full[:, 2 * d_att:])
    w_qm, w_om = gathered['q'].reshape(-1, d_att), gathered['o'].reshape(-1, d_model)
    nkv = _rms_fwd(h2, kv_nw, "kv_norm")
    k_raw = _mm(nkv, w_k, 'nn', "proj_k")
    v_att = _mm(nkv, w_v, 'nn', "proj_v", out_dtype=BF16)
    f_raw = _mm(nkv, w_f, 'nn', "proj_f")
    k_att = _rms_fwd(k_raw, k_nw, "k_norm")
    cum = _forget_fwd(f_raw, bf_pad, "forget")
    cq_rep = jnp.repeat(cum[:, :n_att], ATT_HEAD_DIM, axis=1)
    tb = min(ATT_BLOCK, s_len)
    ck = cum[:, :n_att].T.reshape(n_att, s_len // tb, 1, tb)
    n3 = _rms_fwd(h2, b_nw, "b_norm")
    q_raw = _mm(n3, w_qm, 'nn', "proj_q")
    q_att = _rms_fwd(q_raw, q_nw, "q_norm")
    ck_half = cum[:, :n_att].T.reshape(n_att, 2 * s_len // tb, 1, tb // 2)
    (o_att, lse_rep), arrived = _attn_fwd(q_att, k_att, v_att, cq_rep, ck_half, "attn",
                                          beside=gather(['gu1', 'dn1']))
    gathered.update(zip(['gu1', 'dn1'], arrived))
    h3 = _mm(o_att, w_om, 'nn', "proj_o", add=h2)
    n4, gu1, act1, h4 = ffn_fwd(h3, 1, [])

    def loss_fn(h, t):
        err = h - t
        sq = jnp.sum(jnp.sum(err * err, axis=1, keepdims=True), axis=0, keepdims=True)
        return err * (1.0 / d_model), err * (1.0 / d_model), sq * (0.5 / d_model)

    dh4, dh4_b, loss_part = _rowwise(loss_fn, "loss", [h4, target], outs=[(d_model, F32), (d_model, BF16)],
                                     accs=[(1, 1)])

    n_f = n_att
    g_big, chip_sums, parts = {}, {}, {}

    def col_shards(pieces):
        fullw = jnp.concatenate(pieces, axis=1)
        return fullw.reshape(fullw.shape[0], N_CHIPS, -1).transpose(1, 0, 2)

    def row_shards(a):
        return a.reshape(N_CHIPS, -1, a.shape[1])

    def add_sibling(keys, got):
        for k, a in zip(keys, got):
            chip_sums[k] = _sum_half(g_big[k], a, cc, "reduce_sum2_" + k)

    def scatter(keys, tag):
        todo = [k for k in keys if k not in chip_sums]
        if todo:
            g_list = [g_big[k] for k in todo]
            hrs = [g.shape[1] // 2 for g in g_list]
            add_sibling(todo, _exchange("reduce_d2d_" + tag, g_list,
                                        [((N_CHIPS, hr, g.shape[2]), BF16) for g, hr in zip(g_list, hrs)], FLIP_C,
                                        lambda i, r, me, peer: r.at[:, pl.ds((1 - me[2]) * hrs[i], hrs[i]), :],
                                        lambda i, r, sender, k: r))
        return _scatter_plan([chip_sums[k] for k in keys])

    def ffn_bwd(dh, dh_b, h, nrm, gu, act, i, hosted):
        dact = _mm(dh_b, w_dn[i], 'nt', f"ffn{i}_down_dx", beside=scatter(hosted[:1], f"a{i}") if hosted else None)
        g_dn = _mm(act, dh_b, 'tn', f"ffn{i}_down_dw", out_dtype=BF16,
                   beside=scatter(hosted[1:], f"b{i}") if hosted else None)
        if hosted:
            (dact, first), (g_dn, second) = dact, g_dn
            parts.update(zip(hosted, first + second))
        dgu = _swiglu_bwd(gu, dact, d_ff, f"ffn{i}_act_bwd")
        dn = _mm(dgu, w_gu[i], 'nt', f"ffn{i}_up_dx", shards=True)
        g_gu = _mm(nrm, dgu, 'tn', f"ffn{i}_up_dw", out_dtype=BF16, shards=True)
        mine = [f"gu{i}", f"dn{i}"]
        g_big.update(zip(mine, [g_gu, row_shards(g_dn)]))
        (dh_new, dh_new_b, g_nw), got = _rms_bwd(h, ffn_nw[i], dn, f"ffn{i}_norm_bwd", extra=dh, copy_bf16=True,
                                                 beside=_swap_plan([g_big[k] for k in mine]))
        add_sibling(mine, got)
        return dh_new, dh_new_b, g_nw

    dh3, dh3_b, g_fnw1 = ffn_bwd(dh4, dh4_b, h3, n4, gu1, act1, 1, [])
    do_att = _mm(dh3_b, w_om, 'nt', "proj_o_dx", out_dtype=BF16)
    g_wo = _mm(o_att, dh3_b, 'tn', "proj_o_dw", out_dtype=BF16)
    g_big.update(o=row_shards(g_wo))
    late_keys = ['gu1', 'dn1', 'o']
    plan_late = scatter(late_keys, "late")

    def delta_fn(a, b):
        prod = a.astype(F32) * b.astype(F32)
        return jnp.concatenate([jnp.broadcast_to(jnp.sum(prod[:, g], axis=1, keepdims=True), (a.shape[0], ATT_HEAD_DIM))
                                for g in _lane_groups(a.shape[1], ATT_HEAD_DIM)], axis=1)

    delta_rep = _rowwise(delta_fn, "attn_delta", [do_att, o_att], outs=[(d_att, F32)])[0]
    (dq_att, dk_att, dv_att, dcq_rep, dck), arrived = _attn_bwd(q_att, k_att, v_att, do_att, lse_rep, delta_rep,
                                                                cq_rep, ck, "attn_bwd", beside=plan_late)
    parts.update(zip(late_keys, arrived))
    dq_raw, g_qnw = _rms_bwd(q_raw, q_nw, dq_att, "q_norm_bwd", out_dtype=BF16)
    dn3 = _mm(dq_raw, w_qm, 'nt', "proj_q_dx")
    g_wq = _mm(n3, dq_raw, 'tn', "proj_q_dw", out_dtype=BF16)
    dh2, g_bnw = _rms_bwd(h2, b_nw, dn3, "b_norm_bwd", extra=dh3)
    dk_raw, g_knw = _rms_bwd(k_raw, k_nw, dk_att, "k_norm_bwd", out_dtype=BF16)
    dcum = pad_cols(dcq_rep.reshape(s_len, n_att, ATT_HEAD_DIM)[:, :, 0] + dck.reshape(n_att, s_len).T)
    df_raw, g_bf = _forget_bwd(f_raw, bf_pad, dcum, "forget_bwd")
    dnkv = _mm(dk_raw, w_k, 'nt', "proj_k_dx")
    dnkv = _mm(dv_att, w_v, 'nt', "proj_v_dx", add=dnkv)
    dnkv = _mm(df_raw, w_f, 'nt', "proj_f_dx", add=dnkv)
    g_wk = _mm(nkv, dk_raw, 'tn', "proj_k_dw", out_dtype=BF16)
    g_wv = _mm(nkv, dv_att, 'tn', "proj_v_dw", out_dtype=BF16)
    g_wf = _mm(nkv, df_raw, 'tn', "proj_f_dw", out_dtype=BF16)
    g_big.update(q=row_shards(g_wq), kvf=col_shards([g_wk, g_wv, g_wf[:, :n_f]]))
    (dh2, dh2_b, g_kvnw), got = _rms_bwd(h2, kv_nw, dnkv, "kv_norm_bwd", extra=dh2, copy_bf16=True,
                                         beside=_swap_plan([g_big['q'], g_big['kvf']]))
    add_sibling(['q', 'kvf'], got)
    dh1, dh1_b, g_fnw0 = ffn_bwd(dh2, dh2_b, h1, n2, gu0, act0, 0, ['q', 'kvf'])
    plan_ffn0 = scatter(['gu0', 'dn0'], "ffn0")

    dyn = _mm(dh1_b, w_out, 'nt', "out_proj_dx")
    g_wout = _mm(yn, dh1_b, 'tn', "out_proj_dw", out_dtype=BF16)
    dy_ssd, dxs_skip, dz, g_dexp, g_gnorm = _gate_bwd(y_ssd, xbc, z, d_exp, gnorm, dyn, d_inner, "gate_bwd")
    (dxs, d_b, d_c, ddt_g, dacc_g, dacr_g), arrived = _ssd_bwd(xbc, dt_g, ac_g, act_g, states, dy_ssd, dxs_skip,
                                                               d_inner, d_state, kh, "ssd_bwd", beside=plan_ffn0)
    parts.update(zip(['gu0', 'dn0'], arrived))
    g_big.update(out=row_shards(g_wout))
    (du, g_convw, g_convb), (parts['out'],) = _conv_bwd(xbc_raw, conv_w, conv_b, [dxs, d_b, d_c], "conv_bwd",
                                                        beside=scatter(['out'], "out"))
    draw, g_dtb, g_alog = _dt_bwd(dtraw, dt_bias, a_log, from_groups(ddt_g), from_groups(dacc_g, dacr_g), "dt_bwd")
    g_wz = _mm(n1, dz, 'tn', "in_z_dw", out_dtype=BF16)
    g_wxbc = _mm(n1, du, 'tn', "in_xbc_dw", out_dtype=BF16)
    g_wdt = _mm(n1, draw, 'tn', "in_dt_dw", out_dtype=BF16)
    g_in = jnp.concatenate([g_wz, g_wxbc, g_wdt[:, :n_heads_ssm]], axis=1)
    g_in = g_in.reshape(2, d_model // 2, N_CHIPS, -1).transpose(0, 2, 1, 3)
    g_big.update(in0=g_in[0], in1=g_in[1])
    dn1, (parts['in0'],) = _mm(du, w_xbc, 'nt', "in_xbc_dx", beside=scatter(['in0'], "in0"))
    dn1, (parts['in1'],) = _mm(dz, w_z, 'nt', "in_z_dx", add=dn1, beside=scatter(['in1'], "in1"))
    dn1 = _mm(draw, w_dt, 'nt', "in_dt_dx", add=dn1)
    dx, g_norm_a = _rms_bwd(xs_in, norm_a, dn1, "norm_a_bwd", extra=dh1)

    g_small = {
        'a_norm_w': g_norm_a, 'a_conv_w': g_convw[None], 'a_conv_b': g_convb,
        'a_dt_bias': g_dtb[:, :n_heads_ssm], 'a_A_log': g_alog[:, :n_heads_ssm],
        'a_D': g_dexp.reshape(n_heads_ssm, SSM_HEAD_DIM).sum(axis=1).reshape(1, -1), 'a_gnorm_w': g_gnorm,
        'kv_norm_w': g_kvnw.reshape(-1), 'b_f': g_bf[0, :n_f], 'k_norm_w': g_knw.reshape(-1), 'b_norm_w': g_bnw,
        'q_norm_w': g_qnw, 'ffn_norm_w': jnp.concatenate([g_fnw0, g_fnw1], axis=0),
    }

    sg = _pack([g_small[n] for n in SMALL] + [loss_part], F32, 8 * PACK_COLS).reshape(-1, PACK_COLS)
    sg_all = _exchange("reduce_small", [sg], [((2 * N_CHIPS,) + sg.shape, F32)], FLIP_ALL,
                       lambda i, r, me, peer: r, lambda i, r, sender, k: r.at[_device(sender)],
                       lambda i, s, d, me: (s, d.at[_device(me)]))[0]
    sg_sum = _sum_arrays([sg_all[d] for d in range(2 * N_CHIPS)], F32, "reduce_small_sum").reshape(-1)
    red_small, off = {}, 0
    for n in SMALL:
        shp = g_small[n].shape
        red_small[n] = sg_sum[off:off + math.prod(shp)].reshape(shp)
        off += math.prod(shp)
    loss = sg_sum[off]

    red_keys = list(parts)
    half_sums = [_sum_parts(chip_sums[k], parts[k], my_chip, "reduce_sum4_" + k) for k in red_keys]
    others = _exchange("reduce_back", half_sums, [(h.shape, F32) for h in half_sums], FLIP_C,
                       lambda i, r, me, peer: r, lambda i, r, sender, k: r)
    mine_of, theirs_of = dict(zip(red_keys, half_sums)), dict(zip(red_keys, others))

    grads, delta, new_m, new_v = {}, {}, {}, {}
    layers_of = {'a_in_proj': ['in0', 'in1'], 'a_out_proj': ['out'], 'w_kvf': ['kvf'], 'w_q': ['q'], 'w_o': ['o'],
                 'w_gate_up': ['gu0', 'gu1'], 'w_down': ['dn0', 'dn1']}
    for n, keys in layers_of.items():
        three_d = (len(keys), math.prod(shapes[n][:-1]) // len(keys), shapes[n][-1])
        res = None
        for layer, k in enumerate(keys):
            res = _adamw_big(w[n].reshape(three_d), mom[n].reshape(three_d), var[n].reshape(three_d), mine_of[k],
                             theirs_of[k], cc, layer, res, "adamw_" + k)
        grads[n], delta[n], new_m[n], new_v[n] = [r.reshape(shapes[n]) for r in res]
    for n in SMALL:
        if n in SMALL_SHARDED:
            ax = SMALL_SHARDED[n]
            grads[n] = lax.dynamic_slice_in_dim(red_small[n], my_chip * shapes[n][ax], shapes[n][ax], axis=ax)
        else:
            grads[n] = red_small[n]

    packed = [_pack([src[n] for n in SMALL], F32, 8 * LANES).reshape(-1, LANES) for src in (w, grads, mom, var)]
    small_out = _adamw(*packed, "adamw_small")
    for store, flat in zip((delta, new_m, new_v), small_out):
        flat, off = flat.reshape(-1), 0
        for n in SMALL:
            sz = math.prod(shapes[n])
            store[n] = flat[off:off + sz].reshape(shapes[n])
            off += sz

    return (loss, dx[None], *[grads[n] for n in WEIGHTS], *[delta[n] for n in WEIGHTS],
            *[new_m[n] for n in WEIGHTS], *[new_v[n] for n in WEIGHTS])
```
